```python
import jax
import jax.numpy as jnp
from jax import lax
import numpy as np

D_MODEL = 1024
BATCH = 8
SEQ = 2048
DEPTH = 2

GRID_W = 64
CTX_LEN = 256
D_LRU = 512
N_LRU_HEADS = 8
LRU_HEAD_DIM = D_LRU // N_LRU_HEADS
CONV_W = 4
CONV_LEFT = 2
LRU_C = 8.0
D_POOL = 512
POOL_WINDOWS = (2, 4, 8, 16)
N_POOL_GROUPS = len(POOL_WINDOWS)
POOL_GROUP_DIM = D_POOL // N_POOL_GROUPS
D_IN = 2 * D_LRU + D_POOL
D_MIX = D_LRU + D_POOL
N_EXPERTS = 32
TOP_K = 4
D_EXPERT = D_MODEL
SWIGLU_LIMIT = 7.0
SWIGLU_ALPHA = 1.702
LN_EPS = 1e-5

kernel_name = 'hybrid_lru_pool_moe_dit'


def layer_norm(x, g=None, b=None):
    xf = x.astype(jnp.float32)
    mu = jnp.mean(xf, axis=-1, keepdims=True)
    var = jnp.mean(jnp.square(xf - mu), axis=-1, keepdims=True)
    y = (xf - mu) * lax.rsqrt(var + LN_EPS)
    if g is not None:
        y = y * g.astype(jnp.float32) + b.astype(jnp.float32)
    return y.astype(x.dtype)


def modulate(x, shift, scale):
    return x * (1 + scale) + shift


def grid_sincos(rows, cols, d):
    quarter = d // 4
    omega = 1.0 / (10000.0 ** (jnp.arange(quarter, dtype=jnp.float32) / quarter))

    def emb1d(n):
        ang = jnp.arange(n, dtype=jnp.float32)[:, None] * omega[None, :]
        return jnp.concatenate([jnp.sin(ang), jnp.cos(ang)], axis=-1)

    er = jnp.broadcast_to(emb1d(rows)[:, None, :], (rows, cols, d // 2))
    ec = jnp.broadcast_to(emb1d(cols)[None, :, :], (rows, cols, d // 2))
    return jnp.concatenate([er, ec], axis=-1).reshape(rows * cols, d)


def conv_centred(u, w, b):
    L = u.shape[1]
    up = jnp.pad(u, ((0, 0), (CONV_LEFT, CONV_W - 1 - CONV_LEFT), (0, 0)))
    y = b
    for k in range(CONV_W):
        y = y + up[:, k:k + L, :] * w[k]
    return y


def block_diag(u, w, b):
    B, L, _ = u.shape
    uh = u.reshape(B, L, N_LRU_HEADS, LRU_HEAD_DIM)
    return jnp.einsum('blhi,hij->blhj', uh, w).reshape(B, L, D_LRU) + b


def lru_coeffs(u, wa, ba, wx, bx, lam):
    uf = u.astype(jnp.float32)
    r = jax.nn.sigmoid(block_diag(uf, wa.astype(jnp.float32), ba.astype(jnp.float32)))
    i = jax.nn.sigmoid(block_diag(uf, wx.astype(jnp.float32), bx.astype(jnp.float32)))
    log_a = -LRU_C * r * jax.nn.softplus(-lam.astype(jnp.float32))
    a = jnp.exp(log_a)
    bterm = jnp.sqrt(-jnp.expm1(2.0 * log_a)) * (i * uf)
    return a, bterm


def _combine(e1, e2):
    a1, b1 = e1
    a2, b2 = e2
    return a1 * a2, a2 * b1 + b2


def linear_scan(a, bterm, h0, reverse):
    edge = -1 if reverse else 0
    bterm = bterm.at[:, edge].add(a[:, edge] * h0)
    _, h = lax.associative_scan(_combine, (a, bterm), reverse=reverse, axis=1)
    return h


def rglru_bidir(uc, ul, ga_w, ga_b, gx_w, gx_b, lam):
    hc_dirs = []
    hl_sum = None
    for d, rev in enumerate((False, True)):
        a, bt = lru_coeffs(uc, ga_w[d], ga_b[d], gx_w[d], gx_b[d], lam[d])
        h_c = linear_scan(a, bt, jnp.zeros_like(a[:, 0]), rev)
        h_end = h_c[:, 0] if rev else h_c[:, -1]
        a, bt = lru_coeffs(ul, ga_w[d], ga_b[d], gx_w[d], gx_b[d], lam[d])
        h_l = linear_scan(a, bt, h_end, rev)
        hc_dirs.append(h_c)
        hl_sum = h_l if hl_sum is None else hl_sum + h_l
    return hc_dirs, hl_sum


def multiscale_pool(u, w, b, scale, grid_w):
    B, L, _ = u.shape
    t = jnp.arange(L)
    if grid_w is None:
        seg_lo, seg_hi = 0, L
    else:
        seg_lo = (t // grid_w) * grid_w
        seg_hi = seg_lo + grid_w
    uf = u.astype(jnp.float32)
    prefix = jnp.pad(jnp.cumsum(uf, axis=1), ((0, 0), (1, 0), (0, 0)))
    diffs = []
    for g, win in enumerate(POOL_WINDOWS):
        lo = jnp.maximum(t - win // 2, seg_lo)
        hi = jnp.minimum(t + win // 2, seg_hi)
        sl = slice(g * POOL_GROUP_DIM, (g + 1) * POOL_GROUP_DIM)
        pg = prefix[:, :, sl]
        cnt = (hi - lo).astype(jnp.float32)[None, :, None]
        diffs.append((pg[:, hi] - pg[:, lo]) / cnt - uf[:, :, sl])
    dmat = jnp.stack(diffs, axis=2)
    y = jnp.einsum('blgi,gij->blgj', dmat, w.astype(jnp.float32)).reshape(B, L, D_POOL)
    y = (y + b.astype(jnp.float32)) * scale.astype(jnp.float32)
    return y.astype(u.dtype)


def token_mixer(hl, hc, w_in, conv_w, conv_b, ga_w, ga_b, gx_w, gx_b, lam,
                pool_w, pool_b, pool_scale, w_out, with_ctx_out):
    zl = hl @ w_in
    zc = hc @ w_in
    xr_l, y_l, xp_l = jnp.split(zl, [D_LRU, 2 * D_LRU], axis=-1)
    xr_c, y_c, xp_c = jnp.split(zc, [D_LRU, 2 * D_LRU], axis=-1)
    ul = conv_centred(xr_l, conv_w, conv_b)
    uc = conv_centred(xr_c, conv_w, conv_b)
    hc_dirs, hr_l = rglru_bidir(uc, ul, ga_w, ga_b, gx_w, gx_b, lam)
    lat = jnp.concatenate(
        [hr_l.astype(hl.dtype) * jax.nn.gelu(y_l),
         multiscale_pool(xp_l, pool_w, pool_b, pool_scale, GRID_W)], axis=-1) @ w_out
    if not with_ctx_out:
        return lat, None
    hr_c = (hc_dirs[0] + hc_dirs[1]).astype(hc.dtype)
    ctx_out = jnp.concatenate(
        [hr_c * jax.nn.gelu(y_c),
         multiscale_pool(xp_c, pool_w, pool_b, pool_scale, None)], axis=-1) @ w_out
    return lat, ctx_out


def moe_ffn(h, router_w, router_b, w1, b1, w2, b2):
    logits = (h @ router_w + router_b).astype(jnp.float32)
    top_v, top_i = lax.top_k(logits, TOP_K)
    top_p = jax.nn.softmax(top_v, axis=-1)
    gates = jnp.sum(jax.nn.one_hot(top_i, N_EXPERTS, dtype=jnp.float32) * top_p[..., None],
                    axis=1).astype(h.dtype)
    out = jnp.zeros_like(h)
    for e in range(N_EXPERTS):
        gu = h @ w1[e] + b1[e]
        glu = jnp.minimum(gu[:, :D_EXPERT], SWIGLU_LIMIT)
        lin = jnp.clip(gu[:, D_EXPERT:], -SWIGLU_LIMIT, SWIGLU_LIMIT)
        act = glu * jax.nn.sigmoid(SWIGLU_ALPHA * glu) * (lin + 1)
        out = out + gates[:, e:e + 1] * (act @ w2[e] + b2[e])
    return out


def setup_inputs(seed: int = 0) -> dict:
    key = jax.random.key(seed)
    ks = iter(jax.random.split(key, 40))

    def nrm(shape, s):
        return jax.random.normal(next(ks), shape, jnp.float32) * s

    beta = (8.0 * DEPTH) ** -0.25
    u = jax.random.uniform(next(ks), (DEPTH, 2, D_LRU), jnp.float32, 0.9, 0.999)
    s = u ** (1.0 / LRU_C)
    lru_lambda = jnp.log(s) - jnp.log1p(-s)
    return {
        'x': nrm((BATCH, SEQ, D_MODEL), 1.0),
        'c': nrm((BATCH, D_MODEL), 1.0),
        'ctx': nrm((BATCH, CTX_LEN, D_MODEL), 1.0),
        'c_ctx': nrm((D_MODEL,), 1.0),
        'w_mod': nrm((DEPTH, D_MODEL, 6 * D_MODEL), D_MODEL ** -0.5),
        'b_mod': nrm((DEPTH, 6 * D_MODEL), 0.01),
        'w_in': nrm((DEPTH, D_MODEL, D_IN), D_MODEL ** -0.5),
        'conv_w': nrm((DEPTH, CONV_W, D_LRU), CONV_W ** -0.5),
        'conv_b': nrm((DEPTH, D_LRU), 0.01),
        'gate_a_w': nrm((DEPTH, 2, N_LRU_HEADS, LRU_HEAD_DIM, LRU_HEAD_DIM), LRU_HEAD_DIM ** -0.5),
        'gate_a_b': nrm((DEPTH, 2, D_LRU), 0.01),
        'gate_x_w': nrm((DEPTH, 2, N_LRU_HEADS, LRU_HEAD_DIM, LRU_HEAD_DIM), LRU_HEAD_DIM ** -0.5),
        'gate_x_b': nrm((DEPTH, 2, D_LRU), 0.01),
        'lru_lambda': lru_lambda,
        'pool_w': nrm((DEPTH, N_POOL_GROUPS, POOL_GROUP_DIM, POOL_GROUP_DIM), POOL_GROUP_DIM ** -0.5),
        'pool_b': nrm((DEPTH, D_POOL), 0.01),
        'pool_scale': 1.0 + nrm((DEPTH, D_POOL), 0.02),
        'w_out': nrm((DEPTH, D_MIX, D_MODEL), beta * D_MIX ** -0.5),
        'ln1_g': 1.0 + nrm((DEPTH, D_MODEL), 0.02),
        'ln1_b': nrm((DEPTH, D_MODEL), 0.02),
        'router_w': nrm((DEPTH, D_MODEL, N_EXPERTS), D_MODEL ** -0.5),
        'router_b': nrm((DEPTH, N_EXPERTS), 0.01),
        'exp_w1': nrm((DEPTH, N_EXPERTS, D_MODEL, 2 * D_EXPERT), D_MODEL ** -0.5),
        'exp_b1': nrm((DEPTH, N_EXPERTS, 2 * D_EXPERT), 0.01),
        'exp_w2': nrm((DEPTH, N_EXPERTS, D_EXPERT, D_MODEL), beta * D_EXPERT ** -0.5),
        'exp_b2': nrm((DEPTH, N_EXPERTS, D_MODEL), 0.01),
        'ln2_g': 1.0 + nrm((DEPTH, D_MODEL), 0.02),
        'ln2_b': nrm((DEPTH, D_MODEL), 0.02),
    }


def reference(x, c, ctx, c_ctx, w_mod, b_mod, w_in, conv_w, conv_b, gate_a_w, gate_a_b,
              gate_x_w, gate_x_b, lru_lambda, pool_w, pool_b, pool_scale, w_out, ln1_g, ln1_b,
              router_w, router_b, exp_w1, exp_b1, exp_w2, exp_b2, ln2_g, ln2_b):
    B, S, D = x.shape
    ROWS = S // GRID_W
    alpha = (2.0 * DEPTH) ** 0.25
    xl = layer_norm(x + grid_sincos(ROWS, GRID_W, D).astype(x.dtype))
    xc = layer_norm(ctx)
    silu_c = jax.nn.silu(c)
    silu_cc = jax.nn.silu(c_ctx)
    for l in range(DEPTH):
        with_ctx = l < DEPTH - 1
        mod_l = silu_c @ w_mod[l] + b_mod[l]
        mod_c = silu_cc @ w_mod[l] + b_mod[l]
        sh1_l, sc1_l, g1_l, sh2_l, sc2_l, g2_l = jnp.split(mod_l[:, None, :], 6, axis=-1)
        sh1_c, sc1_c, g1_c, sh2_c, sc2_c, g2_c = jnp.split(mod_c, 6, axis=-1)

        ml, mc = token_mixer(modulate(xl, sh1_l, sc1_l), modulate(xc, sh1_c, sc1_c),
                             w_in[l], conv_w[l], conv_b[l], gate_a_w[l], gate_a_b[l],
                             gate_x_w[l], gate_x_b[l], lru_lambda[l], pool_w[l], pool_b[l],
                             pool_scale[l], w_out[l], with_ctx)
        xl = layer_norm(alpha * xl + g1_l * ml, ln1_g[l], ln1_b[l])
        hl2 = modulate(xl, sh2_l, sc2_l).reshape(-1, D)
        if with_ctx:
            xc = layer_norm(alpha * xc + g1_c * mc, ln1_g[l], ln1_b[l])
            hc2 = modulate(xc, sh2_c, sc2_c).reshape(-1, D)
            tokens = jnp.concatenate([hl2, hc2], axis=0)
        else:
            tokens = hl2

        f = moe_ffn(tokens, router_w[l], router_b[l], exp_w1[l], exp_b1[l], exp_w2[l], exp_b2[l])
        xl = layer_norm(alpha * xl + g2_l * f[:B * S].reshape(B, S, D), ln2_g[l], ln2_b[l])
        if with_ctx:
            fc = f[B * S:].reshape(B, -1, D)
            xc = layer_norm(alpha * xc + g2_c * fc, ln2_g[l], ln2_b[l])
    return xl
```

```python
import functools

import jax
import jax.numpy as jnp
from jax import lax
from jax.experimental import pallas as pl
from jax.experimental.pallas import tpu as pltpu

D_MODEL = 1024
BATCH = 8
SEQ = 2048
DEPTH = 2
GRID_W = 64
CTX_LEN = 256
D_LRU = 512
N_LRU_HEADS = 8
LRU_HEAD_DIM = D_LRU // N_LRU_HEADS
LRU_C = 8.0
D_POOL = 512
POOL_WINDOWS = (2, 4, 8, 16)
POOL_GROUP_DIM = D_POOL // len(POOL_WINDOWS)
D_IN = 2 * D_LRU + D_POOL
N_EXPERTS = 32
TOP_K = 4
D_EXPERT = D_MODEL
SWIGLU_LIMIT = 7.0
SWIGLU_ALPHA = 1.702
LN_EPS = 1e-5
ALPHA = (2.0 * DEPTH) ** 0.25

F32 = jnp.float32
BF16 = jnp.bfloat16

SUBLANES = 8
LANES = 128
TT = GRID_W
TR = TT * BATCH
T_ALL = CTX_LEN + SEQ
R_ALL = T_ALL * BATCH
N_TILES = T_ALL // TT
CTX_TILES = CTX_LEN // TT
GATE_BLK = 256
POOL_HALO = max(POOL_WINDOWS) // 2
TM = 256
VMEM_LIMIT = 56 * 1024 * 1024


def _cparams(sem):
    return pltpu.CompilerParams(dimension_semantics=sem, vmem_limit_bytes=VMEM_LIMIT)


def _sigmoid(x):
    return 1.0 / (1.0 + jnp.exp(-x))


def _layer_norm(x):
    mu = jnp.mean(x, axis=-1, keepdims=True)
    xc = x - mu
    var = jnp.mean(xc * xc, axis=-1, keepdims=True)
    return xc * lax.rsqrt(var + LN_EPS)


def _per_batch(x, v, op):
    r, d = x.shape
    return op(x.reshape(r // BATCH, BATCH, d), v[None]).reshape(r, d)


def _mod_kernel(c_ref, w_ref, b_ref, o_ref):
    c = c_ref[...]
    s = c * _sigmoid(c)
    o_ref[...] = jnp.dot(s, w_ref[...], precision=lax.Precision.HIGHEST,
                         preferred_element_type=F32) + b_ref[...]


def _modulation(cvec, w_mod, b_mod):
    tn = 512
    return pl.pallas_call(
        _mod_kernel,
        grid=(DEPTH, 6 * D_MODEL // tn),
        in_specs=[
            pl.BlockSpec((2 * SUBLANES, D_MODEL), lambda l, j: (0, 0)),
            pl.BlockSpec((None, D_MODEL, tn), lambda l, j: (l, 0, j)),
            pl.BlockSpec((None, 1, tn), lambda l, j: (l, 0, j)),
        ],
        out_specs=pl.BlockSpec((None, 2 * SUBLANES, tn), lambda l, j: (l, 0, j)),
        out_shape=jax.ShapeDtypeStruct((DEPTH, 2 * SUBLANES, 6 * D_MODEL), F32),
        compiler_params=_cparams(("arbitrary", "arbitrary")),
        name="modulation",
    )(cvec, w_mod, b_mod.reshape(DEPTH, 1, 6 * D_MODEL))


def _entry_kernel(x_ref, er_ref, ec_ref, o_ref):
    i = pl.program_id(0)
    x = x_ref[...]
    lat = (i >= CTX_TILES).astype(F32)
    pos = jnp.concatenate(
        [jnp.broadcast_to(er_ref[0], (TR, D_MODEL // 2)), ec_ref[...] * lat], axis=1)
    o_ref[...] = _layer_norm(x + pos)


def _entry(x_tm, er, ec):
    return pl.pallas_call(
        _entry_kernel,
        grid=(N_TILES,),
        in_specs=[
            pl.BlockSpec((TR, D_MODEL), lambda i: (i, 0)),
            pl.BlockSpec((1, 1, D_MODEL // 2), lambda i: (i, 0, 0)),
            pl.BlockSpec((TR, D_MODEL // 2), lambda i: (0, 0)),
        ],
        out_specs=pl.BlockSpec((TR, D_MODEL), lambda i: (i, 0)),
        out_shape=jax.ShapeDtypeStruct((R_ALL, D_MODEL), F32),
        compiler_params=_cparams(("arbitrary",)),
        name="entry_ln",
    )(x_tm, er, ec)


def _inproj_kernel(x_ref, sh_ref, sc_ref, w_ref, z_ref):
    h = _per_batch(x_ref[...], 1.0 + sc_ref[0], jnp.multiply)
    h = _per_batch(h, sh_ref[0], jnp.add)
    z_ref[...] = jnp.dot(h.astype(BF16), w_ref[...], preferred_element_type=F32)


def _mod_spec(chunk):
    return pl.BlockSpec((1, BATCH, D_MODEL),
                        lambda i: ((i >= CTX_TILES).astype(jnp.int32), 0, chunk))


def _inproj(x, mod, w_in_bf):
    return pl.pallas_call(
        _inproj_kernel,
        grid=(N_TILES,),
        in_specs=[
            pl.BlockSpec((TR, D_MODEL), lambda i: (i, 0)),
            _mod_spec(0), _mod_spec(1),
            pl.BlockSpec((D_MODEL, D_IN), lambda i: (0, 0)),
        ],
        out_specs=pl.BlockSpec((TR, D_IN), lambda i: (i, 0)),
        out_shape=jax.ShapeDtypeStruct((R_ALL, D_IN), F32),
        compiler_params=_cparams(("arbitrary",)),
        name="in_proj",
    )(x, mod, mod, w_in_bf)


def _rev_tile(i):
    return jnp.where(i < CTX_TILES, CTX_TILES - 1 - i, N_TILES - 1 + CTX_TILES - i)


def _block_diag_dot(u_bf, w_ref, d, g):
    halves = [jnp.dot(u_bf[:, k * GATE_BLK:(k + 1) * GATE_BLK], w_ref[d, g, k],
                      preferred_element_type=F32) for k in range(D_LRU // GATE_BLK)]
    return jnp.concatenate(halves, axis=1)


def _lru_coeffs(tile, z_ref, zp_ref, zn_ref, cw_ref, cb_ref, gw_ref, gb_ref, lam_ref, d):
    x = z_ref[...]
    seg_first = (tile == 0) | (tile == CTX_TILES)
    seg_last = (tile == CTX_TILES - 1) | (tile == N_TILES - 1)
    prev = jnp.where(seg_first, 0.0, zp_ref[...])
    nxt = jnp.where(seg_last, 0.0, zn_ref[...])
    xm2 = jnp.concatenate([prev, x[:-2 * BATCH]], axis=0)
    xm1 = jnp.concatenate([prev[BATCH:], x[:-BATCH]], axis=0)
    xp1 = jnp.concatenate([x[BATCH:], nxt], axis=0)
    u = (cb_ref[...] + xm2 * cw_ref[0:1] + xm1 * cw_ref[1:2]
         + x * cw_ref[2:3] + xp1 * cw_ref[3:4])
    u_bf = u.astype(BF16)
    r = _sigmoid(_block_diag_dot(u_bf, gw_ref, d, 0) + gb_ref[d, 0:1])
    ig = _sigmoid(_block_diag_dot(u_bf, gw_ref, d, 1) + gb_ref[d, 1:2])
    nl = -lam_ref[d:d + 1]
    softplus = jnp.maximum(nl, 0.0) + jnp.log(1.0 + jnp.exp(-jnp.abs(nl)))
    log_a = (-LRU_C) * r * softplus
    a = jnp.exp(log_a)
    mult = jnp.sqrt(1.0 - a * a)
    return a, mult * (ig * u)


def _scan_kernel(zf_ref, zfp_ref, zfn_ref, zr_ref, zrp_ref, zrn_ref,
                 cw_ref, cb_ref, gw_ref, gb_ref, lam_ref,
                 hf_ref, hr_ref, state_ref):
    i = pl.program_id(0)

    @pl.when(i == 0)
    def _():
        state_ref[...] = jnp.zeros_like(state_ref)

    af, bf = _lru_coeffs(i, zf_ref, zfp_ref, zfn_ref, cw_ref, cb_ref, gw_ref, gb_ref,
                         lam_ref, 0)
    ar, br = _lru_coeffs(_rev_tile(i), zr_ref, zrp_ref, zrn_ref, cw_ref, cb_ref, gw_ref,
                         gb_ref, lam_ref, 1)
    hf = state_ref[0]
    hr = state_ref[1]
    for s in range(TT):
        f = slice(s * BATCH, (s + 1) * BATCH)
        hf = af[f] * hf + bf[f]
        hf_ref[f, :] = hf
        b = slice((TT - 1 - s) * BATCH, (TT - s) * BATCH)
        hr = ar[b] * hr + br[b]
        hr_ref[b, :] = hr
    state_ref[0] = hf
    state_ref[1] = hr


def _scan(z, conv_w, conv_b, gate_w, gate_b, lam):
    prev_rows = 2 * BATCH
    tiles_per_prev = TR // prev_rows
    tiles_per_next = TR // BATCH
    last_next = R_ALL // BATCH - 1

    def cur(f):
        return pl.BlockSpec((TR, D_LRU), lambda i: (f(i), 0))

    def prev(f):
        return pl.BlockSpec((prev_rows, D_LRU),
                            lambda i: (jnp.maximum(f(i) * tiles_per_prev - 1, 0), 0))

    def nxt(f):
        return pl.BlockSpec((BATCH, D_LRU),
                            lambda i: (jnp.minimum((f(i) + 1) * tiles_per_next, last_next), 0))

    fwd = lambda i: i
    const = lambda shape: pl.BlockSpec(shape, lambda i: (0,) * len(shape))
    return pl.pallas_call(
        _scan_kernel,
        grid=(N_TILES,),
        in_specs=[cur(fwd), prev(fwd), nxt(fwd), cur(_rev_tile), prev(_rev_tile), nxt(_rev_tile),
                  const((4, D_LRU)), const((1, D_LRU)),
                  const((2, 2, D_LRU // GATE_BLK, GATE_BLK, GATE_BLK)),
                  const((2, 2, D_LRU)), const((2, D_LRU))],
        out_specs=[pl.BlockSpec((TR, D_LRU), lambda i: (i, 0)),
                   pl.BlockSpec((TR, D_LRU), lambda i: (_rev_tile(i), 0))],
        out_shape=[jax.ShapeDtypeStruct((R_ALL, D_LRU), F32)] * 2,
        scratch_shapes=[pltpu.VMEM((2, BATCH, D_LRU), F32)],
        compiler_params=_cparams(("arbitrary",)),
        name="lru_scan",
    )(z, z, z, z, z, z, conv_w, conv_b, gate_w, gate_b, lam)


def _gelu_tanh(y):
    c = 0.7978845608028654
    return 0.5 * y * (1.0 + jnp.tanh(c * (y + 0.044715 * (y * y * y))))


def _pool_groups(tile, xp_ref, xpp_ref, xpn_ref):
    in_ctx = tile < CTX_TILES
    prev_ok = in_ctx & (tile > 0)
    next_ok = in_ctx & (tile < CTX_TILES - 1)
    x = xp_ref[...]
    p = jnp.concatenate([jnp.where(prev_ok, xpp_ref[...], 0.0), x,
                         jnp.where(next_ok, xpn_ref[...], 0.0)], axis=0)
    p = p.reshape(TT + 2 * POOL_HALO, BATCH, D_POOL)
    lo = jnp.where(in_ctx, -TT * tile, 0)
    hi = jnp.where(in_ctx, CTX_LEN - TT * tile, TT)
    t = lax.broadcasted_iota(jnp.int32, (TT, BATCH, POOL_GROUP_DIM), 0)
    outs = []
    for g, win in enumerate(POOL_WINDOWS):
        half = win // 2
        acc = p[:, :, g * POOL_GROUP_DIM:(g + 1) * POOL_GROUP_DIM]
        width = 1
        while width < win:
            acc = acc[:acc.shape[0] - width] + acc[width:]
            width *= 2
        start = POOL_HALO - half
        wsum = acc[start:start + TT]
        cnt = (jnp.minimum(t + half, hi) - jnp.maximum(t - half, lo)).astype(F32)
        centre = p[POOL_HALO:POOL_HALO + TT, :, g * POOL_GROUP_DIM:(g + 1) * POOL_GROUP_DIM]
        outs.append((wsum / cnt - centre).reshape(TR, POOL_GROUP_DIM))
    return outs


def _mixer_kernel(t_off, hf_ref, hr_ref, y_ref, xp_ref, xpp_ref, xpn_ref, x_ref,
                  g1_ref, sh2_ref, sc2_ref, pw_ref, pb_ref, ps_ref, wo_ref,
                  lg_ref, lb_ref, rw_ref, rb_ref, tri_ref,
                  xo_ref, h2_ref, route_ref, cnt_ref, run_ref):
    i = pl.program_id(0)
    tile = i + t_off

    @pl.when(i == 0)
    def _():
        run_ref[...] = jnp.zeros_like(run_ref)

    lru = (hf_ref[...] + hr_ref[...]) * _gelu_tanh(y_ref[...])
    diffs = _pool_groups(tile, xp_ref, xpp_ref, xpn_ref)
    pooled = jnp.concatenate(
        [jnp.dot(d.astype(BF16), pw_ref[g], preferred_element_type=F32)
         for g, d in enumerate(diffs)], axis=1)
    pooled = (pooled + pb_ref[...]) * ps_ref[...]
    mix = (jnp.dot(lru.astype(BF16), wo_ref[0:D_LRU, :], preferred_element_type=F32)
           + jnp.dot(pooled.astype(BF16), wo_ref[D_LRU:, :], preferred_element_type=F32))
    x = ALPHA * x_ref[...] + _per_batch(mix, g1_ref[0], jnp.multiply)
    x = _layer_norm(x) * lg_ref[...] + lb_ref[...]
    xo_ref[...] = x
    h2 = _per_batch(x, 1.0 + sc2_ref[0], jnp.multiply)
    h2 = _per_batch(h2, sh2_ref[0], jnp.add)
    h2_ref[...] = h2

    logits = jnp.dot(h2, rw_ref[...], precision=lax.Precision.HIGHEST,
                     preferred_element_type=F32) + rb_ref[...]
    lane = lax.broadcasted_iota(jnp.int32, (TR, LANES), 1).astype(F32)
    work = logits
    vals, idxs, sels = [], [], []
    for _ in range(TOP_K):
        m = jnp.max(work, axis=1, keepdims=True)
        idx = jnp.min(jnp.where(work == m, lane, float(LANES)), axis=1, keepdims=True)
        sel = lane == idx
        vals.append(m)
        idxs.append(idx)
        sels.append(sel)
        work = jnp.where(sel, -jnp.inf, work)
    exps = [jnp.exp(v - vals[0]) for v in vals]
    denom = exps[0] + exps[1] + exps[2] + exps[3]
    chosen = jnp.zeros((TR, LANES), F32)
    for sel in sels:
        chosen = chosen + sel.astype(F32)
    before = jnp.dot(tri_ref[...], chosen.astype(BF16), preferred_element_type=F32)
    before = before + run_ref[0:1, :]
    route = jnp.zeros((TR, LANES), F32)
    for k in range(TOP_K):
        rank = jnp.sum(jnp.where(sels[k], before, 0.0), axis=1, keepdims=True)
        route = jnp.where(lane == float(k), exps[k] / denom, route)
        route = jnp.where(lane == float(TOP_K + k), idxs[k], route)
        route = jnp.where(lane == float(2 * TOP_K + k), rank, route)
    route_ref[...] = route
    total = run_ref[0:1, :] + jnp.sum(chosen, axis=0, keepdims=True)
    run_ref[0:1, :] = total
    cnt_ref[...] = jnp.broadcast_to(total, cnt_ref.shape)


def _mixer(t_off, hf, hr, z, x, mod, pool_w_bf, pool_b, pool_scale, w_out_bf,
           ln_g, ln_b, rw_pad, rb_pad, tri):
    n = N_TILES - t_off
    halo_rows = POOL_HALO * BATCH
    per = TR // halo_rows
    last_halo = R_ALL // halo_rows - 1
    xp_col = 2 * D_LRU // D_POOL
    row = lambda w: pl.BlockSpec((TR, w), lambda i: (i + t_off, 0))
    out_row = lambda w: pl.BlockSpec((TR, w), lambda i: (i, 0))
    const = lambda shape: pl.BlockSpec(shape, lambda i: (0,) * len(shape))

    def mod_spec(chunk):
        return pl.BlockSpec((1, BATCH, D_MODEL),
                            lambda i: ((i + t_off >= CTX_TILES).astype(jnp.int32), 0, chunk))

    return pl.pallas_call(
        functools.partial(_mixer_kernel, t_off),
        grid=(n,),
        in_specs=[
            row(D_LRU), row(D_LRU),
            pl.BlockSpec((TR, D_LRU), lambda i: (i + t_off, 1)),
            pl.BlockSpec((TR, D_POOL), lambda i: (i + t_off, xp_col)),
            pl.BlockSpec((halo_rows, D_POOL),
                         lambda i: (jnp.maximum((i + t_off) * per - 1, 0), xp_col)),
            pl.BlockSpec((halo_rows, D_POOL),
                         lambda i: (jnp.minimum((i + t_off + 1) * per, last_halo), xp_col)),
            row(D_MODEL),
            mod_spec(2), mod_spec(3), mod_spec(4),
            const((len(POOL_WINDOWS), POOL_GROUP_DIM, POOL_GROUP_DIM)),
            const((1, D_POOL)), const((1, D_POOL)),
            const((D_MODEL, D_MODEL)),
            const((1, D_MODEL)), const((1, D_MODEL)),
            const((D_MODEL, LANES)), const((1, LANES)),
            const((TR, TR)),
        ],
        out_specs=[out_row(D_MODEL), out_row(D_MODEL), out_row(LANES),
                   const((SUBLANES, LANES))],
        out_shape=[jax.ShapeDtypeStruct((n * TR, D_MODEL), F32),
                   jax.ShapeDtypeStruct((n * TR, D_MODEL), F32),
                   jax.ShapeDtypeStruct((n * TR, LANES), F32),
                   jax.ShapeDtypeStruct((SUBLANES, LANES), F32)],
        scratch_shapes=[pltpu.VMEM((SUBLANES, LANES), F32)],
        compiler_params=_cparams(("arbitrary",)),
        name="mixer_out",
    )(hf, hr, z, z, z, z, x, mod, mod, mod, pool_w_bf, pool_b, pool_scale, w_out_bf,
      ln_g, ln_b, rw_pad, rb_pad, tri)


def _dispatch_kernel(pos_ref, h_ref, zeros_ref, xs_ref, sem):
    del zeros_ref

    def row_copy(r, k):
        return pltpu.make_async_copy(h_ref.at[pl.ds(r, 1)],
                                     xs_ref.at[pl.ds(pos_ref[k, r], 1)], sem)

    def start(r, carry):
        for k in range(TOP_K):
            row_copy(r, k).start()
        return carry

    def wait(r, carry):
        for k in range(TOP_K):
            row_copy(r, k).wait()
        return carry

    lax.fori_loop(0, TR, start, 0)
    lax.fori_loop(0, TR, wait, 0)


def _dispatch(pos, h2, n_sorted):
    n = h2.shape[0] // TR
    return pl.pallas_call(
        _dispatch_kernel,
        grid=(n,),
        in_specs=[
            pl.BlockSpec((TOP_K, TR), lambda i: (0, i), memory_space=pltpu.SMEM),
            pl.BlockSpec((TR, D_MODEL), lambda i: (i, 0)),
            pl.BlockSpec(memory_space=pl.ANY),
        ],
        out_specs=pl.BlockSpec(memory_space=pl.ANY),
        out_shape=jax.ShapeDtypeStruct((n_sorted, D_MODEL), F32),
        input_output_aliases={2: 0},
        scratch_shapes=[pltpu.SemaphoreType.DMA],
        compiler_params=_cparams(("arbitrary",)),
        name="moe_dispatch",
    )(pos, h2, jnp.zeros((n_sorted, D_MODEL), F32))


def _expert_kernel(te_ref, na_ref, x_ref, w1_ref, b1_ref, w2_ref, b2_ref, o_ref,
                   w1c_ref, w2c_ref):
    i = pl.program_id(0)
    e = te_ref[i]
    e_prev = te_ref[jnp.maximum(i - 1, 0)]

    @pl.when((i == 0) | (e != e_prev))
    def _():
        w1c_ref[...] = w1_ref[...].astype(BF16)
        w2c_ref[...] = w2_ref[...].astype(BF16)

    @pl.when(i < na_ref[0])
    def _():
        gu = jnp.dot(x_ref[...].astype(BF16), w1c_ref[...],
                     preferred_element_type=F32) + b1_ref[...]
        glu = jnp.minimum(gu[:, :D_EXPERT], SWIGLU_LIMIT)
        lin = jnp.clip(gu[:, D_EXPERT:], -SWIGLU_LIMIT, SWIGLU_LIMIT)
        act = glu * _sigmoid(SWIGLU_ALPHA * glu) * (lin + 1.0)
        o_ref[...] = jnp.dot(act.astype(BF16), w2c_ref[...],
                             preferred_element_type=F32) + b2_ref[...]

    @pl.when(i >= na_ref[0])
    def _():
        o_ref[...] = jnp.zeros_like(o_ref)


def _experts(tile_expert, n_active, xs, w1, b1, w2, b2):
    n_tiles = xs.shape[0] // TM
    act_tile = lambda i, te, na: (jnp.minimum(i, na[0] - 1), 0)
    grid_spec = pltpu.PrefetchScalarGridSpec(
        num_scalar_prefetch=2,
        grid=(n_tiles,),
        in_specs=[
            pl.BlockSpec((TM, D_MODEL), act_tile),
            pl.BlockSpec((None, D_MODEL, 2 * D_EXPERT), lambda i, te, na: (te[i], 0, 0)),
            pl.BlockSpec((None, 1, 2 * D_EXPERT), lambda i, te, na: (te[i], 0, 0)),
            pl.BlockSpec((None, D_EXPERT, D_MODEL), lambda i, te, na: (te[i], 0, 0)),
            pl.BlockSpec((None, 1, D_MODEL), lambda i, te, na: (te[i], 0, 0)),
        ],
        out_specs=pl.BlockSpec((TM, D_MODEL), lambda i, te, na: (i, 0)),
        scratch_shapes=[pltpu.VMEM((D_MODEL, 2 * D_EXPERT), BF16),
                        pltpu.VMEM((D_EXPERT, D_MODEL), BF16)],
    )
    return pl.pallas_call(
        _expert_kernel,
        grid_spec=grid_spec,
        out_shape=jax.ShapeDtypeStruct(xs.shape, F32),
        compiler_params=_cparams(("arbitrary",)),
        name="moe_experts",
    )(tile_expert, n_active, xs, w1, b1.reshape(N_EXPERTS, 1, -1), w2,
      b2.reshape(N_EXPERTS, 1, -1))


def _combine_kernel(pos_ref, route_ref, x_ref, g2_ref, lg_ref, lb_ref, ys_ref, o_ref,
                    ybuf_ref, sem):
    def row_copy(r, k):
        return pltpu.make_async_copy(ys_ref.at[pl.ds(pos_ref[k, r], 1)],
                                     ybuf_ref.at[k, pl.ds(r, 1)], sem)

    def start(r, carry):
        for k in range(TOP_K):
            row_copy(r, k).start()
        return carry

    def wait(r, carry):
        for k in range(TOP_K):
            row_copy(r, k).wait()
        return carry

    lax.fori_loop(0, TR, start, 0)
    lax.fori_loop(0, TR, wait, 0)
    route = route_ref[...]
    f = route[:, 0:1] * ybuf_ref[0]
    for k in range(1, TOP_K):
        f = f + route[:, k:k + 1] * ybuf_ref[k]
    x = ALPHA * x_ref[...] + _per_batch(f, g2_ref[0], jnp.multiply)
    o_ref[...] = _layer_norm(x) * lg_ref[...] + lb_ref[...]


def _combine(t_off, pos, route, x, mod, ln_g, ln_b, ys):
    n = x.shape[0] // TR
    row = lambda w: pl.BlockSpec((TR, w), lambda i: (i, 0))
    const = lambda shape: pl.BlockSpec(shape, lambda i: (0,) * len(shape))
    return pl.pallas_call(
        _combine_kernel,
        grid=(n,),
        in_specs=[
            pl.BlockSpec((TOP_K, TR), lambda i: (0, i), memory_space=pltpu.SMEM),
            row(LANES), row(D_MODEL),
            pl.BlockSpec((1, BATCH, D_MODEL),
                         lambda i: ((i + t_off >= CTX_TILES).astype(jnp.int32), 0, 5)),
            const((1, D_MODEL)), const((1, D_MODEL)),
            pl.BlockSpec(memory_space=pl.ANY),
        ],
        out_specs=row(D_MODEL),
        out_shape=jax.ShapeDtypeStruct((n * TR, D_MODEL), F32),
        scratch_shapes=[pltpu.VMEM((TOP_K, TR, D_MODEL), F32), pltpu.SemaphoreType.DMA],
        compiler_params=_cparams(("arbitrary",)),
        name="moe_combine",
    )(pos, route, x, mod, ln_g, ln_b, ys)


def _routing_plan(route, counts, n_tok):
    eid = route[:, TOP_K:2 * TOP_K].astype(jnp.int32)
    rank = route[:, 2 * TOP_K:3 * TOP_K].astype(jnp.int32)
    cnt = counts[0, :N_EXPERTS].astype(jnp.int32)
    tiles_e = (cnt + TM - 1) // TM
    tile_end = jnp.cumsum(tiles_e)
    tile_start = tile_end - tiles_e
    n_active = tile_end[-1]
    onehot = eid[:, :, None] == jnp.arange(N_EXPERTS, dtype=jnp.int32)
    start = jnp.sum(jnp.where(onehot, tile_start * TM, 0), axis=-1)
    pos = (start + rank).T
    n_tiles = TOP_K * n_tok // TM + N_EXPERTS
    j = jnp.minimum(jnp.arange(n_tiles, dtype=jnp.int32), n_active - 1)
    tile_expert = jnp.sum((j[:, None] >= tile_end[None, :]).astype(jnp.int32), axis=1)
    tile_expert = jnp.minimum(tile_expert, N_EXPERTS - 1)
    return pos, tile_expert, n_active.reshape(1), n_tiles * TM


def _grid_sincos_parts():
    quarter = D_MODEL // 4
    omega = 1.0 / (10000.0 ** (jnp.arange(quarter, dtype=F32) / quarter))

    def emb1d(n):
        ang = jnp.arange(n, dtype=F32)[:, None] * omega[None, :]
        return jnp.concatenate([jnp.sin(ang), jnp.cos(ang)], axis=-1)

    er = emb1d(SEQ // GRID_W)
    er = jnp.concatenate([jnp.zeros((CTX_TILES, D_MODEL // 2), F32), er], axis=0)
    ec = jnp.repeat(emb1d(GRID_W), BATCH, axis=0)
    return er.reshape(N_TILES, 1, D_MODEL // 2), ec


def _gate_blocks(w):
    per = GATE_BLK // LRU_HEAD_DIM
    w = w.reshape(2, D_LRU // GATE_BLK, per, LRU_HEAD_DIM, LRU_HEAD_DIM)
    eye = jnp.eye(per, dtype=w.dtype)
    blk = jnp.einsum('dkpij,pq->dkpiqj', w, eye)
    return blk.reshape(2, D_LRU // GATE_BLK, GATE_BLK, GATE_BLK)


def kernel(x, c, ctx, c_ctx, w_mod, b_mod, w_in, conv_w, conv_b, gate_a_w, gate_a_b,
           gate_x_w, gate_x_b, lru_lambda, pool_w, pool_b, pool_scale, w_out, ln1_g, ln1_b,
           router_w, router_b, exp_w1, exp_b1, exp_w2, exp_b2, ln2_g, ln2_b):
    x_tm = jnp.concatenate([ctx.transpose(1, 0, 2), x.transpose(1, 0, 2)], axis=0)
    x_tm = x_tm.reshape(R_ALL, D_MODEL)
    er, ec = _grid_sincos_parts()
    xs = _entry(x_tm, er, ec)

    cvec = jnp.concatenate([c, c_ctx[None], jnp.zeros((2 * SUBLANES - BATCH - 1, D_MODEL), F32)])
    mod_all = _modulation(cvec, w_mod, b_mod)
    tri = jnp.tril(jnp.ones((TR, TR), F32), -1).astype(BF16)

    for l in range(DEPTH):
        last = l == DEPTH - 1
        t_off = CTX_TILES if last else 0
        n_tok = R_ALL - t_off * TR
        mod = jnp.stack([jnp.broadcast_to(mod_all[l, BATCH], (BATCH, 6 * D_MODEL)),
                         mod_all[l, :BATCH]])
        z = _inproj(xs, mod, w_in[l].astype(BF16))
        gate_w = jnp.stack([_gate_blocks(gate_a_w[l]), _gate_blocks(gate_x_w[l])],
                           axis=1).astype(BF16)
        gate_b = jnp.stack([gate_a_b[l], gate_x_b[l]], axis=1)
        hf, hr = _scan(z, conv_w[l], conv_b[l][None], gate_w, gate_b, lru_lambda[l])
        rw_pad = jnp.zeros((D_MODEL, LANES), F32).at[:, :N_EXPERTS].set(router_w[l])
        rb_pad = jnp.full((1, LANES), -1e30, F32).at[0, :N_EXPERTS].set(router_b[l])
        xs, h2, route, counts = _mixer(
            t_off, hf, hr, z, xs, mod, pool_w[l].astype(BF16), pool_b[l][None],
            pool_scale[l][None], w_out[l].astype(BF16), ln1_g[l][None], ln1_b[l][None],
            rw_pad, rb_pad, tri)
        pos, tile_expert, n_active, n_sorted = _routing_plan(route, counts, n_tok)
        x_sorted = _dispatch(pos, h2, n_sorted)
        y_sorted = _experts(tile_expert, n_active, x_sorted, exp_w1[l], exp_b1[l],
                            exp_w2[l], exp_b2[l])
        xs = _combine(t_off, pos, route, xs, mod, ln2_g[l][None], ln2_b[l][None], y_sorted)

    out = xs.reshape(SEQ, BATCH, D_MODEL)
    return out.transpose(1, 0, 2)
```

```python
import functools

import jax
import jax.numpy as jnp
from jax import lax
from jax.experimental import pallas as pl
from jax.experimental.pallas import tpu as pltpu

D_MODEL = 1024
BATCH = 8
SEQ = 2048
DEPTH = 2
GRID_W = 64
CTX_LEN = 256
D_LRU = 512
N_LRU_HEADS = 8
LRU_HEAD_DIM = D_LRU // N_LRU_HEADS
LRU_C = 8.0
D_POOL = 512
POOL_WINDOWS = (2, 4, 8, 16)
POOL_GROUP_DIM = D_POOL // len(POOL_WINDOWS)
D_IN = 2 * D_LRU + D_POOL
N_EXPERTS = 32
TOP_K = 4
D_EXPERT = D_MODEL
SWIGLU_LIMIT = 7.0
SWIGLU_ALPHA = 1.702
LN_EPS = 1e-5
ALPHA = (2.0 * DEPTH) ** 0.25

F32 = jnp.float32
BF16 = jnp.bfloat16

SUBLANES = 8
LANES = 128
TT = GRID_W
TR = TT * BATCH
T_ALL = CTX_LEN + SEQ
R_ALL = T_ALL * BATCH
N_TILES = T_ALL // TT
CTX_TILES = CTX_LEN // TT
GATE_BLK = 256
POOL_HALO = max(POOL_WINDOWS) // 2
TM = 256
CHUNK = SUBLANES
N_CHUNKS = TOP_K * TR // CHUNK + N_EXPERTS
RB = N_CHUNKS * CHUNK
KB = 256
D_AUG = D_MODEL + LANES
LANE_P_HI, LANE_P_MID, LANE_P_LO, LANE_EID, LANE_RANK = 0, 4, 8, 12, 16
VMEM_LIMIT = 56 * 1024 * 1024


def _cparams(sem):
    return pltpu.CompilerParams(dimension_semantics=sem, vmem_limit_bytes=VMEM_LIMIT)


def _sigmoid(x):
    return 1.0 / (1.0 + jnp.exp(-x))


def _layer_norm(x):
    mu = jnp.mean(x, axis=-1, keepdims=True)
    xc = x - mu
    var = jnp.mean(xc * xc, axis=-1, keepdims=True)
    return xc * lax.rsqrt(var + LN_EPS)


def _per_batch(x, v, op):
    r, d = x.shape
    return op(x.reshape(r // BATCH, BATCH, d), v[None]).reshape(r, d)


def _mod_kernel(c_ref, w_ref, b_ref, o_ref):
    c = c_ref[...]
    s = c * _sigmoid(c)
    o_ref[...] = jnp.dot(s, w_ref[...], precision=lax.Precision.HIGHEST,
                         preferred_element_type=F32) + b_ref[...]


def _modulation(cvec, w_mod, b_mod):
    tn = 512
    return pl.pallas_call(
        _mod_kernel,
        grid=(DEPTH, 6 * D_MODEL // tn),
        in_specs=[
            pl.BlockSpec((2 * SUBLANES, D_MODEL), lambda l, j: (0, 0)),
            pl.BlockSpec((None, D_MODEL, tn), lambda l, j: (l, 0, j)),
            pl.BlockSpec((None, 1, tn), lambda l, j: (l, 0, j)),
        ],
        out_specs=pl.BlockSpec((None, 2 * SUBLANES, tn), lambda l, j: (l, 0, j)),
        out_shape=jax.ShapeDtypeStruct((DEPTH, 2 * SUBLANES, 6 * D_MODEL), F32),
        compiler_params=_cparams(("arbitrary", "arbitrary")),
        name="modulation",
    )(cvec, w_mod, b_mod.reshape(DEPTH, 1, 6 * D_MODEL))


def _entry_kernel(x_ref, er_ref, ec_ref, o_ref):
    i = pl.program_id(0)
    x = x_ref[...]
    lat = (i >= CTX_TILES).astype(F32)
    pos = jnp.concatenate(
        [jnp.broadcast_to(er_ref[0], (TR, D_MODEL // 2)), ec_ref[...] * lat], axis=1)
    o_ref[...] = _layer_norm(x + pos)


def _entry(x_tm, er, ec):
    return pl.pallas_call(
        _entry_kernel,
        grid=(N_TILES,),
        in_specs=[
            pl.BlockSpec((TR, D_MODEL), lambda i: (i, 0)),
            pl.BlockSpec((1, 1, D_MODEL // 2), lambda i: (i, 0, 0)),
            pl.BlockSpec((TR, D_MODEL // 2), lambda i: (0, 0)),
        ],
        out_specs=pl.BlockSpec((TR, D_MODEL), lambda i: (i, 0)),
        out_shape=jax.ShapeDtypeStruct((R_ALL, D_MODEL), F32),
        compiler_params=_cparams(("arbitrary",)),
        name="entry_ln",
    )(x_tm, er, ec)


def _inproj_kernel(x_ref, sh_ref, sc_ref, w_ref, z_ref):
    h = _per_batch(x_ref[...], 1.0 + sc_ref[0], jnp.multiply)
    h = _per_batch(h, sh_ref[0], jnp.add)
    z_ref[...] = jnp.dot(h.astype(BF16), w_ref[...], preferred_element_type=F32)


def _mod_spec(chunk):
    return pl.BlockSpec((1, BATCH, D_MODEL),
                        lambda i: ((i >= CTX_TILES).astype(jnp.int32), 0, chunk))


def _inproj(x, mod, w_in_bf):
    return pl.pallas_call(
        _inproj_kernel,
        grid=(N_TILES,),
        in_specs=[
            pl.BlockSpec((TR, D_MODEL), lambda i: (i, 0)),
            _mod_spec(0), _mod_spec(1),
            pl.BlockSpec((D_MODEL, D_IN), lambda i: (0, 0)),
        ],
        out_specs=pl.BlockSpec((TR, D_IN), lambda i: (i, 0)),
        out_shape=jax.ShapeDtypeStruct((R_ALL, D_IN), F32),
        compiler_params=_cparams(("arbitrary",)),
        name="in_proj",
    )(x, mod, mod, w_in_bf)


def _rev_tile(i):
    return jnp.where(i < CTX_TILES, CTX_TILES - 1 - i, N_TILES - 1 + CTX_TILES - i)


def _block_diag_dot(u_bf, w_ref, d, g):
    halves = [jnp.dot(u_bf[:, k * GATE_BLK:(k + 1) * GATE_BLK], w_ref[d, g, k],
                      preferred_element_type=F32) for k in range(D_LRU // GATE_BLK)]
    return jnp.concatenate(halves, axis=1)


def _lru_coeffs(tile, z_ref, zp_ref, zn_ref, cw_ref, cb_ref, gw_ref, gb_ref, lam_ref, d):
    x = z_ref[...]
    seg_first = (tile == 0) | (tile == CTX_TILES)
    seg_last = (tile == CTX_TILES - 1) | (tile == N_TILES - 1)
    prev = jnp.where(seg_first, 0.0, zp_ref[...])
    nxt = jnp.where(seg_last, 0.0, zn_ref[...])
    xm2 = jnp.concatenate([prev, x[:-2 * BATCH]], axis=0)
    xm1 = jnp.concatenate([prev[BATCH:], x[:-BATCH]], axis=0)
    xp1 = jnp.concatenate([x[BATCH:], nxt], axis=0)
    u = (cb_ref[...] + xm2 * cw_ref[0:1] + xm1 * cw_ref[1:2]
         + x * cw_ref[2:3] + xp1 * cw_ref[3:4])
    u_bf = u.astype(BF16)
    r = _sigmoid(_block_diag_dot(u_bf, gw_ref, d, 0) + gb_ref[d, 0:1])
    ig = _sigmoid(_block_diag_dot(u_bf, gw_ref, d, 1) + gb_ref[d, 1:2])
    nl = -lam_ref[d:d + 1]
    softplus = jnp.maximum(nl, 0.0) + jnp.log(1.0 + jnp.exp(-jnp.abs(nl)))
    log_a = (-LRU_C) * r * softplus
    a = jnp.exp(log_a)
    mult = jnp.sqrt(1.0 - a * a)
    return a, mult * (ig * u)


def _scan_kernel(zf_ref, zfp_ref, zfn_ref, zr_ref, zrp_ref, zrn_ref,
                 cw_ref, cb_ref, gw_ref, gb_ref, lam_ref,
                 hf_ref, hr_ref, state_ref):
    i = pl.program_id(0)

    @pl.when(i == 0)
    def _():
        state_ref[...] = jnp.zeros_like(state_ref)

    af, bf = _lru_coeffs(i, zf_ref, zfp_ref, zfn_ref, cw_ref, cb_ref, gw_ref, gb_ref,
                         lam_ref, 0)
    ar, br = _lru_coeffs(_rev_tile(i), zr_ref, zrp_ref, zrn_ref, cw_ref, cb_ref, gw_ref,
                         gb_ref, lam_ref, 1)
    hf = state_ref[0]
    hr = state_ref[1]
    for s in range(TT):
        f = slice(s * BATCH, (s + 1) * BATCH)
        hf = af[f] * hf + bf[f]
        hf_ref[f, :] = hf
        b = slice((TT - 1 - s) * BATCH, (TT - s) * BATCH)
        hr = ar[b] * hr + br[b]
        hr_ref[b, :] = hr
    state_ref[0] = hf
    state_ref[1] = hr


def _scan(z, conv_w, conv_b, gate_w, gate_b, lam):
    prev_rows = 2 * BATCH
    tiles_per_prev = TR // prev_rows
    tiles_per_next = TR // BATCH
    last_next = R_ALL // BATCH - 1

    def cur(f):
        return pl.BlockSpec((TR, D_LRU), lambda i: (f(i), 0))

    def prev(f):
        return pl.BlockSpec((prev_rows, D_LRU),
                            lambda i: (jnp.maximum(f(i) * tiles_per_prev - 1, 0), 0))

    def nxt(f):
        return pl.BlockSpec((BATCH, D_LRU),
                            lambda i: (jnp.minimum((f(i) + 1) * tiles_per_next, last_next), 0))

    fwd = lambda i: i
    const = lambda shape: pl.BlockSpec(shape, lambda i: (0,) * len(shape))
    return pl.pallas_call(
        _scan_kernel,
        grid=(N_TILES,),
        in_specs=[cur(fwd), prev(fwd), nxt(fwd), cur(_rev_tile), prev(_rev_tile), nxt(_rev_tile),
                  const((4, D_LRU)), const((1, D_LRU)),
                  const((2, 2, D_LRU // GATE_BLK, GATE_BLK, GATE_BLK)),
                  const((2, 2, D_LRU)), const((2, D_LRU))],
        out_specs=[pl.BlockSpec((TR, D_LRU), lambda i: (i, 0)),
                   pl.BlockSpec((TR, D_LRU), lambda i: (_rev_tile(i), 0))],
        out_shape=[jax.ShapeDtypeStruct((R_ALL, D_LRU), F32)] * 2,
        scratch_shapes=[pltpu.VMEM((2, BATCH, D_LRU), F32)],
        compiler_params=_cparams(("arbitrary",)),
        name="lru_scan",
    )(z, z, z, z, z, z, conv_w, conv_b, gate_w, gate_b, lam)


def _gelu_tanh(y):
    c = 0.7978845608028654
    return 0.5 * y * (1.0 + jnp.tanh(c * (y + 0.044715 * (y * y * y))))


def _pool_groups(tile, xp_ref, xpp_ref, xpn_ref):
    in_ctx = tile < CTX_TILES
    prev_ok = in_ctx & (tile > 0)
    next_ok = in_ctx & (tile < CTX_TILES - 1)
    x = xp_ref[...]
    p = jnp.concatenate([jnp.where(prev_ok, xpp_ref[...], 0.0), x,
                         jnp.where(next_ok, xpn_ref[...], 0.0)], axis=0)
    p = p.reshape(TT + 2 * POOL_HALO, BATCH, D_POOL)
    lo = jnp.where(in_ctx, -TT * tile, 0)
    hi = jnp.where(in_ctx, CTX_LEN - TT * tile, TT)
    t = lax.broadcasted_iota(jnp.int32, (TT, BATCH, POOL_GROUP_DIM), 0)
    outs = []
    for g, win in enumerate(POOL_WINDOWS):
        half = win // 2
        acc = p[:, :, g * POOL_GROUP_DIM:(g + 1) * POOL_GROUP_DIM]
        width = 1
        while width < win:
            acc = acc[:acc.shape[0] - width] + acc[width:]
            width *= 2
        start = POOL_HALO - half
        wsum = acc[start:start + TT]
        cnt = (jnp.minimum(t + half, hi) - jnp.maximum(t - half, lo)).astype(F32)
        centre = p[POOL_HALO:POOL_HALO + TT, :, g * POOL_GROUP_DIM:(g + 1) * POOL_GROUP_DIM]
        outs.append((wsum / cnt - centre).reshape(TR, POOL_GROUP_DIM))
    return outs


def _mixer_kernel(t_off, hf_ref, hr_ref, y_ref, xp_ref, xpp_ref, xpn_ref, x_ref,
                  g1_ref, sh2_ref, sc2_ref, pw_ref, pb_ref, ps_ref, wo_ref,
                  lg_ref, lb_ref, rw_ref, rb_ref, tri_ref,
                  xo_ref, h2_ref, route_ref, cnt_ref):
    i = pl.program_id(0)
    tile = i + t_off

    lru = (hf_ref[...] + hr_ref[...]) * _gelu_tanh(y_ref[...])
    diffs = _pool_groups(tile, xp_ref, xpp_ref, xpn_ref)
    pooled = jnp.concatenate(
        [jnp.dot(d.astype(BF16), pw_ref[g], preferred_element_type=F32)
         for g, d in enumerate(diffs)], axis=1)
    pooled = (pooled + pb_ref[...]) * ps_ref[...]
    mix = (jnp.dot(lru.astype(BF16), wo_ref[0:D_LRU, :], preferred_element_type=F32)
           + jnp.dot(pooled.astype(BF16), wo_ref[D_LRU:, :], preferred_element_type=F32))
    x = ALPHA * x_ref[...] + _per_batch(mix, g1_ref[0], jnp.multiply)
    x = _layer_norm(x) * lg_ref[...] + lb_ref[...]
    xo_ref[...] = x
    h2 = _per_batch(x, 1.0 + sc2_ref[0], jnp.multiply)
    h2 = _per_batch(h2, sh2_ref[0], jnp.add)
    h2_ref[...] = h2

    logits = jnp.dot(h2, rw_ref[...], precision=lax.Precision.HIGHEST,
                     preferred_element_type=F32) + rb_ref[...]
    lane = lax.broadcasted_iota(jnp.int32, (TR, LANES), 1).astype(F32)
    work = logits
    vals, idxs, sels = [], [], []
    for _ in range(TOP_K):
        m = jnp.max(work, axis=1, keepdims=True)
        idx = jnp.min(jnp.where(work == m, lane, float(LANES)), axis=1, keepdims=True)
        sel = lane == idx
        vals.append(m)
        idxs.append(idx)
        sels.append(sel)
        work = jnp.where(sel, -jnp.inf, work)
    exps = [jnp.exp(v - vals[0]) for v in vals]
    denom = exps[0] + exps[1] + exps[2] + exps[3]
    chosen = jnp.zeros((TR, LANES), F32)
    for sel in sels:
        chosen = chosen + sel.astype(F32)
    before = jnp.dot(tri_ref[...], chosen.astype(BF16), preferred_element_type=F32)
    route = jnp.zeros((TR, LANES), F32)
    for k in range(TOP_K):
        rank = jnp.sum(jnp.where(sels[k], before, 0.0), axis=1, keepdims=True)
        p = exps[k] / denom
        p_hi = p.astype(BF16).astype(F32)
        p_mid = (p - p_hi).astype(BF16).astype(F32)
        p_lo = p - p_hi - p_mid
        for base, val in ((LANE_P_HI, p_hi), (LANE_P_MID, p_mid), (LANE_P_LO, p_lo),
                          (LANE_EID, idxs[k]), (LANE_RANK, rank)):
            route = jnp.where(lane == float(base + k), val, route)
    route_ref[...] = route
    cnt_ref[0] = jnp.broadcast_to(jnp.sum(chosen, axis=0, keepdims=True), (SUBLANES, LANES))


def _mixer(t_off, hf, hr, z, x, mod, pool_w_bf, pool_b, pool_scale, w_out_bf,
           ln_g, ln_b, rw_pad, rb_pad, tri):
    n = N_TILES - t_off
    halo_rows = POOL_HALO * BATCH
    per = TR // halo_rows
    last_halo = R_ALL // halo_rows - 1
    xp_col = 2 * D_LRU // D_POOL
    row = lambda w: pl.BlockSpec((TR, w), lambda i: (i + t_off, 0))
    out_row = lambda w: pl.BlockSpec((TR, w), lambda i: (i, 0))
    const = lambda shape: pl.BlockSpec(shape, lambda i: (0,) * len(shape))

    def mod_spec(chunk):
        return pl.BlockSpec((1, BATCH, D_MODEL),
                            lambda i: ((i + t_off >= CTX_TILES).astype(jnp.int32), 0, chunk))

    return pl.pallas_call(
        functools.partial(_mixer_kernel, t_off),
        grid=(n,),
        in_specs=[
            row(D_LRU), row(D_LRU),
            pl.BlockSpec((TR, D_LRU), lambda i: (i + t_off, 1)),
            pl.BlockSpec((TR, D_POOL), lambda i: (i + t_off, xp_col)),
            pl.BlockSpec((halo_rows, D_POOL),
                         lambda i: (jnp.maximum((i + t_off) * per - 1, 0), xp_col)),
            pl.BlockSpec((halo_rows, D_POOL),
                         lambda i: (jnp.minimum((i + t_off + 1) * per, last_halo), xp_col)),
            row(D_MODEL),
            mod_spec(2), mod_spec(3), mod_spec(4),
            const((len(POOL_WINDOWS), POOL_GROUP_DIM, POOL_GROUP_DIM)),
            const((1, D_POOL)), const((1, D_POOL)),
            const((D_MODEL, D_MODEL)),
            const((1, D_MODEL)), const((1, D_MODEL)),
            const((D_MODEL, LANES)), const((1, LANES)),
            const((TR, TR)),
        ],
        out_specs=[out_row(D_MODEL), out_row(D_MODEL), out_row(LANES),
                   pl.BlockSpec((1, SUBLANES, LANES), lambda i: (i, 0, 0))],
        out_shape=[jax.ShapeDtypeStruct((n * TR, D_MODEL), F32),
                   jax.ShapeDtypeStruct((n * TR, D_MODEL), F32),
                   jax.ShapeDtypeStruct((n * TR, LANES), F32),
                   jax.ShapeDtypeStruct((n, SUBLANES, LANES), F32)],
        compiler_params=_cparams(("arbitrary",)),
        name="mixer_out",
    )(hf, hr, z, z, z, z, x, mod, mod, mod, pool_w_bf, pool_b, pool_scale, w_out_bf,
      ln_g, ln_b, rw_pad, rb_pad, tri)


def _one_hot_rows(targets, index):
    hit = jnp.zeros(index.shape, F32)
    for t in reversed(targets):
        hit = jnp.where(index == t, 1.0, hit)
    return hit.astype(BF16)


def _chunk_copies(n, make_copy):
    def start(j, carry):
        make_copy(j).start()
        return carry

    def wait(j, carry):
        make_copy(j).wait()
        return carry

    lax.fori_loop(0, n, start, 0)
    lax.fori_loop(0, n, wait, 0)


def _rows(chunk):
    return pl.ds(pl.multiple_of(chunk * CHUNK, CHUNK), CHUNK)


def _dispatch_kernel(dst_ref, nch_ref, tail_s_ref, tail_n_ref, lpos_ref, h_ref, r_ref,
                     xs_ref, stage_ref, sem):
    i = pl.program_id(0)
    n_blocks = pl.num_programs(0) - 1

    @pl.when(i < n_blocks)
    def _():
        n = nch_ref[i]
        lpos = lpos_ref[...]
        targets = [lpos[k:k + 1, :] for k in range(TOP_K)]
        h = h_ref[...].astype(BF16)
        r = r_ref[...].astype(BF16)

        def permute(jb, carry):
            base = pl.multiple_of(jb * KB, KB)
            rows = lax.broadcasted_iota(jnp.int32, (KB, TR), 0) + base
            sel = _one_hot_rows(targets, rows)
            stage_ref[pl.ds(base, KB), 0:D_MODEL] = jnp.dot(
                sel, h, preferred_element_type=F32)
            stage_ref[pl.ds(base, KB), D_MODEL:D_AUG] = jnp.dot(
                sel, r, preferred_element_type=F32)
            return carry

        lax.fori_loop(0, (n * CHUNK + KB - 1) // KB, permute, 0)
        _chunk_copies(n, lambda j: pltpu.make_async_copy(
            stage_ref.at[_rows(j)], xs_ref.at[_rows(dst_ref[i, j])], sem))

    @pl.when(i == n_blocks)
    def _():
        stage_ref[0:CHUNK, :] = jnp.zeros((CHUNK, D_AUG), F32)

        def fill(e, carry):
            _chunk_copies(tail_n_ref[e], lambda j: pltpu.make_async_copy(
                stage_ref.at[0:CHUNK], xs_ref.at[_rows(tail_s_ref[e] + j)], sem))
            return carry

        lax.fori_loop(0, N_EXPERTS + 1, fill, 0)


def _dispatch(plan, h2, route):
    n = h2.shape[0] // TR
    last = n - 1
    grid_spec = pltpu.PrefetchScalarGridSpec(
        num_scalar_prefetch=4,
        grid=(n + 1,),
        in_specs=[
            pl.BlockSpec((None, SUBLANES, TR), lambda i, *_: (jnp.minimum(i, last), 0, 0)),
            pl.BlockSpec((TR, D_MODEL), lambda i, *_: (jnp.minimum(i, last), 0)),
            pl.BlockSpec((TR, LANES), lambda i, *_: (jnp.minimum(i, last), 0)),
        ],
        out_specs=pl.BlockSpec(memory_space=pl.ANY),
        scratch_shapes=[pltpu.VMEM((RB, D_AUG), F32), pltpu.SemaphoreType.DMA],
    )
    return pl.pallas_call(
        _dispatch_kernel,
        grid_spec=grid_spec,
        out_shape=jax.ShapeDtypeStruct((plan["n_sorted"], D_AUG), F32),
        compiler_params=_cparams(("arbitrary",)),
        name="moe_dispatch",
    )(plan["dst_chunk"], plan["n_chunks"], plan["tail_start"], plan["tail_len"],
      plan["lpos_t"], h2, route)


def _expert_kernel(te_ref, na_ref, x_ref, w1_ref, b1_ref, w2_ref, b2_ref, o_ref,
                   w1c_ref, w2c_ref):
    i = pl.program_id(0)
    e = te_ref[i]
    e_prev = te_ref[jnp.maximum(i - 1, 0)]

    @pl.when((i == 0) | (e != e_prev))
    def _():
        w1c_ref[...] = w1_ref[...].astype(BF16)
        w2c_ref[...] = w2_ref[...].astype(BF16)

    @pl.when(i < na_ref[0])
    def _():
        aug = x_ref[:, D_MODEL:D_AUG]
        e_f = e.astype(F32)
        gate = jnp.zeros((TM, 1), F32)
        for k in range(TOP_K):
            p = (aug[:, LANE_P_HI + k:LANE_P_HI + k + 1]
                 + aug[:, LANE_P_MID + k:LANE_P_MID + k + 1]
                 + aug[:, LANE_P_LO + k:LANE_P_LO + k + 1])
            gate = gate + jnp.where(aug[:, LANE_EID + k:LANE_EID + k + 1] == e_f, p, 0.0)
        gu = jnp.dot(x_ref[:, 0:D_MODEL].astype(BF16), w1c_ref[...],
                     preferred_element_type=F32) + b1_ref[...]
        glu = jnp.minimum(gu[:, :D_EXPERT], SWIGLU_LIMIT)
        lin = jnp.clip(gu[:, D_EXPERT:], -SWIGLU_LIMIT, SWIGLU_LIMIT)
        act = glu * _sigmoid(SWIGLU_ALPHA * glu) * (lin + 1.0)
        y = jnp.dot(act.astype(BF16), w2c_ref[...], preferred_element_type=F32) + b2_ref[...]
        o_ref[...] = gate * y

    @pl.when(i >= na_ref[0])
    def _():
        o_ref[...] = jnp.zeros_like(o_ref)


def _experts(layer, plan, xs, w1, b1, w2, b2):
    n_tiles = xs.shape[0] // TM
    act_tile = lambda i, te, na: (jnp.minimum(i, na[0] - 1), 0)
    expert = lambda i, te, na: (layer, te[i], 0, 0)
    grid_spec = pltpu.PrefetchScalarGridSpec(
        num_scalar_prefetch=2,
        grid=(n_tiles,),
        in_specs=[
            pl.BlockSpec((TM, D_AUG), act_tile),
            pl.BlockSpec((None, None, D_MODEL, 2 * D_EXPERT), expert),
            pl.BlockSpec((None, None, 1, 2 * D_EXPERT), expert),
            pl.BlockSpec((None, None, D_EXPERT, D_MODEL), expert),
            pl.BlockSpec((None, None, 1, D_MODEL), expert),
        ],
        out_specs=pl.BlockSpec((TM, D_MODEL), lambda i, te, na: (i, 0)),
        scratch_shapes=[pltpu.VMEM((D_MODEL, 2 * D_EXPERT), BF16),
                        pltpu.VMEM((D_EXPERT, D_MODEL), BF16)],
    )
    return pl.pallas_call(
        _expert_kernel,
        grid_spec=grid_spec,
        out_shape=jax.ShapeDtypeStruct((xs.shape[0], D_MODEL), F32),
        compiler_params=_cparams(("arbitrary",)),
        name="moe_experts",
    )(plan["tile_expert"], plan["n_active"], xs, w1,
      b1.reshape(DEPTH, N_EXPERTS, 1, -1), w2, b2.reshape(DEPTH, N_EXPERTS, 1, -1))


def _combine_kernel(dst_ref, nch_ref, lpos_ref, x_ref, g2_ref, lg_ref, lb_ref, ys_ref, o_ref,
                    ybuf_ref, sem):
    i = pl.program_id(0)

    @pl.when(i == 0)
    def _():
        ybuf_ref[...] = jnp.zeros_like(ybuf_ref)

    n = nch_ref[i]
    _chunk_copies(n, lambda j: pltpu.make_async_copy(
        ys_ref.at[_rows(dst_ref[i, j])], ybuf_ref.at[_rows(j)], sem))
    lpos = lpos_ref[...]
    targets = [jnp.broadcast_to(lpos[:, k:k + 1], (TR, KB)) for k in range(TOP_K)]
    cols = lax.broadcasted_iota(jnp.int32, (TR, KB), 1)
    f = jnp.zeros((TR, D_MODEL), F32)
    for jb in range(RB // KB):
        sel = _one_hot_rows(targets, cols + jb * KB)
        f = f + jnp.dot(sel, ybuf_ref[jb * KB:(jb + 1) * KB, :].astype(BF16),
                        preferred_element_type=F32)
    x = ALPHA * x_ref[...] + _per_batch(f, g2_ref[0], jnp.multiply)
    o_ref[...] = _layer_norm(x) * lg_ref[...] + lb_ref[...]


def _combine(t_off, plan, x, mod, ln_g, ln_b, ys):
    n = x.shape[0] // TR
    row = lambda w: pl.BlockSpec((TR, w), lambda i, *_: (i, 0))
    const = lambda shape: pl.BlockSpec(shape, lambda i, *_: (0,) * len(shape))
    grid_spec = pltpu.PrefetchScalarGridSpec(
        num_scalar_prefetch=2,
        grid=(n,),
        in_specs=[
            row(TOP_K), row(D_MODEL),
            pl.BlockSpec((1, BATCH, D_MODEL),
                         lambda i, *_: ((i + t_off >= CTX_TILES).astype(jnp.int32), 0, 5)),
            const((1, D_MODEL)), const((1, D_MODEL)),
            pl.BlockSpec(memory_space=pl.ANY),
        ],
        out_specs=row(D_MODEL),
        scratch_shapes=[pltpu.VMEM((RB, D_MODEL), F32), pltpu.SemaphoreType.DMA],
    )
    return pl.pallas_call(
        _combine_kernel,
        grid_spec=grid_spec,
        out_shape=jax.ShapeDtypeStruct((n * TR, D_MODEL), F32),
        compiler_params=_cparams(("arbitrary",)),
        name="moe_combine",
    )(plan["dst_chunk"], plan["n_chunks"], plan["lpos"], x, mod, ln_g, ln_b, ys)


def _routing_plan(route, counts):
    i32 = jnp.int32
    nb = counts.shape[0]
    experts = jnp.arange(N_EXPERTS, dtype=i32)
    eid = route[:, LANE_EID:LANE_EID + TOP_K].astype(i32).reshape(nb, TR, TOP_K)
    rank = route[:, LANE_RANK:LANE_RANK + TOP_K].astype(i32).reshape(nb, TR, TOP_K)
    n = counts[:, 0, :N_EXPERTS].astype(i32)
    m = (n + CHUNK - 1) // CHUNK * CHUNK
    m_end = jnp.cumsum(m, axis=1)
    l_off = m_end - m
    seg = jnp.sum(m, axis=0)
    tiles_e = (seg + TM - 1) // TM
    tile_end = jnp.cumsum(tiles_e)
    e_start = (tile_end - tiles_e) * TM
    n_active = tile_end[-1]
    dst_row = e_start[None, :] + jnp.cumsum(m, axis=0) - m

    onehot = eid[..., None] == experts
    lpos = jnp.sum(jnp.where(onehot, l_off[:, None, None, :], 0), axis=-1) + rank
    lpos_t = jnp.concatenate(
        [lpos.transpose(0, 2, 1), jnp.full((nb, SUBLANES - TOP_K, TR), -1, i32)], axis=1)

    j = jnp.arange(N_CHUNKS, dtype=i32)
    e_of = jnp.sum((j[None, :, None] >= (m_end // CHUNK)[:, None, :]).astype(i32), axis=-1)
    e_of = jnp.minimum(e_of, N_EXPERTS - 1)
    shift = (dst_row - l_off) // CHUNK
    dst_chunk = jnp.sum(jnp.where(e_of[..., None] == experts, shift[:, None, :], 0),
                        axis=-1) + j[None, :]

    rows_max = TOP_K * nb * TR + nb * N_EXPERTS * (CHUNK - 1)
    n_tiles = (rows_max + TM - 1) // TM + N_EXPERTS
    n_sorted = n_tiles * TM
    tail_start = jnp.concatenate([e_start + seg, (n_active * TM)[None]]) // CHUNK
    tail_len = jnp.concatenate([tile_end * TM - e_start - seg,
                                (n_sorted - n_active * TM)[None]]) // CHUNK
    t = jnp.minimum(jnp.arange(n_tiles, dtype=i32), n_active - 1)
    tile_expert = jnp.sum((t[:, None] >= tile_end[None, :]).astype(i32), axis=1)
    tile_expert = jnp.minimum(tile_expert, N_EXPERTS - 1)
    return dict(lpos=lpos.reshape(nb * TR, TOP_K), lpos_t=lpos_t, dst_chunk=dst_chunk,
                n_chunks=m_end[:, -1] // CHUNK, tail_start=tail_start, tail_len=tail_len,
                tile_expert=tile_expert, n_active=n_active.reshape(1), n_sorted=n_sorted)


def _grid_sincos_parts():
    quarter = D_MODEL // 4
    omega = 1.0 / (10000.0 ** (jnp.arange(quarter, dtype=F32) / quarter))

    def emb1d(n):
        ang = jnp.arange(n, dtype=F32)[:, None] * omega[None, :]
        return jnp.concatenate([jnp.sin(ang), jnp.cos(ang)], axis=-1)

    er = emb1d(SEQ // GRID_W)
    er = jnp.concatenate([jnp.zeros((CTX_TILES, D_MODEL // 2), F32), er], axis=0)
    ec = jnp.repeat(emb1d(GRID_W), BATCH, axis=0)
    return er.reshape(N_TILES, 1, D_MODEL // 2), ec


def _gate_blocks(w):
    per = GATE_BLK // LRU_HEAD_DIM
    w = w.reshape(2, D_LRU // GATE_BLK, per, LRU_HEAD_DIM, LRU_HEAD_DIM)
    eye = jnp.eye(per, dtype=w.dtype)
    blk = jnp.einsum('dkpij,pq->dkpiqj', w, eye)
    return blk.reshape(2, D_LRU // GATE_BLK, GATE_BLK, GATE_BLK)


def kernel(x, c, ctx, c_ctx, w_mod, b_mod, w_in, conv_w, conv_b, gate_a_w, gate_a_b,
           gate_x_w, gate_x_b, lru_lambda, pool_w, pool_b, pool_scale, w_out, ln1_g, ln1_b,
           router_w, router_b, exp_w1, exp_b1, exp_w2, exp_b2, ln2_g, ln2_b):
    x_tm = jnp.concatenate([ctx.transpose(1, 0, 2), x.transpose(1, 0, 2)], axis=0)
    x_tm = x_tm.reshape(R_ALL, D_MODEL)
    er, ec = _grid_sincos_parts()
    xs = _entry(x_tm, er, ec)

    cvec = jnp.concatenate([c, c_ctx[None], jnp.zeros((2 * SUBLANES - BATCH - 1, D_MODEL), F32)])
    mod_all = _modulation(cvec, w_mod, b_mod)
    tri = jnp.tril(jnp.ones((TR, TR), F32), -1).astype(BF16)

    for l in range(DEPTH):
        last = l == DEPTH - 1
        t_off = CTX_TILES if last else 0
        mod = jnp.stack([jnp.broadcast_to(mod_all[l, BATCH], (BATCH, 6 * D_MODEL)),
                         mod_all[l, :BATCH]])
        z = _inproj(xs, mod, w_in[l].astype(BF16))
        gate_w = jnp.stack([_gate_blocks(gate_a_w[l]), _gate_blocks(gate_x_w[l])],
                           axis=1).astype(BF16)
        gate_b = jnp.stack([gate_a_b[l], gate_x_b[l]], axis=1)
        hf, hr = _scan(z, conv_w[l], conv_b[l][None], gate_w, gate_b, lru_lambda[l])
        rw_pad = jnp.zeros((D_MODEL, LANES), F32).at[:, :N_EXPERTS].set(router_w[l])
        rb_pad = jnp.full((1, LANES), -1e30, F32).at[0, :N_EXPERTS].set(router_b[l])
        xs, h2, route, counts = _mixer(
            t_off, hf, hr, z, xs, mod, pool_w[l].astype(BF16), pool_b[l][None],
            pool_scale[l][None], w_out[l].astype(BF16), ln1_g[l][None], ln1_b[l][None],
            rw_pad, rb_pad, tri)
        plan = _routing_plan(route, counts)
        x_sorted = _dispatch(plan, h2, route)
        y_sorted = _experts(l, plan, x_sorted, exp_w1, exp_b1, exp_w2, exp_b2)
        xs = _combine(t_off, plan, xs, mod, ln2_g[l][None], ln2_b[l][None], y_sorted)

    out = xs.reshape(SEQ, BATCH, D_MODEL)
    return out.transpose(1, 0, 2)
```

```python
import functools

import jax
import jax.numpy as jnp
from jax import lax
from jax.experimental import pallas as pl
from jax.experimental.pallas import tpu as pltpu

D_MODEL = 1024
BATCH = 8
SEQ = 2048
DEPTH = 2
GRID_W = 64
CTX_LEN = 256
D_LRU = 512
N_LRU_HEADS = 8
LRU_HEAD_DIM = D_LRU // N_LRU_HEADS
LRU_C = 8.0
D_POOL = 512
POOL_WINDOWS = (2, 4, 8, 16)
POOL_GROUP_DIM = D_POOL // len(POOL_WINDOWS)
D_IN = 2 * D_LRU + D_POOL
N_EXPERTS = 32
TOP_K = 4
D_EXPERT = D_MODEL
SWIGLU_LIMIT = 7.0
SWIGLU_ALPHA = 1.702
LN_EPS = 1e-5
ALPHA = (2.0 * DEPTH) ** 0.25

F32 = jnp.float32
BF16 = jnp.bfloat16

SUBLANES = 8
LANES = 128
TT = GRID_W
TR = TT * BATCH
T_ALL = CTX_LEN + SEQ
R_ALL = T_ALL * BATCH
N_TILES = T_ALL // TT
CTX_TILES = CTX_LEN // TT
GATE_BLK = 256
POOL_HALO = max(POOL_WINDOWS) // 2
TM = 256
GROUP = SUBLANES
RB = TOP_K * TR + N_EXPERTS * GROUP
KB = 256
D_AUG = D_MODEL + LANES
LANE_P_HI, LANE_P_MID, LANE_P_LO, LANE_EID, LANE_RANK = 0, 4, 8, 12, 16
VMEM_LIMIT = 56 * 1024 * 1024


def _cparams(sem):
    return pltpu.CompilerParams(dimension_semantics=sem, vmem_limit_bytes=VMEM_LIMIT)


def _sigmoid(x):
    return 1.0 / (1.0 + jnp.exp(-x))


def _layer_norm(x):
    mu = jnp.mean(x, axis=-1, keepdims=True)
    xc = x - mu
    var = jnp.mean(xc * xc, axis=-1, keepdims=True)
    return xc * lax.rsqrt(var + LN_EPS)


def _per_batch(x, v, op):
    r, d = x.shape
    return op(x.reshape(r // BATCH, BATCH, d), v[None]).reshape(r, d)


def _mod_kernel(c_ref, w_ref, b_ref, o_ref):
    c = c_ref[...]
    s = c * _sigmoid(c)
    o_ref[...] = jnp.dot(s, w_ref[...], precision=lax.Precision.HIGHEST,
                         preferred_element_type=F32) + b_ref[...]


def _modulation(cvec, w_mod, b_mod):
    tn = 512
    return pl.pallas_call(
        _mod_kernel,
        grid=(DEPTH, 6 * D_MODEL // tn),
        in_specs=[
            pl.BlockSpec((2 * SUBLANES, D_MODEL), lambda l, j: (0, 0)),
            pl.BlockSpec((None, D_MODEL, tn), lambda l, j: (l, 0, j)),
            pl.BlockSpec((None, 1, tn), lambda l, j: (l, 0, j)),
        ],
        out_specs=pl.BlockSpec((None, 2 * SUBLANES, tn), lambda l, j: (l, 0, j)),
        out_shape=jax.ShapeDtypeStruct((DEPTH, 2 * SUBLANES, 6 * D_MODEL), F32),
        compiler_params=_cparams(("arbitrary", "arbitrary")),
        name="modulation",
    )(cvec, w_mod, b_mod.reshape(DEPTH, 1, 6 * D_MODEL))


def _entry_kernel(x_ref, er_ref, ec_ref, o_ref):
    i = pl.program_id(0)
    x = x_ref[...]
    lat = (i >= CTX_TILES).astype(F32)
    pos = jnp.concatenate(
        [jnp.broadcast_to(er_ref[0], (TR, D_MODEL // 2)), ec_ref[...] * lat], axis=1)
    o_ref[...] = _layer_norm(x + pos)


def _entry(x_tm, er, ec):
    return pl.pallas_call(
        _entry_kernel,
        grid=(N_TILES,),
        in_specs=[
            pl.BlockSpec((TR, D_MODEL), lambda i: (i, 0)),
            pl.BlockSpec((1, 1, D_MODEL // 2), lambda i: (i, 0, 0)),
            pl.BlockSpec((TR, D_MODEL // 2), lambda i: (0, 0)),
        ],
        out_specs=pl.BlockSpec((TR, D_MODEL), lambda i: (i, 0)),
        out_shape=jax.ShapeDtypeStruct((R_ALL, D_MODEL), F32),
        compiler_params=_cparams(("arbitrary",)),
        name="entry_ln",
    )(x_tm, er, ec)


def _inproj_kernel(x_ref, sh_ref, sc_ref, w_ref, z_ref):
    h = _per_batch(x_ref[...], 1.0 + sc_ref[0], jnp.multiply)
    h = _per_batch(h, sh_ref[0], jnp.add)
    z_ref[...] = jnp.dot(h.astype(BF16), w_ref[...], preferred_element_type=F32)


def _mod_spec(chunk):
    return pl.BlockSpec((1, BATCH, D_MODEL),
                        lambda i: ((i >= CTX_TILES).astype(jnp.int32), 0, chunk))


def _inproj(x, mod, w_in_bf):
    return pl.pallas_call(
        _inproj_kernel,
        grid=(N_TILES,),
        in_specs=[
            pl.BlockSpec((TR, D_MODEL), lambda i: (i, 0)),
            _mod_spec(0), _mod_spec(1),
            pl.BlockSpec((D_MODEL, D_IN), lambda i: (0, 0)),
        ],
        out_specs=pl.BlockSpec((TR, D_IN), lambda i: (i, 0)),
        out_shape=jax.ShapeDtypeStruct((R_ALL, D_IN), F32),
        compiler_params=_cparams(("arbitrary",)),
        name="in_proj",
    )(x, mod, mod, w_in_bf)


def _rev_tile(i):
    return jnp.where(i < CTX_TILES, CTX_TILES - 1 - i, N_TILES - 1 + CTX_TILES - i)


def _block_diag_dot(u_bf, w_ref, d, g):
    halves = [jnp.dot(u_bf[:, k * GATE_BLK:(k + 1) * GATE_BLK], w_ref[d, g, k],
                      preferred_element_type=F32) for k in range(D_LRU // GATE_BLK)]
    return jnp.concatenate(halves, axis=1)


def _lru_coeffs(tile, z_ref, zp_ref, zn_ref, cw_ref, cb_ref, gw_ref, gb_ref, lam_ref, d):
    x = z_ref[...]
    seg_first = (tile == 0) | (tile == CTX_TILES)
    seg_last = (tile == CTX_TILES - 1) | (tile == N_TILES - 1)
    prev = jnp.where(seg_first, 0.0, zp_ref[...])
    nxt = jnp.where(seg_last, 0.0, zn_ref[...])
    xm2 = jnp.concatenate([prev, x[:-2 * BATCH]], axis=0)
    xm1 = jnp.concatenate([prev[BATCH:], x[:-BATCH]], axis=0)
    xp1 = jnp.concatenate([x[BATCH:], nxt], axis=0)
    u = (cb_ref[...] + xm2 * cw_ref[0:1] + xm1 * cw_ref[1:2]
         + x * cw_ref[2:3] + xp1 * cw_ref[3:4])
    u_bf = u.astype(BF16)
    r = _sigmoid(_block_diag_dot(u_bf, gw_ref, d, 0) + gb_ref[d, 0:1])
    ig = _sigmoid(_block_diag_dot(u_bf, gw_ref, d, 1) + gb_ref[d, 1:2])
    nl = -lam_ref[d:d + 1]
    softplus = jnp.maximum(nl, 0.0) + jnp.log(1.0 + jnp.exp(-jnp.abs(nl)))
    log_a = (-LRU_C) * r * softplus
    a = jnp.exp(log_a)
    mult = jnp.sqrt(1.0 - a * a)
    return a, mult * (ig * u)


def _scan_kernel(zf_ref, zfp_ref, zfn_ref, zr_ref, zrp_ref, zrn_ref,
                 cw_ref, cb_ref, gw_ref, gb_ref, lam_ref,
                 hf_ref, hr_ref, state_ref):
    i = pl.program_id(0)

    @pl.when(i == 0)
    def _():
        state_ref[...] = jnp.zeros_like(state_ref)

    af, bf = _lru_coeffs(i, zf_ref, zfp_ref, zfn_ref, cw_ref, cb_ref, gw_ref, gb_ref,
                         lam_ref, 0)
    ar, br = _lru_coeffs(_rev_tile(i), zr_ref, zrp_ref, zrn_ref, cw_ref, cb_ref, gw_ref,
                         gb_ref, lam_ref, 1)
    hf = state_ref[0]
    hr = state_ref[1]
    for s in range(TT):
        f = slice(s * BATCH, (s + 1) * BATCH)
        hf = af[f] * hf + bf[f]
        hf_ref[f, :] = hf
        b = slice((TT - 1 - s) * BATCH, (TT - s) * BATCH)
        hr = ar[b] * hr + br[b]
        hr_ref[b, :] = hr
    state_ref[0] = hf
    state_ref[1] = hr


def _scan(z, conv_w, conv_b, gate_w, gate_b, lam):
    prev_rows = 2 * BATCH
    tiles_per_prev = TR // prev_rows
    tiles_per_next = TR // BATCH
    last_next = R_ALL // BATCH - 1

    def cur(f):
        return pl.BlockSpec((TR, D_LRU), lambda i: (f(i), 0))

    def prev(f):
        return pl.BlockSpec((prev_rows, D_LRU),
                            lambda i: (jnp.maximum(f(i) * tiles_per_prev - 1, 0), 0))

    def nxt(f):
        return pl.BlockSpec((BATCH, D_LRU),
                            lambda i: (jnp.minimum((f(i) + 1) * tiles_per_next, last_next), 0))

    fwd = lambda i: i
    const = lambda shape: pl.BlockSpec(shape, lambda i: (0,) * len(shape))
    return pl.pallas_call(
        _scan_kernel,
        grid=(N_TILES,),
        in_specs=[cur(fwd), prev(fwd), nxt(fwd), cur(_rev_tile), prev(_rev_tile), nxt(_rev_tile),
                  const((4, D_LRU)), const((1, D_LRU)),
                  const((2, 2, D_LRU // GATE_BLK, GATE_BLK, GATE_BLK)),
                  const((2, 2, D_LRU)), const((2, D_LRU))],
        out_specs=[pl.BlockSpec((TR, D_LRU), lambda i: (i, 0)),
                   pl.BlockSpec((TR, D_LRU), lambda i: (_rev_tile(i), 0))],
        out_shape=[jax.ShapeDtypeStruct((R_ALL, D_LRU), F32)] * 2,
        scratch_shapes=[pltpu.VMEM((2, BATCH, D_LRU), F32)],
        compiler_params=_cparams(("arbitrary",)),
        name="lru_scan",
    )(z, z, z, z, z, z, conv_w, conv_b, gate_w, gate_b, lam)


def _gelu_tanh(y):
    c = 0.7978845608028654
    return 0.5 * y * (1.0 + jnp.tanh(c * (y + 0.044715 * (y * y * y))))


def _pool_groups(tile, xp_ref, xpp_ref, xpn_ref):
    in_ctx = tile < CTX_TILES
    prev_ok = in_ctx & (tile > 0)
    next_ok = in_ctx & (tile < CTX_TILES - 1)
    x = xp_ref[...]
    p = jnp.concatenate([jnp.where(prev_ok, xpp_ref[...], 0.0), x,
                         jnp.where(next_ok, xpn_ref[...], 0.0)], axis=0)
    p = p.reshape(TT + 2 * POOL_HALO, BATCH, D_POOL)
    lo = jnp.where(in_ctx, -TT * tile, 0)
    hi = jnp.where(in_ctx, CTX_LEN - TT * tile, TT)
    t = lax.broadcasted_iota(jnp.int32, (TT, BATCH, POOL_GROUP_DIM), 0)
    outs = []
    for g, win in enumerate(POOL_WINDOWS):
        half = win // 2
        acc = p[:, :, g * POOL_GROUP_DIM:(g + 1) * POOL_GROUP_DIM]
        width = 1
        while width < win:
            acc = acc[:acc.shape[0] - width] + acc[width:]
            width *= 2
        start = POOL_HALO - half
        wsum = acc[start:start + TT]
        cnt = (jnp.minimum(t + half, hi) - jnp.maximum(t - half, lo)).astype(F32)
        centre = p[POOL_HALO:POOL_HALO + TT, :, g * POOL_GROUP_DIM:(g + 1) * POOL_GROUP_DIM]
        outs.append((wsum / cnt - centre).reshape(TR, POOL_GROUP_DIM))
    return outs


def _mixer_kernel(t_off, hf_ref, hr_ref, y_ref, xp_ref, xpp_ref, xpn_ref, x_ref,
                  g1_ref, sh2_ref, sc2_ref, pw_ref, pb_ref, ps_ref, wo_ref,
                  lg_ref, lb_ref, rw_ref, rb_ref, tri_ref,
                  xo_ref, h2_ref, route_ref, cnt_ref):
    i = pl.program_id(0)
    tile = i + t_off

    lru = (hf_ref[...] + hr_ref[...]) * _gelu_tanh(y_ref[...])
    diffs = _pool_groups(tile, xp_ref, xpp_ref, xpn_ref)
    pooled = jnp.concatenate(
        [jnp.dot(d.astype(BF16), pw_ref[g], preferred_element_type=F32)
         for g, d in enumerate(diffs)], axis=1)
    pooled = (pooled + pb_ref[...]) * ps_ref[...]
    mix = (jnp.dot(lru.astype(BF16), wo_ref[0:D_LRU, :], preferred_element_type=F32)
           + jnp.dot(pooled.astype(BF16), wo_ref[D_LRU:, :], preferred_element_type=F32))
    x = ALPHA * x_ref[...] + _per_batch(mix, g1_ref[0], jnp.multiply)
    x = _layer_norm(x) * lg_ref[...] + lb_ref[...]
    xo_ref[...] = x
    h2 = _per_batch(x, 1.0 + sc2_ref[0], jnp.multiply)
    h2 = _per_batch(h2, sh2_ref[0], jnp.add)
    h2_ref[...] = h2

    logits = jnp.dot(h2, rw_ref[...], precision=lax.Precision.HIGHEST,
                     preferred_element_type=F32) + rb_ref[...]
    lane = lax.broadcasted_iota(jnp.int32, (TR, LANES), 1).astype(F32)
    work = logits
    vals, idxs, sels = [], [], []
    for _ in range(TOP_K):
        m = jnp.max(work, axis=1, keepdims=True)
        idx = jnp.min(jnp.where(work == m, lane, float(LANES)), axis=1, keepdims=True)
        sel = lane == idx
        vals.append(m)
        idxs.append(idx)
        sels.append(sel)
        work = jnp.where(sel, -jnp.inf, work)
    exps = [jnp.exp(v - vals[0]) for v in vals]
    denom = exps[0] + exps[1] + exps[2] + exps[3]
    chosen = jnp.zeros((TR, LANES), F32)
    for sel in sels:
        chosen = chosen + sel.astype(F32)
    before = jnp.dot(tri_ref[...], chosen.astype(BF16), preferred_element_type=F32)
    route = jnp.zeros((TR, LANES), F32)
    for k in range(TOP_K):
        rank = jnp.sum(jnp.where(sels[k], before, 0.0), axis=1, keepdims=True)
        p = exps[k] / denom
        p_hi = p.astype(BF16).astype(F32)
        p_mid = (p - p_hi).astype(BF16).astype(F32)
        p_lo = p - p_hi - p_mid
        for base, val in ((LANE_P_HI, p_hi), (LANE_P_MID, p_mid), (LANE_P_LO, p_lo),
                          (LANE_EID, idxs[k]), (LANE_RANK, rank)):
            route = jnp.where(lane == float(base + k), val, route)
    route_ref[...] = route
    cnt_ref[0] = jnp.broadcast_to(jnp.sum(chosen, axis=0, keepdims=True), (SUBLANES, LANES))


def _mixer(t_off, hf, hr, z, x, mod, pool_w_bf, pool_b, pool_scale, w_out_bf,
           ln_g, ln_b, rw_pad, rb_pad, tri):
    n = N_TILES - t_off
    halo_rows = POOL_HALO * BATCH
    per = TR // halo_rows
    last_halo = R_ALL // halo_rows - 1
    xp_col = 2 * D_LRU // D_POOL
    row = lambda w: pl.BlockSpec((TR, w), lambda i: (i + t_off, 0))
    out_row = lambda w: pl.BlockSpec((TR, w), lambda i: (i, 0))
    const = lambda shape: pl.BlockSpec(shape, lambda i: (0,) * len(shape))

    def mod_spec(chunk):
        return pl.BlockSpec((1, BATCH, D_MODEL),
                            lambda i: ((i + t_off >= CTX_TILES).astype(jnp.int32), 0, chunk))

    return pl.pallas_call(
        functools.partial(_mixer_kernel, t_off),
        grid=(n,),
        in_specs=[
            row(D_LRU), row(D_LRU),
            pl.BlockSpec((TR, D_LRU), lambda i: (i + t_off, 1)),
            pl.BlockSpec((TR, D_POOL), lambda i: (i + t_off, xp_col)),
            pl.BlockSpec((halo_rows, D_POOL),
                         lambda i: (jnp.maximum((i + t_off) * per - 1, 0), xp_col)),
            pl.BlockSpec((halo_rows, D_POOL),
                         lambda i: (jnp.minimum((i + t_off + 1) * per, last_halo), xp_col)),
            row(D_MODEL),
            mod_spec(2), mod_spec(3), mod_spec(4),
            const((len(POOL_WINDOWS), POOL_GROUP_DIM, POOL_GROUP_DIM)),
            const((1, D_POOL)), const((1, D_POOL)),
            const((D_MODEL, D_MODEL)),
            const((1, D_MODEL)), const((1, D_MODEL)),
            const((D_MODEL, LANES)), const((1, LANES)),
            const((TR, TR)),
        ],
        out_specs=[out_row(D_MODEL), out_row(D_MODEL), out_row(LANES),
                   pl.BlockSpec((1, SUBLANES, LANES), lambda i: (i, 0, 0))],
        out_shape=[jax.ShapeDtypeStruct((n * TR, D_MODEL), F32),
                   jax.ShapeDtypeStruct((n * TR, D_MODEL), F32),
                   jax.ShapeDtypeStruct((n * TR, LANES), F32),
                   jax.ShapeDtypeStruct((n, SUBLANES, LANES), F32)],
        compiler_params=_cparams(("arbitrary",)),
        name="mixer_out",
    )(hf, hr, z, z, z, z, x, mod, mod, mod, pool_w_bf, pool_b, pool_scale, w_out_bf,
      ln_g, ln_b, rw_pad, rb_pad, tri)


def _one_hot_rows(targets, index):
    hit = jnp.zeros(index.shape, F32)
    for t in reversed(targets):
        hit = jnp.where(index == t, 1.0, hit)
    return hit.astype(BF16)


def _for_each_group(block, len_ref, body):
    def step(e, carry):
        n = pl.multiple_of(len_ref[block, e], GROUP)

        @pl.when(n > 0)
        def _():
            body(e, n)

        return carry

    lax.fori_loop(0, N_EXPERTS, step, 0)


def _group(start, n):
    return pl.ds(pl.multiple_of(start, GROUP), n)


def _dispatch_kernel(src_ref, len_ref, dst_ref, tot_ref, tail_s_ref, tail_n_ref, na_ref,
                     lpos_ref, h_ref, r_ref, xs_ref, stage_ref, zero_ref, sems):
    i = pl.program_id(0)
    n_blocks = pl.num_programs(0) - 1
    slot = i % 2

    @pl.when(i < n_blocks)
    def _():
        lpos = lpos_ref[...]
        targets = [lpos[k:k + 1, :] for k in range(TOP_K)]
        h = h_ref[...].astype(BF16)
        r = r_ref[...].astype(BF16)

        def permute(jb, carry):
            base = pl.multiple_of(jb * KB, KB)
            rows = lax.broadcasted_iota(jnp.int32, (KB, TR), 0) + base
            sel = _one_hot_rows(targets, rows)
            stage_ref[slot, pl.ds(base, KB), 0:D_MODEL] = jnp.dot(
                sel, h, preferred_element_type=F32)
            stage_ref[slot, pl.ds(base, KB), D_MODEL:D_AUG] = jnp.dot(
                sel, r, preferred_element_type=F32)
            return carry

        lax.fori_loop(0, (tot_ref[i] + KB - 1) // KB, permute, 0)
        _for_each_group(i, len_ref, lambda e, n: pltpu.make_async_copy(
            stage_ref.at[slot, _group(src_ref[i, e], n)],
            xs_ref.at[_group(dst_ref[i, e], n)], sems.at[slot]).start())

    @pl.when(i > 0)
    def _():
        rows = _group(0, pl.multiple_of(tot_ref[i - 1], GROUP))
        pltpu.make_async_copy(stage_ref.at[1 - slot, rows], xs_ref.at[rows],
                              sems.at[1 - slot]).wait()

    @pl.when(i == n_blocks)
    def _():
        zero_ref[...] = jnp.zeros_like(zero_ref)
        fill = sems.at[2]

        def expert_tail(e, n):
            return pltpu.make_async_copy(zero_ref.at[_group(0, n)],
                                         xs_ref.at[_group(tail_s_ref[0, e], n)], fill)

        def whole_tile(t):
            return pltpu.make_async_copy(zero_ref, xs_ref.at[_group(t * TM, TM)], fill)

        n_tiles = xs_ref.shape[0] // TM
        _for_each_group(0, tail_n_ref, lambda e, n: expert_tail(e, n).start())
        lax.fori_loop(na_ref[0], n_tiles, lambda t, c: (whole_tile(t).start(), c)[1], 0)
        _for_each_group(0, tail_n_ref, lambda e, n: expert_tail(e, n).wait())
        lax.fori_loop(na_ref[0], n_tiles, lambda t, c: (whole_tile(t).wait(), c)[1], 0)


def _dispatch(plan, h2, route):
    n = h2.shape[0] // TR
    last = n - 1
    grid_spec = pltpu.PrefetchScalarGridSpec(
        num_scalar_prefetch=7,
        grid=(n + 1,),
        in_specs=[
            pl.BlockSpec((None, SUBLANES, TR), lambda i, *_: (jnp.minimum(i, last), 0, 0)),
            pl.BlockSpec((TR, D_MODEL), lambda i, *_: (jnp.minimum(i, last), 0)),
            pl.BlockSpec((TR, LANES), lambda i, *_: (jnp.minimum(i, last), 0)),
        ],
        out_specs=pl.BlockSpec(memory_space=pl.ANY),
        scratch_shapes=[pltpu.VMEM((2, RB, D_AUG), F32),
                        pltpu.VMEM((TM, D_AUG), F32),
                        pltpu.SemaphoreType.DMA((3,))],
    )
    return pl.pallas_call(
        _dispatch_kernel,
        grid_spec=grid_spec,
        out_shape=jax.ShapeDtypeStruct((plan["n_sorted"], D_AUG), F32),
        compiler_params=_cparams(("arbitrary",)),
        name="moe_dispatch",
    )(plan["src_row"], plan["n_rows"], plan["dst_row"], plan["block_rows"],
      plan["tail_start"][None], plan["tail_len"][None], plan["n_active"], plan["lpos_t"],
      h2, route)


def _expert_kernel(layer, te_ref, na_ref, first_ref, nxt_ref, slot_ref,
                   x_ref, b1_ref, b2_ref, w1_hbm, w2_hbm, o_ref,
                   w1buf_ref, w2buf_ref, w1c_ref, w2c_ref, sems):
    i = pl.program_id(0)
    e = te_ref[i]
    slot = slot_ref[i]

    def fetch(expert, s):
        return (pltpu.make_async_copy(w1_hbm.at[layer, expert], w1buf_ref.at[s], sems.at[s, 0]),
                pltpu.make_async_copy(w2_hbm.at[layer, expert], w2buf_ref.at[s], sems.at[s, 1]))

    @pl.when(i == 0)
    def _():
        for copy in fetch(e, slot):
            copy.start()

    @pl.when(first_ref[i] == 1)
    def _():
        for copy in fetch(e, slot):
            copy.wait()
        w1c_ref[...] = w1buf_ref[slot].astype(BF16)
        w2c_ref[...] = w2buf_ref[slot].astype(BF16)

        @pl.when(nxt_ref[i] >= 0)
        def _():
            for copy in fetch(nxt_ref[i], 1 - slot):
                copy.start()

    @pl.when(i < na_ref[0])
    def _():
        aug = x_ref[:, D_MODEL:D_AUG]
        e_f = e.astype(F32)
        gate = jnp.zeros((TM, 1), F32)
        for k in range(TOP_K):
            p = (aug[:, LANE_P_HI + k:LANE_P_HI + k + 1]
                 + aug[:, LANE_P_MID + k:LANE_P_MID + k + 1]
                 + aug[:, LANE_P_LO + k:LANE_P_LO + k + 1])
            gate = gate + jnp.where(aug[:, LANE_EID + k:LANE_EID + k + 1] == e_f, p, 0.0)
        gu = jnp.dot(x_ref[:, 0:D_MODEL].astype(BF16), w1c_ref[...],
                     preferred_element_type=F32) + b1_ref[...]
        glu = jnp.minimum(gu[:, :D_EXPERT], SWIGLU_LIMIT)
        lin = jnp.clip(gu[:, D_EXPERT:], -SWIGLU_LIMIT, SWIGLU_LIMIT)
        act = glu * _sigmoid(SWIGLU_ALPHA * glu) * (lin + 1.0)
        y = jnp.dot(act.astype(BF16), w2c_ref[...], preferred_element_type=F32) + b2_ref[...]
        o_ref[...] = gate * y

    @pl.when(i >= na_ref[0])
    def _():
        o_ref[...] = jnp.zeros_like(o_ref)


def _experts(layer, plan, xs, w1, b1, w2, b2):
    n_tiles = xs.shape[0] // TM
    act_tile = lambda i, te, na, *_: (jnp.maximum(jnp.minimum(i, na[0] - 1), 0), 0)
    expert = lambda i, te, *_: (layer, te[i], 0, 0)
    grid_spec = pltpu.PrefetchScalarGridSpec(
        num_scalar_prefetch=5,
        grid=(n_tiles,),
        in_specs=[
            pl.BlockSpec((TM, D_AUG), act_tile),
            pl.BlockSpec((None, None, 1, 2 * D_EXPERT), expert),
            pl.BlockSpec((None, None, 1, D_MODEL), expert),
            pl.BlockSpec(memory_space=pl.ANY),
            pl.BlockSpec(memory_space=pl.ANY),
        ],
        out_specs=pl.BlockSpec((TM, D_MODEL), lambda i, *_: (i, 0)),
        scratch_shapes=[pltpu.VMEM((2, D_MODEL, 2 * D_EXPERT), F32),
                        pltpu.VMEM((2, D_EXPERT, D_MODEL), F32),
                        pltpu.VMEM((D_MODEL, 2 * D_EXPERT), BF16),
                        pltpu.VMEM((D_EXPERT, D_MODEL), BF16),
                        pltpu.SemaphoreType.DMA((2, 2))],
    )
    return pl.pallas_call(
        functools.partial(_expert_kernel, layer),
        grid_spec=grid_spec,
        out_shape=jax.ShapeDtypeStruct((xs.shape[0], D_MODEL), F32),
        compiler_params=_cparams(("arbitrary",)),
        name="moe_experts",
    )(plan["tile_expert"], plan["n_active"], plan["tile_first"], plan["tile_next"],
      plan["tile_slot"], xs, b1.reshape(DEPTH, N_EXPERTS, 1, -1),
      b2.reshape(DEPTH, N_EXPERTS, 1, -1), w1, w2)


def _combine_kernel(src_ref, len_ref, dst_ref, tot_ref, lpos_ref, x_ref, g2_ref, lg_ref,
                    lb_ref, ys_ref, o_ref, ybuf_ref, sems):
    i = pl.program_id(0)
    n_blocks = pl.num_programs(0)
    slot = i % 2

    def start_block(block, s):
        _for_each_group(block, len_ref, lambda e, n: pltpu.make_async_copy(
            ys_ref.at[_group(dst_ref[block, e], n)],
            ybuf_ref.at[s, _group(src_ref[block, e], n)], sems.at[s]).start())

    @pl.when(i == 0)
    def _():
        ybuf_ref[...] = jnp.zeros_like(ybuf_ref)
        start_block(0, 0)

    @pl.when(i + 1 < n_blocks)
    def _():
        start_block(i + 1, 1 - slot)

    rows = _group(0, pl.multiple_of(tot_ref[i], GROUP))
    pltpu.make_async_copy(ys_ref.at[rows], ybuf_ref.at[slot, rows], sems.at[slot]).wait()
    lpos = lpos_ref[...]
    targets = [jnp.broadcast_to(lpos[:, k:k + 1], (TR, KB)) for k in range(TOP_K)]
    cols = lax.broadcasted_iota(jnp.int32, (TR, KB), 1)
    f = jnp.zeros((TR, D_MODEL), F32)
    for jb in range(RB // KB):
        sel = _one_hot_rows(targets, cols + jb * KB)
        f = f + jnp.dot(sel, ybuf_ref[slot, jb * KB:(jb + 1) * KB, :].astype(BF16),
                        preferred_element_type=F32)
    x = ALPHA * x_ref[...] + _per_batch(f, g2_ref[0], jnp.multiply)
    o_ref[...] = _layer_norm(x) * lg_ref[...] + lb_ref[...]


def _combine(t_off, plan, x, mod, ln_g, ln_b, ys):
    n = x.shape[0] // TR
    row = lambda w: pl.BlockSpec((TR, w), lambda i, *_: (i, 0))
    const = lambda shape: pl.BlockSpec(shape, lambda i, *_: (0,) * len(shape))
    grid_spec = pltpu.PrefetchScalarGridSpec(
        num_scalar_prefetch=4,
        grid=(n,),
        in_specs=[
            row(TOP_K), row(D_MODEL),
            pl.BlockSpec((1, BATCH, D_MODEL),
                         lambda i, *_: ((i + t_off >= CTX_TILES).astype(jnp.int32), 0, 5)),
            const((1, D_MODEL)), const((1, D_MODEL)),
            pl.BlockSpec(memory_space=pl.ANY),
        ],
        out_specs=row(D_MODEL),
        scratch_shapes=[pltpu.VMEM((2, RB, D_MODEL), F32),
                        pltpu.SemaphoreType.DMA((2,))],
    )
    return pl.pallas_call(
        _combine_kernel,
        grid_spec=grid_spec,
        out_shape=jax.ShapeDtypeStruct((n * TR, D_MODEL), F32),
        compiler_params=_cparams(("arbitrary",)),
        name="moe_combine",
    )(plan["src_row"], plan["n_rows"], plan["dst_row"], plan["block_rows"], plan["lpos"],
      x, mod, ln_g, ln_b, ys)


def _routing_plan(route, counts):
    i32 = jnp.int32
    nb = counts.shape[0]
    experts = jnp.arange(N_EXPERTS, dtype=i32)
    eid = route[:, LANE_EID:LANE_EID + TOP_K].astype(i32).reshape(nb, TR, TOP_K)
    rank = route[:, LANE_RANK:LANE_RANK + TOP_K].astype(i32).reshape(nb, TR, TOP_K)
    n = counts[:, 0, :N_EXPERTS].astype(i32)
    n = (n + GROUP - 1) // GROUP * GROUP
    src_row = jnp.cumsum(n, axis=1) - n
    seg = jnp.sum(n, axis=0)
    tiles_e = (seg + TM - 1) // TM
    tile_end = jnp.cumsum(tiles_e)
    e_start = (tile_end - tiles_e) * TM
    n_active = tile_end[-1]
    dst_row = e_start[None, :] + jnp.cumsum(n, axis=0) - n

    onehot = eid[..., None] == experts
    lpos = jnp.sum(jnp.where(onehot, src_row[:, None, None, :], 0), axis=-1) + rank
    lpos_t = jnp.concatenate(
        [lpos.transpose(0, 2, 1), jnp.full((nb, SUBLANES - TOP_K, TR), -1, i32)], axis=1)

    n_tiles = (nb * RB + TM - 1) // TM + N_EXPERTS
    t = jnp.minimum(jnp.arange(n_tiles, dtype=i32), n_active - 1)
    tile_expert = jnp.sum((t[:, None] >= tile_end[None, :]).astype(i32), axis=1)
    tile_expert = jnp.minimum(tile_expert, N_EXPERTS - 1)
    tile_first = jnp.concatenate(
        [jnp.ones((1,), i32), (tile_expert[1:] != tile_expert[:-1]).astype(i32)])
    has_rows = tiles_e > 0
    later = jnp.where(has_rows[None, :] & (experts[None, :] > experts[:, None]),
                      experts[None, :], N_EXPERTS)
    next_e = jnp.min(later, axis=1)
    next_e = jnp.where(next_e == N_EXPERTS, -1, next_e)
    group = jnp.cumsum(has_rows.astype(i32)) - 1
    pick = tile_expert[:, None] == experts[None, :]
    tile_next = jnp.sum(jnp.where(pick, next_e[None, :], 0), axis=1)
    tile_slot = jnp.sum(jnp.where(pick, group[None, :], 0), axis=1) % 2
    return dict(lpos=lpos.reshape(nb * TR, TOP_K), lpos_t=lpos_t, src_row=src_row, n_rows=n,
                dst_row=dst_row, block_rows=jnp.sum(n, axis=1), tail_start=e_start + seg,
                tail_len=tile_end * TM - e_start - seg,
                tile_expert=tile_expert, n_active=n_active.reshape(1), tile_first=tile_first,
                tile_next=tile_next, tile_slot=tile_slot, n_sorted=n_tiles * TM)


def _grid_sincos_parts():
    quarter = D_MODEL // 4
    omega = 1.0 / (10000.0 ** (jnp.arange(quarter, dtype=F32) / quarter))

    def emb1d(n):
        ang = jnp.arange(n, dtype=F32)[:, None] * omega[None, :]
        return jnp.concatenate([jnp.sin(ang), jnp.cos(ang)], axis=-1)

    er = emb1d(SEQ // GRID_W)
    er = jnp.concatenate([jnp.zeros((CTX_TILES, D_MODEL // 2), F32), er], axis=0)
    ec = jnp.repeat(emb1d(GRID_W), BATCH, axis=0)
    return er.reshape(N_TILES, 1, D_MODEL // 2), ec


def _gate_blocks(w):
    per = GATE_BLK // LRU_HEAD_DIM
    w = w.reshape(2, D_LRU // GATE_BLK, per, LRU_HEAD_DIM, LRU_HEAD_DIM)
    eye = jnp.eye(per, dtype=w.dtype)
    blk = jnp.einsum('dkpij,pq->dkpiqj', w, eye)
    return blk.reshape(2, D_LRU // GATE_BLK, GATE_BLK, GATE_BLK)


def kernel(x, c, ctx, c_ctx, w_mod, b_mod, w_in, conv_w, conv_b, gate_a_w, gate_a_b,
           gate_x_w, gate_x_b, lru_lambda, pool_w, pool_b, pool_scale, w_out, ln1_g, ln1_b,
           router_w, router_b, exp_w1, exp_b1, exp_w2, exp_b2, ln2_g, ln2_b):
    x_tm = jnp.concatenate([ctx.transpose(1, 0, 2), x.transpose(1, 0, 2)], axis=0)
    x_tm = x_tm.reshape(R_ALL, D_MODEL)
    er, ec = _grid_sincos_parts()
    xs = _entry(x_tm, er, ec)

    cvec = jnp.concatenate([c, c_ctx[None], jnp.zeros((2 * SUBLANES - BATCH - 1, D_MODEL), F32)])
    mod_all = _modulation(cvec, w_mod, b_mod)
    tri = jnp.tril(jnp.ones((TR, TR), F32), -1).astype(BF16)

    for l in range(DEPTH):
        last = l == DEPTH - 1
        t_off = CTX_TILES if last else 0
        mod = jnp.stack([jnp.broadcast_to(mod_all[l, BATCH], (BATCH, 6 * D_MODEL)),
                         mod_all[l, :BATCH]])
        z = _inproj(xs, mod, w_in[l].astype(BF16))
        gate_w = jnp.stack([_gate_blocks(gate_a_w[l]), _gate_blocks(gate_x_w[l])],
                           axis=1).astype(BF16)
        gate_b = jnp.stack([gate_a_b[l], gate_x_b[l]], axis=1)
        hf, hr = _scan(z, conv_w[l], conv_b[l][None], gate_w, gate_b, lru_lambda[l])
        rw_pad = jnp.zeros((D_MODEL, LANES), F32).at[:, :N_EXPERTS].set(router_w[l])
        rb_pad = jnp.full((1, LANES), -1e30, F32).at[0, :N_EXPERTS].set(router_b[l])
        xs, h2, route, counts = _mixer(
            t_off, hf, hr, z, xs, mod, pool_w[l].astype(BF16), pool_b[l][None],
            pool_scale[l][None], w_out[l].astype(BF16), ln1_g[l][None], ln1_b[l][None],
            rw_pad, rb_pad, tri)
        plan = _routing_plan(route, counts)
        x_sorted = _dispatch(plan, h2, route)
        y_sorted = _experts(l, plan, x_sorted, exp_w1, exp_b1, exp_w2, exp_b2)
        xs = _combine(t_off, plan, xs, mod, ln2_g[l][None], ln2_b[l][None], y_sorted)

    out = xs.reshape(SEQ, BATCH, D_MODEL)
    return out.transpose(1, 0, 2)
```

```python
import functools

import jax
import jax.numpy as jnp
from jax import lax
from jax.experimental import pallas as pl
from jax.experimental.pallas import tpu as pltpu

D_MODEL = 1024
BATCH = 8
SEQ = 2048
DEPTH = 2
GRID_W = 64
CTX_LEN = 256
D_LRU = 512
N_LRU_HEADS = 8
LRU_HEAD_DIM = D_LRU // N_LRU_HEADS
LRU_C = 8.0
D_POOL = 512
POOL_WINDOWS = (2, 4, 8, 16)
POOL_GROUP_DIM = D_POOL // len(POOL_WINDOWS)
D_IN = 2 * D_LRU + D_POOL
N_EXPERTS = 32
TOP_K = 4
D_EXPERT = D_MODEL
SWIGLU_LIMIT = 7.0
SWIGLU_ALPHA = 1.702
LN_EPS = 1e-5
ALPHA = (2.0 * DEPTH) ** 0.25

F32 = jnp.float32
BF16 = jnp.bfloat16

SUBLANES = 8
LANES = 128
TT = GRID_W
TR = TT * BATCH
T_ALL = CTX_LEN + SEQ
R_ALL = T_ALL * BATCH
N_TILES = T_ALL // TT
CTX_TILES = CTX_LEN // TT
GATE_BLK = 256
POOL_HALO = max(POOL_WINDOWS) // 2
TM = 512
FFN_CHUNK = 512
GROUP = SUBLANES
RB = TOP_K * TR + N_EXPERTS * GROUP
KB = 256
D_AUG = D_MODEL + LANES
LANE_P_HI, LANE_P_MID, LANE_P_LO, LANE_EID, LANE_RANK = 0, 4, 8, 12, 16
VMEM_LIMIT = 56 * 1024 * 1024


def _cparams(sem):
    return pltpu.CompilerParams(dimension_semantics=sem, vmem_limit_bytes=VMEM_LIMIT)


def _sigmoid(x):
    return 1.0 / (1.0 + jnp.exp(-x))


def _layer_norm(x):
    mu = jnp.mean(x, axis=-1, keepdims=True)
    xc = x - mu
    var = jnp.mean(xc * xc, axis=-1, keepdims=True)
    return xc * lax.rsqrt(var + LN_EPS)


def _per_batch(x, v, op):
    r, d = x.shape
    return op(x.reshape(r // BATCH, BATCH, d), v[None]).reshape(r, d)


def _mod_kernel(c_ref, w_ref, b_ref, o_ref):
    c = c_ref[...]
    s = c * _sigmoid(c)
    o_ref[...] = jnp.dot(s, w_ref[...], precision=lax.Precision.HIGHEST,
                         preferred_element_type=F32) + b_ref[...]


def _modulation(cvec, w_mod, b_mod):
    tn = 512
    return pl.pallas_call(
        _mod_kernel,
        grid=(DEPTH, 6 * D_MODEL // tn),
        in_specs=[
            pl.BlockSpec((2 * SUBLANES, D_MODEL), lambda l, j: (0, 0)),
            pl.BlockSpec((None, D_MODEL, tn), lambda l, j: (l, 0, j)),
            pl.BlockSpec((None, 1, tn), lambda l, j: (l, 0, j)),
        ],
        out_specs=pl.BlockSpec((None, 2 * SUBLANES, tn), lambda l, j: (l, 0, j)),
        out_shape=jax.ShapeDtypeStruct((DEPTH, 2 * SUBLANES, 6 * D_MODEL), F32),
        compiler_params=_cparams(("arbitrary", "arbitrary")),
        name="modulation",
    )(cvec, w_mod, b_mod.reshape(DEPTH, 1, 6 * D_MODEL))


def _entry_kernel(x_ref, er_ref, ec_ref, o_ref):
    i = pl.program_id(0)
    x = x_ref[...]
    lat = (i >= CTX_TILES).astype(F32)
    pos = jnp.concatenate(
        [jnp.broadcast_to(er_ref[0], (TR, D_MODEL // 2)), ec_ref[...] * lat], axis=1)
    o_ref[...] = _layer_norm(x + pos)


def _entry(x_tm, er, ec):
    return pl.pallas_call(
        _entry_kernel,
        grid=(N_TILES,),
        in_specs=[
            pl.BlockSpec((TR, D_MODEL), lambda i: (i, 0)),
            pl.BlockSpec((1, 1, D_MODEL // 2), lambda i: (i, 0, 0)),
            pl.BlockSpec((TR, D_MODEL // 2), lambda i: (0, 0)),
        ],
        out_specs=pl.BlockSpec((TR, D_MODEL), lambda i: (i, 0)),
        out_shape=jax.ShapeDtypeStruct((R_ALL, D_MODEL), F32),
        compiler_params=_cparams(("arbitrary",)),
        name="entry_ln",
    )(x_tm, er, ec)


def _inproj_kernel(x_ref, sh_ref, sc_ref, w_ref, z_ref):
    h = _per_batch(x_ref[...], 1.0 + sc_ref[0], jnp.multiply)
    h = _per_batch(h, sh_ref[0], jnp.add)
    z_ref[...] = jnp.dot(h.astype(BF16), w_ref[...], preferred_element_type=F32)


def _mod_spec(chunk):
    return pl.BlockSpec((1, BATCH, D_MODEL),
                        lambda i: ((i >= CTX_TILES).astype(jnp.int32), 0, chunk))


def _inproj(x, mod, w_in_bf):
    return pl.pallas_call(
        _inproj_kernel,
        grid=(N_TILES,),
        in_specs=[
            pl.BlockSpec((TR, D_MODEL), lambda i: (i, 0)),
            _mod_spec(0), _mod_spec(1),
            pl.BlockSpec((D_MODEL, D_IN), lambda i: (0, 0)),
        ],
        out_specs=pl.BlockSpec((TR, D_IN), lambda i: (i, 0)),
        out_shape=jax.ShapeDtypeStruct((R_ALL, D_IN), F32),
        compiler_params=_cparams(("arbitrary",)),
        name="in_proj",
    )(x, mod, mod, w_in_bf)


def _rev_tile(i):
    return jnp.where(i < CTX_TILES, CTX_TILES - 1 - i, N_TILES - 1 + CTX_TILES - i)


def _block_diag_dot(u_bf, w_ref, d, g):
    halves = [jnp.dot(u_bf[:, k * GATE_BLK:(k + 1) * GATE_BLK], w_ref[d, g, k],
                      preferred_element_type=F32) for k in range(D_LRU // GATE_BLK)]
    return jnp.concatenate(halves, axis=1)


def _lru_coeffs(tile, z_ref, zp_ref, zn_ref, cw_ref, cb_ref, gw_ref, gb_ref, lam_ref, d):
    x = z_ref[...]
    seg_first = (tile == 0) | (tile == CTX_TILES)
    seg_last = (tile == CTX_TILES - 1) | (tile == N_TILES - 1)
    prev = jnp.where(seg_first, 0.0, zp_ref[...])
    nxt = jnp.where(seg_last, 0.0, zn_ref[...])
    xm2 = jnp.concatenate([prev, x[:-2 * BATCH]], axis=0)
    xm1 = jnp.concatenate([prev[BATCH:], x[:-BATCH]], axis=0)
    xp1 = jnp.concatenate([x[BATCH:], nxt], axis=0)
    u = (cb_ref[...] + xm2 * cw_ref[0:1] + xm1 * cw_ref[1:2]
         + x * cw_ref[2:3] + xp1 * cw_ref[3:4])
    u_bf = u.astype(BF16)
    r = _sigmoid(_block_diag_dot(u_bf, gw_ref, d, 0) + gb_ref[d, 0:1])
    ig = _sigmoid(_block_diag_dot(u_bf, gw_ref, d, 1) + gb_ref[d, 1:2])
    nl = -lam_ref[d:d + 1]
    softplus = jnp.maximum(nl, 0.0) + jnp.log(1.0 + jnp.exp(-jnp.abs(nl)))
    log_a = (-LRU_C) * r * softplus
    a = jnp.exp(log_a)
    mult = jnp.sqrt(1.0 - a * a)
    return a, mult * (ig * u)


def _scan_kernel(zf_ref, zfp_ref, zfn_ref, zr_ref, zrp_ref, zrn_ref,
                 cw_ref, cb_ref, gw_ref, gb_ref, lam_ref,
                 hf_ref, hr_ref, state_ref):
    i = pl.program_id(0)

    @pl.when(i == 0)
    def _():
        state_ref[...] = jnp.zeros_like(state_ref)

    af, bf = _lru_coeffs(i, zf_ref, zfp_ref, zfn_ref, cw_ref, cb_ref, gw_ref, gb_ref,
                         lam_ref, 0)
    ar, br = _lru_coeffs(_rev_tile(i), zr_ref, zrp_ref, zrn_ref, cw_ref, cb_ref, gw_ref,
                         gb_ref, lam_ref, 1)
    hf = state_ref[0]
    hr = state_ref[1]
    for s in range(TT):
        f = slice(s * BATCH, (s + 1) * BATCH)
        hf = af[f] * hf + bf[f]
        hf_ref[f, :] = hf
        b = slice((TT - 1 - s) * BATCH, (TT - s) * BATCH)
        hr = ar[b] * hr + br[b]
        hr_ref[b, :] = hr
    state_ref[0] = hf
    state_ref[1] = hr


def _scan(z, conv_w, conv_b, gate_w, gate_b, lam):
    prev_rows = 2 * BATCH
    tiles_per_prev = TR // prev_rows
    tiles_per_next = TR // BATCH
    last_next = R_ALL // BATCH - 1

    def cur(f):
        return pl.BlockSpec((TR, D_LRU), lambda i: (f(i), 0))

    def prev(f):
        return pl.BlockSpec((prev_rows, D_LRU),
                            lambda i: (jnp.maximum(f(i) * tiles_per_prev - 1, 0), 0))

    def nxt(f):
        return pl.BlockSpec((BATCH, D_LRU),
                            lambda i: (jnp.minimum((f(i) + 1) * tiles_per_next, last_next), 0))

    fwd = lambda i: i
    const = lambda shape: pl.BlockSpec(shape, lambda i: (0,) * len(shape))
    return pl.pallas_call(
        _scan_kernel,
        grid=(N_TILES,),
        in_specs=[cur(fwd), prev(fwd), nxt(fwd), cur(_rev_tile), prev(_rev_tile), nxt(_rev_tile),
                  const((4, D_LRU)), const((1, D_LRU)),
                  const((2, 2, D_LRU // GATE_BLK, GATE_BLK, GATE_BLK)),
                  const((2, 2, D_LRU)), const((2, D_LRU))],
        out_specs=[pl.BlockSpec((TR, D_LRU), lambda i: (i, 0)),
                   pl.BlockSpec((TR, D_LRU), lambda i: (_rev_tile(i), 0))],
        out_shape=[jax.ShapeDtypeStruct((R_ALL, D_LRU), F32)] * 2,
        scratch_shapes=[pltpu.VMEM((2, BATCH, D_LRU), F32)],
        compiler_params=_cparams(("arbitrary",)),
        name="lru_scan",
    )(z, z, z, z, z, z, conv_w, conv_b, gate_w, gate_b, lam)


def _gelu_tanh(y):
    c = 0.7978845608028654
    return 0.5 * y * (1.0 + jnp.tanh(c * (y + 0.044715 * (y * y * y))))


def _pool_groups(tile, xp_ref, xpp_ref, xpn_ref):
    in_ctx = tile < CTX_TILES
    prev_ok = in_ctx & (tile > 0)
    next_ok = in_ctx & (tile < CTX_TILES - 1)
    x = xp_ref[...]
    p = jnp.concatenate([jnp.where(prev_ok, xpp_ref[...], 0.0), x,
                         jnp.where(next_ok, xpn_ref[...], 0.0)], axis=0)
    p = p.reshape(TT + 2 * POOL_HALO, BATCH, D_POOL)
    lo = jnp.where(in_ctx, -TT * tile, 0)
    hi = jnp.where(in_ctx, CTX_LEN - TT * tile, TT)
    t = lax.broadcasted_iota(jnp.int32, (TT, BATCH, POOL_GROUP_DIM), 0)
    outs = []
    for g, win in enumerate(POOL_WINDOWS):
        half = win // 2
        acc = p[:, :, g * POOL_GROUP_DIM:(g + 1) * POOL_GROUP_DIM]
        width = 1
        while width < win:
            acc = acc[:acc.shape[0] - width] + acc[width:]
            width *= 2
        start = POOL_HALO - half
        wsum = acc[start:start + TT]
        cnt = (jnp.minimum(t + half, hi) - jnp.maximum(t - half, lo)).astype(F32)
        centre = p[POOL_HALO:POOL_HALO + TT, :, g * POOL_GROUP_DIM:(g + 1) * POOL_GROUP_DIM]
        outs.append((wsum / cnt - centre).reshape(TR, POOL_GROUP_DIM))
    return outs


def _mixer_kernel(t_off, hf_ref, hr_ref, y_ref, xp_ref, xpp_ref, xpn_ref, x_ref,
                  g1_ref, sh2_ref, sc2_ref, pw_ref, pb_ref, ps_ref, wo_ref,
                  lg_ref, lb_ref, rw_ref, rb_ref, tri_ref,
                  xo_ref, h2_ref, route_ref, cnt_ref):
    i = pl.program_id(0)
    tile = i + t_off

    lru = (hf_ref[...] + hr_ref[...]) * _gelu_tanh(y_ref[...])
    diffs = _pool_groups(tile, xp_ref, xpp_ref, xpn_ref)
    pooled = jnp.concatenate(
        [jnp.dot(d.astype(BF16), pw_ref[g], preferred_element_type=F32)
         for g, d in enumerate(diffs)], axis=1)
    pooled = (pooled + pb_ref[...]) * ps_ref[...]
    mix = (jnp.dot(lru.astype(BF16), wo_ref[0:D_LRU, :], preferred_element_type=F32)
           + jnp.dot(pooled.astype(BF16), wo_ref[D_LRU:, :], preferred_element_type=F32))
    x = ALPHA * x_ref[...] + _per_batch(mix, g1_ref[0], jnp.multiply)
    x = _layer_norm(x) * lg_ref[...] + lb_ref[...]
    xo_ref[...] = x
    h2 = _per_batch(x, 1.0 + sc2_ref[0], jnp.multiply)
    h2 = _per_batch(h2, sh2_ref[0], jnp.add)
    h2_ref[...] = h2

    h_hi = h2.astype(BF16)
    h_lo = (h2 - h_hi.astype(F32)).astype(BF16)
    logits = (jnp.dot(h_hi, rw_ref[0], preferred_element_type=F32)
              + jnp.dot(h_lo, rw_ref[0], preferred_element_type=F32)
              + jnp.dot(h_hi, rw_ref[1], preferred_element_type=F32)) + rb_ref[...]
    lane = lax.broadcasted_iota(jnp.int32, (TR, LANES), 1).astype(F32)
    work = logits
    vals, idxs, sels = [], [], []
    for _ in range(TOP_K):
        m = jnp.max(work, axis=1, keepdims=True)
        idx = jnp.min(jnp.where(work == m, lane, float(LANES)), axis=1, keepdims=True)
        sel = lane == idx
        vals.append(m)
        idxs.append(idx)
        sels.append(sel)
        work = jnp.where(sel, -jnp.inf, work)
    exps = [jnp.exp(v - vals[0]) for v in vals]
    denom = exps[0] + exps[1] + exps[2] + exps[3]
    chosen = jnp.zeros((TR, LANES), F32)
    for sel in sels:
        chosen = chosen + sel.astype(F32)
    before = jnp.dot(tri_ref[...], chosen.astype(BF16), preferred_element_type=F32)
    route = jnp.zeros((TR, LANES), F32)
    for k in range(TOP_K):
        rank = jnp.sum(jnp.where(sels[k], before, 0.0), axis=1, keepdims=True)
        p = exps[k] / denom
        p_hi = p.astype(BF16).astype(F32)
        p_mid = (p - p_hi).astype(BF16).astype(F32)
        p_lo = p - p_hi - p_mid
        for base, val in ((LANE_P_HI, p_hi), (LANE_P_MID, p_mid), (LANE_P_LO, p_lo),
                          (LANE_EID, idxs[k]), (LANE_RANK, rank)):
            route = jnp.where(lane == float(base + k), val, route)
    route_ref[...] = route
    cnt_ref[0] = jnp.broadcast_to(jnp.sum(chosen, axis=0, keepdims=True), (SUBLANES, LANES))


def _mixer(t_off, hf, hr, z, x, mod, pool_w_bf, pool_b, pool_scale, w_out_bf,
           ln_g, ln_b, rw_pad, rb_pad, tri):
    n = N_TILES - t_off
    halo_rows = POOL_HALO * BATCH
    per = TR // halo_rows
    last_halo = R_ALL // halo_rows - 1
    xp_col = 2 * D_LRU // D_POOL
    row = lambda w: pl.BlockSpec((TR, w), lambda i: (i + t_off, 0))
    out_row = lambda w: pl.BlockSpec((TR, w), lambda i: (i, 0))
    const = lambda shape: pl.BlockSpec(shape, lambda i: (0,) * len(shape))

    def mod_spec(chunk):
        return pl.BlockSpec((1, BATCH, D_MODEL),
                            lambda i: ((i + t_off >= CTX_TILES).astype(jnp.int32), 0, chunk))

    return pl.pallas_call(
        functools.partial(_mixer_kernel, t_off),
        grid=(n,),
        in_specs=[
            row(D_LRU), row(D_LRU),
            pl.BlockSpec((TR, D_LRU), lambda i: (i + t_off, 1)),
            pl.BlockSpec((TR, D_POOL), lambda i: (i + t_off, xp_col)),
            pl.BlockSpec((halo_rows, D_POOL),
                         lambda i: (jnp.maximum((i + t_off) * per - 1, 0), xp_col)),
            pl.BlockSpec((halo_rows, D_POOL),
                         lambda i: (jnp.minimum((i + t_off + 1) * per, last_halo), xp_col)),
            row(D_MODEL),
            mod_spec(2), mod_spec(3), mod_spec(4),
            const((len(POOL_WINDOWS), POOL_GROUP_DIM, POOL_GROUP_DIM)),
            const((1, D_POOL)), const((1, D_POOL)),
            const((D_MODEL, D_MODEL)),
            const((1, D_MODEL)), const((1, D_MODEL)),
            const((2, D_MODEL, LANES)), const((1, LANES)),
            const((TR, TR)),
        ],
        out_specs=[out_row(D_MODEL), out_row(D_MODEL), out_row(LANES),
                   pl.BlockSpec((1, SUBLANES, LANES), lambda i: (i, 0, 0))],
        out_shape=[jax.ShapeDtypeStruct((n * TR, D_MODEL), F32),
                   jax.ShapeDtypeStruct((n * TR, D_MODEL), F32),
                   jax.ShapeDtypeStruct((n * TR, LANES), F32),
                   jax.ShapeDtypeStruct((n, SUBLANES, LANES), F32)],
        compiler_params=_cparams(("arbitrary",)),
        name="mixer_out",
    )(hf, hr, z, z, z, z, x, mod, mod, mod, pool_w_bf, pool_b, pool_scale, w_out_bf,
      ln_g, ln_b, rw_pad, rb_pad, tri)


def _one_hot_rows(targets, index):
    hit = jnp.zeros(index.shape, F32)
    for t in reversed(targets):
        hit = jnp.where(index == t, 1.0, hit)
    return hit.astype(BF16)


def _for_each_group(block, len_ref, body):
    def step(e, carry):
        n = pl.multiple_of(len_ref[block, e], GROUP)

        @pl.when(n > 0)
        def _():
            body(e, n)

        return carry

    lax.fori_loop(0, N_EXPERTS, step, 0)


def _group(start, n):
    return pl.ds(pl.multiple_of(start, GROUP), n)


def _dispatch_kernel(src_ref, len_ref, dst_ref, tot_ref, tail_s_ref, tail_n_ref, na_ref,
                     lpos_ref, h_ref, r_ref, xs_ref, stage_ref, zero_ref, sems):
    i = pl.program_id(0)
    n_blocks = pl.num_programs(0) - 1
    slot = i % 2

    @pl.when(i < n_blocks)
    def _():
        lpos = lpos_ref[...]
        targets = [lpos[k:k + 1, :] for k in range(TOP_K)]
        h = h_ref[...].astype(BF16)
        r = r_ref[...].astype(BF16)

        def permute(jb, carry):
            base = pl.multiple_of(jb * KB, KB)
            rows = lax.broadcasted_iota(jnp.int32, (KB, TR), 0) + base
            sel = _one_hot_rows(targets, rows)
            stage_ref[slot, pl.ds(base, KB), 0:D_MODEL] = jnp.dot(
                sel, h, preferred_element_type=F32)
            stage_ref[slot, pl.ds(base, KB), D_MODEL:D_AUG] = jnp.dot(
                sel, r, preferred_element_type=F32)
            return carry

        lax.fori_loop(0, (tot_ref[i] + KB - 1) // KB, permute, 0)
        _for_each_group(i, len_ref, lambda e, n: pltpu.make_async_copy(
            stage_ref.at[slot, _group(src_ref[i, e], n)],
            xs_ref.at[_group(dst_ref[i, e], n)], sems.at[slot]).start())

    @pl.when(i > 0)
    def _():
        rows = _group(0, pl.multiple_of(tot_ref[i - 1], GROUP))
        pltpu.make_async_copy(stage_ref.at[1 - slot, rows], xs_ref.at[rows],
                              sems.at[1 - slot]).wait()

    @pl.when(i == n_blocks)
    def _():
        zero_ref[...] = jnp.zeros_like(zero_ref)
        fill = sems.at[2]

        def expert_tail(e, n):
            return pltpu.make_async_copy(zero_ref.at[_group(0, n)],
                                         xs_ref.at[_group(tail_s_ref[0, e], n)], fill)

        def whole_tile(t):
            return pltpu.make_async_copy(zero_ref, xs_ref.at[_group(t * TM, TM)], fill)

        n_tiles = xs_ref.shape[0] // TM
        _for_each_group(0, tail_n_ref, lambda e, n: expert_tail(e, n).start())
        lax.fori_loop(na_ref[0], n_tiles, lambda t, c: (whole_tile(t).start(), c)[1], 0)
        _for_each_group(0, tail_n_ref, lambda e, n: expert_tail(e, n).wait())
        lax.fori_loop(na_ref[0], n_tiles, lambda t, c: (whole_tile(t).wait(), c)[1], 0)


def _dispatch(plan, h2, route):
    n = h2.shape[0] // TR
    last = n - 1
    grid_spec = pltpu.PrefetchScalarGridSpec(
        num_scalar_prefetch=7,
        grid=(n + 1,),
        in_specs=[
            pl.BlockSpec((None, SUBLANES, TR), lambda i, *_: (jnp.minimum(i, last), 0, 0)),
            pl.BlockSpec((TR, D_MODEL), lambda i, *_: (jnp.minimum(i, last), 0)),
            pl.BlockSpec((TR, LANES), lambda i, *_: (jnp.minimum(i, last), 0)),
        ],
        out_specs=pl.BlockSpec(memory_space=pl.ANY),
        scratch_shapes=[pltpu.VMEM((2, RB, D_AUG), F32),
                        pltpu.VMEM((TM, D_AUG), F32),
                        pltpu.SemaphoreType.DMA((3,))],
    )
    return pl.pallas_call(
        _dispatch_kernel,
        grid_spec=grid_spec,
        out_shape=jax.ShapeDtypeStruct((plan["n_sorted"], D_AUG), F32),
        compiler_params=_cparams(("arbitrary",)),
        name="moe_dispatch",
    )(plan["src_row"], plan["n_rows"], plan["dst_row"], plan["block_rows"],
      plan["tail_start"][None], plan["tail_len"][None], plan["n_active"], plan["lpos_t"],
      h2, route)


def _expert_kernel(layer, te_ref, na_ref, first_ref, nxt_ref, slot_ref,
                   x_ref, b1_ref, b2_ref, w1_hbm, w2_hbm, o_ref,
                   w1buf_ref, w2buf_ref, w1c_ref, w2c_ref, sems):
    i = pl.program_id(0)
    e = te_ref[i]
    slot = slot_ref[i]

    def fetch(expert, s):
        return (pltpu.make_async_copy(w1_hbm.at[layer, expert], w1buf_ref.at[s], sems.at[s, 0]),
                pltpu.make_async_copy(w2_hbm.at[layer, expert], w2buf_ref.at[s], sems.at[s, 1]))

    @pl.when(i == 0)
    def _():
        for copy in fetch(e, slot):
            copy.start()

    @pl.when(first_ref[i] == 1)
    def _():
        for copy in fetch(e, slot):
            copy.wait()
        w1c_ref[...] = w1buf_ref[slot].astype(BF16)
        w2c_ref[...] = w2buf_ref[slot].astype(BF16)

        @pl.when(nxt_ref[i] >= 0)
        def _():
            for copy in fetch(nxt_ref[i], 1 - slot):
                copy.start()

    @pl.when(i < na_ref[0])
    def _():
        aug = x_ref[:, D_MODEL:D_AUG]
        e_f = e.astype(F32)
        gate = jnp.zeros((TM, 1), F32)
        for k in range(TOP_K):
            p = (aug[:, LANE_P_HI + k:LANE_P_HI + k + 1]
                 + aug[:, LANE_P_MID + k:LANE_P_MID + k + 1]
                 + aug[:, LANE_P_LO + k:LANE_P_LO + k + 1])
            gate = gate + jnp.where(aug[:, LANE_EID + k:LANE_EID + k + 1] == e_f, p, 0.0)
        x = x_ref[:, 0:D_MODEL].astype(BF16)
        y = jnp.broadcast_to(b2_ref[...], (TM, D_MODEL))
        for c in range(D_EXPERT // FFN_CHUNK):
            g_cols = slice(c * FFN_CHUNK, (c + 1) * FFN_CHUNK)
            l_cols = slice(D_EXPERT + c * FFN_CHUNK, D_EXPERT + (c + 1) * FFN_CHUNK)
            glu = jnp.dot(x, w1c_ref[:, g_cols], preferred_element_type=F32) + b1_ref[:, g_cols]
            lin = jnp.dot(x, w1c_ref[:, l_cols], preferred_element_type=F32) + b1_ref[:, l_cols]
            glu = jnp.minimum(glu, SWIGLU_LIMIT)
            lin = jnp.clip(lin, -SWIGLU_LIMIT, SWIGLU_LIMIT)
            act = glu * _sigmoid(SWIGLU_ALPHA * glu) * (lin + 1.0)
            y = y + jnp.dot(act.astype(BF16), w2c_ref[g_cols, :], preferred_element_type=F32)
        o_ref[...] = gate * y

    @pl.when(i >= na_ref[0])
    def _():
        o_ref[...] = jnp.zeros_like(o_ref)


def _experts(layer, plan, xs, w1, b1, w2, b2):
    n_tiles = xs.shape[0] // TM
    act_tile = lambda i, te, na, *_: (jnp.maximum(jnp.minimum(i, na[0] - 1), 0), 0)
    expert = lambda i, te, *_: (layer, te[i], 0, 0)
    grid_spec = pltpu.PrefetchScalarGridSpec(
        num_scalar_prefetch=5,
        grid=(n_tiles,),
        in_specs=[
            pl.BlockSpec((TM, D_AUG), act_tile),
            pl.BlockSpec((None, None, 1, 2 * D_EXPERT), expert),
            pl.BlockSpec((None, None, 1, D_MODEL), expert),
            pl.BlockSpec(memory_space=pl.ANY),
            pl.BlockSpec(memory_space=pl.ANY),
        ],
        out_specs=pl.BlockSpec((TM, D_MODEL), lambda i, *_: (i, 0)),
        scratch_shapes=[pltpu.VMEM((2, D_MODEL, 2 * D_EXPERT), F32),
                        pltpu.VMEM((2, D_EXPERT, D_MODEL), F32),
                        pltpu.VMEM((D_MODEL, 2 * D_EXPERT), BF16),
                        pltpu.VMEM((D_EXPERT, D_MODEL), BF16),
                        pltpu.SemaphoreType.DMA((2, 2))],
    )
    return pl.pallas_call(
        functools.partial(_expert_kernel, layer),
        grid_spec=grid_spec,
        out_shape=jax.ShapeDtypeStruct((xs.shape[0], D_MODEL), F32),
        compiler_params=_cparams(("arbitrary",)),
        name="moe_experts",
    )(plan["tile_expert"], plan["n_active"], plan["tile_first"], plan["tile_next"],
      plan["tile_slot"], xs, b1.reshape(DEPTH, N_EXPERTS, 1, -1),
      b2.reshape(DEPTH, N_EXPERTS, 1, -1), w1, w2)


def _combine_kernel(src_ref, len_ref, dst_ref, tot_ref, lpos_ref, x_ref, g2_ref, lg_ref,
                    lb_ref, ys_ref, o_ref, ybuf_ref, sems):
    i = pl.program_id(0)
    n_blocks = pl.num_programs(0)
    slot = i % 2

    def start_block(block, s):
        _for_each_group(block, len_ref, lambda e, n: pltpu.make_async_copy(
            ys_ref.at[_group(dst_ref[block, e], n)],
            ybuf_ref.at[s, _group(src_ref[block, e], n)], sems.at[s]).start())

    @pl.when(i == 0)
    def _():
        ybuf_ref[...] = jnp.zeros_like(ybuf_ref)
        start_block(0, 0)

    @pl.when(i + 1 < n_blocks)
    def _():
        start_block(i + 1, 1 - slot)

    rows = _group(0, pl.multiple_of(tot_ref[i], GROUP))
    pltpu.make_async_copy(ys_ref.at[rows], ybuf_ref.at[slot, rows], sems.at[slot]).wait()
    lpos = lpos_ref[...]
    targets = [jnp.broadcast_to(lpos[:, k:k + 1], (TR, KB)) for k in range(TOP_K)]
    cols = lax.broadcasted_iota(jnp.int32, (TR, KB), 1)
    f = jnp.zeros((TR, D_MODEL), F32)
    for jb in range(RB // KB):
        sel = _one_hot_rows(targets, cols + jb * KB)
        f = f + jnp.dot(sel, ybuf_ref[slot, jb * KB:(jb + 1) * KB, :].astype(BF16),
                        preferred_element_type=F32)
    x = ALPHA * x_ref[...] + _per_batch(f, g2_ref[0], jnp.multiply)
    o_ref[...] = _layer_norm(x) * lg_ref[...] + lb_ref[...]


def _combine(t_off, plan, x, mod, ln_g, ln_b, ys):
    n = x.shape[0] // TR
    row = lambda w: pl.BlockSpec((TR, w), lambda i, *_: (i, 0))
    const = lambda shape: pl.BlockSpec(shape, lambda i, *_: (0,) * len(shape))
    grid_spec = pltpu.PrefetchScalarGridSpec(
        num_scalar_prefetch=4,
        grid=(n,),
        in_specs=[
            row(TOP_K), row(D_MODEL),
            pl.BlockSpec((1, BATCH, D_MODEL),
                         lambda i, *_: ((i + t_off >= CTX_TILES).astype(jnp.int32), 0, 5)),
            const((1, D_MODEL)), const((1, D_MODEL)),
            pl.BlockSpec(memory_space=pl.ANY),
        ],
        out_specs=row(D_MODEL),
        scratch_shapes=[pltpu.VMEM((2, RB, D_MODEL), F32),
                        pltpu.SemaphoreType.DMA((2,))],
    )
    return pl.pallas_call(
        _combine_kernel,
        grid_spec=grid_spec,
        out_shape=jax.ShapeDtypeStruct((n * TR, D_MODEL), F32),
        compiler_params=_cparams(("arbitrary",)),
        name="moe_combine",
    )(plan["src_row"], plan["n_rows"], plan["dst_row"], plan["block_rows"], plan["lpos"],
      x, mod, ln_g, ln_b, ys)


def _routing_plan(route, counts):
    i32 = jnp.int32
    nb = counts.shape[0]
    experts = jnp.arange(N_EXPERTS, dtype=i32)
    eid = route[:, LANE_EID:LANE_EID + TOP_K].astype(i32).reshape(nb, TR, TOP_K)
    rank = route[:, LANE_RANK:LANE_RANK + TOP_K].astype(i32).reshape(nb, TR, TOP_K)
    n = counts[:, 0, :N_EXPERTS].astype(i32)
    n = (n + GROUP - 1) // GROUP * GROUP
    src_row = jnp.cumsum(n, axis=1) - n
    seg = jnp.sum(n, axis=0)
    tiles_e = (seg + TM - 1) // TM
    tile_end = jnp.cumsum(tiles_e)
    e_start = (tile_end - tiles_e) * TM
    n_active = tile_end[-1]
    dst_row = e_start[None, :] + jnp.cumsum(n, axis=0) - n

    onehot = eid[..., None] == experts
    lpos = jnp.sum(jnp.where(onehot, src_row[:, None, None, :], 0), axis=-1) + rank
    lpos_t = jnp.concatenate(
        [lpos.transpose(0, 2, 1), jnp.full((nb, SUBLANES - TOP_K, TR), -1, i32)], axis=1)

    n_tiles = (nb * RB + TM - 1) // TM + N_EXPERTS
    t = jnp.minimum(jnp.arange(n_tiles, dtype=i32), n_active - 1)
    tile_expert = jnp.sum((t[:, None] >= tile_end[None, :]).astype(i32), axis=1)
    tile_expert = jnp.minimum(tile_expert, N_EXPERTS - 1)
    tile_first = jnp.concatenate(
        [jnp.ones((1,), i32), (tile_expert[1:] != tile_expert[:-1]).astype(i32)])
    has_rows = tiles_e > 0
    later = jnp.where(has_rows[None, :] & (experts[None, :] > experts[:, None]),
                      experts[None, :], N_EXPERTS)
    next_e = jnp.min(later, axis=1)
    next_e = jnp.where(next_e == N_EXPERTS, -1, next_e)
    group = jnp.cumsum(has_rows.astype(i32)) - 1
    pick = tile_expert[:, None] == experts[None, :]
    tile_next = jnp.sum(jnp.where(pick, next_e[None, :], 0), axis=1)
    tile_slot = jnp.sum(jnp.where(pick, group[None, :], 0), axis=1) % 2
    return dict(lpos=lpos.reshape(nb * TR, TOP_K), lpos_t=lpos_t, src_row=src_row, n_rows=n,
                dst_row=dst_row, block_rows=jnp.sum(n, axis=1), tail_start=e_start + seg,
                tail_len=tile_end * TM - e_start - seg,
                tile_expert=tile_expert, n_active=n_active.reshape(1), tile_first=tile_first,
                tile_next=tile_next, tile_slot=tile_slot, n_sorted=n_tiles * TM)


def _grid_sincos_parts():
    quarter = D_MODEL // 4
    omega = 1.0 / (10000.0 ** (jnp.arange(quarter, dtype=F32) / quarter))

    def emb1d(n):
        ang = jnp.arange(n, dtype=F32)[:, None] * omega[None, :]
        return jnp.concatenate([jnp.sin(ang), jnp.cos(ang)], axis=-1)

    er = emb1d(SEQ // GRID_W)
    er = jnp.concatenate([jnp.zeros((CTX_TILES, D_MODEL // 2), F32), er], axis=0)
    ec = jnp.repeat(emb1d(GRID_W), BATCH, axis=0)
    return er.reshape(N_TILES, 1, D_MODEL // 2), ec


def _gate_blocks(w):
    per = GATE_BLK // LRU_HEAD_DIM
    w = w.reshape(2, D_LRU // GATE_BLK, per, LRU_HEAD_DIM, LRU_HEAD_DIM)
    eye = jnp.eye(per, dtype=w.dtype)
    blk = jnp.einsum('dkpij,pq->dkpiqj', w, eye)
    return blk.reshape(2, D_LRU // GATE_BLK, GATE_BLK, GATE_BLK)


def kernel(x, c, ctx, c_ctx, w_mod, b_mod, w_in, conv_w, conv_b, gate_a_w, gate_a_b,
           gate_x_w, gate_x_b, lru_lambda, pool_w, pool_b, pool_scale, w_out, ln1_g, ln1_b,
           router_w, router_b, exp_w1, exp_b1, exp_w2, exp_b2, ln2_g, ln2_b):
    x_tm = jnp.concatenate([ctx.transpose(1, 0, 2), x.transpose(1, 0, 2)], axis=0)
    x_tm = x_tm.reshape(R_ALL, D_MODEL)
    er, ec = _grid_sincos_parts()
    xs = _entry(x_tm, er, ec)

    cvec = jnp.concatenate([c, c_ctx[None], jnp.zeros((2 * SUBLANES - BATCH - 1, D_MODEL), F32)])
    mod_all = _modulation(cvec, w_mod, b_mod)
    tri = jnp.tril(jnp.ones((TR, TR), F32), -1).astype(BF16)

    for l in range(DEPTH):
        last = l == DEPTH - 1
        t_off = CTX_TILES if last else 0
        mod = jnp.stack([jnp.broadcast_to(mod_all[l, BATCH], (BATCH, 6 * D_MODEL)),
                         mod_all[l, :BATCH]])
        z = _inproj(xs, mod, w_in[l].astype(BF16))
        gate_w = jnp.stack([_gate_blocks(gate_a_w[l]), _gate_blocks(gate_x_w[l])],
                           axis=1).astype(BF16)
        gate_b = jnp.stack([gate_a_b[l], gate_x_b[l]], axis=1)
        hf, hr = _scan(z, conv_w[l], conv_b[l][None], gate_w, gate_b, lru_lambda[l])
        rw_pad = jnp.zeros((D_MODEL, LANES), F32).at[:, :N_EXPERTS].set(router_w[l])
        rw_hi = rw_pad.astype(BF16)
        rw_pad = jnp.stack([rw_hi, (rw_pad - rw_hi.astype(F32)).astype(BF16)])
        rb_pad = jnp.full((1, LANES), -1e30, F32).at[0, :N_EXPERTS].set(router_b[l])
        xs, h2, route, counts = _mixer(
            t_off, hf, hr, z, xs, mod, pool_w[l].astype(BF16), pool_b[l][None],
            pool_scale[l][None], w_out[l].astype(BF16), ln1_g[l][None], ln1_b[l][None],
            rw_pad, rb_pad, tri)
        plan = _routing_plan(route, counts)
        x_sorted = _dispatch(plan, h2, route)
        y_sorted = _experts(l, plan, x_sorted, exp_w1, exp_b1, exp_w2, exp_b2)
        xs = _combine(t_off, plan, xs, mod, ln2_g[l][None], ln2_b[l][None], y_sorted)

    out = xs.reshape(SEQ, BATCH, D_MODEL)
    return out.transpose(1, 0, 2)
```

```python
import functools

import jax
import jax.numpy as jnp
from jax import lax
from jax.experimental import pallas as pl
from jax.experimental.pallas import tpu as pltpu

D_MODEL = 1024
BATCH = 8
SEQ = 2048
DEPTH = 2
GRID_W = 64
CTX_LEN = 256
D_LRU = 512
N_LRU_HEADS = 8
LRU_HEAD_DIM = D_LRU // N_LRU_HEADS
LRU_C = 8.0
D_POOL = 512
POOL_WINDOWS = (2, 4, 8, 16)
POOL_GROUP_DIM = D_POOL // len(POOL_WINDOWS)
D_IN = 2 * D_LRU + D_POOL
N_EXPERTS = 32
TOP_K = 4
D_EXPERT = D_MODEL
SWIGLU_LIMIT = 7.0
SWIGLU_ALPHA = 1.702
LN_EPS = 1e-5
ALPHA = (2.0 * DEPTH) ** 0.25

F32 = jnp.float32
BF16 = jnp.bfloat16

SUBLANES = 8
LANES = 128
TT = GRID_W
TR = TT * BATCH
T_ALL = CTX_LEN + SEQ
R_ALL = T_ALL * BATCH
N_TILES = T_ALL // TT
CTX_TILES = CTX_LEN // TT
GATE_BLK = 256
POOL_HALO = max(POOL_WINDOWS) // 2
TM = 512
FFN_CHUNK = 512
GROUP = SUBLANES
RB = TOP_K * TR + N_EXPERTS * GROUP
KB = 256
MB = 512
RB_STAGE = (RB + MB - 1) // MB * MB
D_AUG = D_MODEL + LANES
LANE_P_HI, LANE_P_MID, LANE_P_LO, LANE_EID, LANE_RANK = 0, 4, 8, 12, 16
VMEM_LIMIT = 56 * 1024 * 1024


def _cparams(sem):
    return pltpu.CompilerParams(dimension_semantics=sem, vmem_limit_bytes=VMEM_LIMIT)


def _sigmoid(x):
    return 0.5 * (1.0 + jnp.tanh(0.5 * x))


def _layer_norm(x):
    mu = jnp.mean(x, axis=-1, keepdims=True)
    xc = x - mu
    var = jnp.mean(xc * xc, axis=-1, keepdims=True)
    return xc * lax.rsqrt(var + LN_EPS)


def _per_batch(x, v, op):
    r, d = x.shape
    return op(x.reshape(r // BATCH, BATCH, d), v[None]).reshape(r, d)


def _mod_kernel(c_ref, w_ref, b_ref, o_ref):
    c = c_ref[...]
    s = c * _sigmoid(c)
    o_ref[...] = jnp.dot(s, w_ref[...], precision=lax.Precision.HIGHEST,
                         preferred_element_type=F32) + b_ref[...]


def _modulation(cvec, w_mod, b_mod):
    tn = 512
    return pl.pallas_call(
        _mod_kernel,
        grid=(DEPTH, 6 * D_MODEL // tn),
        in_specs=[
            pl.BlockSpec((2 * SUBLANES, D_MODEL), lambda l, j: (0, 0)),
            pl.BlockSpec((None, D_MODEL, tn), lambda l, j: (l, 0, j)),
            pl.BlockSpec((None, 1, tn), lambda l, j: (l, 0, j)),
        ],
        out_specs=pl.BlockSpec((None, 2 * SUBLANES, tn), lambda l, j: (l, 0, j)),
        out_shape=jax.ShapeDtypeStruct((DEPTH, 2 * SUBLANES, 6 * D_MODEL), F32),
        compiler_params=_cparams(("arbitrary", "arbitrary")),
        name="modulation",
    )(cvec, w_mod, b_mod.reshape(DEPTH, 1, 6 * D_MODEL))


def _entry_kernel(ctx_ref, x_ref, er_ref, ec_ref, o_ref):
    i = pl.program_id(0)

    def time_major(src_ref):
        return jnp.transpose(src_ref[...], (1, 0, 2)).reshape(TR, D_MODEL)

    @pl.when(i < CTX_TILES)
    def _():
        o_ref[...] = _layer_norm(time_major(ctx_ref))

    @pl.when(i >= CTX_TILES)
    def _():
        pos = jnp.concatenate(
            [jnp.broadcast_to(er_ref[0], (TR, D_MODEL // 2)), ec_ref[...]], axis=1)
        o_ref[...] = _layer_norm(time_major(x_ref) + pos)


def _entry(ctx, x, er, ec):
    return pl.pallas_call(
        _entry_kernel,
        grid=(N_TILES,),
        in_specs=[
            pl.BlockSpec((BATCH, TT, D_MODEL), lambda i: (0, jnp.minimum(i, CTX_TILES - 1), 0)),
            pl.BlockSpec((BATCH, TT, D_MODEL), lambda i: (0, jnp.maximum(i - CTX_TILES, 0), 0)),
            pl.BlockSpec((1, 1, D_MODEL // 2), lambda i: (jnp.maximum(i - CTX_TILES, 0), 0, 0)),
            pl.BlockSpec((TR, D_MODEL // 2), lambda i: (0, 0)),
        ],
        out_specs=pl.BlockSpec((TR, D_MODEL), lambda i: (i, 0)),
        out_shape=jax.ShapeDtypeStruct((R_ALL, D_MODEL), F32),
        compiler_params=_cparams(("arbitrary",)),
        name="entry_ln",
    )(ctx, x, er, ec)


def _inproj_kernel(x_ref, sh_ref, sc_ref, w_ref, z_ref):
    h = _per_batch(x_ref[...], 1.0 + sc_ref[0], jnp.multiply)
    h = _per_batch(h, sh_ref[0], jnp.add)
    z_ref[...] = jnp.dot(h.astype(BF16), w_ref[...], preferred_element_type=F32)


def _mod_spec(chunk):
    return pl.BlockSpec((1, BATCH, D_MODEL),
                        lambda i: ((i >= CTX_TILES).astype(jnp.int32), 0, chunk))


def _inproj(x, mod, w_in_bf):
    return pl.pallas_call(
        _inproj_kernel,
        grid=(N_TILES,),
        in_specs=[
            pl.BlockSpec((TR, D_MODEL), lambda i: (i, 0)),
            _mod_spec(0), _mod_spec(1),
            pl.BlockSpec((D_MODEL, D_IN), lambda i: (0, 0)),
        ],
        out_specs=pl.BlockSpec((TR, D_IN), lambda i: (i, 0)),
        out_shape=jax.ShapeDtypeStruct((R_ALL, D_IN), F32),
        compiler_params=_cparams(("arbitrary",)),
        name="in_proj",
    )(x, mod, mod, w_in_bf)


def _rev_tile(i):
    return jnp.where(i < CTX_TILES, CTX_TILES - 1 - i, N_TILES - 1 + CTX_TILES - i)


def _block_diag_dot(u_bf, w_ref, d, g):
    halves = [jnp.dot(u_bf[:, k * GATE_BLK:(k + 1) * GATE_BLK], w_ref[d, g, k],
                      preferred_element_type=F32) for k in range(D_LRU // GATE_BLK)]
    return jnp.concatenate(halves, axis=1)


def _lru_coeffs(tile, z_ref, zp_ref, zn_ref, cw_ref, cb_ref, gw_ref, gb_ref, lam_ref, d):
    x = z_ref[...]
    seg_first = (tile == 0) | (tile == CTX_TILES)
    seg_last = (tile == CTX_TILES - 1) | (tile == N_TILES - 1)
    prev = jnp.where(seg_first, 0.0, zp_ref[...])
    nxt = jnp.where(seg_last, 0.0, zn_ref[...])
    xm2 = jnp.concatenate([prev, x[:-2 * BATCH]], axis=0)
    xm1 = jnp.concatenate([prev[BATCH:], x[:-BATCH]], axis=0)
    xp1 = jnp.concatenate([x[BATCH:], nxt], axis=0)
    u = (cb_ref[...] + xm2 * cw_ref[0:1] + xm1 * cw_ref[1:2]
         + x * cw_ref[2:3] + xp1 * cw_ref[3:4])
    u_bf = u.astype(BF16)
    r = _sigmoid(_block_diag_dot(u_bf, gw_ref, d, 0) + gb_ref[d, 0:1])
    ig = _sigmoid(_block_diag_dot(u_bf, gw_ref, d, 1) + gb_ref[d, 1:2])
    nl = -lam_ref[d:d + 1]
    softplus = jnp.maximum(nl, 0.0) + jnp.log(1.0 + jnp.exp(-jnp.abs(nl)))
    log_a = (-LRU_C) * r * softplus
    a = jnp.exp(log_a)
    mult = jnp.sqrt(1.0 - a * a)
    return a, mult * (ig * u)


def _scan_kernel(zf_ref, zfp_ref, zfn_ref, zr_ref, zrp_ref, zrn_ref,
                 cw_ref, cb_ref, gw_ref, gb_ref, lam_ref,
                 hf_ref, hr_ref, state_ref):
    i = pl.program_id(0)

    @pl.when(i == 0)
    def _():
        state_ref[...] = jnp.zeros_like(state_ref)

    af, bf = _lru_coeffs(i, zf_ref, zfp_ref, zfn_ref, cw_ref, cb_ref, gw_ref, gb_ref,
                         lam_ref, 0)
    ar, br = _lru_coeffs(_rev_tile(i), zr_ref, zrp_ref, zrn_ref, cw_ref, cb_ref, gw_ref,
                         gb_ref, lam_ref, 1)
    hf = state_ref[0]
    hr = state_ref[1]
    for s in range(TT):
        f = slice(s * BATCH, (s + 1) * BATCH)
        hf = af[f] * hf + bf[f]
        hf_ref[f, :] = hf
        b = slice((TT - 1 - s) * BATCH, (TT - s) * BATCH)
        hr = ar[b] * hr + br[b]
        hr_ref[b, :] = hr
    state_ref[0] = hf
    state_ref[1] = hr


def _scan(z, conv_w, conv_b, gate_w, gate_b, lam):
    prev_rows = 2 * BATCH
    tiles_per_prev = TR // prev_rows
    tiles_per_next = TR // BATCH
    last_next = R_ALL // BATCH - 1

    def cur(f):
        return pl.BlockSpec((TR, D_LRU), lambda i: (f(i), 0))

    def prev(f):
        return pl.BlockSpec((prev_rows, D_LRU),
                            lambda i: (jnp.maximum(f(i) * tiles_per_prev - 1, 0), 0))

    def nxt(f):
        return pl.BlockSpec((BATCH, D_LRU),
                            lambda i: (jnp.minimum((f(i) + 1) * tiles_per_next, last_next), 0))

    fwd = lambda i: i
    const = lambda shape: pl.BlockSpec(shape, lambda i: (0,) * len(shape))
    return pl.pallas_call(
        _scan_kernel,
        grid=(N_TILES,),
        in_specs=[cur(fwd), prev(fwd), nxt(fwd), cur(_rev_tile), prev(_rev_tile), nxt(_rev_tile),
                  const((4, D_LRU)), const((1, D_LRU)),
                  const((2, 2, D_LRU // GATE_BLK, GATE_BLK, GATE_BLK)),
                  const((2, 2, D_LRU)), const((2, D_LRU))],
        out_specs=[pl.BlockSpec((TR, D_LRU), lambda i: (i, 0)),
                   pl.BlockSpec((TR, D_LRU), lambda i: (_rev_tile(i), 0))],
        out_shape=[jax.ShapeDtypeStruct((R_ALL, D_LRU), F32)] * 2,
        scratch_shapes=[pltpu.VMEM((2, BATCH, D_LRU), F32)],
        compiler_params=_cparams(("arbitrary",)),
        name="lru_scan",
    )(z, z, z, z, z, z, conv_w, conv_b, gate_w, gate_b, lam)


def _gelu_tanh(y):
    c = 0.7978845608028654
    return 0.5 * y * (1.0 + jnp.tanh(c * (y + 0.044715 * (y * y * y))))


def _pool_groups(tile, xp_ref, xpp_ref, xpn_ref):
    in_ctx = tile < CTX_TILES
    prev_ok = in_ctx & (tile > 0)
    next_ok = in_ctx & (tile < CTX_TILES - 1)
    x = xp_ref[...]
    p = jnp.concatenate([jnp.where(prev_ok, xpp_ref[...], 0.0), x,
                         jnp.where(next_ok, xpn_ref[...], 0.0)], axis=0)
    p = p.reshape(TT + 2 * POOL_HALO, BATCH, D_POOL)
    lo = jnp.where(in_ctx, -TT * tile, 0)
    hi = jnp.where(in_ctx, CTX_LEN - TT * tile, TT)
    t = lax.broadcasted_iota(jnp.int32, (TT, BATCH, POOL_GROUP_DIM), 0)
    outs = []
    for g, win in enumerate(POOL_WINDOWS):
        half = win // 2
        acc = p[:, :, g * POOL_GROUP_DIM:(g + 1) * POOL_GROUP_DIM]
        width = 1
        while width < win:
            acc = acc[:acc.shape[0] - width] + acc[width:]
            width *= 2
        start = POOL_HALO - half
        wsum = acc[start:start + TT]
        cnt = (jnp.minimum(t + half, hi) - jnp.maximum(t - half, lo)).astype(F32)
        centre = p[POOL_HALO:POOL_HALO + TT, :, g * POOL_GROUP_DIM:(g + 1) * POOL_GROUP_DIM]
        outs.append((wsum / cnt - centre).reshape(TR, POOL_GROUP_DIM))
    return outs


def _mixer_kernel(t_off, hf_ref, hr_ref, y_ref, xp_ref, xpp_ref, xpn_ref, x_ref,
                  g1_ref, sh2_ref, sc2_ref, pw_ref, pb_ref, ps_ref, wo_ref,
                  lg_ref, lb_ref, rw_ref, rb_ref, tri_ref,
                  xo_ref, h2_ref, route_ref, cnt_ref):
    i = pl.program_id(0)
    tile = i + t_off

    lru = (hf_ref[...] + hr_ref[...]) * _gelu_tanh(y_ref[...])
    diffs = _pool_groups(tile, xp_ref, xpp_ref, xpn_ref)
    pooled = jnp.concatenate(
        [jnp.dot(d.astype(BF16), pw_ref[g], preferred_element_type=F32)
         for g, d in enumerate(diffs)], axis=1)
    pooled = (pooled + pb_ref[...]) * ps_ref[...]
    mix = (jnp.dot(lru.astype(BF16), wo_ref[0:D_LRU, :], preferred_element_type=F32)
           + jnp.dot(pooled.astype(BF16), wo_ref[D_LRU:, :], preferred_element_type=F32))
    x = ALPHA * x_ref[...] + _per_batch(mix, g1_ref[0], jnp.multiply)
    x = _layer_norm(x) * lg_ref[...] + lb_ref[...]
    xo_ref[...] = x
    h2 = _per_batch(x, 1.0 + sc2_ref[0], jnp.multiply)
    h2 = _per_batch(h2, sh2_ref[0], jnp.add)
    h2_ref[...] = h2

    h_hi = h2.astype(BF16)
    h_lo = (h2 - h_hi.astype(F32)).astype(BF16)
    logits = (jnp.dot(h_hi, rw_ref[0], preferred_element_type=F32)
              + jnp.dot(h_lo, rw_ref[0], preferred_element_type=F32)
              + jnp.dot(h_hi, rw_ref[1], preferred_element_type=F32)) + rb_ref[...]
    lane = lax.broadcasted_iota(jnp.int32, (TR, LANES), 1).astype(F32)
    work = logits
    vals, idxs, sels = [], [], []
    for _ in range(TOP_K):
        m = jnp.max(work, axis=1, keepdims=True)
        idx = jnp.min(jnp.where(work == m, lane, float(LANES)), axis=1, keepdims=True)
        sel = lane == idx
        vals.append(m)
        idxs.append(idx)
        sels.append(sel)
        work = jnp.where(sel, -jnp.inf, work)
    exps = [jnp.exp(v - vals[0]) for v in vals]
    denom = exps[0] + exps[1] + exps[2] + exps[3]
    chosen = jnp.zeros((TR, LANES), F32)
    for sel in sels:
        chosen = chosen + sel.astype(F32)
    before = jnp.dot(tri_ref[...], chosen.astype(BF16), preferred_element_type=F32)
    route = jnp.zeros((TR, LANES), F32)
    for k in range(TOP_K):
        rank = jnp.sum(jnp.where(sels[k], before, 0.0), axis=1, keepdims=True)
        p = exps[k] / denom
        p_hi = p.astype(BF16).astype(F32)
        p_mid = (p - p_hi).astype(BF16).astype(F32)
        p_lo = p - p_hi - p_mid
        for base, val in ((LANE_P_HI, p_hi), (LANE_P_MID, p_mid), (LANE_P_LO, p_lo),
                          (LANE_EID, idxs[k]), (LANE_RANK, rank)):
            route = jnp.where(lane == float(base + k), val, route)
    route_ref[...] = route
    cnt_ref[0] = jnp.broadcast_to(jnp.sum(chosen, axis=0, keepdims=True), (SUBLANES, LANES))


def _mixer(t_off, hf, hr, z, x, mod, pool_w_bf, pool_b, pool_scale, w_out_bf,
           ln_g, ln_b, rw_pad, rb_pad, tri):
    n = N_TILES - t_off
    halo_rows = POOL_HALO * BATCH
    per = TR // halo_rows
    last_halo = R_ALL // halo_rows - 1
    xp_col = 2 * D_LRU // D_POOL
    row = lambda w: pl.BlockSpec((TR, w), lambda i: (i + t_off, 0))
    out_row = lambda w: pl.BlockSpec((TR, w), lambda i: (i, 0))
    const = lambda shape: pl.BlockSpec(shape, lambda i: (0,) * len(shape))

    def mod_spec(chunk):
        return pl.BlockSpec((1, BATCH, D_MODEL),
                            lambda i: ((i + t_off >= CTX_TILES).astype(jnp.int32), 0, chunk))

    return pl.pallas_call(
        functools.partial(_mixer_kernel, t_off),
        grid=(n,),
        in_specs=[
            row(D_LRU), row(D_LRU),
            pl.BlockSpec((TR, D_LRU), lambda i: (i + t_off, 1)),
            pl.BlockSpec((TR, D_POOL), lambda i: (i + t_off, xp_col)),
            pl.BlockSpec((halo_rows, D_POOL),
                         lambda i: (jnp.maximum((i + t_off) * per - 1, 0), xp_col)),
            pl.BlockSpec((halo_rows, D_POOL),
                         lambda i: (jnp.minimum((i + t_off + 1) * per, last_halo), xp_col)),
            row(D_MODEL),
            mod_spec(2), mod_spec(3), mod_spec(4),
            const((len(POOL_WINDOWS), POOL_GROUP_DIM, POOL_GROUP_DIM)),
            const((1, D_POOL)), const((1, D_POOL)),
            const((D_MODEL, D_MODEL)),
            const((1, D_MODEL)), const((1, D_MODEL)),
            const((2, D_MODEL, LANES)), const((1, LANES)),
            const((TR, TR)),
        ],
        out_specs=[out_row(D_MODEL), out_row(D_MODEL), out_row(LANES),
                   pl.BlockSpec((1, SUBLANES, LANES), lambda i: (i, 0, 0))],
        out_shape=[jax.ShapeDtypeStruct((n * TR, D_MODEL), F32),
                   jax.ShapeDtypeStruct((n * TR, D_MODEL), F32),
                   jax.ShapeDtypeStruct((n * TR, LANES), F32),
                   jax.ShapeDtypeStruct((n, SUBLANES, LANES), F32)],
        compiler_params=_cparams(("arbitrary",)),
        name="mixer_out",
    )(hf, hr, z, z, z, z, x, mod, mod, mod, pool_w_bf, pool_b, pool_scale, w_out_bf,
      ln_g, ln_b, rw_pad, rb_pad, tri)


def _one_hot_rows(targets, index):
    hit = jnp.zeros(index.shape, F32)
    for t in reversed(targets):
        hit = jnp.where(index == t, 1.0, hit)
    return hit.astype(BF16)


def _for_each_group(block, len_ref, body):
    def step(e, carry):
        n = pl.multiple_of(len_ref[block, e], GROUP)

        @pl.when(n > 0)
        def _():
            body(e, n)

        return carry

    lax.fori_loop(0, N_EXPERTS, step, 0)


def _group(start, n):
    return pl.ds(pl.multiple_of(start, GROUP), n)


def _dispatch_kernel(src_ref, len_ref, dst_ref, tot_ref, tail_s_ref, tail_n_ref, na_ref,
                     lpos_ref, h_ref, r_ref, xs_ref, stage_ref, zero_ref, sems):
    i = pl.program_id(0)
    n_blocks = pl.num_programs(0) - 1
    slot = i % 2

    @pl.when(i < n_blocks)
    def _():
        lpos = lpos_ref[...]
        targets = [lpos[k:k + 1, :] for k in range(TOP_K)]
        h = h_ref[...].astype(BF16)
        r = r_ref[...].astype(BF16)

        def permute(jb, carry):
            base = pl.multiple_of(jb * MB, MB)
            rows = lax.broadcasted_iota(jnp.int32, (MB, TR), 0) + base
            sel = _one_hot_rows(targets, rows)
            stage_ref[slot, pl.ds(base, MB), 0:D_MODEL] = jnp.dot(
                sel, h, preferred_element_type=F32)
            stage_ref[slot, pl.ds(base, MB), D_MODEL:D_AUG] = jnp.dot(
                sel, r, preferred_element_type=F32)
            return carry

        lax.fori_loop(0, (tot_ref[i] + MB - 1) // MB, permute, 0)
        _for_each_group(i, len_ref, lambda e, n: pltpu.make_async_copy(
            stage_ref.at[slot, _group(src_ref[i, e], n)],
            xs_ref.at[_group(dst_ref[i, e], n)], sems.at[slot]).start())

    @pl.when(i > 0)
    def _():
        rows = _group(0, pl.multiple_of(tot_ref[i - 1], GROUP))
        pltpu.make_async_copy(stage_ref.at[1 - slot, rows], xs_ref.at[rows],
                              sems.at[1 - slot]).wait()

    @pl.when(i == n_blocks)
    def _():
        zero_ref[...] = jnp.zeros_like(zero_ref)
        fill = sems.at[2]

        def expert_tail(e, n):
            return pltpu.make_async_copy(zero_ref.at[_group(0, n)],
                                         xs_ref.at[_group(tail_s_ref[0, e], n)], fill)

        def whole_tile(t):
            return pltpu.make_async_copy(zero_ref, xs_ref.at[_group(t * TM, TM)], fill)

        n_tiles = xs_ref.shape[0] // TM
        _for_each_group(0, tail_n_ref, lambda e, n: expert_tail(e, n).start())
        lax.fori_loop(na_ref[0], n_tiles, lambda t, c: (whole_tile(t).start(), c)[1], 0)
        _for_each_group(0, tail_n_ref, lambda e, n: expert_tail(e, n).wait())
        lax.fori_loop(na_ref[0], n_tiles, lambda t, c: (whole_tile(t).wait(), c)[1], 0)


def _dispatch(plan, h2, route):
    n = h2.shape[0] // TR
    last = n - 1
    grid_spec = pltpu.PrefetchScalarGridSpec(
        num_scalar_prefetch=7,
        grid=(n + 1,),
        in_specs=[
            pl.BlockSpec((None, SUBLANES, TR), lambda i, *_: (jnp.minimum(i, last), 0, 0)),
            pl.BlockSpec((TR, D_MODEL), lambda i, *_: (jnp.minimum(i, last), 0)),
            pl.BlockSpec((TR, LANES), lambda i, *_: (jnp.minimum(i, last), 0)),
        ],
        out_specs=pl.BlockSpec(memory_space=pl.ANY),
        scratch_shapes=[pltpu.VMEM((2, RB_STAGE, D_AUG), F32),
                        pltpu.VMEM((TM, D_AUG), F32),
                        pltpu.SemaphoreType.DMA((3,))],
    )
    return pl.pallas_call(
        _dispatch_kernel,
        grid_spec=grid_spec,
        out_shape=jax.ShapeDtypeStruct((plan["n_sorted"], D_AUG), F32),
        compiler_params=_cparams(("arbitrary",)),
        name="moe_dispatch",
    )(plan["src_row"], plan["n_rows"], plan["dst_row"], plan["block_rows"],
      plan["tail_start"][None], plan["tail_len"][None], plan["n_active"], plan["lpos_t"],
      h2, route)


def _expert_kernel(layer, te_ref, na_ref, first_ref, nxt_ref, slot_ref,
                   x_ref, b1_ref, b2_ref, w1_hbm, w2_hbm, o_ref,
                   w1buf_ref, w2buf_ref, w1c_ref, w2c_ref, sems):
    i = pl.program_id(0)
    e = te_ref[i]
    slot = slot_ref[i]

    def fetch(expert, s):
        return (pltpu.make_async_copy(w1_hbm.at[layer, expert], w1buf_ref.at[s], sems.at[s, 0]),
                pltpu.make_async_copy(w2_hbm.at[layer, expert], w2buf_ref.at[s], sems.at[s, 1]))

    @pl.when(i == 0)
    def _():
        for copy in fetch(e, slot):
            copy.start()

    @pl.when(first_ref[i] == 1)
    def _():
        for copy in fetch(e, slot):
            copy.wait()
        w1c_ref[...] = w1buf_ref[slot].astype(BF16)
        w2c_ref[...] = w2buf_ref[slot].astype(BF16)

        @pl.when(nxt_ref[i] >= 0)
        def _():
            for copy in fetch(nxt_ref[i], 1 - slot):
                copy.start()

    @pl.when(i < na_ref[0])
    def _():
        aug = x_ref[:, D_MODEL:D_AUG]
        e_f = e.astype(F32)
        gate = jnp.zeros((TM, 1), F32)
        for k in range(TOP_K):
            p = (aug[:, LANE_P_HI + k:LANE_P_HI + k + 1]
                 + aug[:, LANE_P_MID + k:LANE_P_MID + k + 1]
                 + aug[:, LANE_P_LO + k:LANE_P_LO + k + 1])
            gate = gate + jnp.where(aug[:, LANE_EID + k:LANE_EID + k + 1] == e_f, p, 0.0)
        x = x_ref[:, 0:D_MODEL].astype(BF16)
        y = jnp.broadcast_to(b2_ref[...], (TM, D_MODEL))
        for c in range(D_EXPERT // FFN_CHUNK):
            g_cols = slice(c * FFN_CHUNK, (c + 1) * FFN_CHUNK)
            l_cols = slice(D_EXPERT + c * FFN_CHUNK, D_EXPERT + (c + 1) * FFN_CHUNK)
            glu = jnp.dot(x, w1c_ref[:, g_cols], preferred_element_type=F32) + b1_ref[:, g_cols]
            lin = jnp.dot(x, w1c_ref[:, l_cols], preferred_element_type=F32) + b1_ref[:, l_cols]
            glu = jnp.minimum(glu, SWIGLU_LIMIT)
            lin = jnp.clip(lin, -SWIGLU_LIMIT, SWIGLU_LIMIT)
            act = glu * _sigmoid(SWIGLU_ALPHA * glu) * (lin + 1.0)
            y = y + jnp.dot(act.astype(BF16), w2c_ref[g_cols, :], preferred_element_type=F32)
        o_ref[...] = gate * y

    @pl.when(i >= na_ref[0])
    def _():
        o_ref[...] = jnp.zeros_like(o_ref)


def _experts(layer, plan, xs, w1, b1, w2, b2):
    n_tiles = xs.shape[0] // TM
    act_tile = lambda i, te, na, *_: (jnp.maximum(jnp.minimum(i, na[0] - 1), 0), 0)
    expert = lambda i, te, *_: (layer, te[i], 0, 0)
    grid_spec = pltpu.PrefetchScalarGridSpec(
        num_scalar_prefetch=5,
        grid=(n_tiles,),
        in_specs=[
            pl.BlockSpec((TM, D_AUG), act_tile),
            pl.BlockSpec((None, None, 1, 2 * D_EXPERT), expert),
            pl.BlockSpec((None, None, 1, D_MODEL), expert),
            pl.BlockSpec(memory_space=pl.ANY),
            pl.BlockSpec(memory_space=pl.ANY),
        ],
        out_specs=pl.BlockSpec((TM, D_MODEL), lambda i, *_: (i, 0)),
        scratch_shapes=[pltpu.VMEM((2, D_MODEL, 2 * D_EXPERT), F32),
                        pltpu.VMEM((2, D_EXPERT, D_MODEL), F32),
                        pltpu.VMEM((D_MODEL, 2 * D_EXPERT), BF16),
                        pltpu.VMEM((D_EXPERT, D_MODEL), BF16),
                        pltpu.SemaphoreType.DMA((2, 2))],
    )
    return pl.pallas_call(
        functools.partial(_expert_kernel, layer),
        grid_spec=grid_spec,
        out_shape=jax.ShapeDtypeStruct((xs.shape[0], D_MODEL), F32),
        compiler_params=_cparams(("arbitrary",)),
        name="moe_experts",
    )(plan["tile_expert"], plan["n_active"], plan["tile_first"], plan["tile_next"],
      plan["tile_slot"], xs, b1.reshape(DEPTH, N_EXPERTS, 1, -1),
      b2.reshape(DEPTH, N_EXPERTS, 1, -1), w1, w2)


def _combine_kernel(batch_major_out, src_ref, len_ref, dst_ref, tot_ref, lpos_ref, x_ref,
                    g2_ref, lg_ref, lb_ref, ys_ref, o_ref, ybuf_ref, sems):
    i = pl.program_id(0)
    n_blocks = pl.num_programs(0)
    slot = i % 2

    def start_block(block, s):
        _for_each_group(block, len_ref, lambda e, n: pltpu.make_async_copy(
            ys_ref.at[_group(dst_ref[block, e], n)],
            ybuf_ref.at[s, _group(src_ref[block, e], n)], sems.at[s]).start())

    @pl.when(i == 0)
    def _():
        ybuf_ref[...] = jnp.zeros_like(ybuf_ref)
        start_block(0, 0)

    @pl.when(i + 1 < n_blocks)
    def _():
        start_block(i + 1, 1 - slot)

    rows = _group(0, pl.multiple_of(tot_ref[i], GROUP))
    pltpu.make_async_copy(ys_ref.at[rows], ybuf_ref.at[slot, rows], sems.at[slot]).wait()
    lpos = lpos_ref[...]
    targets = [jnp.broadcast_to(lpos[:, k:k + 1], (TR, KB)) for k in range(TOP_K)]
    cols = lax.broadcasted_iota(jnp.int32, (TR, KB), 1)
    f = jnp.zeros((TR, D_MODEL), F32)
    for jb in range(RB // KB):
        sel = _one_hot_rows(targets, cols + jb * KB)
        f = f + jnp.dot(sel, ybuf_ref[slot, jb * KB:(jb + 1) * KB, :].astype(BF16),
                        preferred_element_type=F32)
    x = ALPHA * x_ref[...] + _per_batch(f, g2_ref[0], jnp.multiply)
    out = _layer_norm(x) * lg_ref[...] + lb_ref[...]
    if batch_major_out:
        out = jnp.transpose(out.reshape(TT, BATCH, D_MODEL), (1, 0, 2))
    o_ref[...] = out


def _combine(t_off, plan, x, mod, ln_g, ln_b, ys, batch_major_out):
    n = x.shape[0] // TR
    row = lambda w: pl.BlockSpec((TR, w), lambda i, *_: (i, 0))
    const = lambda shape: pl.BlockSpec(shape, lambda i, *_: (0,) * len(shape))
    if batch_major_out:
        out_spec = pl.BlockSpec((BATCH, TT, D_MODEL), lambda i, *_: (0, i, 0))
        out_shape = jax.ShapeDtypeStruct((BATCH, n * TT, D_MODEL), F32)
    else:
        out_spec = row(D_MODEL)
        out_shape = jax.ShapeDtypeStruct((n * TR, D_MODEL), F32)
    grid_spec = pltpu.PrefetchScalarGridSpec(
        num_scalar_prefetch=4,
        grid=(n,),
        in_specs=[
            row(TOP_K), row(D_MODEL),
            pl.BlockSpec((1, BATCH, D_MODEL),
                         lambda i, *_: ((i + t_off >= CTX_TILES).astype(jnp.int32), 0, 5)),
            const((1, D_MODEL)), const((1, D_MODEL)),
            pl.BlockSpec(memory_space=pl.ANY),
        ],
        out_specs=out_spec,
        scratch_shapes=[pltpu.VMEM((2, RB, D_MODEL), F32),
                        pltpu.SemaphoreType.DMA((2,))],
    )
    return pl.pallas_call(
        functools.partial(_combine_kernel, batch_major_out),
        grid_spec=grid_spec,
        out_shape=out_shape,
        compiler_params=_cparams(("arbitrary",)),
        name="moe_combine",
    )(plan["src_row"], plan["n_rows"], plan["dst_row"], plan["block_rows"], plan["lpos"],
      x, mod, ln_g, ln_b, ys)


def _routing_plan(route, counts):
    i32 = jnp.int32
    nb = counts.shape[0]
    experts = jnp.arange(N_EXPERTS, dtype=i32)
    eid = route[:, LANE_EID:LANE_EID + TOP_K].astype(i32).reshape(nb, TR, TOP_K)
    rank = route[:, LANE_RANK:LANE_RANK + TOP_K].astype(i32).reshape(nb, TR, TOP_K)
    n = counts[:, 0, :N_EXPERTS].astype(i32)
    n = (n + GROUP - 1) // GROUP * GROUP
    src_row = jnp.cumsum(n, axis=1) - n
    seg = jnp.sum(n, axis=0)
    tiles_e = (seg + TM - 1) // TM
    tile_end = jnp.cumsum(tiles_e)
    e_start = (tile_end - tiles_e) * TM
    n_active = tile_end[-1]
    dst_row = e_start[None, :] + jnp.cumsum(n, axis=0) - n

    onehot = eid[..., None] == experts
    lpos = jnp.sum(jnp.where(onehot, src_row[:, None, None, :], 0), axis=-1) + rank
    lpos_t = jnp.concatenate(
        [lpos.transpose(0, 2, 1), jnp.full((nb, SUBLANES - TOP_K, TR), -1, i32)], axis=1)

    n_tiles = (nb * RB + TM - 1) // TM + N_EXPERTS
    t = jnp.minimum(jnp.arange(n_tiles, dtype=i32), n_active - 1)
    tile_expert = jnp.sum((t[:, None] >= tile_end[None, :]).astype(i32), axis=1)
    tile_expert = jnp.minimum(tile_expert, N_EXPERTS - 1)
    tile_first = jnp.concatenate(
        [jnp.ones((1,), i32), (tile_expert[1:] != tile_expert[:-1]).astype(i32)])
    has_rows = tiles_e > 0
    later = jnp.where(has_rows[None, :] & (experts[None, :] > experts[:, None]),
                      experts[None, :], N_EXPERTS)
    next_e = jnp.min(later, axis=1)
    next_e = jnp.where(next_e == N_EXPERTS, -1, next_e)
    group = jnp.cumsum(has_rows.astype(i32)) - 1
    pick = tile_expert[:, None] == experts[None, :]
    tile_next = jnp.sum(jnp.where(pick, next_e[None, :], 0), axis=1)
    tile_slot = jnp.sum(jnp.where(pick, group[None, :], 0), axis=1) % 2
    return dict(lpos=lpos.reshape(nb * TR, TOP_K), lpos_t=lpos_t, src_row=src_row, n_rows=n,
                dst_row=dst_row, block_rows=jnp.sum(n, axis=1), tail_start=e_start + seg,
                tail_len=tile_end * TM - e_start - seg,
                tile_expert=tile_expert, n_active=n_active.reshape(1), tile_first=tile_first,
                tile_next=tile_next, tile_slot=tile_slot, n_sorted=n_tiles * TM)


def _grid_sincos_parts():
    quarter = D_MODEL // 4
    omega = 1.0 / (10000.0 ** (jnp.arange(quarter, dtype=F32) / quarter))

    def emb1d(n):
        ang = jnp.arange(n, dtype=F32)[:, None] * omega[None, :]
        return jnp.concatenate([jnp.sin(ang), jnp.cos(ang)], axis=-1)

    er = emb1d(SEQ // GRID_W)
    ec = jnp.repeat(emb1d(GRID_W), BATCH, axis=0)
    return er.reshape(SEQ // GRID_W, 1, D_MODEL // 2), ec


def _gate_blocks(w):
    per = GATE_BLK // LRU_HEAD_DIM
    w = w.reshape(2, D_LRU // GATE_BLK, per, LRU_HEAD_DIM, LRU_HEAD_DIM)
    eye = jnp.eye(per, dtype=w.dtype)
    blk = jnp.einsum('dkpij,pq->dkpiqj', w, eye)
    return blk.reshape(2, D_LRU // GATE_BLK, GATE_BLK, GATE_BLK)


def kernel(x, c, ctx, c_ctx, w_mod, b_mod, w_in, conv_w, conv_b, gate_a_w, gate_a_b,
           gate_x_w, gate_x_b, lru_lambda, pool_w, pool_b, pool_scale, w_out, ln1_g, ln1_b,
           router_w, router_b, exp_w1, exp_b1, exp_w2, exp_b2, ln2_g, ln2_b):
    er, ec = _grid_sincos_parts()
    xs = _entry(ctx, x, er, ec)

    cvec = jnp.concatenate([c, c_ctx[None], jnp.zeros((2 * SUBLANES - BATCH - 1, D_MODEL), F32)])
    mod_all = _modulation(cvec, w_mod, b_mod)
    tri = jnp.tril(jnp.ones((TR, TR), F32), -1).astype(BF16)

    for l in range(DEPTH):
        last = l == DEPTH - 1
        t_off = CTX_TILES if last else 0
        mod = jnp.stack([jnp.broadcast_to(mod_all[l, BATCH], (BATCH, 6 * D_MODEL)),
                         mod_all[l, :BATCH]])
        z = _inproj(xs, mod, w_in[l].astype(BF16))
        gate_w = jnp.stack([_gate_blocks(gate_a_w[l]), _gate_blocks(gate_x_w[l])],
                           axis=1).astype(BF16)
        gate_b = jnp.stack([gate_a_b[l], gate_x_b[l]], axis=1)
        hf, hr = _scan(z, conv_w[l], conv_b[l][None], gate_w, gate_b, lru_lambda[l])
        rw_pad = jnp.zeros((D_MODEL, LANES), F32).at[:, :N_EXPERTS].set(router_w[l])
        rw_hi = rw_pad.astype(BF16)
        rw_pad = jnp.stack([rw_hi, (rw_pad - rw_hi.astype(F32)).astype(BF16)])
        rb_pad = jnp.full((1, LANES), -1e30, F32).at[0, :N_EXPERTS].set(router_b[l])
        xs, h2, route, counts = _mixer(
            t_off, hf, hr, z, xs, mod, pool_w[l].astype(BF16), pool_b[l][None],
            pool_scale[l][None], w_out[l].astype(BF16), ln1_g[l][None], ln1_b[l][None],
            rw_pad, rb_pad, tri)
        plan = _routing_plan(route, counts)
        x_sorted = _dispatch(plan, h2, route)
        y_sorted = _experts(l, plan, x_sorted, exp_w1, exp_b1, exp_w2, exp_b2)
        xs = _combine(t_off, plan, xs, mod, ln2_g[l][None], ln2_b[l][None], y_sorted, last)
    return xs
```

```python
import functools

import jax
import jax.numpy as jnp
from jax import lax
from jax.experimental import pallas as pl
from jax.experimental.pallas import tpu as pltpu

D_MODEL = 1024
BATCH = 8
SEQ = 2048
DEPTH = 2
GRID_W = 64
CTX_LEN = 256
D_LRU = 512
N_LRU_HEADS = 8
LRU_HEAD_DIM = D_LRU // N_LRU_HEADS
LRU_C = 8.0
D_POOL = 512
POOL_WINDOWS = (2, 4, 8, 16)
POOL_GROUP_DIM = D_POOL // len(POOL_WINDOWS)
D_IN = 2 * D_LRU + D_POOL
N_EXPERTS = 32
TOP_K = 4
D_EXPERT = D_MODEL
SWIGLU_LIMIT = 7.0
SWIGLU_ALPHA = 1.702
LN_EPS = 1e-5
ALPHA = (2.0 * DEPTH) ** 0.25

F32 = jnp.float32
BF16 = jnp.bfloat16

SUBLANES = 8
LANES = 128
TT = GRID_W
TR = TT * BATCH
T_ALL = CTX_LEN + SEQ
R_ALL = T_ALL * BATCH
N_TILES = T_ALL // TT
CTX_TILES = CTX_LEN // TT
GATE_BLK = 256
POOL_HALO = max(POOL_WINDOWS) // 2
TM = 512
TM_PART = 256
FFN_CHUNK = 512
GROUP = SUBLANES
RB = TOP_K * TR + N_EXPERTS * GROUP
KB = 256
MB = 512
RB_STAGE = (RB + MB - 1) // MB * MB
D_AUG = D_MODEL + LANES
LANE_P_HI, LANE_P_MID, LANE_P_LO, LANE_EID, LANE_RANK = 0, 4, 8, 12, 16
VMEM_LIMIT = 56 * 1024 * 1024


def _cparams(sem):
    return pltpu.CompilerParams(dimension_semantics=sem, vmem_limit_bytes=VMEM_LIMIT)


def _sigmoid(x):
    return 0.5 * (1.0 + jnp.tanh(0.5 * x))


def _layer_norm(x):
    mu = jnp.mean(x, axis=-1, keepdims=True)
    xc = x - mu
    var = jnp.mean(xc * xc, axis=-1, keepdims=True)
    return xc * lax.rsqrt(var + LN_EPS)


def _per_batch(x, v, op):
    r, d = x.shape
    return op(x.reshape(r // BATCH, BATCH, d), v[None]).reshape(r, d)


def _mod_kernel(c_ref, w_ref, b_ref, o_ref):
    c = c_ref[...]
    s = c * _sigmoid(c)
    o_ref[...] = jnp.dot(s, w_ref[...], precision=lax.Precision.HIGHEST,
                         preferred_element_type=F32) + b_ref[...]


def _modulation(cvec, w_mod, b_mod):
    tn = 512
    return pl.pallas_call(
        _mod_kernel,
        grid=(DEPTH, 6 * D_MODEL // tn),
        in_specs=[
            pl.BlockSpec((2 * SUBLANES, D_MODEL), lambda l, j: (0, 0)),
            pl.BlockSpec((None, D_MODEL, tn), lambda l, j: (l, 0, j)),
            pl.BlockSpec((None, 1, tn), lambda l, j: (l, 0, j)),
        ],
        out_specs=pl.BlockSpec((None, 2 * SUBLANES, tn), lambda l, j: (l, 0, j)),
        out_shape=jax.ShapeDtypeStruct((DEPTH, 2 * SUBLANES, 6 * D_MODEL), F32),
        compiler_params=_cparams(("arbitrary", "arbitrary")),
        name="modulation",
    )(cvec, w_mod, b_mod.reshape(DEPTH, 1, 6 * D_MODEL))


def _in_projection(x, sh_ref, sc_ref, w_ref):
    h = _per_batch(x, 1.0 + sc_ref[0], jnp.multiply)
    h = _per_batch(h, sh_ref[0], jnp.add)
    return jnp.dot(h.astype(BF16), w_ref[...], preferred_element_type=F32)


def _entry_kernel(ctx_ref, x_ref, er_ref, ec_ref, sh_ref, sc_ref, w_ref, o_ref, z_ref):
    i = pl.program_id(0)

    def time_major(src_ref):
        return jnp.transpose(src_ref[...], (1, 0, 2)).reshape(TR, D_MODEL)

    @pl.when(i < CTX_TILES)
    def _():
        o_ref[...] = _layer_norm(time_major(ctx_ref))

    @pl.when(i >= CTX_TILES)
    def _():
        pos = jnp.concatenate(
            [jnp.broadcast_to(er_ref[0], (TR, D_MODEL // 2)), ec_ref[...]], axis=1)
        o_ref[...] = _layer_norm(time_major(x_ref) + pos)

    z_ref[...] = _in_projection(o_ref[...], sh_ref, sc_ref, w_ref)


def _mod_spec(chunk, t_off=0):
    return pl.BlockSpec((1, BATCH, D_MODEL),
                        lambda i, *_: ((i + t_off >= CTX_TILES).astype(jnp.int32), 0, chunk))


def _entry(ctx, x, er, ec, mod, w_in_bf):
    return pl.pallas_call(
        _entry_kernel,
        grid=(N_TILES,),
        in_specs=[
            pl.BlockSpec((BATCH, TT, D_MODEL), lambda i: (0, jnp.minimum(i, CTX_TILES - 1), 0)),
            pl.BlockSpec((BATCH, TT, D_MODEL), lambda i: (0, jnp.maximum(i - CTX_TILES, 0), 0)),
            pl.BlockSpec((1, 1, D_MODEL // 2), lambda i: (jnp.maximum(i - CTX_TILES, 0), 0, 0)),
            pl.BlockSpec((TR, D_MODEL // 2), lambda i: (0, 0)),
            _mod_spec(0), _mod_spec(1),
            pl.BlockSpec((D_MODEL, D_IN), lambda i: (0, 0)),
        ],
        out_specs=[pl.BlockSpec((TR, D_MODEL), lambda i: (i, 0)),
                   pl.BlockSpec((TR, D_IN), lambda i: (i, 0))],
        out_shape=[jax.ShapeDtypeStruct((R_ALL, D_MODEL), F32),
                   jax.ShapeDtypeStruct((R_ALL, D_IN), F32)],
        compiler_params=_cparams(("arbitrary",)),
        name="entry_ln",
    )(ctx, x, er, ec, mod, mod, w_in_bf)


def _rev_tile(i):
    return jnp.where(i < CTX_TILES, CTX_TILES - 1 - i, N_TILES - 1 + CTX_TILES - i)


def _block_diag_dot(u_bf, w_ref, d, g):
    halves = [jnp.dot(u_bf[:, k * GATE_BLK:(k + 1) * GATE_BLK], w_ref[d, g, k],
                      preferred_element_type=F32) for k in range(D_LRU // GATE_BLK)]
    return jnp.concatenate(halves, axis=1)


def _lru_coeffs(tile, z_ref, zp_ref, zn_ref, cw_ref, cb_ref, gw_ref, gb_ref, lam_ref, d):
    x = z_ref[...]
    seg_first = (tile == 0) | (tile == CTX_TILES)
    seg_last = (tile == CTX_TILES - 1) | (tile == N_TILES - 1)
    prev = jnp.where(seg_first, 0.0, zp_ref[...])
    nxt = jnp.where(seg_last, 0.0, zn_ref[...])
    xm2 = jnp.concatenate([prev, x[:-2 * BATCH]], axis=0)
    xm1 = jnp.concatenate([prev[BATCH:], x[:-BATCH]], axis=0)
    xp1 = jnp.concatenate([x[BATCH:], nxt], axis=0)
    u = (cb_ref[...] + xm2 * cw_ref[0:1] + xm1 * cw_ref[1:2]
         + x * cw_ref[2:3] + xp1 * cw_ref[3:4])
    u_bf = u.astype(BF16)
    r = _sigmoid(_block_diag_dot(u_bf, gw_ref, d, 0) + gb_ref[d, 0:1])
    ig = _sigmoid(_block_diag_dot(u_bf, gw_ref, d, 1) + gb_ref[d, 1:2])
    nl = -lam_ref[d:d + 1]
    softplus = jnp.maximum(nl, 0.0) + jnp.log(1.0 + jnp.exp(-jnp.abs(nl)))
    log_a = (-LRU_C) * r * softplus
    a = jnp.exp(log_a)
    mult = jnp.sqrt(1.0 - a * a)
    return a, mult * (ig * u)


def _scan_kernel(zf_ref, zfp_ref, zfn_ref, zr_ref, zrp_ref, zrn_ref,
                 cw_ref, cb_ref, gw_ref, gb_ref, lam_ref,
                 hf_ref, hr_ref, state_ref):
    i = pl.program_id(0)

    @pl.when(i == 0)
    def _():
        state_ref[...] = jnp.zeros_like(state_ref)

    af, bf = _lru_coeffs(i, zf_ref, zfp_ref, zfn_ref, cw_ref, cb_ref, gw_ref, gb_ref,
                         lam_ref, 0)
    ar, br = _lru_coeffs(_rev_tile(i), zr_ref, zrp_ref, zrn_ref, cw_ref, cb_ref, gw_ref,
                         gb_ref, lam_ref, 1)
    hf = state_ref[0]
    hr = state_ref[1]
    for s in range(TT):
        f = slice(s * BATCH, (s + 1) * BATCH)
        hf = af[f] * hf + bf[f]
        hf_ref[f, :] = hf
        b = slice((TT - 1 - s) * BATCH, (TT - s) * BATCH)
        hr = ar[b] * hr + br[b]
        hr_ref[b, :] = hr
    state_ref[0] = hf
    state_ref[1] = hr


def _scan(z, conv_w, conv_b, gate_w, gate_b, lam):
    prev_rows = 2 * BATCH
    tiles_per_prev = TR // prev_rows
    tiles_per_next = TR // BATCH
    last_next = R_ALL // BATCH - 1

    def cur(f):
        return pl.BlockSpec((TR, D_LRU), lambda i: (f(i), 0))

    def prev(f):
        return pl.BlockSpec((prev_rows, D_LRU),
                            lambda i: (jnp.maximum(f(i) * tiles_per_prev - 1, 0), 0))

    def nxt(f):
        return pl.BlockSpec((BATCH, D_LRU),
                            lambda i: (jnp.minimum((f(i) + 1) * tiles_per_next, last_next), 0))

    fwd = lambda i: i
    const = lambda shape: pl.BlockSpec(shape, lambda i: (0,) * len(shape))
    return pl.pallas_call(
        _scan_kernel,
        grid=(N_TILES,),
        in_specs=[cur(fwd), prev(fwd), nxt(fwd), cur(_rev_tile), prev(_rev_tile), nxt(_rev_tile),
                  const((4, D_LRU)), const((1, D_LRU)),
                  const((2, 2, D_LRU // GATE_BLK, GATE_BLK, GATE_BLK)),
                  const((2, 2, D_LRU)), const((2, D_LRU))],
        out_specs=[pl.BlockSpec((TR, D_LRU), lambda i: (i, 0)),
                   pl.BlockSpec((TR, D_LRU), lambda i: (_rev_tile(i), 0))],
        out_shape=[jax.ShapeDtypeStruct((R_ALL, D_LRU), F32)] * 2,
        scratch_shapes=[pltpu.VMEM((2, BATCH, D_LRU), F32)],
        compiler_params=_cparams(("arbitrary",)),
        name="lru_scan",
    )(z, z, z, z, z, z, conv_w, conv_b, gate_w, gate_b, lam)


def _gelu_tanh(y):
    c = 0.7978845608028654
    return 0.5 * y * (1.0 + jnp.tanh(c * (y + 0.044715 * (y * y * y))))


def _pool_groups(tile, xp_ref, xpp_ref, xpn_ref):
    in_ctx = tile < CTX_TILES
    prev_ok = in_ctx & (tile > 0)
    next_ok = in_ctx & (tile < CTX_TILES - 1)
    x = xp_ref[...]
    p = jnp.concatenate([jnp.where(prev_ok, xpp_ref[...], 0.0), x,
                         jnp.where(next_ok, xpn_ref[...], 0.0)], axis=0)
    p = p.reshape(TT + 2 * POOL_HALO, BATCH, D_POOL)
    lo = jnp.where(in_ctx, -TT * tile, 0)
    hi = jnp.where(in_ctx, CTX_LEN - TT * tile, TT)
    t = lax.broadcasted_iota(jnp.int32, (TT, BATCH, POOL_GROUP_DIM), 0)
    outs = []
    for g, win in enumerate(POOL_WINDOWS):
        half = win // 2
        acc = p[:, :, g * POOL_GROUP_DIM:(g + 1) * POOL_GROUP_DIM]
        width = 1
        while width < win:
            acc = acc[:acc.shape[0] - width] + acc[width:]
            width *= 2
        start = POOL_HALO - half
        wsum = acc[start:start + TT]
        cnt = (jnp.minimum(t + half, hi) - jnp.maximum(t - half, lo)).astype(F32)
        centre = p[POOL_HALO:POOL_HALO + TT, :, g * POOL_GROUP_DIM:(g + 1) * POOL_GROUP_DIM]
        outs.append((wsum / cnt - centre).reshape(TR, POOL_GROUP_DIM))
    return outs


def _mixer_kernel(t_off, hf_ref, hr_ref, y_ref, xp_ref, xpp_ref, xpn_ref, x_ref,
                  g1_ref, sh2_ref, sc2_ref, pw_ref, pb_ref, ps_ref, wo_ref,
                  lg_ref, lb_ref, rw_ref, rb_ref, tri_ref,
                  xo_ref, h2_ref, route_ref, cnt_ref):
    i = pl.program_id(0)
    tile = i + t_off

    lru = (hf_ref[...] + hr_ref[...]) * _gelu_tanh(y_ref[...])
    diffs = _pool_groups(tile, xp_ref, xpp_ref, xpn_ref)
    pooled = jnp.concatenate(
        [jnp.dot(d.astype(BF16), pw_ref[g], preferred_element_type=F32)
         for g, d in enumerate(diffs)], axis=1)
    pooled = (pooled + pb_ref[...]) * ps_ref[...]
    mix = (jnp.dot(lru.astype(BF16), wo_ref[0:D_LRU, :], preferred_element_type=F32)
           + jnp.dot(pooled.astype(BF16), wo_ref[D_LRU:, :], preferred_element_type=F32))
    x = ALPHA * x_ref[...] + _per_batch(mix, g1_ref[0], jnp.multiply)
    x = _layer_norm(x) * lg_ref[...] + lb_ref[...]
    xo_ref[...] = x
    h2 = _per_batch(x, 1.0 + sc2_ref[0], jnp.multiply)
    h2 = _per_batch(h2, sh2_ref[0], jnp.add)
    h2_ref[...] = h2

    h_hi = h2.astype(BF16)
    h_lo = (h2 - h_hi.astype(F32)).astype(BF16)
    logits = (jnp.dot(h_hi, rw_ref[0], preferred_element_type=F32)
              + jnp.dot(h_lo, rw_ref[0], preferred_element_type=F32)
              + jnp.dot(h_hi, rw_ref[1], preferred_element_type=F32)) + rb_ref[...]
    lane = lax.broadcasted_iota(jnp.int32, (TR, LANES), 1).astype(F32)
    work = logits
    vals, idxs, sels = [], [], []
    for _ in range(TOP_K):
        m = jnp.max(work, axis=1, keepdims=True)
        idx = jnp.min(jnp.where(work == m, lane, float(LANES)), axis=1, keepdims=True)
        sel = lane == idx
        vals.append(m)
        idxs.append(idx)
        sels.append(sel)
        work = jnp.where(sel, -jnp.inf, work)
    exps = [jnp.exp(v - vals[0]) for v in vals]
    denom = exps[0] + exps[1] + exps[2] + exps[3]
    chosen = jnp.zeros((TR, LANES), F32)
    for sel in sels:
        chosen = chosen + sel.astype(F32)
    before = jnp.dot(tri_ref[...], chosen.astype(BF16), preferred_element_type=F32)
    route = jnp.zeros((TR, LANES), F32)
    for k in range(TOP_K):
        rank = jnp.sum(jnp.where(sels[k], before, 0.0), axis=1, keepdims=True)
        p = exps[k] / denom
        p_hi = p.astype(BF16).astype(F32)
        p_mid = (p - p_hi).astype(BF16).astype(F32)
        p_lo = p - p_hi - p_mid
        for base, val in ((LANE_P_HI, p_hi), (LANE_P_MID, p_mid), (LANE_P_LO, p_lo),
                          (LANE_EID, idxs[k]), (LANE_RANK, rank)):
            route = jnp.where(lane == float(base + k), val, route)
    route_ref[...] = route
    cnt_ref[0] = jnp.broadcast_to(jnp.sum(chosen, axis=0, keepdims=True), (SUBLANES, LANES))


def _mixer(t_off, hf, hr, z, x, mod, pool_w_bf, pool_b, pool_scale, w_out_bf,
           ln_g, ln_b, rw_pad, rb_pad, tri):
    n = N_TILES - t_off
    halo_rows = POOL_HALO * BATCH
    per = TR // halo_rows
    last_halo = R_ALL // halo_rows - 1
    xp_col = 2 * D_LRU // D_POOL
    row = lambda w: pl.BlockSpec((TR, w), lambda i: (i + t_off, 0))
    out_row = lambda w: pl.BlockSpec((TR, w), lambda i: (i, 0))
    const = lambda shape: pl.BlockSpec(shape, lambda i: (0,) * len(shape))

    def mod_spec(chunk):
        return pl.BlockSpec((1, BATCH, D_MODEL),
                            lambda i: ((i + t_off >= CTX_TILES).astype(jnp.int32), 0, chunk))

    return pl.pallas_call(
        functools.partial(_mixer_kernel, t_off),
        grid=(n,),
        in_specs=[
            row(D_LRU), row(D_LRU),
            pl.BlockSpec((TR, D_LRU), lambda i: (i + t_off, 1)),
            pl.BlockSpec((TR, D_POOL), lambda i: (i + t_off, xp_col)),
            pl.BlockSpec((halo_rows, D_POOL),
                         lambda i: (jnp.maximum((i + t_off) * per - 1, 0), xp_col)),
            pl.BlockSpec((halo_rows, D_POOL),
                         lambda i: (jnp.minimum((i + t_off + 1) * per, last_halo), xp_col)),
            row(D_MODEL),
            mod_spec(2), mod_spec(3), mod_spec(4),
            const((len(POOL_WINDOWS), POOL_GROUP_DIM, POOL_GROUP_DIM)),
            const((1, D_POOL)), const((1, D_POOL)),
            const((D_MODEL, D_MODEL)),
            const((1, D_MODEL)), const((1, D_MODEL)),
            const((2, D_MODEL, LANES)), const((1, LANES)),
            const((TR, TR)),
        ],
        out_specs=[out_row(D_MODEL), out_row(D_MODEL), out_row(LANES),
                   pl.BlockSpec((1, SUBLANES, LANES), lambda i: (i, 0, 0))],
        out_shape=[jax.ShapeDtypeStruct((n * TR, D_MODEL), F32),
                   jax.ShapeDtypeStruct((n * TR, D_MODEL), F32),
                   jax.ShapeDtypeStruct((n * TR, LANES), F32),
                   jax.ShapeDtypeStruct((n, SUBLANES, LANES), F32)],
        compiler_params=_cparams(("arbitrary",)),
        name="mixer_out",
    )(hf, hr, z, z, z, z, x, mod, mod, mod, pool_w_bf, pool_b, pool_scale, w_out_bf,
      ln_g, ln_b, rw_pad, rb_pad, tri)


def _one_hot_rows(targets, index):
    hit = jnp.zeros(index.shape, F32)
    for t in reversed(targets):
        hit = jnp.where(index == t, 1.0, hit)
    return hit.astype(BF16)


def _for_each_group(block, len_ref, body):
    def step(e, carry):
        n = pl.multiple_of(len_ref[block, e], GROUP)

        @pl.when(n > 0)
        def _():
            body(e, n)

        return carry

    lax.fori_loop(0, N_EXPERTS, step, 0)


def _group(start, n):
    return pl.ds(pl.multiple_of(start, GROUP), n)


def _dispatch_kernel(src_ref, len_ref, dst_ref, tot_ref, tail_s_ref, tail_n_ref, na_ref,
                     lpos_ref, h_ref, r_ref, xs_ref, stage_ref, zero_ref, sems):
    i = pl.program_id(0)
    n_blocks = pl.num_programs(0) - 1
    slot = i % 2

    @pl.when(i < n_blocks)
    def _():
        lpos = lpos_ref[...]
        targets = [lpos[k:k + 1, :] for k in range(TOP_K)]
        h = h_ref[...].astype(BF16)
        r = r_ref[...].astype(BF16)

        def permute(jb, carry):
            base = pl.multiple_of(jb * MB, MB)
            rows = lax.broadcasted_iota(jnp.int32, (MB, TR), 0) + base
            sel = _one_hot_rows(targets, rows)
            stage_ref[slot, pl.ds(base, MB), 0:D_MODEL] = jnp.dot(
                sel, h, preferred_element_type=F32)
            stage_ref[slot, pl.ds(base, MB), D_MODEL:D_AUG] = jnp.dot(
                sel, r, preferred_element_type=F32)
            return carry

        lax.fori_loop(0, (tot_ref[i] + MB - 1) // MB, permute, 0)
        _for_each_group(i, len_ref, lambda e, n: pltpu.make_async_copy(
            stage_ref.at[slot, _group(src_ref[i, e], n)],
            xs_ref.at[_group(dst_ref[i, e], n)], sems.at[slot]).start())

    @pl.when(i > 0)
    def _():
        rows = _group(0, pl.multiple_of(tot_ref[i - 1], GROUP))
        pltpu.make_async_copy(stage_ref.at[1 - slot, rows], xs_ref.at[rows],
                              sems.at[1 - slot]).wait()

    @pl.when(i == n_blocks)
    def _():
        zero_ref[...] = jnp.zeros_like(zero_ref)
        fill = sems.at[2]

        def expert_tail(e, n):
            return pltpu.make_async_copy(zero_ref.at[_group(0, n)],
                                         xs_ref.at[_group(tail_s_ref[0, e], n)], fill)

        def whole_tile(t):
            return pltpu.make_async_copy(zero_ref, xs_ref.at[_group(t * TM, TM)], fill)

        n_tiles = xs_ref.shape[0] // TM
        _for_each_group(0, tail_n_ref, lambda e, n: expert_tail(e, n).start())
        lax.fori_loop(na_ref[0], n_tiles, lambda t, c: (whole_tile(t).start(), c)[1], 0)
        _for_each_group(0, tail_n_ref, lambda e, n: expert_tail(e, n).wait())
        lax.fori_loop(na_ref[0], n_tiles, lambda t, c: (whole_tile(t).wait(), c)[1], 0)


def _dispatch(plan, h2, route):
    n = h2.shape[0] // TR
    last = n - 1
    grid_spec = pltpu.PrefetchScalarGridSpec(
        num_scalar_prefetch=7,
        grid=(n + 1,),
        in_specs=[
            pl.BlockSpec((None, SUBLANES, TR), lambda i, *_: (jnp.minimum(i, last), 0, 0)),
            pl.BlockSpec((TR, D_MODEL), lambda i, *_: (jnp.minimum(i, last), 0)),
            pl.BlockSpec((TR, LANES), lambda i, *_: (jnp.minimum(i, last), 0)),
        ],
        out_specs=pl.BlockSpec(memory_space=pl.ANY),
        scratch_shapes=[pltpu.VMEM((2, RB_STAGE, D_AUG), F32),
                        pltpu.VMEM((TM, D_AUG), F32),
                        pltpu.SemaphoreType.DMA((3,))],
    )
    return pl.pallas_call(
        _dispatch_kernel,
        grid_spec=grid_spec,
        out_shape=jax.ShapeDtypeStruct((plan["n_sorted"], D_AUG), F32),
        compiler_params=_cparams(("arbitrary",)),
        name="moe_dispatch",
    )(plan["src_row"], plan["n_rows"], plan["dst_row"], plan["block_rows"],
      plan["tail_start"][None], plan["tail_len"][None], plan["n_active"], plan["lpos_t"],
      h2, route)


def _expert_kernel(layer, te_ref, na_ref, first_ref, nxt_ref, slot_ref, rows_ref,
                   x_ref, b1_ref, b2_ref, w1_hbm, w2_hbm, o_ref,
                   w1buf_ref, w2buf_ref, w1c_ref, w2c_ref, sems):
    i = pl.program_id(0)
    e = te_ref[i]
    slot = slot_ref[i]

    def fetch(expert, s):
        return (pltpu.make_async_copy(w1_hbm.at[layer, expert], w1buf_ref.at[s], sems.at[s, 0]),
                pltpu.make_async_copy(w2_hbm.at[layer, expert], w2buf_ref.at[s], sems.at[s, 1]))

    @pl.when(i == 0)
    def _():
        for copy in fetch(e, slot):
            copy.start()

    @pl.when(first_ref[i] == 1)
    def _():
        for copy in fetch(e, slot):
            copy.wait()
        w1c_ref[...] = w1buf_ref[slot].astype(BF16)
        w2c_ref[...] = w2buf_ref[slot].astype(BF16)

        @pl.when(nxt_ref[i] >= 0)
        def _():
            for copy in fetch(nxt_ref[i], 1 - slot):
                copy.start()

    def ffn(rows):
        aug = x_ref[rows, D_MODEL:D_AUG]
        e_f = e.astype(F32)
        gate = jnp.zeros((TM_PART, 1), F32)
        for k in range(TOP_K):
            p = (aug[:, LANE_P_HI + k:LANE_P_HI + k + 1]
                 + aug[:, LANE_P_MID + k:LANE_P_MID + k + 1]
                 + aug[:, LANE_P_LO + k:LANE_P_LO + k + 1])
            gate = gate + jnp.where(aug[:, LANE_EID + k:LANE_EID + k + 1] == e_f, p, 0.0)
        x = x_ref[rows, 0:D_MODEL].astype(BF16)
        y = jnp.broadcast_to(b2_ref[...], (TM_PART, D_MODEL))
        for c in range(D_EXPERT // FFN_CHUNK):
            g_cols = slice(c * FFN_CHUNK, (c + 1) * FFN_CHUNK)
            l_cols = slice(D_EXPERT + c * FFN_CHUNK, D_EXPERT + (c + 1) * FFN_CHUNK)
            glu = jnp.dot(x, w1c_ref[:, g_cols], preferred_element_type=F32) + b1_ref[:, g_cols]
            lin = jnp.dot(x, w1c_ref[:, l_cols], preferred_element_type=F32) + b1_ref[:, l_cols]
            glu = jnp.minimum(glu, SWIGLU_LIMIT)
            lin = jnp.clip(lin, -SWIGLU_LIMIT, SWIGLU_LIMIT)
            act = glu * _sigmoid(SWIGLU_ALPHA * glu) * (lin + 1.0)
            y = y + jnp.dot(act.astype(BF16), w2c_ref[g_cols, :], preferred_element_type=F32)
        o_ref[rows, :] = gate * y

    for part in range(TM // TM_PART):
        rows = slice(part * TM_PART, (part + 1) * TM_PART)
        used = (i < na_ref[0]) & (rows_ref[i] > part * TM_PART)

        @pl.when(used)
        def _():
            ffn(rows)

        @pl.when(jnp.logical_not(used))
        def _():
            o_ref[rows, :] = jnp.zeros((TM_PART, D_MODEL), F32)


def _experts(layer, plan, xs, w1, b1, w2, b2):
    n_tiles = xs.shape[0] // TM
    act_tile = lambda i, te, na, *_: (jnp.maximum(jnp.minimum(i, na[0] - 1), 0), 0)
    expert = lambda i, te, *_: (layer, te[i], 0, 0)
    grid_spec = pltpu.PrefetchScalarGridSpec(
        num_scalar_prefetch=6,
        grid=(n_tiles,),
        in_specs=[
            pl.BlockSpec((TM, D_AUG), act_tile),
            pl.BlockSpec((None, None, 1, 2 * D_EXPERT), expert),
            pl.BlockSpec((None, None, 1, D_MODEL), expert),
            pl.BlockSpec(memory_space=pl.ANY),
            pl.BlockSpec(memory_space=pl.ANY),
        ],
        out_specs=pl.BlockSpec((TM, D_MODEL), lambda i, *_: (i, 0)),
        scratch_shapes=[pltpu.VMEM((2, D_MODEL, 2 * D_EXPERT), F32),
                        pltpu.VMEM((2, D_EXPERT, D_MODEL), F32),
                        pltpu.VMEM((D_MODEL, 2 * D_EXPERT), BF16),
                        pltpu.VMEM((D_EXPERT, D_MODEL), BF16),
                        pltpu.SemaphoreType.DMA((2, 2))],
    )
    return pl.pallas_call(
        functools.partial(_expert_kernel, layer),
        grid_spec=grid_spec,
        out_shape=jax.ShapeDtypeStruct((xs.shape[0], D_MODEL), F32),
        compiler_params=_cparams(("arbitrary",)),
        name="moe_experts",
    )(plan["tile_expert"], plan["n_active"], plan["tile_first"], plan["tile_next"],
      plan["tile_slot"], plan["tile_rows"], xs, b1.reshape(DEPTH, N_EXPERTS, 1, -1),
      b2.reshape(DEPTH, N_EXPERTS, 1, -1), w1, w2)


def _combine_kernel(last, src_ref, len_ref, dst_ref, tot_ref, lpos_ref, x_ref,
                    g2_ref, lg_ref, lb_ref, ys_ref, *rest):
    if last:
        o_ref, ybuf_ref, sems = rest
    else:
        sh_ref, sc_ref, w_ref, o_ref, z_ref, ybuf_ref, sems = rest
    i = pl.program_id(0)
    n_blocks = pl.num_programs(0)
    slot = i % 2

    def start_block(block, s):
        _for_each_group(block, len_ref, lambda e, n: pltpu.make_async_copy(
            ys_ref.at[_group(dst_ref[block, e], n)],
            ybuf_ref.at[s, _group(src_ref[block, e], n)], sems.at[s]).start())

    @pl.when(i == 0)
    def _():
        ybuf_ref[...] = jnp.zeros_like(ybuf_ref)
        start_block(0, 0)

    @pl.when(i + 1 < n_blocks)
    def _():
        start_block(i + 1, 1 - slot)

    rows = _group(0, pl.multiple_of(tot_ref[i], GROUP))
    pltpu.make_async_copy(ys_ref.at[rows], ybuf_ref.at[slot, rows], sems.at[slot]).wait()
    lpos = lpos_ref[...]
    targets = [jnp.broadcast_to(lpos[:, k:k + 1], (TR, KB)) for k in range(TOP_K)]
    cols = lax.broadcasted_iota(jnp.int32, (TR, KB), 1)
    f = jnp.zeros((TR, D_MODEL), F32)
    for jb in range(RB // KB):
        sel = _one_hot_rows(targets, cols + jb * KB)
        f = f + jnp.dot(sel, ybuf_ref[slot, jb * KB:(jb + 1) * KB, :].astype(BF16),
                        preferred_element_type=F32)
    x = ALPHA * x_ref[...] + _per_batch(f, g2_ref[0], jnp.multiply)
    out = _layer_norm(x) * lg_ref[...] + lb_ref[...]
    if last:
        o_ref[...] = jnp.transpose(out.reshape(TT, BATCH, D_MODEL), (1, 0, 2))
    else:
        o_ref[...] = out
        z_ref[...] = _in_projection(out, sh_ref, sc_ref, w_ref)


def _combine(t_off, plan, x, mod, ln_g, ln_b, ys, next_layer=None):
    n = x.shape[0] // TR
    last = next_layer is None
    row = lambda w: pl.BlockSpec((TR, w), lambda i, *_: (i, 0))
    const = lambda shape: pl.BlockSpec(shape, lambda i, *_: (0,) * len(shape))
    if last:
        extra_specs, extra_args = [], ()
        out_spec = pl.BlockSpec((BATCH, TT, D_MODEL), lambda i, *_: (0, i, 0))
        out_shape = jax.ShapeDtypeStruct((BATCH, n * TT, D_MODEL), F32)
    else:
        next_mod, w_in_bf = next_layer
        extra_specs = [_mod_spec(0, t_off), _mod_spec(1, t_off), const((D_MODEL, D_IN))]
        extra_args = (next_mod, next_mod, w_in_bf)
        out_spec = [row(D_MODEL), row(D_IN)]
        out_shape = [jax.ShapeDtypeStruct((n * TR, D_MODEL), F32),
                     jax.ShapeDtypeStruct((n * TR, D_IN), F32)]
    grid_spec = pltpu.PrefetchScalarGridSpec(
        num_scalar_prefetch=4,
        grid=(n,),
        in_specs=[
            row(TOP_K), row(D_MODEL), _mod_spec(5, t_off),
            const((1, D_MODEL)), const((1, D_MODEL)),
            pl.BlockSpec(memory_space=pl.ANY),
        ] + extra_specs,
        out_specs=out_spec,
        scratch_shapes=[pltpu.VMEM((2, RB, D_MODEL), F32),
                        pltpu.SemaphoreType.DMA((2,))],
    )
    return pl.pallas_call(
        functools.partial(_combine_kernel, last),
        grid_spec=grid_spec,
        out_shape=out_shape,
        compiler_params=_cparams(("arbitrary",)),
        name="moe_combine",
    )(plan["src_row"], plan["n_rows"], plan["dst_row"], plan["block_rows"], plan["lpos"],
      x, mod, ln_g, ln_b, ys, *extra_args)

def _routing_plan(route, counts):
    i32 = jnp.int32
    nb = counts.shape[0]
    experts = jnp.arange(N_EXPERTS, dtype=i32)
    eid = route[:, LANE_EID:LANE_EID + TOP_K].astype(i32).reshape(nb, TR, TOP_K)
    rank = route[:, LANE_RANK:LANE_RANK + TOP_K].astype(i32).reshape(nb, TR, TOP_K)
    n = counts[:, 0, :N_EXPERTS].astype(i32)
    n = (n + GROUP - 1) // GROUP * GROUP
    src_row = jnp.cumsum(n, axis=1) - n
    seg = jnp.sum(n, axis=0)
    tiles_e = (seg + TM - 1) // TM
    tile_end = jnp.cumsum(tiles_e)
    e_start = (tile_end - tiles_e) * TM
    n_active = tile_end[-1]
    dst_row = e_start[None, :] + jnp.cumsum(n, axis=0) - n

    onehot = eid[..., None] == experts
    lpos = jnp.sum(jnp.where(onehot, src_row[:, None, None, :], 0), axis=-1) + rank
    lpos_t = jnp.concatenate(
        [lpos.transpose(0, 2, 1), jnp.full((nb, SUBLANES - TOP_K, TR), -1, i32)], axis=1)

    n_tiles = (nb * RB + TM - 1) // TM + N_EXPERTS
    t = jnp.minimum(jnp.arange(n_tiles, dtype=i32), n_active - 1)
    tile_expert = jnp.sum((t[:, None] >= tile_end[None, :]).astype(i32), axis=1)
    tile_expert = jnp.minimum(tile_expert, N_EXPERTS - 1)
    tile_first = jnp.concatenate(
        [jnp.ones((1,), i32), (tile_expert[1:] != tile_expert[:-1]).astype(i32)])
    has_rows = tiles_e > 0
    later = jnp.where(has_rows[None, :] & (experts[None, :] > experts[:, None]),
                      experts[None, :], N_EXPERTS)
    next_e = jnp.min(later, axis=1)
    next_e = jnp.where(next_e == N_EXPERTS, -1, next_e)
    group = jnp.cumsum(has_rows.astype(i32)) - 1
    pick = tile_expert[:, None] == experts[None, :]
    tile_next = jnp.sum(jnp.where(pick, next_e[None, :], 0), axis=1)
    tile_slot = jnp.sum(jnp.where(pick, group[None, :], 0), axis=1) % 2
    tile_rows = jnp.sum(jnp.where(pick, (seg + e_start)[None, :], 0), axis=1) - t * TM
    return dict(lpos=lpos.reshape(nb * TR, TOP_K), lpos_t=lpos_t, src_row=src_row, n_rows=n,
                dst_row=dst_row, block_rows=jnp.sum(n, axis=1), tail_start=e_start + seg,
                tail_len=tile_end * TM - e_start - seg,
                tile_expert=tile_expert, n_active=n_active.reshape(1), tile_first=tile_first,
                tile_next=tile_next, tile_slot=tile_slot, tile_rows=tile_rows,
                n_sorted=n_tiles * TM)


def _grid_sincos_parts():
    quarter = D_MODEL // 4
    omega = 1.0 / (10000.0 ** (jnp.arange(quarter, dtype=F32) / quarter))

    def emb1d(n):
        ang = jnp.arange(n, dtype=F32)[:, None] * omega[None, :]
        return jnp.concatenate([jnp.sin(ang), jnp.cos(ang)], axis=-1)

    er = emb1d(SEQ // GRID_W)
    ec = jnp.repeat(emb1d(GRID_W), BATCH, axis=0)
    return er.reshape(SEQ // GRID_W, 1, D_MODEL // 2), ec


def _gate_blocks(w):
    per = GATE_BLK // LRU_HEAD_DIM
    w = w.reshape(2, D_LRU // GATE_BLK, per, LRU_HEAD_DIM, LRU_HEAD_DIM)
    eye = jnp.eye(per, dtype=w.dtype)
    blk = jnp.einsum('dkpij,pq->dkpiqj', w, eye)
    return blk.reshape(2, D_LRU // GATE_BLK, GATE_BLK, GATE_BLK)


def kernel(x, c, ctx, c_ctx, w_mod, b_mod, w_in, conv_w, conv_b, gate_a_w, gate_a_b,
           gate_x_w, gate_x_b, lru_lambda, pool_w, pool_b, pool_scale, w_out, ln1_g, ln1_b,
           router_w, router_b, exp_w1, exp_b1, exp_w2, exp_b2, ln2_g, ln2_b):
    cvec = jnp.concatenate([c, c_ctx[None], jnp.zeros((2 * SUBLANES - BATCH - 1, D_MODEL), F32)])
    mod_all = _modulation(cvec, w_mod, b_mod)
    mods = jnp.stack([jnp.broadcast_to(mod_all[:, BATCH:BATCH + 1], (DEPTH, BATCH, 6 * D_MODEL)),
                      mod_all[:, :BATCH]], axis=1)
    w_in_bf = w_in.astype(BF16)
    tri = jnp.tril(jnp.ones((TR, TR), F32), -1).astype(BF16)

    er, ec = _grid_sincos_parts()
    xs, z = _entry(ctx, x, er, ec, mods[0], w_in_bf[0])
    for l in range(DEPTH):
        last = l == DEPTH - 1
        t_off = CTX_TILES if last else 0
        mod = mods[l]
        gate_w = jnp.stack([_gate_blocks(gate_a_w[l]), _gate_blocks(gate_x_w[l])],
                           axis=1).astype(BF16)
        gate_b = jnp.stack([gate_a_b[l], gate_x_b[l]], axis=1)
        hf, hr = _scan(z, conv_w[l], conv_b[l][None], gate_w, gate_b, lru_lambda[l])
        rw_pad = jnp.zeros((D_MODEL, LANES), F32).at[:, :N_EXPERTS].set(router_w[l])
        rw_hi = rw_pad.astype(BF16)
        rw_pad = jnp.stack([rw_hi, (rw_pad - rw_hi.astype(F32)).astype(BF16)])
        rb_pad = jnp.full((1, LANES), -1e30, F32).at[0, :N_EXPERTS].set(router_b[l])
        xs, h2, route, counts = _mixer(
            t_off, hf, hr, z, xs, mod, pool_w[l].astype(BF16), pool_b[l][None],
            pool_scale[l][None], w_out[l].astype(BF16), ln1_g[l][None], ln1_b[l][None],
            rw_pad, rb_pad, tri)
        plan = _routing_plan(route, counts)
        x_sorted = _dispatch(plan, h2, route)
        y_sorted = _experts(l, plan, x_sorted, exp_w1, exp_b1, exp_w2, exp_b2)
        if last:
            return _combine(t_off, plan, xs, mod, ln2_g[l][None], ln2_b[l][None], y_sorted)
        xs, z = _combine(t_off, plan, xs, mod, ln2_g[l][None], ln2_b[l][None], y_sorted,
                         (mods[l + 1], w_in_bf[l + 1]))
```

```python
import functools

import jax
import jax.numpy as jnp
from jax import lax
from jax.experimental import pallas as pl
from jax.experimental.pallas import tpu as pltpu

D_MODEL = 1024
BATCH = 8
SEQ = 2048
DEPTH = 2
GRID_W = 64
CTX_LEN = 256
D_LRU = 512
N_LRU_HEADS = 8
LRU_HEAD_DIM = D_LRU // N_LRU_HEADS
LRU_C = 8.0
D_POOL = 512
POOL_WINDOWS = (2, 4, 8, 16)
POOL_GROUP_DIM = D_POOL // len(POOL_WINDOWS)
D_IN = 2 * D_LRU + D_POOL
N_EXPERTS = 32
TOP_K = 4
D_EXPERT = D_MODEL
SWIGLU_LIMIT = 7.0
SWIGLU_ALPHA = 1.702
LN_EPS = 1e-5
ALPHA = (2.0 * DEPTH) ** 0.25

F32 = jnp.float32
BF16 = jnp.bfloat16

SUBLANES = 8
LANES = 128
TT = GRID_W
TR = TT * BATCH
T_ALL = CTX_LEN + SEQ
R_ALL = T_ALL * BATCH
N_TILES = T_ALL // TT
CTX_TILES = CTX_LEN // TT
GATE_BLK = 256
POOL_HALO = max(POOL_WINDOWS) // 2
TM = 512
FFN_CHUNK = 512
GROUP = SUBLANES
RB = TOP_K * TR + N_EXPERTS * GROUP
KB = 256
MB = 512
RB_STAGE = (RB + MB - 1) // MB * MB
D_AUG = D_MODEL + LANES
LANE_P_HI, LANE_P_MID, LANE_P_LO, LANE_EID, LANE_RANK = 0, 4, 8, 12, 16
VMEM_LIMIT = 56 * 1024 * 1024


def _cparams(sem):
    return pltpu.CompilerParams(dimension_semantics=sem, vmem_limit_bytes=VMEM_LIMIT)


def _sigmoid(x):
    return 0.5 * (1.0 + jnp.tanh(0.5 * x))


def _layer_norm(x):
    mu = jnp.mean(x, axis=-1, keepdims=True)
    xc = x - mu
    var = jnp.mean(xc * xc, axis=-1, keepdims=True)
    return xc * lax.rsqrt(var + LN_EPS)


def _per_batch(x, v, op):
    r, d = x.shape
    return op(x.reshape(r // BATCH, BATCH, d), v[None]).reshape(r, d)


def _mod_kernel(c_ref, w_ref, b_ref, o_ref):
    c = c_ref[...]
    s = c * _sigmoid(c)
    o_ref[...] = jnp.dot(s, w_ref[...], precision=lax.Precision.HIGHEST,
                         preferred_element_type=F32) + b_ref[...]


def _modulation(cvec, w_mod, b_mod):
    tn = 512
    return pl.pallas_call(
        _mod_kernel,
        grid=(DEPTH, 6 * D_MODEL // tn),
        in_specs=[
            pl.BlockSpec((2 * SUBLANES, D_MODEL), lambda l, j: (0, 0)),
            pl.BlockSpec((None, D_MODEL, tn), lambda l, j: (l, 0, j)),
            pl.BlockSpec((None, 1, tn), lambda l, j: (l, 0, j)),
        ],
        out_specs=pl.BlockSpec((None, 2 * SUBLANES, tn), lambda l, j: (l, 0, j)),
        out_shape=jax.ShapeDtypeStruct((DEPTH, 2 * SUBLANES, 6 * D_MODEL), F32),
        compiler_params=_cparams(("arbitrary", "arbitrary")),
        name="modulation",
    )(cvec, w_mod, b_mod.reshape(DEPTH, 1, 6 * D_MODEL))


def _modulate(x, sh_ref, sc_ref):
    h = _per_batch(x, 1.0 + sc_ref[0], jnp.multiply)
    return _per_batch(h, sh_ref[0], jnp.add)


def _in_projection(x, sh_ref, sc_ref, w_ref):
    return jnp.dot(_modulate(x, sh_ref, sc_ref).astype(BF16), w_ref[...],
                   preferred_element_type=F32)


def _entry_kernel(ctx_ref, x_ref, er_ref, ec_ref, sh_ref, sc_ref, w_ref, o_ref, z_ref):
    i = pl.program_id(0)

    def time_major(src_ref):
        return jnp.transpose(src_ref[...], (1, 0, 2)).reshape(TR, D_MODEL)

    @pl.when(i < CTX_TILES)
    def _():
        o_ref[...] = _layer_norm(time_major(ctx_ref))

    @pl.when(i >= CTX_TILES)
    def _():
        pos = jnp.concatenate(
            [jnp.broadcast_to(er_ref[0], (TR, D_MODEL // 2)), ec_ref[...]], axis=1)
        o_ref[...] = _layer_norm(time_major(x_ref) + pos)

    z_ref[...] = _in_projection(o_ref[...], sh_ref, sc_ref, w_ref)


def _mod_spec(chunk, t_off=0):
    return pl.BlockSpec((1, BATCH, D_MODEL),
                        lambda i, *_: ((i + t_off >= CTX_TILES).astype(jnp.int32), 0, chunk))


def _entry(ctx, x, er, ec, mod, w_in_bf):
    return pl.pallas_call(
        _entry_kernel,
        grid=(N_TILES,),
        in_specs=[
            pl.BlockSpec((BATCH, TT, D_MODEL), lambda i: (0, jnp.minimum(i, CTX_TILES - 1), 0)),
            pl.BlockSpec((BATCH, TT, D_MODEL), lambda i: (0, jnp.maximum(i - CTX_TILES, 0), 0)),
            pl.BlockSpec((1, 1, D_MODEL // 2), lambda i: (jnp.maximum(i - CTX_TILES, 0), 0, 0)),
            pl.BlockSpec((TR, D_MODEL // 2), lambda i: (0, 0)),
            _mod_spec(0), _mod_spec(1),
            pl.BlockSpec((D_MODEL, D_IN), lambda i: (0, 0)),
        ],
        out_specs=[pl.BlockSpec((TR, D_MODEL), lambda i: (i, 0)),
                   pl.BlockSpec((TR, D_IN), lambda i: (i, 0))],
        out_shape=[jax.ShapeDtypeStruct((R_ALL, D_MODEL), F32),
                   jax.ShapeDtypeStruct((R_ALL, D_IN), F32)],
        compiler_params=_cparams(("arbitrary",)),
        name="entry_ln",
    )(ctx, x, er, ec, mod, mod, w_in_bf)


def _rev_tile(i):
    return jnp.where(i < CTX_TILES, CTX_TILES - 1 - i, N_TILES - 1 + CTX_TILES - i)


def _block_diag_dot(u_bf, w_ref, d, g):
    halves = [jnp.dot(u_bf[:, k * GATE_BLK:(k + 1) * GATE_BLK], w_ref[d, g, k],
                      preferred_element_type=F32) for k in range(D_LRU // GATE_BLK)]
    return jnp.concatenate(halves, axis=1)


def _lru_coeffs(tile, z_ref, zp_ref, zn_ref, cw_ref, cb_ref, gw_ref, gb_ref, lam_ref, d):
    x = z_ref[...]
    seg_first = (tile == 0) | (tile == CTX_TILES)
    seg_last = (tile == CTX_TILES - 1) | (tile == N_TILES - 1)
    prev = jnp.where(seg_first, 0.0, zp_ref[...])
    nxt = jnp.where(seg_last, 0.0, zn_ref[...])
    xm2 = jnp.concatenate([prev, x[:-2 * BATCH]], axis=0)
    xm1 = jnp.concatenate([prev[BATCH:], x[:-BATCH]], axis=0)
    xp1 = jnp.concatenate([x[BATCH:], nxt], axis=0)
    u = (cb_ref[...] + xm2 * cw_ref[0:1] + xm1 * cw_ref[1:2]
         + x * cw_ref[2:3] + xp1 * cw_ref[3:4])
    u_bf = u.astype(BF16)
    r = _sigmoid(_block_diag_dot(u_bf, gw_ref, d, 0) + gb_ref[d, 0:1])
    ig = _sigmoid(_block_diag_dot(u_bf, gw_ref, d, 1) + gb_ref[d, 1:2])
    nl = -lam_ref[d:d + 1]
    softplus = jnp.maximum(nl, 0.0) + jnp.log(1.0 + jnp.exp(-jnp.abs(nl)))
    log_a = (-LRU_C) * r * softplus
    a = jnp.exp(log_a)
    gap = 1.0 - a * a
    mult = jnp.where(gap > 0.0, gap * lax.rsqrt(gap), 0.0)
    return a, mult * (ig * u)


def _scan_kernel(zf_ref, zfp_ref, zfn_ref, zr_ref, zrp_ref, zrn_ref,
                 cw_ref, cb_ref, gw_ref, gb_ref, lam_ref,
                 hf_ref, hr_ref, state_ref):
    i = pl.program_id(0)

    @pl.when(i == 0)
    def _():
        state_ref[...] = jnp.zeros_like(state_ref)

    af, bf = _lru_coeffs(i, zf_ref, zfp_ref, zfn_ref, cw_ref, cb_ref, gw_ref, gb_ref,
                         lam_ref, 0)
    ar, br = _lru_coeffs(_rev_tile(i), zr_ref, zrp_ref, zrn_ref, cw_ref, cb_ref, gw_ref,
                         gb_ref, lam_ref, 1)
    hf = state_ref[0]
    hr = state_ref[1]
    for s in range(TT):
        f = slice(s * BATCH, (s + 1) * BATCH)
        hf = af[f] * hf + bf[f]
        hf_ref[f, :] = hf
        b = slice((TT - 1 - s) * BATCH, (TT - s) * BATCH)
        hr = ar[b] * hr + br[b]
        hr_ref[b, :] = hr
    state_ref[0] = hf
    state_ref[1] = hr


def _scan(z, conv_w, conv_b, gate_w, gate_b, lam):
    prev_rows = 2 * BATCH
    tiles_per_prev = TR // prev_rows
    tiles_per_next = TR // BATCH
    last_next = R_ALL // BATCH - 1

    def cur(f):
        return pl.BlockSpec((TR, D_LRU), lambda i: (f(i), 0))

    def prev(f):
        return pl.BlockSpec((prev_rows, D_LRU),
                            lambda i: (jnp.maximum(f(i) * tiles_per_prev - 1, 0), 0))

    def nxt(f):
        return pl.BlockSpec((BATCH, D_LRU),
                            lambda i: (jnp.minimum((f(i) + 1) * tiles_per_next, last_next), 0))

    fwd = lambda i: i
    const = lambda shape: pl.BlockSpec(shape, lambda i: (0,) * len(shape))
    return pl.pallas_call(
        _scan_kernel,
        grid=(N_TILES,),
        in_specs=[cur(fwd), prev(fwd), nxt(fwd), cur(_rev_tile), prev(_rev_tile), nxt(_rev_tile),
                  const((4, D_LRU)), const((1, D_LRU)),
                  const((2, 2, D_LRU // GATE_BLK, GATE_BLK, GATE_BLK)),
                  const((2, 2, D_LRU)), const((2, D_LRU))],
        out_specs=[pl.BlockSpec((TR, D_LRU), lambda i: (i, 0)),
                   pl.BlockSpec((TR, D_LRU), lambda i: (_rev_tile(i), 0))],
        out_shape=[jax.ShapeDtypeStruct((R_ALL, D_LRU), F32)] * 2,
        scratch_shapes=[pltpu.VMEM((2, BATCH, D_LRU), F32)],
        compiler_params=_cparams(("arbitrary",)),
        name="lru_scan",
    )(z, z, z, z, z, z, conv_w, conv_b, gate_w, gate_b, lam)


def _gelu_tanh(y):
    c = 0.7978845608028654
    return 0.5 * y * (1.0 + jnp.tanh(c * (y + 0.044715 * (y * y * y))))


def _pool_groups(tile, xp_ref, xpp_ref, xpn_ref):
    in_ctx = tile < CTX_TILES
    prev_ok = in_ctx & (tile > 0)
    next_ok = in_ctx & (tile < CTX_TILES - 1)
    x = xp_ref[...]
    p = jnp.concatenate([jnp.where(prev_ok, xpp_ref[...], 0.0), x,
                         jnp.where(next_ok, xpn_ref[...], 0.0)], axis=0)
    p = p.reshape(TT + 2 * POOL_HALO, BATCH, D_POOL)
    lo = jnp.where(in_ctx, -TT * tile, 0)
    hi = jnp.where(in_ctx, CTX_LEN - TT * tile, TT)
    t = lax.broadcasted_iota(jnp.int32, (TT, BATCH, POOL_GROUP_DIM), 0)
    outs = []
    for g, win in enumerate(POOL_WINDOWS):
        half = win // 2
        acc = p[:, :, g * POOL_GROUP_DIM:(g + 1) * POOL_GROUP_DIM]
        width = 1
        while width < win:
            acc = acc[:acc.shape[0] - width] + acc[width:]
            width *= 2
        start = POOL_HALO - half
        wsum = acc[start:start + TT]
        cnt = (jnp.minimum(t + half, hi) - jnp.maximum(t - half, lo)).astype(F32)
        centre = p[POOL_HALO:POOL_HALO + TT, :, g * POOL_GROUP_DIM:(g + 1) * POOL_GROUP_DIM]
        outs.append((wsum / cnt - centre).reshape(TR, POOL_GROUP_DIM))
    return outs


def _mixer_kernel(t_off, hf_ref, hr_ref, y_ref, xp_ref, xpp_ref, xpn_ref, x_ref,
                  g1_ref, sh2_ref, sc2_ref, pw_ref, pb_ref, ps_ref, wo_ref,
                  lg_ref, lb_ref, rw_ref, rb_ref, tri_ref,
                  xo_ref, route_ref, cnt_ref):
    i = pl.program_id(0)
    tile = i + t_off

    lru = (hf_ref[...] + hr_ref[...]) * _gelu_tanh(y_ref[...])
    diffs = _pool_groups(tile, xp_ref, xpp_ref, xpn_ref)
    pooled = jnp.concatenate(
        [jnp.dot(d.astype(BF16), pw_ref[g], preferred_element_type=F32)
         for g, d in enumerate(diffs)], axis=1)
    pooled = (pooled + pb_ref[...]) * ps_ref[...]
    mix = (jnp.dot(lru.astype(BF16), wo_ref[0:D_LRU, :], preferred_element_type=F32)
           + jnp.dot(pooled.astype(BF16), wo_ref[D_LRU:, :], preferred_element_type=F32))
    x = ALPHA * x_ref[...] + _per_batch(mix, g1_ref[0], jnp.multiply)
    x = _layer_norm(x) * lg_ref[...] + lb_ref[...]
    xo_ref[...] = x
    h2 = _modulate(x, sh2_ref, sc2_ref)

    h_hi = h2.astype(BF16)
    h_lo = (h2 - h_hi.astype(F32)).astype(BF16)
    logits = (jnp.dot(h_hi, rw_ref[0], preferred_element_type=F32)
              + jnp.dot(h_lo, rw_ref[0], preferred_element_type=F32)
              + jnp.dot(h_hi, rw_ref[1], preferred_element_type=F32)) + rb_ref[...]
    lane = lax.broadcasted_iota(jnp.int32, (TR, LANES), 1).astype(F32)
    work = logits
    vals, idxs, sels = [], [], []
    for _ in range(TOP_K):
        m = jnp.max(work, axis=1, keepdims=True)
        idx = jnp.min(jnp.where(work == m, lane, float(LANES)), axis=1, keepdims=True)
        sel = lane == idx
        vals.append(m)
        idxs.append(idx)
        sels.append(sel)
        work = jnp.where(sel, -jnp.inf, work)
    exps = [jnp.exp(v - vals[0]) for v in vals]
    denom = exps[0] + exps[1] + exps[2] + exps[3]
    chosen = jnp.zeros((TR, LANES), F32)
    for sel in sels:
        chosen = chosen + sel.astype(F32)
    before = jnp.dot(tri_ref[...], chosen.astype(BF16), preferred_element_type=F32)
    route = jnp.zeros((TR, LANES), F32)
    for k in range(TOP_K):
        rank = jnp.sum(jnp.where(sels[k], before, 0.0), axis=1, keepdims=True)
        p = exps[k] / denom
        p_hi = p.astype(BF16).astype(F32)
        p_mid = (p - p_hi).astype(BF16).astype(F32)
        p_lo = p - p_hi - p_mid
        for base, val in ((LANE_P_HI, p_hi), (LANE_P_MID, p_mid), (LANE_P_LO, p_lo),
                          (LANE_EID, idxs[k]), (LANE_RANK, rank)):
            route = jnp.where(lane == float(base + k), val, route)
    route_ref[...] = route
    cnt_ref[0] = jnp.broadcast_to(jnp.sum(chosen, axis=0, keepdims=True), (SUBLANES, LANES))


def _mixer(t_off, hf, hr, z, x, mod, pool_w_bf, pool_b, pool_scale, w_out_bf,
           ln_g, ln_b, rw_pad, rb_pad, tri):
    n = N_TILES - t_off
    halo_rows = POOL_HALO * BATCH
    per = TR // halo_rows
    last_halo = R_ALL // halo_rows - 1
    xp_col = 2 * D_LRU // D_POOL
    row = lambda w: pl.BlockSpec((TR, w), lambda i: (i + t_off, 0))
    out_row = lambda w: pl.BlockSpec((TR, w), lambda i: (i, 0))
    const = lambda shape: pl.BlockSpec(shape, lambda i: (0,) * len(shape))

    def mod_spec(chunk):
        return pl.BlockSpec((1, BATCH, D_MODEL),
                            lambda i: ((i + t_off >= CTX_TILES).astype(jnp.int32), 0, chunk))

    return pl.pallas_call(
        functools.partial(_mixer_kernel, t_off),
        grid=(n,),
        in_specs=[
            row(D_LRU), row(D_LRU),
            pl.BlockSpec((TR, D_LRU), lambda i: (i + t_off, 1)),
            pl.BlockSpec((TR, D_POOL), lambda i: (i + t_off, xp_col)),
            pl.BlockSpec((halo_rows, D_POOL),
                         lambda i: (jnp.maximum((i + t_off) * per - 1, 0), xp_col)),
            pl.BlockSpec((halo_rows, D_POOL),
                         lambda i: (jnp.minimum((i + t_off + 1) * per, last_halo), xp_col)),
            row(D_MODEL),
            mod_spec(2), mod_spec(3), mod_spec(4),
            const((len(POOL_WINDOWS), POOL_GROUP_DIM, POOL_GROUP_DIM)),
            const((1, D_POOL)), const((1, D_POOL)),
            const((D_MODEL, D_MODEL)),
            const((1, D_MODEL)), const((1, D_MODEL)),
            const((2, D_MODEL, LANES)), const((1, LANES)),
            const((TR, TR)),
        ],
        out_specs=[out_row(D_MODEL), out_row(LANES),
                   pl.BlockSpec((1, SUBLANES, LANES), lambda i: (i, 0, 0))],
        out_shape=[jax.ShapeDtypeStruct((n * TR, D_MODEL), F32),
                   jax.ShapeDtypeStruct((n * TR, LANES), F32),
                   jax.ShapeDtypeStruct((n, SUBLANES, LANES), F32)],
        compiler_params=_cparams(("arbitrary",)),
        name="mixer_out",
    )(hf, hr, z, z, z, z, x, mod, mod, mod, pool_w_bf, pool_b, pool_scale, w_out_bf,
      ln_g, ln_b, rw_pad, rb_pad, tri)


def _one_hot_rows(targets, index):
    hit = jnp.zeros(index.shape, F32)
    for t in reversed(targets):
        hit = jnp.where(index == t, 1.0, hit)
    return hit.astype(BF16)


def _for_each_group(block, len_ref, body):
    def step(e, carry):
        n = pl.multiple_of(len_ref[block, e], GROUP)

        @pl.when(n > 0)
        def _():
            body(e, n)

        return carry

    lax.fori_loop(0, N_EXPERTS, step, 0)


def _group(start, n):
    return pl.ds(pl.multiple_of(start, GROUP), n)


def _dispatch_kernel(src_ref, len_ref, dst_ref, tot_ref, tail_s_ref, tail_n_ref, na_ref,
                     lpos_ref, x_ref, sh2_ref, sc2_ref, r_ref, xs_ref, stage_ref, zero_ref,
                     sems):
    i = pl.program_id(0)
    n_blocks = pl.num_programs(0) - 1
    slot = i % 2

    @pl.when(i < n_blocks)
    def _():
        lpos = lpos_ref[...]
        targets = [lpos[k:k + 1, :] for k in range(TOP_K)]
        h = _modulate(x_ref[...], sh2_ref, sc2_ref).astype(BF16)
        r = r_ref[...].astype(BF16)

        def permute(jb, carry):
            base = pl.multiple_of(jb * MB, MB)
            rows = lax.broadcasted_iota(jnp.int32, (MB, TR), 0) + base
            sel = _one_hot_rows(targets, rows)
            stage_ref[slot, pl.ds(base, MB), 0:D_MODEL] = jnp.dot(
                sel, h, preferred_element_type=F32)
            stage_ref[slot, pl.ds(base, MB), D_MODEL:D_AUG] = jnp.dot(
                sel, r, preferred_element_type=F32)
            return carry

        lax.fori_loop(0, (tot_ref[i] + MB - 1) // MB, permute, 0)
        _for_each_group(i, len_ref, lambda e, n: pltpu.make_async_copy(
            stage_ref.at[slot, _group(src_ref[i, e], n)],
            xs_ref.at[_group(dst_ref[i, e], n)], sems.at[slot]).start())

    @pl.when(i > 0)
    def _():
        rows = _group(0, pl.multiple_of(tot_ref[i - 1], GROUP))
        pltpu.make_async_copy(stage_ref.at[1 - slot, rows], xs_ref.at[rows],
                              sems.at[1 - slot]).wait()

    @pl.when(i == n_blocks)
    def _():
        zero_ref[...] = jnp.zeros_like(zero_ref)
        fill = sems.at[2]

        def expert_tail(e, n):
            return pltpu.make_async_copy(zero_ref.at[_group(0, n)],
                                         xs_ref.at[_group(tail_s_ref[0, e], n)], fill)

        def whole_tile(t):
            return pltpu.make_async_copy(zero_ref, xs_ref.at[_group(t * TM, TM)], fill)

        n_tiles = xs_ref.shape[0] // TM
        _for_each_group(0, tail_n_ref, lambda e, n: expert_tail(e, n).start())
        lax.fori_loop(na_ref[0], n_tiles, lambda t, c: (whole_tile(t).start(), c)[1], 0)
        _for_each_group(0, tail_n_ref, lambda e, n: expert_tail(e, n).wait())
        lax.fori_loop(na_ref[0], n_tiles, lambda t, c: (whole_tile(t).wait(), c)[1], 0)


def _dispatch(t_off, plan, x, mod, route):
    n = x.shape[0] // TR
    last = n - 1
    grid_spec = pltpu.PrefetchScalarGridSpec(
        num_scalar_prefetch=7,
        grid=(n + 1,),
        in_specs=[
            pl.BlockSpec((None, SUBLANES, TR), lambda i, *_: (jnp.minimum(i, last), 0, 0)),
            pl.BlockSpec((TR, D_MODEL), lambda i, *_: (jnp.minimum(i, last), 0)),
            _mod_spec(3, t_off), _mod_spec(4, t_off),
            pl.BlockSpec((TR, LANES), lambda i, *_: (jnp.minimum(i, last), 0)),
        ],
        out_specs=pl.BlockSpec(memory_space=pl.ANY),
        scratch_shapes=[pltpu.VMEM((2, RB_STAGE, D_AUG), F32),
                        pltpu.VMEM((TM, D_AUG), F32),
                        pltpu.SemaphoreType.DMA((3,))],
    )
    return pl.pallas_call(
        _dispatch_kernel,
        grid_spec=grid_spec,
        out_shape=jax.ShapeDtypeStruct((plan["n_sorted"], D_AUG), F32),
        compiler_params=_cparams(("arbitrary",)),
        name="moe_dispatch",
    )(plan["src_row"], plan["n_rows"], plan["dst_row"], plan["block_rows"],
      plan["tail_start"][None], plan["tail_len"][None], plan["n_active"], plan["lpos_t"],
      x, mod, mod, route)


def _expert_kernel(layer, te_ref, na_ref, first_ref, nxt_ref, slot_ref,
                   x_ref, b1_ref, b2_ref, w1_hbm, w2_hbm, o_ref,
                   w1buf_ref, w2buf_ref, w1c_ref, w2c_ref, sems):
    i = pl.program_id(0)
    e = te_ref[i]
    slot = slot_ref[i]

    def fetch(expert, s):
        return (pltpu.make_async_copy(w1_hbm.at[layer, expert], w1buf_ref.at[s], sems.at[s, 0]),
                pltpu.make_async_copy(w2_hbm.at[layer, expert], w2buf_ref.at[s], sems.at[s, 1]))

    @pl.when(i == 0)
    def _():
        for copy in fetch(e, slot):
            copy.start()

    @pl.when(first_ref[i] == 1)
    def _():
        for copy in fetch(e, slot):
            copy.wait()
        w1c_ref[...] = w1buf_ref[slot].astype(BF16)
        w2c_ref[...] = w2buf_ref[slot].astype(BF16)

        @pl.when(nxt_ref[i] >= 0)
        def _():
            for copy in fetch(nxt_ref[i], 1 - slot):
                copy.start()

    @pl.when(i < na_ref[0])
    def _():
        aug = x_ref[:, D_MODEL:D_AUG]
        e_f = e.astype(F32)
        gate = jnp.zeros((TM, 1), F32)
        for k in range(TOP_K):
            p = (aug[:, LANE_P_HI + k:LANE_P_HI + k + 1]
                 + aug[:, LANE_P_MID + k:LANE_P_MID + k + 1]
                 + aug[:, LANE_P_LO + k:LANE_P_LO + k + 1])
            gate = gate + jnp.where(aug[:, LANE_EID + k:LANE_EID + k + 1] == e_f, p, 0.0)
        x = x_ref[:, 0:D_MODEL].astype(BF16)
        y = jnp.broadcast_to(b2_ref[...], (TM, D_MODEL))
        for c in range(D_EXPERT // FFN_CHUNK):
            g_cols = slice(c * FFN_CHUNK, (c + 1) * FFN_CHUNK)
            l_cols = slice(D_EXPERT + c * FFN_CHUNK, D_EXPERT + (c + 1) * FFN_CHUNK)
            glu = jnp.dot(x, w1c_ref[:, g_cols], preferred_element_type=F32) + b1_ref[:, g_cols]
            lin = jnp.dot(x, w1c_ref[:, l_cols], preferred_element_type=F32) + b1_ref[:, l_cols]
            glu = jnp.minimum(glu, SWIGLU_LIMIT)
            lin = jnp.clip(lin, -SWIGLU_LIMIT, SWIGLU_LIMIT)
            act = glu * _sigmoid(SWIGLU_ALPHA * glu) * (lin + 1.0)
            y = y + jnp.dot(act.astype(BF16), w2c_ref[g_cols, :], preferred_element_type=F32)
        o_ref[...] = gate * y

    @pl.when(i >= na_ref[0])
    def _():
        o_ref[...] = jnp.zeros_like(o_ref)


def _experts(layer, plan, xs, w1, b1, w2, b2):
    n_tiles = xs.shape[0] // TM
    act_tile = lambda i, te, na, *_: (jnp.maximum(jnp.minimum(i, na[0] - 1), 0), 0)
    expert = lambda i, te, *_: (layer, te[i], 0, 0)
    grid_spec = pltpu.PrefetchScalarGridSpec(
        num_scalar_prefetch=5,
        grid=(n_tiles,),
        in_specs=[
            pl.BlockSpec((TM, D_AUG), act_tile),
            pl.BlockSpec((None, None, 1, 2 * D_EXPERT), expert),
            pl.BlockSpec((None, None, 1, D_MODEL), expert),
            pl.BlockSpec(memory_space=pl.ANY),
            pl.BlockSpec(memory_space=pl.ANY),
        ],
        out_specs=pl.BlockSpec((TM, D_MODEL), lambda i, *_: (i, 0)),
        scratch_shapes=[pltpu.VMEM((2, D_MODEL, 2 * D_EXPERT), F32),
                        pltpu.VMEM((2, D_EXPERT, D_MODEL), F32),
                        pltpu.VMEM((D_MODEL, 2 * D_EXPERT), BF16),
                        pltpu.VMEM((D_EXPERT, D_MODEL), BF16),
                        pltpu.SemaphoreType.DMA((2, 2))],
    )
    return pl.pallas_call(
        functools.partial(_expert_kernel, layer),
        grid_spec=grid_spec,
        out_shape=jax.ShapeDtypeStruct((xs.shape[0], D_MODEL), F32),
        compiler_params=_cparams(("arbitrary",)),
        name="moe_experts",
    )(plan["tile_expert"], plan["n_active"], plan["tile_first"], plan["tile_next"],
      plan["tile_slot"], xs, b1.reshape(DEPTH, N_EXPERTS, 1, -1),
      b2.reshape(DEPTH, N_EXPERTS, 1, -1), w1, w2)


def _combine_kernel(last, src_ref, len_ref, dst_ref, tot_ref, lpos_ref, x_ref,
                    g2_ref, lg_ref, lb_ref, ys_ref, *rest):
    if last:
        o_ref, ybuf_ref, sems = rest
    else:
        sh_ref, sc_ref, w_ref, o_ref, z_ref, ybuf_ref, sems = rest
    i = pl.program_id(0)
    n_blocks = pl.num_programs(0)
    slot = i % 2

    def start_block(block, s):
        _for_each_group(block, len_ref, lambda e, n: pltpu.make_async_copy(
            ys_ref.at[_group(dst_ref[block, e], n)],
            ybuf_ref.at[s, _group(src_ref[block, e], n)], sems.at[s]).start())

    @pl.when(i == 0)
    def _():
        ybuf_ref[...] = jnp.zeros_like(ybuf_ref)
        start_block(0, 0)

    @pl.when(i + 1 < n_blocks)
    def _():
        start_block(i + 1, 1 - slot)

    rows = _group(0, pl.multiple_of(tot_ref[i], GROUP))
    pltpu.make_async_copy(ys_ref.at[rows], ybuf_ref.at[slot, rows], sems.at[slot]).wait()
    lpos = lpos_ref[...]
    targets = [jnp.broadcast_to(lpos[:, k:k + 1], (TR, KB)) for k in range(TOP_K)]
    cols = lax.broadcasted_iota(jnp.int32, (TR, KB), 1)
    f = jnp.zeros((TR, D_MODEL), F32)
    for jb in range(RB // KB):
        sel = _one_hot_rows(targets, cols + jb * KB)
        f = f + jnp.dot(sel, ybuf_ref[slot, jb * KB:(jb + 1) * KB, :].astype(BF16),
                        preferred_element_type=F32)
    x = ALPHA * x_ref[...] + _per_batch(f, g2_ref[0], jnp.multiply)
    out = _layer_norm(x) * lg_ref[...] + lb_ref[...]
    if last:
        o_ref[...] = jnp.transpose(out.reshape(TT, BATCH, D_MODEL), (1, 0, 2))
    else:
        o_ref[...] = out
        z_ref[...] = _in_projection(out, sh_ref, sc_ref, w_ref)


def _combine(t_off, plan, x, mod, ln_g, ln_b, ys, next_layer=None):
    n = x.shape[0] // TR
    last = next_layer is None
    row = lambda w: pl.BlockSpec((TR, w), lambda i, *_: (i, 0))
    const = lambda shape: pl.BlockSpec(shape, lambda i, *_: (0,) * len(shape))
    if last:
        extra_specs, extra_args = [], ()
        out_spec = pl.BlockSpec((BATCH, TT, D_MODEL), lambda i, *_: (0, i, 0))
        out_shape = jax.ShapeDtypeStruct((BATCH, n * TT, D_MODEL), F32)
    else:
        next_mod, w_in_bf = next_layer
        extra_specs = [_mod_spec(0, t_off), _mod_spec(1, t_off), const((D_MODEL, D_IN))]
        extra_args = (next_mod, next_mod, w_in_bf)
        out_spec = [row(D_MODEL), row(D_IN)]
        out_shape = [jax.ShapeDtypeStruct((n * TR, D_MODEL), F32),
                     jax.ShapeDtypeStruct((n * TR, D_IN), F32)]
    grid_spec = pltpu.PrefetchScalarGridSpec(
        num_scalar_prefetch=4,
        grid=(n,),
        in_specs=[
            row(TOP_K), row(D_MODEL), _mod_spec(5, t_off),
            const((1, D_MODEL)), const((1, D_MODEL)),
            pl.BlockSpec(memory_space=pl.ANY),
        ] + extra_specs,
        out_specs=out_spec,
        scratch_shapes=[pltpu.VMEM((2, RB, D_MODEL), F32),
                        pltpu.SemaphoreType.DMA((2,))],
    )
    return pl.pallas_call(
        functools.partial(_combine_kernel, last),
        grid_spec=grid_spec,
        out_shape=out_shape,
        compiler_params=_cparams(("arbitrary",)),
        name="moe_combine",
    )(plan["src_row"], plan["n_rows"], plan["dst_row"], plan["block_rows"], plan["lpos"],
      x, mod, ln_g, ln_b, ys, *extra_args)

def _routing_plan(route, counts):
    i32 = jnp.int32
    nb = counts.shape[0]
    experts = jnp.arange(N_EXPERTS, dtype=i32)
    eid = route[:, LANE_EID:LANE_EID + TOP_K].astype(i32).reshape(nb, TR, TOP_K)
    rank = route[:, LANE_RANK:LANE_RANK + TOP_K].astype(i32).reshape(nb, TR, TOP_K)
    n = counts[:, 0, :N_EXPERTS].astype(i32)
    n = (n + GROUP - 1) // GROUP * GROUP
    src_row = jnp.cumsum(n, axis=1) - n
    seg = jnp.sum(n, axis=0)
    tiles_e = (seg + TM - 1) // TM
    tile_end = jnp.cumsum(tiles_e)
    e_start = (tile_end - tiles_e) * TM
    n_active = tile_end[-1]
    dst_row = e_start[None, :] + jnp.cumsum(n, axis=0) - n

    onehot = eid[..., None] == experts
    lpos = jnp.sum(jnp.where(onehot, src_row[:, None, None, :], 0), axis=-1) + rank
    lpos_t = jnp.concatenate(
        [lpos.transpose(0, 2, 1), jnp.full((nb, SUBLANES - TOP_K, TR), -1, i32)], axis=1)

    n_tiles = (nb * RB + TM - 1) // TM + N_EXPERTS
    t = jnp.minimum(jnp.arange(n_tiles, dtype=i32), n_active - 1)
    tile_expert = jnp.sum((t[:, None] >= tile_end[None, :]).astype(i32), axis=1)
    tile_expert = jnp.minimum(tile_expert, N_EXPERTS - 1)
    tile_first = jnp.concatenate(
        [jnp.ones((1,), i32), (tile_expert[1:] != tile_expert[:-1]).astype(i32)])
    has_rows = tiles_e > 0
    later = jnp.where(has_rows[None, :] & (experts[None, :] > experts[:, None]),
                      experts[None, :], N_EXPERTS)
    next_e = jnp.min(later, axis=1)
    next_e = jnp.where(next_e == N_EXPERTS, -1, next_e)
    group = jnp.cumsum(has_rows.astype(i32)) - 1
    pick = tile_expert[:, None] == experts[None, :]
    tile_next = jnp.sum(jnp.where(pick, next_e[None, :], 0), axis=1)
    tile_slot = jnp.sum(jnp.where(pick, group[None, :], 0), axis=1) % 2
    return dict(lpos=lpos.reshape(nb * TR, TOP_K), lpos_t=lpos_t, src_row=src_row, n_rows=n,
                dst_row=dst_row, block_rows=jnp.sum(n, axis=1), tail_start=e_start + seg,
                tail_len=tile_end * TM - e_start - seg,
                tile_expert=tile_expert, n_active=n_active.reshape(1), tile_first=tile_first,
                tile_next=tile_next, tile_slot=tile_slot,
                n_sorted=n_tiles * TM)


def _grid_sincos_parts():
    quarter = D_MODEL // 4
    omega = 1.0 / (10000.0 ** (jnp.arange(quarter, dtype=F32) / quarter))

    def emb1d(n):
        ang = jnp.arange(n, dtype=F32)[:, None] * omega[None, :]
        return jnp.concatenate([jnp.sin(ang), jnp.cos(ang)], axis=-1)

    er = emb1d(SEQ // GRID_W)
    ec = jnp.repeat(emb1d(GRID_W), BATCH, axis=0)
    return er.reshape(SEQ // GRID_W, 1, D_MODEL // 2), ec


def _gate_blocks(w):
    per = GATE_BLK // LRU_HEAD_DIM
    w = w.reshape(2, D_LRU // GATE_BLK, per, LRU_HEAD_DIM, LRU_HEAD_DIM)
    eye = jnp.eye(per, dtype=w.dtype)
    blk = jnp.einsum('dkpij,pq->dkpiqj', w, eye)
    return blk.reshape(2, D_LRU // GATE_BLK, GATE_BLK, GATE_BLK)


def kernel(x, c, ctx, c_ctx, w_mod, b_mod, w_in, conv_w, conv_b, gate_a_w, gate_a_b,
           gate_x_w, gate_x_b, lru_lambda, pool_w, pool_b, pool_scale, w_out, ln1_g, ln1_b,
           router_w, router_b, exp_w1, exp_b1, exp_w2, exp_b2, ln2_g, ln2_b):
    cvec = jnp.concatenate([c, c_ctx[None], jnp.zeros((2 * SUBLANES - BATCH - 1, D_MODEL), F32)])
    mod_all = _modulation(cvec, w_mod, b_mod)
    mods = jnp.stack([jnp.broadcast_to(mod_all[:, BATCH:BATCH + 1], (DEPTH, BATCH, 6 * D_MODEL)),
                      mod_all[:, :BATCH]], axis=1)
    w_in_bf = w_in.astype(BF16)
    tri = jnp.tril(jnp.ones((TR, TR), F32), -1).astype(BF16)

    er, ec = _grid_sincos_parts()
    xs, z = _entry(ctx, x, er, ec, mods[0], w_in_bf[0])
    for l in range(DEPTH):
        last = l == DEPTH - 1
        t_off = CTX_TILES if last else 0
        mod = mods[l]
        gate_w = jnp.stack([_gate_blocks(gate_a_w[l]), _gate_blocks(gate_x_w[l])],
                           axis=1).astype(BF16)
        gate_b = jnp.stack([gate_a_b[l], gate_x_b[l]], axis=1)
        hf, hr = _scan(z, conv_w[l], conv_b[l][None], gate_w, gate_b, lru_lambda[l])
        rw_pad = jnp.zeros((D_MODEL, LANES), F32).at[:, :N_EXPERTS].set(router_w[l])
        rw_hi = rw_pad.astype(BF16)
        rw_pad = jnp.stack([rw_hi, (rw_pad - rw_hi.astype(F32)).astype(BF16)])
        rb_pad = jnp.full((1, LANES), -1e30, F32).at[0, :N_EXPERTS].set(router_b[l])
        xs, route, counts = _mixer(
            t_off, hf, hr, z, xs, mod, pool_w[l].astype(BF16), pool_b[l][None],
            pool_scale[l][None], w_out[l].astype(BF16), ln1_g[l][None], ln1_b[l][None],
            rw_pad, rb_pad, tri)
        plan = _routing_plan(route, counts)
        x_sorted = _dispatch(t_off, plan, xs, mod, route)
        y_sorted = _experts(l, plan, x_sorted, exp_w1, exp_b1, exp_w2, exp_b2)
        if last:
            return _combine(t_off, plan, xs, mod, ln2_g[l][None], ln2_b[l][None], y_sorted)
        xs, z = _combine(t_off, plan, xs, mod, ln2_g[l][None], ln2_b[l][None], y_sorted,
                         (mods[l + 1], w_in_bf[l + 1]))
```

```python
import functools

import jax
import jax.numpy as jnp
from jax import lax
from jax.experimental import pallas as pl
from jax.experimental.pallas import tpu as pltpu

D_MODEL = 1024
BATCH = 8
SEQ = 2048
DEPTH = 2
GRID_W = 64
CTX_LEN = 256
D_LRU = 512
N_LRU_HEADS = 8
LRU_HEAD_DIM = D_LRU // N_LRU_HEADS
LRU_C = 8.0
D_POOL = 512
POOL_WINDOWS = (2, 4, 8, 16)
POOL_GROUP_DIM = D_POOL // len(POOL_WINDOWS)
D_IN = 2 * D_LRU + D_POOL
N_EXPERTS = 32
TOP_K = 4
D_EXPERT = D_MODEL
SWIGLU_LIMIT = 7.0
SWIGLU_ALPHA = 1.702
LN_EPS = 1e-5
ALPHA = (2.0 * DEPTH) ** 0.25

F32 = jnp.float32
BF16 = jnp.bfloat16

SUBLANES = 8
LANES = 128
TT = GRID_W
TR = TT * BATCH
T_ALL = CTX_LEN + SEQ
R_ALL = T_ALL * BATCH
N_TILES = T_ALL // TT
CTX_TILES = CTX_LEN // TT
GATE_BLK = 256
POOL_HALO = max(POOL_WINDOWS) // 2
TM = 512
FFN_CHUNK = 512
GROUP = SUBLANES
RB = TOP_K * TR + N_EXPERTS * GROUP
KB = 256
MB = 512
RB_STAGE = (RB + MB - 1) // MB * MB
D_AUG = D_MODEL + LANES
LANE_P, LANE_EID, LANE_RANK = 0, 4, 8
VMEM_LIMIT = 56 * 1024 * 1024


def _cparams(sem):
    return pltpu.CompilerParams(dimension_semantics=sem, vmem_limit_bytes=VMEM_LIMIT)


def _sigmoid(x):
    return 0.5 * (1.0 + jnp.tanh(0.5 * x))


def _layer_norm(x):
    mu = jnp.mean(x, axis=-1, keepdims=True)
    xc = x - mu
    var = jnp.mean(xc * xc, axis=-1, keepdims=True)
    return xc * lax.rsqrt(var + LN_EPS)


def _per_batch(x, v, op):
    r, d = x.shape
    return op(x.reshape(r // BATCH, BATCH, d), v[None]).reshape(r, d)


def _mod_kernel(c_ref, w_ref, b_ref, o_ref):
    c = c_ref[...]
    s = c * _sigmoid(c)
    w = w_ref[...]
    s_hi, w_hi = s.astype(BF16), w.astype(BF16)
    s_lo = (s - s_hi.astype(F32)).astype(BF16)
    w_lo = (w - w_hi.astype(F32)).astype(BF16)
    o_ref[...] = (jnp.dot(s_hi, w_hi, preferred_element_type=F32)
                  + jnp.dot(s_lo, w_hi, preferred_element_type=F32)
                  + jnp.dot(s_hi, w_lo, preferred_element_type=F32)) + b_ref[...]


def _modulation(cvec, w_mod, b_mod):
    tn = 512
    return pl.pallas_call(
        _mod_kernel,
        grid=(DEPTH, 6 * D_MODEL // tn),
        in_specs=[
            pl.BlockSpec((2 * SUBLANES, D_MODEL), lambda l, j: (0, 0)),
            pl.BlockSpec((None, D_MODEL, tn), lambda l, j: (l, 0, j)),
            pl.BlockSpec((None, 1, tn), lambda l, j: (l, 0, j)),
        ],
        out_specs=pl.BlockSpec((None, 2 * SUBLANES, tn), lambda l, j: (l, 0, j)),
        out_shape=jax.ShapeDtypeStruct((DEPTH, 2 * SUBLANES, 6 * D_MODEL), F32),
        compiler_params=_cparams(("arbitrary", "arbitrary")),
        name="modulation",
    )(cvec, w_mod, b_mod.reshape(DEPTH, 1, 6 * D_MODEL))


def _modulate(x, sh_ref, sc_ref):
    h = _per_batch(x, 1.0 + sc_ref[0], jnp.multiply)
    return _per_batch(h, sh_ref[0], jnp.add)


def _in_projection(x, sh_ref, sc_ref, w_ref):
    return jnp.dot(_modulate(x, sh_ref, sc_ref).astype(BF16), w_ref[...],
                   preferred_element_type=F32)


def _entry_kernel(ctx_ref, x_ref, er_ref, ec_ref, sh_ref, sc_ref, w_ref, o_ref, z_ref):
    i = pl.program_id(0)

    def time_major(src_ref):
        return jnp.transpose(src_ref[...], (1, 0, 2)).reshape(TR, D_MODEL)

    @pl.when(i < CTX_TILES)
    def _():
        o_ref[...] = _layer_norm(time_major(ctx_ref))

    @pl.when(i >= CTX_TILES)
    def _():
        pos = jnp.concatenate(
            [jnp.broadcast_to(er_ref[0], (TR, D_MODEL // 2)), ec_ref[...]], axis=1)
        o_ref[...] = _layer_norm(time_major(x_ref) + pos)

    z_ref[...] = _in_projection(o_ref[...], sh_ref, sc_ref, w_ref)


def _mod_spec(chunk, t_off=0):
    return pl.BlockSpec((1, BATCH, D_MODEL),
                        lambda i, *_: ((i + t_off >= CTX_TILES).astype(jnp.int32), 0, chunk))


def _entry(ctx, x, er, ec, mod, w_in_bf):
    return pl.pallas_call(
        _entry_kernel,
        grid=(N_TILES,),
        in_specs=[
            pl.BlockSpec((BATCH, TT, D_MODEL), lambda i: (0, jnp.minimum(i, CTX_TILES - 1), 0)),
            pl.BlockSpec((BATCH, TT, D_MODEL), lambda i: (0, jnp.maximum(i - CTX_TILES, 0), 0)),
            pl.BlockSpec((1, 1, D_MODEL // 2), lambda i: (jnp.maximum(i - CTX_TILES, 0), 0, 0)),
            pl.BlockSpec((TR, D_MODEL // 2), lambda i: (0, 0)),
            _mod_spec(0), _mod_spec(1),
            pl.BlockSpec((D_MODEL, D_IN), lambda i: (0, 0)),
        ],
        out_specs=[pl.BlockSpec((TR, D_MODEL), lambda i: (i, 0)),
                   pl.BlockSpec((TR, D_IN), lambda i: (i, 0))],
        out_shape=[jax.ShapeDtypeStruct((R_ALL, D_MODEL), F32),
                   jax.ShapeDtypeStruct((R_ALL, D_IN), F32)],
        compiler_params=_cparams(("arbitrary",)),
        name="entry_ln",
    )(ctx, x, er, ec, mod, mod, w_in_bf)


def _rev_tile(i):
    return jnp.where(i < CTX_TILES, CTX_TILES - 1 - i, N_TILES - 1 + CTX_TILES - i)


def _block_diag_dot(u_bf, w_ref, d, g):
    halves = [jnp.dot(u_bf[:, k * GATE_BLK:(k + 1) * GATE_BLK], w_ref[d, g, k],
                      preferred_element_type=F32) for k in range(D_LRU // GATE_BLK)]
    return jnp.concatenate(halves, axis=1)


def _lru_coeffs(tile, z_ref, zp_ref, zn_ref, cw_ref, cb_ref, gw_ref, gb_ref, lam_ref, d):
    x = z_ref[...]
    seg_first = (tile == 0) | (tile == CTX_TILES)
    seg_last = (tile == CTX_TILES - 1) | (tile == N_TILES - 1)
    prev = jnp.where(seg_first, 0.0, zp_ref[...])
    nxt = jnp.where(seg_last, 0.0, zn_ref[...])
    xm2 = jnp.concatenate([prev, x[:-2 * BATCH]], axis=0)
    xm1 = jnp.concatenate([prev[BATCH:], x[:-BATCH]], axis=0)
    xp1 = jnp.concatenate([x[BATCH:], nxt], axis=0)
    u = (cb_ref[...] + xm2 * cw_ref[0:1] + xm1 * cw_ref[1:2]
         + x * cw_ref[2:3] + xp1 * cw_ref[3:4])
    u_bf = u.astype(BF16)
    r = _sigmoid(_block_diag_dot(u_bf, gw_ref, d, 0) + gb_ref[d, 0:1])
    ig = _sigmoid(_block_diag_dot(u_bf, gw_ref, d, 1) + gb_ref[d, 1:2])
    nl = -lam_ref[d:d + 1]
    softplus = jnp.maximum(nl, 0.0) + jnp.log(1.0 + jnp.exp(-jnp.abs(nl)))
    log_a = (-LRU_C) * r * softplus
    a = jnp.exp(log_a)
    gap = 1.0 - a * a
    mult = jnp.where(gap > 0.0, gap * lax.rsqrt(gap), 0.0)
    return a, mult * (ig * u)


def _scan_kernel(zf_ref, zfp_ref, zfn_ref, zr_ref, zrp_ref, zrn_ref,
                 cw_ref, cb_ref, gw_ref, gb_ref, lam_ref,
                 hf_ref, hr_ref, state_ref):
    i = pl.program_id(0)

    @pl.when(i == 0)
    def _():
        state_ref[...] = jnp.zeros_like(state_ref)

    af, bf = _lru_coeffs(i, zf_ref, zfp_ref, zfn_ref, cw_ref, cb_ref, gw_ref, gb_ref,
                         lam_ref, 0)
    ar, br = _lru_coeffs(_rev_tile(i), zr_ref, zrp_ref, zrn_ref, cw_ref, cb_ref, gw_ref,
                         gb_ref, lam_ref, 1)
    hf = state_ref[0]
    hr = state_ref[1]
    for s in range(TT):
        f = slice(s * BATCH, (s + 1) * BATCH)
        hf = af[f] * hf + bf[f]
        hf_ref[f, :] = hf
        b = slice((TT - 1 - s) * BATCH, (TT - s) * BATCH)
        hr = ar[b] * hr + br[b]
        hr_ref[b, :] = hr
    state_ref[0] = hf
    state_ref[1] = hr


def _scan(z, conv_w, conv_b, gate_w, gate_b, lam):
    prev_rows = 2 * BATCH
    tiles_per_prev = TR // prev_rows
    tiles_per_next = TR // BATCH
    last_next = R_ALL // BATCH - 1

    def cur(f):
        return pl.BlockSpec((TR, D_LRU), lambda i: (f(i), 0))

    def prev(f):
        return pl.BlockSpec((prev_rows, D_LRU),
                            lambda i: (jnp.maximum(f(i) * tiles_per_prev - 1, 0), 0))

    def nxt(f):
        return pl.BlockSpec((BATCH, D_LRU),
                            lambda i: (jnp.minimum((f(i) + 1) * tiles_per_next, last_next), 0))

    fwd = lambda i: i
    const = lambda shape: pl.BlockSpec(shape, lambda i: (0,) * len(shape))
    return pl.pallas_call(
        _scan_kernel,
        grid=(N_TILES,),
        in_specs=[cur(fwd), prev(fwd), nxt(fwd), cur(_rev_tile), prev(_rev_tile), nxt(_rev_tile),
                  const((4, D_LRU)), const((1, D_LRU)),
                  const((2, 2, D_LRU // GATE_BLK, GATE_BLK, GATE_BLK)),
                  const((2, 2, D_LRU)), const((2, D_LRU))],
        out_specs=[pl.BlockSpec((TR, D_LRU), lambda i: (i, 0)),
                   pl.BlockSpec((TR, D_LRU), lambda i: (_rev_tile(i), 0))],
        out_shape=[jax.ShapeDtypeStruct((R_ALL, D_LRU), F32)] * 2,
        scratch_shapes=[pltpu.VMEM((2, BATCH, D_LRU), F32)],
        compiler_params=_cparams(("arbitrary",)),
        name="lru_scan",
    )(z, z, z, z, z, z, conv_w, conv_b, gate_w, gate_b, lam)


def _gelu_tanh(y):
    c = 0.7978845608028654
    return 0.5 * y * (1.0 + jnp.tanh(c * (y + 0.044715 * (y * y * y))))


def _pool_groups(tile, xp_ref, xpp_ref, xpn_ref):
    in_ctx = tile < CTX_TILES
    prev_ok = in_ctx & (tile > 0)
    next_ok = in_ctx & (tile < CTX_TILES - 1)
    x = xp_ref[...]
    p = jnp.concatenate([jnp.where(prev_ok, xpp_ref[...], 0.0), x,
                         jnp.where(next_ok, xpn_ref[...], 0.0)], axis=0)
    p = p.reshape(TT + 2 * POOL_HALO, BATCH, D_POOL)
    lo = jnp.where(in_ctx, -TT * tile, 0)
    hi = jnp.where(in_ctx, CTX_LEN - TT * tile, TT)
    t = lax.broadcasted_iota(jnp.int32, (TT, BATCH, POOL_GROUP_DIM), 0)
    outs = []
    for g, win in enumerate(POOL_WINDOWS):
        half = win // 2
        acc = p[:, :, g * POOL_GROUP_DIM:(g + 1) * POOL_GROUP_DIM]
        width = 1
        while width < win:
            acc = acc[:acc.shape[0] - width] + acc[width:]
            width *= 2
        start = POOL_HALO - half
        wsum = acc[start:start + TT]
        cnt = (jnp.minimum(t + half, hi) - jnp.maximum(t - half, lo)).astype(F32)
        centre = p[POOL_HALO:POOL_HALO + TT, :, g * POOL_GROUP_DIM:(g + 1) * POOL_GROUP_DIM]
        outs.append((wsum / cnt - centre).reshape(TR, POOL_GROUP_DIM))
    return outs


def _mixer_kernel(t_off, hf_ref, hr_ref, y_ref, xp_ref, xpp_ref, xpn_ref, x_ref,
                  g1_ref, sh2_ref, sc2_ref, pw_ref, pb_ref, ps_ref, wo_ref,
                  lg_ref, lb_ref, rw_ref, rb_ref, tri_ref,
                  xo_ref, route_ref, cnt_ref):
    i = pl.program_id(0)
    tile = i + t_off

    lru = (hf_ref[...] + hr_ref[...]) * _gelu_tanh(y_ref[...])
    diffs = _pool_groups(tile, xp_ref, xpp_ref, xpn_ref)
    pooled = jnp.concatenate(
        [jnp.dot(d.astype(BF16), pw_ref[g], preferred_element_type=F32)
         for g, d in enumerate(diffs)], axis=1)
    pooled = (pooled + pb_ref[...]) * ps_ref[...]
    mix = (jnp.dot(lru.astype(BF16), wo_ref[0:D_LRU, :], preferred_element_type=F32)
           + jnp.dot(pooled.astype(BF16), wo_ref[D_LRU:, :], preferred_element_type=F32))
    x = ALPHA * x_ref[...] + _per_batch(mix, g1_ref[0], jnp.multiply)
    x = _layer_norm(x) * lg_ref[...] + lb_ref[...]
    xo_ref[...] = x
    h2 = _modulate(x, sh2_ref, sc2_ref)

    h_hi = h2.astype(BF16)
    h_lo = (h2 - h_hi.astype(F32)).astype(BF16)
    logits = (jnp.dot(h_hi, rw_ref[0], preferred_element_type=F32)
              + jnp.dot(h_lo, rw_ref[0], preferred_element_type=F32)
              + jnp.dot(h_hi, rw_ref[1], preferred_element_type=F32)) + rb_ref[...]
    lane = lax.broadcasted_iota(jnp.int32, (TR, LANES), 1).astype(F32)
    work = logits
    vals, idxs, sels = [], [], []
    for _ in range(TOP_K):
        m = jnp.max(work, axis=1, keepdims=True)
        idx = jnp.min(jnp.where(work == m, lane, float(LANES)), axis=1, keepdims=True)
        sel = lane == idx
        vals.append(m)
        idxs.append(idx)
        sels.append(sel)
        work = jnp.where(sel, -jnp.inf, work)
    exps = [jnp.exp(v - vals[0]) for v in vals]
    denom = exps[0] + exps[1] + exps[2] + exps[3]
    chosen = jnp.zeros((TR, LANES), F32)
    for sel in sels:
        chosen = chosen + sel.astype(F32)
    before = jnp.dot(tri_ref[...], chosen.astype(BF16), preferred_element_type=F32)
    route = jnp.zeros((TR, LANES), F32)
    for k in range(TOP_K):
        rank = jnp.sum(jnp.where(sels[k], before, 0.0), axis=1, keepdims=True)
        for base, val in ((LANE_P, exps[k] / denom), (LANE_EID, idxs[k]), (LANE_RANK, rank)):
            route = jnp.where(lane == float(base + k), val, route)
    route_ref[...] = route
    cnt_ref[0] = jnp.broadcast_to(jnp.sum(chosen, axis=0, keepdims=True), (SUBLANES, LANES))


def _mixer(t_off, hf, hr, z, x, mod, pool_w_bf, pool_b, pool_scale, w_out_bf,
           ln_g, ln_b, rw_pad, rb_pad, tri):
    n = N_TILES - t_off
    halo_rows = POOL_HALO * BATCH
    per = TR // halo_rows
    last_halo = R_ALL // halo_rows - 1
    xp_col = 2 * D_LRU // D_POOL
    row = lambda w: pl.BlockSpec((TR, w), lambda i: (i + t_off, 0))
    out_row = lambda w: pl.BlockSpec((TR, w), lambda i: (i, 0))
    const = lambda shape: pl.BlockSpec(shape, lambda i: (0,) * len(shape))

    def mod_spec(chunk):
        return pl.BlockSpec((1, BATCH, D_MODEL),
                            lambda i: ((i + t_off >= CTX_TILES).astype(jnp.int32), 0, chunk))

    return pl.pallas_call(
        functools.partial(_mixer_kernel, t_off),
        grid=(n,),
        in_specs=[
            row(D_LRU), row(D_LRU),
            pl.BlockSpec((TR, D_LRU), lambda i: (i + t_off, 1)),
            pl.BlockSpec((TR, D_POOL), lambda i: (i + t_off, xp_col)),
            pl.BlockSpec((halo_rows, D_POOL),
                         lambda i: (jnp.maximum((i + t_off) * per - 1, 0), xp_col)),
            pl.BlockSpec((halo_rows, D_POOL),
                         lambda i: (jnp.minimum((i + t_off + 1) * per, last_halo), xp_col)),
            row(D_MODEL),
            mod_spec(2), mod_spec(3), mod_spec(4),
            const((len(POOL_WINDOWS), POOL_GROUP_DIM, POOL_GROUP_DIM)),
            const((1, D_POOL)), const((1, D_POOL)),
            const((D_MODEL, D_MODEL)),
            const((1, D_MODEL)), const((1, D_MODEL)),
            const((2, D_MODEL, LANES)), const((1, LANES)),
            const((TR, TR)),
        ],
        out_specs=[out_row(D_MODEL), out_row(LANES),
                   pl.BlockSpec((1, SUBLANES, LANES), lambda i: (i, 0, 0))],
        out_shape=[jax.ShapeDtypeStruct((n * TR, D_MODEL), F32),
                   jax.ShapeDtypeStruct((n * TR, LANES), F32),
                   jax.ShapeDtypeStruct((n, SUBLANES, LANES), F32)],
        compiler_params=_cparams(("arbitrary",)),
        name="mixer_out",
    )(hf, hr, z, z, z, z, x, mod, mod, mod, pool_w_bf, pool_b, pool_scale, w_out_bf,
      ln_g, ln_b, rw_pad, rb_pad, tri)


def _one_hot_rows(targets, index, weights=None):
    matches = [index == t for t in targets]
    hit = jnp.zeros(index.shape, F32)
    for m in reversed(matches):
        hit = jnp.where(m, 1.0, hit)
    if weights is None:
        return hit
    picked = jnp.zeros(index.shape, F32)
    for m, w in zip(reversed(matches), reversed(weights)):
        picked = jnp.where(m, w, picked)
    return hit, picked


def _for_each_group(block, len_ref, body):
    def step(e, carry):
        n = pl.multiple_of(len_ref[block, e], GROUP)

        @pl.when(n > 0)
        def _():
            body(e, n)

        return carry

    lax.fori_loop(0, N_EXPERTS, step, 0)


def _group(start, n):
    return pl.ds(pl.multiple_of(start, GROUP), n)


def _dispatch_kernel(src_ref, len_ref, dst_ref, tot_ref, tail_s_ref, tail_n_ref, na_ref,
                     lpos_ref, x_ref, sh2_ref, sc2_ref, xs_ref, stage_ref, zero_ref, sems):
    i = pl.program_id(0)
    n_blocks = pl.num_programs(0) - 1
    slot = i % 2

    @pl.when(i < n_blocks)
    def _():
        lpos = lpos_ref[...]
        targets = [lpos[k:k + 1, :] for k in range(TOP_K)]
        gates = [lpos[TOP_K + k:TOP_K + k + 1, :] for k in range(TOP_K)]
        h = _modulate(x_ref[...], sh2_ref, sc2_ref).astype(BF16)

        def permute(jb, carry):
            base = pl.multiple_of(jb * MB, MB)
            rows = (lax.broadcasted_iota(jnp.int32, (MB, TR), 0) + base).astype(F32)
            sel, picked = _one_hot_rows(targets, rows, gates)
            stage_ref[slot, pl.ds(base, MB), 0:D_MODEL] = jnp.dot(
                sel.astype(BF16), h, preferred_element_type=F32)
            gate = jnp.sum(picked, axis=1, keepdims=True)
            stage_ref[slot, pl.ds(base, MB), D_MODEL:D_AUG] = jnp.broadcast_to(gate, (MB, LANES))
            return carry

        lax.fori_loop(0, (tot_ref[i] + MB - 1) // MB, permute, 0)
        _for_each_group(i, len_ref, lambda e, n: pltpu.make_async_copy(
            stage_ref.at[slot, _group(src_ref[i, e], n)],
            xs_ref.at[_group(dst_ref[i, e], n)], sems.at[slot]).start())

    @pl.when(i > 0)
    def _():
        rows = _group(0, pl.multiple_of(tot_ref[i - 1], GROUP))
        pltpu.make_async_copy(stage_ref.at[1 - slot, rows], xs_ref.at[rows],
                              sems.at[1 - slot]).wait()

    @pl.when(i == n_blocks)
    def _():
        zero_ref[...] = jnp.zeros_like(zero_ref)
        fill = sems.at[2]

        def expert_tail(e, n):
            return pltpu.make_async_copy(zero_ref.at[_group(0, n)],
                                         xs_ref.at[_group(tail_s_ref[0, e], n)], fill)

        def whole_tile(t):
            return pltpu.make_async_copy(zero_ref, xs_ref.at[_group(t * TM, TM)], fill)

        n_tiles = xs_ref.shape[0] // TM
        _for_each_group(0, tail_n_ref, lambda e, n: expert_tail(e, n).start())
        lax.fori_loop(na_ref[0], n_tiles, lambda t, c: (whole_tile(t).start(), c)[1], 0)
        _for_each_group(0, tail_n_ref, lambda e, n: expert_tail(e, n).wait())
        lax.fori_loop(na_ref[0], n_tiles, lambda t, c: (whole_tile(t).wait(), c)[1], 0)


def _dispatch(t_off, plan, x, mod):
    n = x.shape[0] // TR
    last = n - 1
    grid_spec = pltpu.PrefetchScalarGridSpec(
        num_scalar_prefetch=7,
        grid=(n + 1,),
        in_specs=[
            pl.BlockSpec((None, SUBLANES, TR), lambda i, *_: (jnp.minimum(i, last), 0, 0)),
            pl.BlockSpec((TR, D_MODEL), lambda i, *_: (jnp.minimum(i, last), 0)),
            _mod_spec(3, t_off), _mod_spec(4, t_off),
        ],
        out_specs=pl.BlockSpec(memory_space=pl.ANY),
        scratch_shapes=[pltpu.VMEM((2, RB_STAGE, D_AUG), F32),
                        pltpu.VMEM((TM, D_AUG), F32),
                        pltpu.SemaphoreType.DMA((3,))],
    )
    return pl.pallas_call(
        _dispatch_kernel,
        grid_spec=grid_spec,
        out_shape=jax.ShapeDtypeStruct((plan["n_sorted"], D_AUG), F32),
        compiler_params=_cparams(("arbitrary",)),
        name="moe_dispatch",
    )(plan["src_row"], plan["n_rows"], plan["dst_row"], plan["block_rows"],
      plan["tail_start"][None], plan["tail_len"][None], plan["n_active"], plan["lpos_t"],
      x, mod, mod)


def _expert_kernel(layer, te_ref, na_ref, first_ref, nxt_ref, slot_ref,
                   x_ref, b1_ref, b2_ref, w1_hbm, w2_hbm, o_ref,
                   w1buf_ref, w2buf_ref, w1c_ref, w2c_ref, sems):
    i = pl.program_id(0)
    e = te_ref[i]
    slot = slot_ref[i]

    def fetch(expert, s):
        return (pltpu.make_async_copy(w1_hbm.at[layer, expert], w1buf_ref.at[s], sems.at[s, 0]),
                pltpu.make_async_copy(w2_hbm.at[layer, expert], w2buf_ref.at[s], sems.at[s, 1]))

    @pl.when(i == 0)
    def _():
        for copy in fetch(e, slot):
            copy.start()

    @pl.when(first_ref[i] == 1)
    def _():
        for copy in fetch(e, slot):
            copy.wait()
        w1c_ref[...] = w1buf_ref[slot].astype(BF16)
        w2c_ref[...] = w2buf_ref[slot].astype(BF16)

        @pl.when(nxt_ref[i] >= 0)
        def _():
            for copy in fetch(nxt_ref[i], 1 - slot):
                copy.start()

    @pl.when(i < na_ref[0])
    def _():
        gate = x_ref[:, D_MODEL:D_MODEL + 1]
        x = x_ref[:, 0:D_MODEL].astype(BF16)
        y = jnp.broadcast_to(b2_ref[...], (TM, D_MODEL))
        for c in range(D_EXPERT // FFN_CHUNK):
            g_cols = slice(c * FFN_CHUNK, (c + 1) * FFN_CHUNK)
            l_cols = slice(D_EXPERT + c * FFN_CHUNK, D_EXPERT + (c + 1) * FFN_CHUNK)
            glu = jnp.dot(x, w1c_ref[:, g_cols], preferred_element_type=F32) + b1_ref[:, g_cols]
            lin = jnp.dot(x, w1c_ref[:, l_cols], preferred_element_type=F32) + b1_ref[:, l_cols]
            glu = jnp.minimum(glu, SWIGLU_LIMIT)
            lin = jnp.clip(lin, -SWIGLU_LIMIT, SWIGLU_LIMIT)
            act = glu * _sigmoid(SWIGLU_ALPHA * glu) * (lin + 1.0)
            y = y + jnp.dot(act.astype(BF16), w2c_ref[g_cols, :], preferred_element_type=F32)
        o_ref[...] = gate * y

    @pl.when(i >= na_ref[0])
    def _():
        o_ref[...] = jnp.zeros_like(o_ref)


def _experts(layer, plan, xs, w1, b1, w2, b2):
    n_tiles = xs.shape[0] // TM
    act_tile = lambda i, te, na, *_: (jnp.maximum(jnp.minimum(i, na[0] - 1), 0), 0)
    expert = lambda i, te, *_: (layer, te[i], 0, 0)
    grid_spec = pltpu.PrefetchScalarGridSpec(
        num_scalar_prefetch=5,
        grid=(n_tiles,),
        in_specs=[
            pl.BlockSpec((TM, D_AUG), act_tile),
            pl.BlockSpec((None, None, 1, 2 * D_EXPERT), expert),
            pl.BlockSpec((None, None, 1, D_MODEL), expert),
            pl.BlockSpec(memory_space=pl.ANY),
            pl.BlockSpec(memory_space=pl.ANY),
        ],
        out_specs=pl.BlockSpec((TM, D_MODEL), lambda i, *_: (i, 0)),
        scratch_shapes=[pltpu.VMEM((2, D_MODEL, 2 * D_EXPERT), F32),
                        pltpu.VMEM((2, D_EXPERT, D_MODEL), F32),
                        pltpu.VMEM((D_MODEL, 2 * D_EXPERT), BF16),
                        pltpu.VMEM((D_EXPERT, D_MODEL), BF16),
                        pltpu.SemaphoreType.DMA((2, 2))],
    )
    return pl.pallas_call(
        functools.partial(_expert_kernel, layer),
        grid_spec=grid_spec,
        out_shape=jax.ShapeDtypeStruct((xs.shape[0], D_MODEL), F32),
        compiler_params=_cparams(("arbitrary",)),
        name="moe_experts",
    )(plan["tile_expert"], plan["n_active"], plan["tile_first"], plan["tile_next"],
      plan["tile_slot"], xs, b1.reshape(DEPTH, N_EXPERTS, 1, -1),
      b2.reshape(DEPTH, N_EXPERTS, 1, -1), w1, w2)


def _combine_kernel(last, src_ref, len_ref, dst_ref, tot_ref, lpos_ref, x_ref,
                    g2_ref, lg_ref, lb_ref, ys_ref, *rest):
    if last:
        o_ref, ybuf_ref, sems = rest
    else:
        sh_ref, sc_ref, w_ref, o_ref, z_ref, ybuf_ref, sems = rest
    i = pl.program_id(0)
    n_blocks = pl.num_programs(0)
    slot = i % 2

    def start_block(block, s):
        _for_each_group(block, len_ref, lambda e, n: pltpu.make_async_copy(
            ys_ref.at[_group(dst_ref[block, e], n)],
            ybuf_ref.at[s, _group(src_ref[block, e], n)], sems.at[s]).start())

    @pl.when(i == 0)
    def _():
        ybuf_ref[...] = jnp.zeros_like(ybuf_ref)
        start_block(0, 0)

    @pl.when(i + 1 < n_blocks)
    def _():
        start_block(i + 1, 1 - slot)

    rows = _group(0, pl.multiple_of(tot_ref[i], GROUP))
    pltpu.make_async_copy(ys_ref.at[rows], ybuf_ref.at[slot, rows], sems.at[slot]).wait()
    lpos = lpos_ref[...]
    targets = [jnp.broadcast_to(lpos[:, k:k + 1], (TR, KB)) for k in range(TOP_K)]
    cols = lax.broadcasted_iota(jnp.int32, (TR, KB), 1)
    f = jnp.zeros((TR, D_MODEL), F32)
    for jb in range(RB // KB):
        sel = _one_hot_rows(targets, cols + jb * KB).astype(BF16)
        f = f + jnp.dot(sel, ybuf_ref[slot, jb * KB:(jb + 1) * KB, :].astype(BF16),
                        preferred_element_type=F32)
    x = ALPHA * x_ref[...] + _per_batch(f, g2_ref[0], jnp.multiply)
    out = _layer_norm(x) * lg_ref[...] + lb_ref[...]
    if last:
        o_ref[...] = jnp.transpose(out.reshape(TT, BATCH, D_MODEL), (1, 0, 2))
    else:
        o_ref[...] = out
        z_ref[...] = _in_projection(out, sh_ref, sc_ref, w_ref)


def _combine(t_off, plan, x, mod, ln_g, ln_b, ys, next_layer=None):
    n = x.shape[0] // TR
    last = next_layer is None
    row = lambda w: pl.BlockSpec((TR, w), lambda i, *_: (i, 0))
    const = lambda shape: pl.BlockSpec(shape, lambda i, *_: (0,) * len(shape))
    if last:
        extra_specs, extra_args = [], ()
        out_spec = pl.BlockSpec((BATCH, TT, D_MODEL), lambda i, *_: (0, i, 0))
        out_shape = jax.ShapeDtypeStruct((BATCH, n * TT, D_MODEL), F32)
    else:
        next_mod, w_in_bf = next_layer
        extra_specs = [_mod_spec(0, t_off), _mod_spec(1, t_off), const((D_MODEL, D_IN))]
        extra_args = (next_mod, next_mod, w_in_bf)
        out_spec = [row(D_MODEL), row(D_IN)]
        out_shape = [jax.ShapeDtypeStruct((n * TR, D_MODEL), F32),
                     jax.ShapeDtypeStruct((n * TR, D_IN), F32)]
    grid_spec = pltpu.PrefetchScalarGridSpec(
        num_scalar_prefetch=4,
        grid=(n,),
        in_specs=[
            row(TOP_K), row(D_MODEL), _mod_spec(5, t_off),
            const((1, D_MODEL)), const((1, D_MODEL)),
            pl.BlockSpec(memory_space=pl.ANY),
        ] + extra_specs,
        out_specs=out_spec,
        scratch_shapes=[pltpu.VMEM((2, RB, D_MODEL), F32),
                        pltpu.SemaphoreType.DMA((2,))],
    )
    return pl.pallas_call(
        functools.partial(_combine_kernel, last),
        grid_spec=grid_spec,
        out_shape=out_shape,
        compiler_params=_cparams(("arbitrary",)),
        name="moe_combine",
    )(plan["src_row"], plan["n_rows"], plan["dst_row"], plan["block_rows"], plan["lpos"],
      x, mod, ln_g, ln_b, ys, *extra_args)

def _routing_plan(route, counts):
    i32 = jnp.int32
    nb = counts.shape[0]
    experts = jnp.arange(N_EXPERTS, dtype=i32)
    eid = route[:, LANE_EID:LANE_EID + TOP_K].astype(i32).reshape(nb, TR, TOP_K)
    rank = route[:, LANE_RANK:LANE_RANK + TOP_K].astype(i32).reshape(nb, TR, TOP_K)
    n = counts[:, 0, :N_EXPERTS].astype(i32)
    n = (n + GROUP - 1) // GROUP * GROUP
    src_row = jnp.cumsum(n, axis=1) - n
    seg = jnp.sum(n, axis=0)
    tiles_e = (seg + TM - 1) // TM
    tile_end = jnp.cumsum(tiles_e)
    e_start = (tile_end - tiles_e) * TM
    n_active = tile_end[-1]
    dst_row = e_start[None, :] + jnp.cumsum(n, axis=0) - n

    onehot = eid[..., None] == experts
    lpos = jnp.sum(jnp.where(onehot, src_row[:, None, None, :], 0), axis=-1) + rank
    gate = route[:, LANE_P:LANE_P + TOP_K].reshape(nb, TR, TOP_K)
    lpos_t = jnp.concatenate([lpos.astype(F32), gate], axis=2).transpose(0, 2, 1)

    n_tiles = (nb * RB + TM - 1) // TM + N_EXPERTS
    t = jnp.minimum(jnp.arange(n_tiles, dtype=i32), n_active - 1)
    tile_expert = jnp.sum((t[:, None] >= tile_end[None, :]).astype(i32), axis=1)
    tile_expert = jnp.minimum(tile_expert, N_EXPERTS - 1)
    tile_first = jnp.concatenate(
        [jnp.ones((1,), i32), (tile_expert[1:] != tile_expert[:-1]).astype(i32)])
    has_rows = tiles_e > 0
    later = jnp.where(has_rows[None, :] & (experts[None, :] > experts[:, None]),
                      experts[None, :], N_EXPERTS)
    next_e = jnp.min(later, axis=1)
    next_e = jnp.where(next_e == N_EXPERTS, -1, next_e)
    group = jnp.cumsum(has_rows.astype(i32)) - 1
    pick = tile_expert[:, None] == experts[None, :]
    tile_next = jnp.sum(jnp.where(pick, next_e[None, :], 0), axis=1)
    tile_slot = jnp.sum(jnp.where(pick, group[None, :], 0), axis=1) % 2
    return dict(lpos=lpos.reshape(nb * TR, TOP_K), lpos_t=lpos_t, src_row=src_row, n_rows=n,
                dst_row=dst_row, block_rows=jnp.sum(n, axis=1), tail_start=e_start + seg,
                tail_len=tile_end * TM - e_start - seg,
                tile_expert=tile_expert, n_active=n_active.reshape(1), tile_first=tile_first,
                tile_next=tile_next, tile_slot=tile_slot,
                n_sorted=n_tiles * TM)


def _grid_sincos_parts():
    quarter = D_MODEL // 4
    omega = 1.0 / (10000.0 ** (jnp.arange(quarter, dtype=F32) / quarter))

    def emb1d(n):
        ang = jnp.arange(n, dtype=F32)[:, None] * omega[None, :]
        return jnp.concatenate([jnp.sin(ang), jnp.cos(ang)], axis=-1)

    er = emb1d(SEQ // GRID_W)
    ec = jnp.repeat(emb1d(GRID_W), BATCH, axis=0)
    return er.reshape(SEQ // GRID_W, 1, D_MODEL // 2), ec


def _gate_blocks(w):
    per = GATE_BLK // LRU_HEAD_DIM
    w = w.reshape(2, D_LRU // GATE_BLK, per, LRU_HEAD_DIM, LRU_HEAD_DIM)
    eye = jnp.eye(per, dtype=w.dtype)
    blk = jnp.einsum('dkpij,pq->dkpiqj', w, eye)
    return blk.reshape(2, D_LRU // GATE_BLK, GATE_BLK, GATE_BLK)


def kernel(x, c, ctx, c_ctx, w_mod, b_mod, w_in, conv_w, conv_b, gate_a_w, gate_a_b,
           gate_x_w, gate_x_b, lru_lambda, pool_w, pool_b, pool_scale, w_out, ln1_g, ln1_b,
           router_w, router_b, exp_w1, exp_b1, exp_w2, exp_b2, ln2_g, ln2_b):
    cvec = jnp.concatenate([c, c_ctx[None], jnp.zeros((2 * SUBLANES - BATCH - 1, D_MODEL), F32)])
    mod_all = _modulation(cvec, w_mod, b_mod)
    mods = jnp.stack([jnp.broadcast_to(mod_all[:, BATCH:BATCH + 1], (DEPTH, BATCH, 6 * D_MODEL)),
                      mod_all[:, :BATCH]], axis=1)
    w_in_bf = w_in.astype(BF16)
    tri = jnp.tril(jnp.ones((TR, TR), F32), -1).astype(BF16)

    er, ec = _grid_sincos_parts()
    xs, z = _entry(ctx, x, er, ec, mods[0], w_in_bf[0])
    for l in range(DEPTH):
        last = l == DEPTH - 1
        t_off = CTX_TILES if last else 0
        mod = mods[l]
        gate_w = jnp.stack([_gate_blocks(gate_a_w[l]), _gate_blocks(gate_x_w[l])],
                           axis=1).astype(BF16)
        gate_b = jnp.stack([gate_a_b[l], gate_x_b[l]], axis=1)
        hf, hr = _scan(z, conv_w[l], conv_b[l][None], gate_w, gate_b, lru_lambda[l])
        rw_pad = jnp.zeros((D_MODEL, LANES), F32).at[:, :N_EXPERTS].set(router_w[l])
        rw_hi = rw_pad.astype(BF16)
        rw_pad = jnp.stack([rw_hi, (rw_pad - rw_hi.astype(F32)).astype(BF16)])
        rb_pad = jnp.full((1, LANES), -1e30, F32).at[0, :N_EXPERTS].set(router_b[l])
        xs, route, counts = _mixer(
            t_off, hf, hr, z, xs, mod, pool_w[l].astype(BF16), pool_b[l][None],
            pool_scale[l][None], w_out[l].astype(BF16), ln1_g[l][None], ln1_b[l][None],
            rw_pad, rb_pad, tri)
        plan = _routing_plan(route, counts)
        x_sorted = _dispatch(t_off, plan, xs, mod)
        y_sorted = _experts(l, plan, x_sorted, exp_w1, exp_b1, exp_w2, exp_b2)
        if last:
            return _combine(t_off, plan, xs, mod, ln2_g[l][None], ln2_b[l][None], y_sorted)
        xs, z = _combine(t_off, plan, xs, mod, ln2_g[l][None], ln2_b[l][None], y_sorted,
                         (mods[l + 1], w_in_bf[l + 1]))
```

```python
import functools

import jax
import jax.numpy as jnp
from jax import lax
from jax.experimental import pallas as pl
from jax.experimental.pallas import tpu as pltpu

D_MODEL = 1024
BATCH = 8
SEQ = 2048
DEPTH = 2
GRID_W = 64
CTX_LEN = 256
D_LRU = 512
N_LRU_HEADS = 8
LRU_HEAD_DIM = D_LRU // N_LRU_HEADS
LRU_C = 8.0
D_POOL = 512
POOL_WINDOWS = (2, 4, 8, 16)
POOL_GROUP_DIM = D_POOL // len(POOL_WINDOWS)
D_IN = 2 * D_LRU + D_POOL
N_EXPERTS = 32
TOP_K = 4
D_EXPERT = D_MODEL
SWIGLU_LIMIT = 7.0
SWIGLU_ALPHA = 1.702
LN_EPS = 1e-5
ALPHA = (2.0 * DEPTH) ** 0.25

F32 = jnp.float32
BF16 = jnp.bfloat16

SUBLANES = 8
LANES = 128
TT = GRID_W
TR = TT * BATCH
T_ALL = CTX_LEN + SEQ
R_ALL = T_ALL * BATCH
N_TILES = T_ALL // TT
CTX_TILES = CTX_LEN // TT
GATE_BLK = 256
POOL_HALO = max(POOL_WINDOWS) // 2
TM = 512
FFN_CHUNK = 512
GROUP = SUBLANES
RB = TOP_K * TR + N_EXPERTS * GROUP
KB = 256
MB = 512
RB_STAGE = (RB + MB - 1) // MB * MB
D_AUG = D_MODEL + LANES
LANE_P, LANE_EID, LANE_RANK = 0, 4, 8
VMEM_LIMIT = 56 * 1024 * 1024


def _cparams(sem):
    return pltpu.CompilerParams(dimension_semantics=sem, vmem_limit_bytes=VMEM_LIMIT)


def _sigmoid(x):
    return 0.5 * (1.0 + jnp.tanh(0.5 * x))


def _layer_norm(x):
    mu = jnp.mean(x, axis=-1, keepdims=True)
    xc = x - mu
    var = jnp.mean(xc * xc, axis=-1, keepdims=True)
    return xc * lax.rsqrt(var + LN_EPS)


def _per_batch(x, v, op):
    r, d = x.shape
    return op(x.reshape(r // BATCH, BATCH, d), v[None]).reshape(r, d)


def _mod_kernel(c_ref, w_ref, b_ref, o_ref):
    c = c_ref[...]
    s = c * _sigmoid(c)
    w = w_ref[...]
    s_hi, w_hi = s.astype(BF16), w.astype(BF16)
    s_lo = (s - s_hi.astype(F32)).astype(BF16)
    w_lo = (w - w_hi.astype(F32)).astype(BF16)
    o_ref[...] = (jnp.dot(s_hi, w_hi, preferred_element_type=F32)
                  + jnp.dot(s_lo, w_hi, preferred_element_type=F32)
                  + jnp.dot(s_hi, w_lo, preferred_element_type=F32)) + b_ref[...]


def _modulation(cvec, w_mod, b_mod):
    tn = 512
    return pl.pallas_call(
        _mod_kernel,
        grid=(DEPTH, 6 * D_MODEL // tn),
        in_specs=[
            pl.BlockSpec((2 * SUBLANES, D_MODEL), lambda l, j: (0, 0)),
            pl.BlockSpec((None, D_MODEL, tn), lambda l, j: (l, 0, j)),
            pl.BlockSpec((None, 1, tn), lambda l, j: (l, 0, j)),
        ],
        out_specs=pl.BlockSpec((None, 2 * SUBLANES, tn), lambda l, j: (l, 0, j)),
        out_shape=jax.ShapeDtypeStruct((DEPTH, 2 * SUBLANES, 6 * D_MODEL), F32),
        compiler_params=_cparams(("arbitrary", "arbitrary")),
        name="modulation",
    )(cvec, w_mod, b_mod.reshape(DEPTH, 1, 6 * D_MODEL))


def _modulate(x, sh_ref, sc_ref):
    h = _per_batch(x, 1.0 + sc_ref[0], jnp.multiply)
    return _per_batch(h, sh_ref[0], jnp.add)


def _in_projection(x, sh_ref, sc_ref, w_ref):
    return jnp.dot(_modulate(x, sh_ref, sc_ref).astype(BF16), w_ref[...],
                   preferred_element_type=F32)


def _cast_once(w_ref, wbf_ref):
    @pl.when(pl.program_id(0) == 0)
    def _():
        wbf_ref[...] = w_ref[...].astype(BF16)


def _entry_kernel(ctx_ref, x_ref, er_ref, ec_ref, sh_ref, sc_ref, w_ref, o_ref, z_ref, wbf_ref):
    i = pl.program_id(0)
    _cast_once(w_ref, wbf_ref)

    def time_major(src_ref):
        return jnp.transpose(src_ref[...], (1, 0, 2)).reshape(TR, D_MODEL)

    @pl.when(i < CTX_TILES)
    def _():
        o_ref[...] = _layer_norm(time_major(ctx_ref))

    @pl.when(i >= CTX_TILES)
    def _():
        pos = jnp.concatenate(
            [jnp.broadcast_to(er_ref[0], (TR, D_MODEL // 2)), ec_ref[...]], axis=1)
        o_ref[...] = _layer_norm(time_major(x_ref) + pos)

    z_ref[...] = _in_projection(o_ref[...], sh_ref, sc_ref, wbf_ref)


def _mod_spec(chunk, t_off=0):
    return pl.BlockSpec((1, BATCH, D_MODEL),
                        lambda i, *_: ((i + t_off >= CTX_TILES).astype(jnp.int32), 0, chunk))


def _entry(ctx, x, er, ec, mod, w_in):
    return pl.pallas_call(
        _entry_kernel,
        grid=(N_TILES,),
        in_specs=[
            pl.BlockSpec((BATCH, TT, D_MODEL), lambda i: (0, jnp.minimum(i, CTX_TILES - 1), 0)),
            pl.BlockSpec((BATCH, TT, D_MODEL), lambda i: (0, jnp.maximum(i - CTX_TILES, 0), 0)),
            pl.BlockSpec((1, 1, D_MODEL // 2), lambda i: (jnp.maximum(i - CTX_TILES, 0), 0, 0)),
            pl.BlockSpec((TR, D_MODEL // 2), lambda i: (0, 0)),
            _mod_spec(0), _mod_spec(1),
            pl.BlockSpec((None, D_MODEL, D_IN), lambda i: (0, 0, 0)),
        ],
        out_specs=[pl.BlockSpec((TR, D_MODEL), lambda i: (i, 0)),
                   pl.BlockSpec((TR, D_IN), lambda i: (i, 0))],
        out_shape=[jax.ShapeDtypeStruct((R_ALL, D_MODEL), F32),
                   jax.ShapeDtypeStruct((R_ALL, D_IN), F32)],
        scratch_shapes=[pltpu.VMEM((D_MODEL, D_IN), BF16)],
        compiler_params=_cparams(("arbitrary",)),
        name="entry_ln",
    )(ctx, x, er, ec, mod, mod, w_in)


def _rev_tile(i):
    return jnp.where(i < CTX_TILES, CTX_TILES - 1 - i, N_TILES - 1 + CTX_TILES - i)


def _block_diag_dot(u_bf, w_ref, d, g):
    halves = [jnp.dot(u_bf[:, k * GATE_BLK:(k + 1) * GATE_BLK], w_ref[d, g, k],
                      preferred_element_type=F32) for k in range(D_LRU // GATE_BLK)]
    return jnp.concatenate(halves, axis=1)


def _lru_coeffs(tile, z_ref, zp_ref, zn_ref, cw_ref, cb_ref, gw_ref, gb_ref, lam_ref, d):
    x = z_ref[...]
    seg_first = (tile == 0) | (tile == CTX_TILES)
    seg_last = (tile == CTX_TILES - 1) | (tile == N_TILES - 1)
    prev = jnp.where(seg_first, 0.0, zp_ref[...])
    nxt = jnp.where(seg_last, 0.0, zn_ref[...])
    xm2 = jnp.concatenate([prev, x[:-2 * BATCH]], axis=0)
    xm1 = jnp.concatenate([prev[BATCH:], x[:-BATCH]], axis=0)
    xp1 = jnp.concatenate([x[BATCH:], nxt], axis=0)
    u = (cb_ref[...] + xm2 * cw_ref[0:1] + xm1 * cw_ref[1:2]
         + x * cw_ref[2:3] + xp1 * cw_ref[3:4])
    u_bf = u.astype(BF16)
    r = _sigmoid(_block_diag_dot(u_bf, gw_ref, d, 0) + gb_ref[d, 0:1])
    ig = _sigmoid(_block_diag_dot(u_bf, gw_ref, d, 1) + gb_ref[d, 1:2])
    nl = -lam_ref[d:d + 1]
    softplus = jnp.maximum(nl, 0.0) + jnp.log(1.0 + jnp.exp(-jnp.abs(nl)))
    log_a = (-LRU_C) * r * softplus
    a = jnp.exp(log_a)
    gap = 1.0 - a * a
    mult = jnp.where(gap > 0.0, gap * lax.rsqrt(gap), 0.0)
    return a, mult * (ig * u)


def _scan_kernel(zf_ref, zfp_ref, zfn_ref, zr_ref, zrp_ref, zrn_ref,
                 cw_ref, cb_ref, gw_ref, gb_ref, lam_ref,
                 hf_ref, hr_ref, state_ref):
    i = pl.program_id(0)

    @pl.when(i == 0)
    def _():
        state_ref[...] = jnp.zeros_like(state_ref)

    af, bf = _lru_coeffs(i, zf_ref, zfp_ref, zfn_ref, cw_ref, cb_ref, gw_ref, gb_ref,
                         lam_ref, 0)
    ar, br = _lru_coeffs(_rev_tile(i), zr_ref, zrp_ref, zrn_ref, cw_ref, cb_ref, gw_ref,
                         gb_ref, lam_ref, 1)
    hf = state_ref[0]
    hr = state_ref[1]
    for s in range(TT):
        f = slice(s * BATCH, (s + 1) * BATCH)
        hf = af[f] * hf + bf[f]
        hf_ref[f, :] = hf
        b = slice((TT - 1 - s) * BATCH, (TT - s) * BATCH)
        hr = ar[b] * hr + br[b]
        hr_ref[b, :] = hr
    state_ref[0] = hf
    state_ref[1] = hr


def _scan(z, conv_w, conv_b, gate_w, gate_b, lam):
    prev_rows = 2 * BATCH
    tiles_per_prev = TR // prev_rows
    tiles_per_next = TR // BATCH
    last_next = R_ALL // BATCH - 1

    def cur(f):
        return pl.BlockSpec((TR, D_LRU), lambda i: (f(i), 0))

    def prev(f):
        return pl.BlockSpec((prev_rows, D_LRU),
                            lambda i: (jnp.maximum(f(i) * tiles_per_prev - 1, 0), 0))

    def nxt(f):
        return pl.BlockSpec((BATCH, D_LRU),
                            lambda i: (jnp.minimum((f(i) + 1) * tiles_per_next, last_next), 0))

    fwd = lambda i: i
    const = lambda shape: pl.BlockSpec(shape, lambda i: (0,) * len(shape))
    return pl.pallas_call(
        _scan_kernel,
        grid=(N_TILES,),
        in_specs=[cur(fwd), prev(fwd), nxt(fwd), cur(_rev_tile), prev(_rev_tile), nxt(_rev_tile),
                  const((4, D_LRU)), const((1, D_LRU)),
                  const((2, 2, D_LRU // GATE_BLK, GATE_BLK, GATE_BLK)),
                  const((2, 2, D_LRU)), const((2, D_LRU))],
        out_specs=[pl.BlockSpec((TR, D_LRU), lambda i: (i, 0)),
                   pl.BlockSpec((TR, D_LRU), lambda i: (_rev_tile(i), 0))],
        out_shape=[jax.ShapeDtypeStruct((R_ALL, D_LRU), F32)] * 2,
        scratch_shapes=[pltpu.VMEM((2, BATCH, D_LRU), F32)],
        compiler_params=_cparams(("arbitrary",)),
        name="lru_scan",
    )(z, z, z, z, z, z, conv_w, conv_b, gate_w, gate_b, lam)


def _gelu_tanh(y):
    c = 0.7978845608028654
    return 0.5 * y * (1.0 + jnp.tanh(c * (y + 0.044715 * (y * y * y))))


def _pool_groups(tile, xp_ref, xpp_ref, xpn_ref):
    in_ctx = tile < CTX_TILES
    prev_ok = in_ctx & (tile > 0)
    next_ok = in_ctx & (tile < CTX_TILES - 1)
    x = xp_ref[...]
    p = jnp.concatenate([jnp.where(prev_ok, xpp_ref[...], 0.0), x,
                         jnp.where(next_ok, xpn_ref[...], 0.0)], axis=0)
    p = p.reshape(TT + 2 * POOL_HALO, BATCH, D_POOL)
    lo = jnp.where(in_ctx, -TT * tile, 0)
    hi = jnp.where(in_ctx, CTX_LEN - TT * tile, TT)
    t = lax.broadcasted_iota(jnp.int32, (TT, BATCH, POOL_GROUP_DIM), 0)
    outs = []
    for g, win in enumerate(POOL_WINDOWS):
        half = win // 2
        acc = p[:, :, g * POOL_GROUP_DIM:(g + 1) * POOL_GROUP_DIM]
        width = 1
        while width < win:
            acc = acc[:acc.shape[0] - width] + acc[width:]
            width *= 2
        start = POOL_HALO - half
        wsum = acc[start:start + TT]
        cnt = (jnp.minimum(t + half, hi) - jnp.maximum(t - half, lo)).astype(F32)
        centre = p[POOL_HALO:POOL_HALO + TT, :, g * POOL_GROUP_DIM:(g + 1) * POOL_GROUP_DIM]
        outs.append((wsum / cnt - centre).reshape(TR, POOL_GROUP_DIM))
    return outs


def _mixer_kernel(t_off, hf_ref, hr_ref, y_ref, xp_ref, xpp_ref, xpn_ref, x_ref,
                  g1_ref, sh2_ref, sc2_ref, pw_ref, pb_ref, ps_ref, wo_ref,
                  lg_ref, lb_ref, rw_ref, rb_ref, tri_ref,
                  xo_ref, route_ref, cnt_ref, wobf_ref):
    i = pl.program_id(0)
    tile = i + t_off
    _cast_once(wo_ref, wobf_ref)

    lru = (hf_ref[...] + hr_ref[...]) * _gelu_tanh(y_ref[...])
    diffs = _pool_groups(tile, xp_ref, xpp_ref, xpn_ref)
    pooled = jnp.concatenate(
        [jnp.dot(d.astype(BF16), pw_ref[g].astype(BF16), preferred_element_type=F32)
         for g, d in enumerate(diffs)], axis=1)
    pooled = (pooled + pb_ref[...]) * ps_ref[...]
    mix = (jnp.dot(lru.astype(BF16), wobf_ref[0:D_LRU, :], preferred_element_type=F32)
           + jnp.dot(pooled.astype(BF16), wobf_ref[D_LRU:, :], preferred_element_type=F32))
    x = ALPHA * x_ref[...] + _per_batch(mix, g1_ref[0], jnp.multiply)
    x = _layer_norm(x) * lg_ref[...] + lb_ref[...]
    xo_ref[...] = x
    h2 = _modulate(x, sh2_ref, sc2_ref)

    h_hi = h2.astype(BF16)
    h_lo = (h2 - h_hi.astype(F32)).astype(BF16)
    logits = (jnp.dot(h_hi, rw_ref[0], preferred_element_type=F32)
              + jnp.dot(h_lo, rw_ref[0], preferred_element_type=F32)
              + jnp.dot(h_hi, rw_ref[1], preferred_element_type=F32)) + rb_ref[...]
    lane = lax.broadcasted_iota(jnp.int32, (TR, LANES), 1).astype(F32)
    work = logits
    vals, idxs, sels = [], [], []
    for _ in range(TOP_K):
        m = jnp.max(work, axis=1, keepdims=True)
        idx = jnp.min(jnp.where(work == m, lane, float(LANES)), axis=1, keepdims=True)
        sel = lane == idx
        vals.append(m)
        idxs.append(idx)
        sels.append(sel)
        work = jnp.where(sel, -jnp.inf, work)
    exps = [jnp.exp(v - vals[0]) for v in vals]
    denom = exps[0] + exps[1] + exps[2] + exps[3]
    chosen = jnp.zeros((TR, LANES), F32)
    for sel in sels:
        chosen = chosen + sel.astype(F32)
    before = jnp.dot(tri_ref[...], chosen.astype(BF16), preferred_element_type=F32)
    route = jnp.zeros((TR, LANES), F32)
    for k in range(TOP_K):
        rank = jnp.sum(jnp.where(sels[k], before, 0.0), axis=1, keepdims=True)
        for base, val in ((LANE_P, exps[k] / denom), (LANE_EID, idxs[k]), (LANE_RANK, rank)):
            route = jnp.where(lane == float(base + k), val, route)
    route_ref[...] = route
    cnt_ref[0] = jnp.broadcast_to(jnp.sum(chosen, axis=0, keepdims=True), (SUBLANES, LANES))


def _mixer(layer, t_off, hf, hr, z, x, mod, pool_w, pool_b, pool_scale, w_out,
           ln_g, ln_b, rw_pad, rb_pad, tri):
    n = N_TILES - t_off
    halo_rows = POOL_HALO * BATCH
    per = TR // halo_rows
    last_halo = R_ALL // halo_rows - 1
    xp_col = 2 * D_LRU // D_POOL
    row = lambda w: pl.BlockSpec((TR, w), lambda i: (i + t_off, 0))
    out_row = lambda w: pl.BlockSpec((TR, w), lambda i: (i, 0))
    const = lambda shape: pl.BlockSpec(shape, lambda i: (0,) * len(shape))

    def mod_spec(chunk):
        return pl.BlockSpec((1, BATCH, D_MODEL),
                            lambda i: ((i + t_off >= CTX_TILES).astype(jnp.int32), 0, chunk))

    return pl.pallas_call(
        functools.partial(_mixer_kernel, t_off),
        grid=(n,),
        in_specs=[
            row(D_LRU), row(D_LRU),
            pl.BlockSpec((TR, D_LRU), lambda i: (i + t_off, 1)),
            pl.BlockSpec((TR, D_POOL), lambda i: (i + t_off, xp_col)),
            pl.BlockSpec((halo_rows, D_POOL),
                         lambda i: (jnp.maximum((i + t_off) * per - 1, 0), xp_col)),
            pl.BlockSpec((halo_rows, D_POOL),
                         lambda i: (jnp.minimum((i + t_off + 1) * per, last_halo), xp_col)),
            row(D_MODEL),
            mod_spec(2), mod_spec(3), mod_spec(4),
            pl.BlockSpec((None, len(POOL_WINDOWS), POOL_GROUP_DIM, POOL_GROUP_DIM),
                         lambda i: (layer, 0, 0, 0)),
            const((1, D_POOL)), const((1, D_POOL)),
            pl.BlockSpec((None, D_MODEL, D_MODEL), lambda i: (layer, 0, 0)),
            const((1, D_MODEL)), const((1, D_MODEL)),
            const((2, D_MODEL, LANES)), const((1, LANES)),
            const((TR, TR)),
        ],
        out_specs=[out_row(D_MODEL), out_row(LANES),
                   pl.BlockSpec((1, SUBLANES, LANES), lambda i: (i, 0, 0))],
        out_shape=[jax.ShapeDtypeStruct((n * TR, D_MODEL), F32),
                   jax.ShapeDtypeStruct((n * TR, LANES), F32),
                   jax.ShapeDtypeStruct((n, SUBLANES, LANES), F32)],
        scratch_shapes=[pltpu.VMEM((D_MODEL, D_MODEL), BF16)],
        compiler_params=_cparams(("arbitrary",)),
        name="mixer_out",
    )(hf, hr, z, z, z, z, x, mod, mod, mod, pool_w, pool_b, pool_scale, w_out,
      ln_g, ln_b, rw_pad, rb_pad, tri)


def _one_hot_rows(targets, index, weights=None):
    matches = [index == t for t in targets]
    hit = jnp.zeros(index.shape, F32)
    for m in reversed(matches):
        hit = jnp.where(m, 1.0, hit)
    if weights is None:
        return hit
    picked = jnp.zeros(index.shape, F32)
    for m, w in zip(reversed(matches), reversed(weights)):
        picked = jnp.where(m, w, picked)
    return hit, picked


def _for_each_group(block, len_ref, body):
    def step(e, carry):
        n = pl.multiple_of(len_ref[block, e], GROUP)

        @pl.when(n > 0)
        def _():
            body(e, n)

        return carry

    lax.fori_loop(0, N_EXPERTS, step, 0)


def _group(start, n):
    return pl.ds(pl.multiple_of(start, GROUP), n)


def _dispatch_kernel(src_ref, len_ref, dst_ref, tot_ref, tail_s_ref, tail_n_ref, na_ref,
                     lpos_ref, x_ref, sh2_ref, sc2_ref, xs_ref, stage_ref, zero_ref, sems):
    i = pl.program_id(0)
    n_blocks = pl.num_programs(0) - 1
    slot = i % 2

    @pl.when(i < n_blocks)
    def _():
        lpos = lpos_ref[...]
        targets = [lpos[k:k + 1, :] for k in range(TOP_K)]
        gates = [lpos[TOP_K + k:TOP_K + k + 1, :] for k in range(TOP_K)]
        h = _modulate(x_ref[...], sh2_ref, sc2_ref).astype(BF16)

        def permute(jb, carry):
            base = pl.multiple_of(jb * MB, MB)
            rows = (lax.broadcasted_iota(jnp.int32, (MB, TR), 0) + base).astype(F32)
            sel, picked = _one_hot_rows(targets, rows, gates)
            stage_ref[slot, pl.ds(base, MB), 0:D_MODEL] = jnp.dot(
                sel.astype(BF16), h, preferred_element_type=F32)
            gate = jnp.sum(picked, axis=1, keepdims=True)
            stage_ref[slot, pl.ds(base, MB), D_MODEL:D_AUG] = jnp.broadcast_to(gate, (MB, LANES))
            return carry

        lax.fori_loop(0, (tot_ref[i] + MB - 1) // MB, permute, 0)
        _for_each_group(i, len_ref, lambda e, n: pltpu.make_async_copy(
            stage_ref.at[slot, _group(src_ref[i, e], n)],
            xs_ref.at[_group(dst_ref[i, e], n)], sems.at[slot]).start())

    @pl.when(i > 0)
    def _():
        rows = _group(0, pl.multiple_of(tot_ref[i - 1], GROUP))
        pltpu.make_async_copy(stage_ref.at[1 - slot, rows], xs_ref.at[rows],
                              sems.at[1 - slot]).wait()

    @pl.when(i == n_blocks)
    def _():
        zero_ref[...] = jnp.zeros_like(zero_ref)
        fill = sems.at[2]

        def expert_tail(e, n):
            return pltpu.make_async_copy(zero_ref.at[_group(0, n)],
                                         xs_ref.at[_group(tail_s_ref[0, e], n)], fill)

        def whole_tile(t):
            return pltpu.make_async_copy(zero_ref, xs_ref.at[_group(t * TM, TM)], fill)

        n_tiles = xs_ref.shape[0] // TM
        _for_each_group(0, tail_n_ref, lambda e, n: expert_tail(e, n).start())
        lax.fori_loop(na_ref[0], n_tiles, lambda t, c: (whole_tile(t).start(), c)[1], 0)
        _for_each_group(0, tail_n_ref, lambda e, n: expert_tail(e, n).wait())
        lax.fori_loop(na_ref[0], n_tiles, lambda t, c: (whole_tile(t).wait(), c)[1], 0)


def _dispatch(t_off, plan, x, mod):
    n = x.shape[0] // TR
    last = n - 1
    grid_spec = pltpu.PrefetchScalarGridSpec(
        num_scalar_prefetch=7,
        grid=(n + 1,),
        in_specs=[
            pl.BlockSpec((None, SUBLANES, TR), lambda i, *_: (jnp.minimum(i, last), 0, 0)),
            pl.BlockSpec((TR, D_MODEL), lambda i, *_: (jnp.minimum(i, last), 0)),
            _mod_spec(3, t_off), _mod_spec(4, t_off),
        ],
        out_specs=pl.BlockSpec(memory_space=pl.ANY),
        scratch_shapes=[pltpu.VMEM((2, RB_STAGE, D_AUG), F32),
                        pltpu.VMEM((TM, D_AUG), F32),
                        pltpu.SemaphoreType.DMA((3,))],
    )
    return pl.pallas_call(
        _dispatch_kernel,
        grid_spec=grid_spec,
        out_shape=jax.ShapeDtypeStruct((plan["n_sorted"], D_AUG), F32),
        compiler_params=_cparams(("arbitrary",)),
        name="moe_dispatch",
    )(plan["src_row"], plan["n_rows"], plan["dst_row"], plan["block_rows"],
      plan["tail_start"][None], plan["tail_len"][None], plan["n_active"], plan["lpos_t"],
      x, mod, mod)


def _expert_kernel(layer, te_ref, na_ref, first_ref, nxt_ref, slot_ref,
                   x_ref, b1_ref, b2_ref, w1_hbm, w2_hbm, o_ref,
                   w1buf_ref, w2buf_ref, w1c_ref, w2c_ref, sems):
    i = pl.program_id(0)
    e = te_ref[i]
    slot = slot_ref[i]

    def fetch(expert, s):
        return (pltpu.make_async_copy(w1_hbm.at[layer, expert], w1buf_ref.at[s], sems.at[s, 0]),
                pltpu.make_async_copy(w2_hbm.at[layer, expert], w2buf_ref.at[s], sems.at[s, 1]))

    @pl.when(i == 0)
    def _():
        for copy in fetch(e, slot):
            copy.start()

    @pl.when(first_ref[i] == 1)
    def _():
        for copy in fetch(e, slot):
            copy.wait()
        w1c_ref[...] = w1buf_ref[slot].astype(BF16)
        w2c_ref[...] = w2buf_ref[slot].astype(BF16)

        @pl.when(nxt_ref[i] >= 0)
        def _():
            for copy in fetch(nxt_ref[i], 1 - slot):
                copy.start()

    @pl.when(i < na_ref[0])
    def _():
        gate = x_ref[:, D_MODEL:D_MODEL + 1]
        x = x_ref[:, 0:D_MODEL].astype(BF16)
        y = jnp.broadcast_to(b2_ref[...], (TM, D_MODEL))
        for c in range(D_EXPERT // FFN_CHUNK):
            g_cols = slice(c * FFN_CHUNK, (c + 1) * FFN_CHUNK)
            l_cols = slice(D_EXPERT + c * FFN_CHUNK, D_EXPERT + (c + 1) * FFN_CHUNK)
            glu = jnp.dot(x, w1c_ref[:, g_cols], preferred_element_type=F32) + b1_ref[:, g_cols]
            lin = jnp.dot(x, w1c_ref[:, l_cols], preferred_element_type=F32) + b1_ref[:, l_cols]
            glu = jnp.minimum(glu, SWIGLU_LIMIT)
            lin = jnp.clip(lin, -SWIGLU_LIMIT, SWIGLU_LIMIT)
            act = glu * _sigmoid(SWIGLU_ALPHA * glu) * (lin + 1.0)
            y = y + jnp.dot(act.astype(BF16), w2c_ref[g_cols, :], preferred_element_type=F32)
        o_ref[...] = gate * y

    @pl.when(i >= na_ref[0])
    def _():
        o_ref[...] = jnp.zeros_like(o_ref)


def _experts(layer, plan, xs, w1, b1, w2, b2):
    n_tiles = xs.shape[0] // TM
    act_tile = lambda i, te, na, *_: (jnp.maximum(jnp.minimum(i, na[0] - 1), 0), 0)
    expert = lambda i, te, *_: (layer, te[i], 0, 0)
    grid_spec = pltpu.PrefetchScalarGridSpec(
        num_scalar_prefetch=5,
        grid=(n_tiles,),
        in_specs=[
            pl.BlockSpec((TM, D_AUG), act_tile),
            pl.BlockSpec((None, None, 1, 2 * D_EXPERT), expert),
            pl.BlockSpec((None, None, 1, D_MODEL), expert),
            pl.BlockSpec(memory_space=pl.ANY),
            pl.BlockSpec(memory_space=pl.ANY),
        ],
        out_specs=pl.BlockSpec((TM, D_MODEL), lambda i, *_: (i, 0)),
        scratch_shapes=[pltpu.VMEM((2, D_MODEL, 2 * D_EXPERT), F32),
                        pltpu.VMEM((2, D_EXPERT, D_MODEL), F32),
                        pltpu.VMEM((D_MODEL, 2 * D_EXPERT), BF16),
                        pltpu.VMEM((D_EXPERT, D_MODEL), BF16),
                        pltpu.SemaphoreType.DMA((2, 2))],
    )
    return pl.pallas_call(
        functools.partial(_expert_kernel, layer),
        grid_spec=grid_spec,
        out_shape=jax.ShapeDtypeStruct((xs.shape[0], D_MODEL), F32),
        compiler_params=_cparams(("arbitrary",)),
        name="moe_experts",
    )(plan["tile_expert"], plan["n_active"], plan["tile_first"], plan["tile_next"],
      plan["tile_slot"], xs, b1.reshape(DEPTH, N_EXPERTS, 1, -1),
      b2.reshape(DEPTH, N_EXPERTS, 1, -1), w1, w2)


def _combine_kernel(last, src_ref, len_ref, dst_ref, tot_ref, lpos_ref, x_ref,
                    g2_ref, lg_ref, lb_ref, ys_ref, *rest):
    if last:
        o_ref, ybuf_ref, sems = rest
    else:
        sh_ref, sc_ref, w_ref, o_ref, z_ref, ybuf_ref, sems, wbf_ref = rest
        _cast_once(w_ref, wbf_ref)
    i = pl.program_id(0)
    n_blocks = pl.num_programs(0)
    slot = i % 2

    def start_block(block, s):
        _for_each_group(block, len_ref, lambda e, n: pltpu.make_async_copy(
            ys_ref.at[_group(dst_ref[block, e], n)],
            ybuf_ref.at[s, _group(src_ref[block, e], n)], sems.at[s]).start())

    @pl.when(i == 0)
    def _():
        ybuf_ref[...] = jnp.zeros_like(ybuf_ref)
        start_block(0, 0)

    @pl.when(i + 1 < n_blocks)
    def _():
        start_block(i + 1, 1 - slot)

    rows = _group(0, pl.multiple_of(tot_ref[i], GROUP))
    pltpu.make_async_copy(ys_ref.at[rows], ybuf_ref.at[slot, rows], sems.at[slot]).wait()
    lpos = lpos_ref[...]
    targets = [jnp.broadcast_to(lpos[:, k:k + 1], (TR, KB)) for k in range(TOP_K)]
    cols = lax.broadcasted_iota(jnp.int32, (TR, KB), 1)
    f = jnp.zeros((TR, D_MODEL), F32)
    for jb in range(RB // KB):
        sel = _one_hot_rows(targets, cols + jb * KB).astype(BF16)
        f = f + jnp.dot(sel, ybuf_ref[slot, jb * KB:(jb + 1) * KB, :].astype(BF16),
                        preferred_element_type=F32)
    x = ALPHA * x_ref[...] + _per_batch(f, g2_ref[0], jnp.multiply)
    out = _layer_norm(x) * lg_ref[...] + lb_ref[...]
    if last:
        o_ref[...] = jnp.transpose(out.reshape(TT, BATCH, D_MODEL), (1, 0, 2))
    else:
        o_ref[...] = out
        z_ref[...] = _in_projection(out, sh_ref, sc_ref, wbf_ref)


def _combine(t_off, plan, x, mod, ln_g, ln_b, ys, next_layer=None):
    n = x.shape[0] // TR
    last = next_layer is None
    row = lambda w: pl.BlockSpec((TR, w), lambda i, *_: (i, 0))
    const = lambda shape: pl.BlockSpec(shape, lambda i, *_: (0,) * len(shape))
    if last:
        extra_specs, extra_args, extra_scratch = [], (), []
        out_spec = pl.BlockSpec((BATCH, TT, D_MODEL), lambda i, *_: (0, i, 0))
        out_shape = jax.ShapeDtypeStruct((BATCH, n * TT, D_MODEL), F32)
    else:
        next_mod, w_in, next_l = next_layer
        extra_specs = [_mod_spec(0, t_off), _mod_spec(1, t_off),
                       pl.BlockSpec((None, D_MODEL, D_IN), lambda i, *_: (next_l, 0, 0))]
        extra_args = (next_mod, next_mod, w_in)
        extra_scratch = [pltpu.VMEM((D_MODEL, D_IN), BF16)]
        out_spec = [row(D_MODEL), row(D_IN)]
        out_shape = [jax.ShapeDtypeStruct((n * TR, D_MODEL), F32),
                     jax.ShapeDtypeStruct((n * TR, D_IN), F32)]
    grid_spec = pltpu.PrefetchScalarGridSpec(
        num_scalar_prefetch=4,
        grid=(n,),
        in_specs=[
            row(TOP_K), row(D_MODEL), _mod_spec(5, t_off),
            const((1, D_MODEL)), const((1, D_MODEL)),
            pl.BlockSpec(memory_space=pl.ANY),
        ] + extra_specs,
        out_specs=out_spec,
        scratch_shapes=[pltpu.VMEM((2, RB, D_MODEL), F32),
                        pltpu.SemaphoreType.DMA((2,))] + extra_scratch,
    )
    return pl.pallas_call(
        functools.partial(_combine_kernel, last),
        grid_spec=grid_spec,
        out_shape=out_shape,
        compiler_params=_cparams(("arbitrary",)),
        name="moe_combine",
    )(plan["src_row"], plan["n_rows"], plan["dst_row"], plan["block_rows"], plan["lpos"],
      x, mod, ln_g, ln_b, ys, *extra_args)

def _routing_plan(route, counts):
    i32 = jnp.int32
    nb = counts.shape[0]
    experts = jnp.arange(N_EXPERTS, dtype=i32)
    eid = route[:, LANE_EID:LANE_EID + TOP_K].astype(i32).reshape(nb, TR, TOP_K)
    rank = route[:, LANE_RANK:LANE_RANK + TOP_K].astype(i32).reshape(nb, TR, TOP_K)
    n = counts[:, 0, :N_EXPERTS].astype(i32)
    n = (n + GROUP - 1) // GROUP * GROUP
    src_row = jnp.cumsum(n, axis=1) - n
    seg = jnp.sum(n, axis=0)
    tiles_e = (seg + TM - 1) // TM
    tile_end = jnp.cumsum(tiles_e)
    e_start = (tile_end - tiles_e) * TM
    n_active = tile_end[-1]
    dst_row = e_start[None, :] + jnp.cumsum(n, axis=0) - n

    onehot = eid[..., None] == experts
    lpos = jnp.sum(jnp.where(onehot, src_row[:, None, None, :], 0), axis=-1) + rank
    gate = route[:, LANE_P:LANE_P + TOP_K].reshape(nb, TR, TOP_K)
    lpos_t = jnp.concatenate([lpos.astype(F32), gate], axis=2).transpose(0, 2, 1)

    n_tiles = (nb * RB + TM - 1) // TM + N_EXPERTS
    t = jnp.minimum(jnp.arange(n_tiles, dtype=i32), n_active - 1)
    tile_expert = jnp.sum((t[:, None] >= tile_end[None, :]).astype(i32), axis=1)
    tile_expert = jnp.minimum(tile_expert, N_EXPERTS - 1)
    tile_first = jnp.concatenate(
        [jnp.ones((1,), i32), (tile_expert[1:] != tile_expert[:-1]).astype(i32)])
    has_rows = tiles_e > 0
    later = jnp.where(has_rows[None, :] & (experts[None, :] > experts[:, None]),
                      experts[None, :], N_EXPERTS)
    next_e = jnp.min(later, axis=1)
    next_e = jnp.where(next_e == N_EXPERTS, -1, next_e)
    group = jnp.cumsum(has_rows.astype(i32)) - 1
    pick = tile_expert[:, None] == experts[None, :]
    tile_next = jnp.sum(jnp.where(pick, next_e[None, :], 0), axis=1)
    tile_slot = jnp.sum(jnp.where(pick, group[None, :], 0), axis=1) % 2
    return dict(lpos=lpos.reshape(nb * TR, TOP_K), lpos_t=lpos_t, src_row=src_row, n_rows=n,
                dst_row=dst_row, block_rows=jnp.sum(n, axis=1), tail_start=e_start + seg,
                tail_len=tile_end * TM - e_start - seg,
                tile_expert=tile_expert, n_active=n_active.reshape(1), tile_first=tile_first,
                tile_next=tile_next, tile_slot=tile_slot,
                n_sorted=n_tiles * TM)


def _grid_sincos_parts():
    quarter = D_MODEL // 4
    omega = 1.0 / (10000.0 ** (jnp.arange(quarter, dtype=F32) / quarter))

    def emb1d(n):
        ang = jnp.arange(n, dtype=F32)[:, None] * omega[None, :]
        return jnp.concatenate([jnp.sin(ang), jnp.cos(ang)], axis=-1)

    er = emb1d(SEQ // GRID_W)
    ec = jnp.repeat(emb1d(GRID_W), BATCH, axis=0)
    return er.reshape(SEQ // GRID_W, 1, D_MODEL // 2), ec


def _gate_blocks(w):
    per = GATE_BLK // LRU_HEAD_DIM
    w = w.reshape(2, D_LRU // GATE_BLK, per, LRU_HEAD_DIM, LRU_HEAD_DIM)
    eye = jnp.eye(per, dtype=w.dtype)
    blk = jnp.einsum('dkpij,pq->dkpiqj', w, eye)
    return blk.reshape(2, D_LRU // GATE_BLK, GATE_BLK, GATE_BLK)


def kernel(x, c, ctx, c_ctx, w_mod, b_mod, w_in, conv_w, conv_b, gate_a_w, gate_a_b,
           gate_x_w, gate_x_b, lru_lambda, pool_w, pool_b, pool_scale, w_out, ln1_g, ln1_b,
           router_w, router_b, exp_w1, exp_b1, exp_w2, exp_b2, ln2_g, ln2_b):
    cvec = jnp.concatenate([c, c_ctx[None], jnp.zeros((2 * SUBLANES - BATCH - 1, D_MODEL), F32)])
    mod_all = _modulation(cvec, w_mod, b_mod)
    mods = jnp.stack([jnp.broadcast_to(mod_all[:, BATCH:BATCH + 1], (DEPTH, BATCH, 6 * D_MODEL)),
                      mod_all[:, :BATCH]], axis=1)
    tri = jnp.tril(jnp.ones((TR, TR), F32), -1).astype(BF16)

    er, ec = _grid_sincos_parts()
    xs, z = _entry(ctx, x, er, ec, mods[0], w_in)
    for l in range(DEPTH):
        last = l == DEPTH - 1
        t_off = CTX_TILES if last else 0
        mod = mods[l]
        gate_w = jnp.stack([_gate_blocks(gate_a_w[l]), _gate_blocks(gate_x_w[l])],
                           axis=1).astype(BF16)
        gate_b = jnp.stack([gate_a_b[l], gate_x_b[l]], axis=1)
        hf, hr = _scan(z, conv_w[l], conv_b[l][None], gate_w, gate_b, lru_lambda[l])
        rw_pad = jnp.zeros((D_MODEL, LANES), F32).at[:, :N_EXPERTS].set(router_w[l])
        rw_hi = rw_pad.astype(BF16)
        rw_pad = jnp.stack([rw_hi, (rw_pad - rw_hi.astype(F32)).astype(BF16)])
        rb_pad = jnp.full((1, LANES), -1e30, F32).at[0, :N_EXPERTS].set(router_b[l])
        xs, route, counts = _mixer(
            l, t_off, hf, hr, z, xs, mod, pool_w, pool_b[l][None], pool_scale[l][None], w_out,
            ln1_g[l][None], ln1_b[l][None], rw_pad, rb_pad, tri)
        plan = _routing_plan(route, counts)
        x_sorted = _dispatch(t_off, plan, xs, mod)
        y_sorted = _experts(l, plan, x_sorted, exp_w1, exp_b1, exp_w2, exp_b2)
        if last:
            return _combine(t_off, plan, xs, mod, ln2_g[l][None], ln2_b[l][None], y_sorted)
        xs, z = _combine(t_off, plan, xs, mod, ln2_g[l][None], ln2_b[l][None], y_sorted,
                         (mods[l + 1], w_in, l + 1))
```

```python
import functools

import jax
import jax.numpy as jnp
from jax import lax
from jax.experimental import pallas as pl
from jax.experimental.pallas import tpu as pltpu

D_MODEL = 1024
BATCH = 8
SEQ = 2048
DEPTH = 2
GRID_W = 64
CTX_LEN = 256
D_LRU = 512
N_LRU_HEADS = 8
LRU_HEAD_DIM = D_LRU // N_LRU_HEADS
LRU_C = 8.0
D_POOL = 512
POOL_WINDOWS = (2, 4, 8, 16)
POOL_GROUP_DIM = D_POOL // len(POOL_WINDOWS)
D_IN = 2 * D_LRU + D_POOL
N_EXPERTS = 32
TOP_K = 4
D_EXPERT = D_MODEL
SWIGLU_LIMIT = 7.0
SWIGLU_ALPHA = 1.702
LN_EPS = 1e-5
ALPHA = (2.0 * DEPTH) ** 0.25

F32 = jnp.float32
BF16 = jnp.bfloat16

SUBLANES = 8
LANES = 128
TT = GRID_W
TR = TT * BATCH
T_ALL = CTX_LEN + SEQ
R_ALL = T_ALL * BATCH
N_TILES = T_ALL // TT
CTX_TILES = CTX_LEN // TT
GATE_BLK = 256
POOL_HALO = max(POOL_WINDOWS) // 2
TM = 512
FFN_CHUNK = 512
GROUP = SUBLANES
RB = TOP_K * TR + N_EXPERTS * GROUP
KB = 256
MB = 512
RB_STAGE = (RB + MB - 1) // MB * MB
D_AUG = D_MODEL + LANES
LANE_P, LANE_EID, LANE_RANK = 0, 4, 8
ROUTE_ROWS = 16
VMEM_LIMIT = 56 * 1024 * 1024


def _cparams(sem):
    return pltpu.CompilerParams(dimension_semantics=sem, vmem_limit_bytes=VMEM_LIMIT)


def _sigmoid(x):
    return 0.5 * (1.0 + jnp.tanh(0.5 * x))


def _layer_norm(x):
    mu = jnp.mean(x, axis=-1, keepdims=True)
    xc = x - mu
    var = jnp.mean(xc * xc, axis=-1, keepdims=True)
    return xc * lax.rsqrt(var + LN_EPS)


def _per_batch(x, v, op):
    r, d = x.shape
    return op(x.reshape(r // BATCH, BATCH, d), v[None]).reshape(r, d)


def _mod_kernel(c_ref, w_ref, b_ref, o_ref):
    c = c_ref[...]
    s = c * _sigmoid(c)
    w = w_ref[...]
    s_hi, w_hi = s.astype(BF16), w.astype(BF16)
    s_lo = (s - s_hi.astype(F32)).astype(BF16)
    w_lo = (w - w_hi.astype(F32)).astype(BF16)
    o_ref[...] = (jnp.dot(s_hi, w_hi, preferred_element_type=F32)
                  + jnp.dot(s_lo, w_hi, preferred_element_type=F32)
                  + jnp.dot(s_hi, w_lo, preferred_element_type=F32)) + b_ref[...]


def _modulation(cvec, w_mod, b_mod):
    tn = 512
    return pl.pallas_call(
        _mod_kernel,
        grid=(DEPTH, 6 * D_MODEL // tn),
        in_specs=[
            pl.BlockSpec((2 * SUBLANES, D_MODEL), lambda l, j: (0, 0)),
            pl.BlockSpec((None, D_MODEL, tn), lambda l, j: (l, 0, j)),
            pl.BlockSpec((None, 1, tn), lambda l, j: (l, 0, j)),
        ],
        out_specs=pl.BlockSpec((None, 2 * SUBLANES, tn), lambda l, j: (l, 0, j)),
        out_shape=jax.ShapeDtypeStruct((DEPTH, 2 * SUBLANES, 6 * D_MODEL), F32),
        compiler_params=_cparams(("arbitrary", "arbitrary")),
        name="modulation",
    )(cvec, w_mod, b_mod.reshape(DEPTH, 1, 6 * D_MODEL))


def _modulate(x, sh_ref, sc_ref):
    h = _per_batch(x, 1.0 + sc_ref[0], jnp.multiply)
    return _per_batch(h, sh_ref[0], jnp.add)


def _in_projection(x, sh_ref, sc_ref, w_ref):
    return jnp.dot(_modulate(x, sh_ref, sc_ref).astype(BF16), w_ref[...],
                   preferred_element_type=F32)


def _cast_once(w_ref, wbf_ref):
    @pl.when(pl.program_id(0) == 0)
    def _():
        wbf_ref[...] = w_ref[...].astype(BF16)


def _entry_kernel(ctx_ref, x_ref, er_ref, ec_ref, sh_ref, sc_ref, w_ref, o_ref, z_ref, wbf_ref):
    i = pl.program_id(0)
    _cast_once(w_ref, wbf_ref)

    def time_major(src_ref):
        return jnp.transpose(src_ref[...], (1, 0, 2)).reshape(TR, D_MODEL)

    @pl.when(i < CTX_TILES)
    def _():
        o_ref[...] = _layer_norm(time_major(ctx_ref))

    @pl.when(i >= CTX_TILES)
    def _():
        pos = jnp.concatenate(
            [jnp.broadcast_to(er_ref[0], (TR, D_MODEL // 2)), ec_ref[...]], axis=1)
        o_ref[...] = _layer_norm(time_major(x_ref) + pos)

    z_ref[...] = _in_projection(o_ref[...], sh_ref, sc_ref, wbf_ref)


def _mod_spec(chunk, t_off=0):
    return pl.BlockSpec((1, BATCH, D_MODEL),
                        lambda i, *_: ((i + t_off >= CTX_TILES).astype(jnp.int32), 0, chunk))


def _entry(ctx, x, er, ec, mod, w_in):
    return pl.pallas_call(
        _entry_kernel,
        grid=(N_TILES,),
        in_specs=[
            pl.BlockSpec((BATCH, TT, D_MODEL), lambda i: (0, jnp.minimum(i, CTX_TILES - 1), 0)),
            pl.BlockSpec((BATCH, TT, D_MODEL), lambda i: (0, jnp.maximum(i - CTX_TILES, 0), 0)),
            pl.BlockSpec((1, 1, D_MODEL // 2), lambda i: (jnp.maximum(i - CTX_TILES, 0), 0, 0)),
            pl.BlockSpec((TR, D_MODEL // 2), lambda i: (0, 0)),
            _mod_spec(0), _mod_spec(1),
            pl.BlockSpec((None, D_MODEL, D_IN), lambda i: (0, 0, 0)),
        ],
        out_specs=[pl.BlockSpec((TR, D_MODEL), lambda i: (i, 0)),
                   pl.BlockSpec((TR, D_IN), lambda i: (i, 0))],
        out_shape=[jax.ShapeDtypeStruct((R_ALL, D_MODEL), F32),
                   jax.ShapeDtypeStruct((R_ALL, D_IN), F32)],
        scratch_shapes=[pltpu.VMEM((D_MODEL, D_IN), BF16)],
        compiler_params=_cparams(("arbitrary",)),
        name="entry_ln",
    )(ctx, x, er, ec, mod, mod, w_in)


def _rev_tile(i):
    return jnp.where(i < CTX_TILES, CTX_TILES - 1 - i, N_TILES - 1 + CTX_TILES - i)


def _block_diag_dot(u_bf, w_ref, d, g):
    halves = [jnp.dot(u_bf[:, k * GATE_BLK:(k + 1) * GATE_BLK], w_ref[d, g, k],
                      preferred_element_type=F32) for k in range(D_LRU // GATE_BLK)]
    return jnp.concatenate(halves, axis=1)


def _lru_coeffs(tile, z_ref, zp_ref, zn_ref, cw_ref, cb_ref, gw_ref, gb_ref, lam_ref, d):
    x = z_ref[...]
    seg_first = (tile == 0) | (tile == CTX_TILES)
    seg_last = (tile == CTX_TILES - 1) | (tile == N_TILES - 1)
    prev = jnp.where(seg_first, 0.0, zp_ref[...])
    nxt = jnp.where(seg_last, 0.0, zn_ref[...])
    xm2 = jnp.concatenate([prev, x[:-2 * BATCH]], axis=0)
    xm1 = jnp.concatenate([prev[BATCH:], x[:-BATCH]], axis=0)
    xp1 = jnp.concatenate([x[BATCH:], nxt], axis=0)
    u = (cb_ref[...] + xm2 * cw_ref[0:1] + xm1 * cw_ref[1:2]
         + x * cw_ref[2:3] + xp1 * cw_ref[3:4])
    u_bf = u.astype(BF16)
    r = _sigmoid(_block_diag_dot(u_bf, gw_ref, d, 0) + gb_ref[d, 0:1])
    ig = _sigmoid(_block_diag_dot(u_bf, gw_ref, d, 1) + gb_ref[d, 1:2])
    nl = -lam_ref[d:d + 1]
    softplus = jnp.maximum(nl, 0.0) + jnp.log(1.0 + jnp.exp(-jnp.abs(nl)))
    log_a = (-LRU_C) * r * softplus
    a = jnp.exp(log_a)
    gap = 1.0 - a * a
    mult = jnp.where(gap > 0.0, gap * lax.rsqrt(gap), 0.0)
    return a, mult * (ig * u)


def _scan_kernel(zf_ref, zfp_ref, zfn_ref, zr_ref, zrp_ref, zrn_ref,
                 cw_ref, cb_ref, gw_ref, gb_ref, lam_ref,
                 hf_ref, hr_ref, state_ref):
    i = pl.program_id(0)

    @pl.when(i == 0)
    def _():
        state_ref[...] = jnp.zeros_like(state_ref)

    af, bf = _lru_coeffs(i, zf_ref, zfp_ref, zfn_ref, cw_ref, cb_ref, gw_ref, gb_ref,
                         lam_ref, 0)
    ar, br = _lru_coeffs(_rev_tile(i), zr_ref, zrp_ref, zrn_ref, cw_ref, cb_ref, gw_ref,
                         gb_ref, lam_ref, 1)
    hf = state_ref[0]
    hr = state_ref[1]
    for s in range(TT):
        f = slice(s * BATCH, (s + 1) * BATCH)
        hf = af[f] * hf + bf[f]
        hf_ref[f, :] = hf
        b = slice((TT - 1 - s) * BATCH, (TT - s) * BATCH)
        hr = ar[b] * hr + br[b]
        hr_ref[b, :] = hr
    state_ref[0] = hf
    state_ref[1] = hr


def _scan(z, conv_w, conv_b, gate_w, gate_b, lam):
    prev_rows = 2 * BATCH
    tiles_per_prev = TR // prev_rows
    tiles_per_next = TR // BATCH
    last_next = R_ALL // BATCH - 1

    def cur(f):
        return pl.BlockSpec((TR, D_LRU), lambda i: (f(i), 0))

    def prev(f):
        return pl.BlockSpec((prev_rows, D_LRU),
                            lambda i: (jnp.maximum(f(i) * tiles_per_prev - 1, 0), 0))

    def nxt(f):
        return pl.BlockSpec((BATCH, D_LRU),
                            lambda i: (jnp.minimum((f(i) + 1) * tiles_per_next, last_next), 0))

    fwd = lambda i: i
    const = lambda shape: pl.BlockSpec(shape, lambda i: (0,) * len(shape))
    return pl.pallas_call(
        _scan_kernel,
        grid=(N_TILES,),
        in_specs=[cur(fwd), prev(fwd), nxt(fwd), cur(_rev_tile), prev(_rev_tile), nxt(_rev_tile),
                  const((4, D_LRU)), const((1, D_LRU)),
                  const((2, 2, D_LRU // GATE_BLK, GATE_BLK, GATE_BLK)),
                  const((2, 2, D_LRU)), const((2, D_LRU))],
        out_specs=[pl.BlockSpec((TR, D_LRU), lambda i: (i, 0)),
                   pl.BlockSpec((TR, D_LRU), lambda i: (_rev_tile(i), 0))],
        out_shape=[jax.ShapeDtypeStruct((R_ALL, D_LRU), F32)] * 2,
        scratch_shapes=[pltpu.VMEM((2, BATCH, D_LRU), F32)],
        compiler_params=_cparams(("arbitrary",)),
        name="lru_scan",
    )(z, z, z, z, z, z, conv_w, conv_b, gate_w, gate_b, lam)


def _gelu_tanh(y):
    c = 0.7978845608028654
    return 0.5 * y * (1.0 + jnp.tanh(c * (y + 0.044715 * (y * y * y))))


def _pool_groups(tile, xp_ref, xpp_ref, xpn_ref):
    in_ctx = tile < CTX_TILES
    prev_ok = in_ctx & (tile > 0)
    next_ok = in_ctx & (tile < CTX_TILES - 1)
    x = xp_ref[...]
    p = jnp.concatenate([jnp.where(prev_ok, xpp_ref[...], 0.0), x,
                         jnp.where(next_ok, xpn_ref[...], 0.0)], axis=0)
    p = p.reshape(TT + 2 * POOL_HALO, BATCH, D_POOL)
    lo = jnp.where(in_ctx, -TT * tile, 0)
    hi = jnp.where(in_ctx, CTX_LEN - TT * tile, TT)
    t = lax.broadcasted_iota(jnp.int32, (TT, BATCH, POOL_GROUP_DIM), 0)
    outs = []
    for g, win in enumerate(POOL_WINDOWS):
        half = win // 2
        acc = p[:, :, g * POOL_GROUP_DIM:(g + 1) * POOL_GROUP_DIM]
        width = 1
        while width < win:
            acc = acc[:acc.shape[0] - width] + acc[width:]
            width *= 2
        start = POOL_HALO - half
        wsum = acc[start:start + TT]
        cnt = (jnp.minimum(t + half, hi) - jnp.maximum(t - half, lo)).astype(F32)
        centre = p[POOL_HALO:POOL_HALO + TT, :, g * POOL_GROUP_DIM:(g + 1) * POOL_GROUP_DIM]
        outs.append((wsum / cnt - centre).reshape(TR, POOL_GROUP_DIM))
    return outs


def _mixer_kernel(t_off, hf_ref, hr_ref, y_ref, xp_ref, xpp_ref, xpn_ref, x_ref,
                  g1_ref, sh2_ref, sc2_ref, pw_ref, pb_ref, ps_ref, wo_ref,
                  lg_ref, lb_ref, rw_ref, rb_ref, tri_ref,
                  xo_ref, route_ref, cnt_ref, wobf_ref):
    i = pl.program_id(0)
    tile = i + t_off
    _cast_once(wo_ref, wobf_ref)

    lru = (hf_ref[...] + hr_ref[...]) * _gelu_tanh(y_ref[...])
    diffs = _pool_groups(tile, xp_ref, xpp_ref, xpn_ref)
    pooled = jnp.concatenate(
        [jnp.dot(d.astype(BF16), pw_ref[g].astype(BF16), preferred_element_type=F32)
         for g, d in enumerate(diffs)], axis=1)
    pooled = (pooled + pb_ref[...]) * ps_ref[...]
    mix = (jnp.dot(lru.astype(BF16), wobf_ref[0:D_LRU, :], preferred_element_type=F32)
           + jnp.dot(pooled.astype(BF16), wobf_ref[D_LRU:, :], preferred_element_type=F32))
    x = ALPHA * x_ref[...] + _per_batch(mix, g1_ref[0], jnp.multiply)
    x = _layer_norm(x) * lg_ref[...] + lb_ref[...]
    xo_ref[...] = x
    h2 = _modulate(x, sh2_ref, sc2_ref)

    h_hi = h2.astype(BF16)
    h_lo = (h2 - h_hi.astype(F32)).astype(BF16)
    logits = (jnp.dot(h_hi, rw_ref[0], preferred_element_type=F32)
              + jnp.dot(h_lo, rw_ref[0], preferred_element_type=F32)
              + jnp.dot(h_hi, rw_ref[1], preferred_element_type=F32)) + rb_ref[...]
    lane = lax.broadcasted_iota(jnp.int32, (TR, LANES), 1).astype(F32)
    work = logits
    vals, idxs, sels = [], [], []
    for _ in range(TOP_K):
        m = jnp.max(work, axis=1, keepdims=True)
        idx = jnp.min(jnp.where(work == m, lane, float(LANES)), axis=1, keepdims=True)
        sel = lane == idx
        vals.append(m)
        idxs.append(idx)
        sels.append(sel)
        work = jnp.where(sel, -jnp.inf, work)
    exps = [jnp.exp(v - vals[0]) for v in vals]
    denom = exps[0] + exps[1] + exps[2] + exps[3]
    chosen = jnp.zeros((TR, LANES), F32)
    for sel in sels:
        chosen = chosen + sel.astype(F32)
    before = jnp.dot(tri_ref[...], chosen.astype(BF16), preferred_element_type=F32)
    route = jnp.zeros((TR, LANES), F32)
    for k in range(TOP_K):
        rank = jnp.sum(jnp.where(sels[k], before, 0.0), axis=1, keepdims=True)
        for base, val in ((LANE_P, exps[k] / denom), (LANE_EID, idxs[k]), (LANE_RANK, rank)):
            route = jnp.where(lane == float(base + k), val, route)
    route_ref[0] = route.T[0:ROUTE_ROWS, :]
    cnt_ref[0] = jnp.broadcast_to(jnp.sum(chosen, axis=0, keepdims=True), (SUBLANES, LANES))


def _mixer(layer, t_off, hf, hr, z, x, mod, pool_w, pool_b, pool_scale, w_out,
           ln_g, ln_b, rw_pad, rb_pad, tri):
    n = N_TILES - t_off
    halo_rows = POOL_HALO * BATCH
    per = TR // halo_rows
    last_halo = R_ALL // halo_rows - 1
    xp_col = 2 * D_LRU // D_POOL
    row = lambda w: pl.BlockSpec((TR, w), lambda i: (i + t_off, 0))
    out_row = lambda w: pl.BlockSpec((TR, w), lambda i: (i, 0))
    const = lambda shape: pl.BlockSpec(shape, lambda i: (0,) * len(shape))

    def mod_spec(chunk):
        return pl.BlockSpec((1, BATCH, D_MODEL),
                            lambda i: ((i + t_off >= CTX_TILES).astype(jnp.int32), 0, chunk))

    return pl.pallas_call(
        functools.partial(_mixer_kernel, t_off),
        grid=(n,),
        in_specs=[
            row(D_LRU), row(D_LRU),
            pl.BlockSpec((TR, D_LRU), lambda i: (i + t_off, 1)),
            pl.BlockSpec((TR, D_POOL), lambda i: (i + t_off, xp_col)),
            pl.BlockSpec((halo_rows, D_POOL),
                         lambda i: (jnp.maximum((i + t_off) * per - 1, 0), xp_col)),
            pl.BlockSpec((halo_rows, D_POOL),
                         lambda i: (jnp.minimum((i + t_off + 1) * per, last_halo), xp_col)),
            row(D_MODEL),
            mod_spec(2), mod_spec(3), mod_spec(4),
            pl.BlockSpec((None, len(POOL_WINDOWS), POOL_GROUP_DIM, POOL_GROUP_DIM),
                         lambda i: (layer, 0, 0, 0)),
            const((1, D_POOL)), const((1, D_POOL)),
            pl.BlockSpec((None, D_MODEL, D_MODEL), lambda i: (layer, 0, 0)),
            const((1, D_MODEL)), const((1, D_MODEL)),
            const((2, D_MODEL, LANES)), const((1, LANES)),
            const((TR, TR)),
        ],
        out_specs=[out_row(D_MODEL),
                   pl.BlockSpec((1, ROUTE_ROWS, TR), lambda i: (i, 0, 0)),
                   pl.BlockSpec((1, SUBLANES, LANES), lambda i: (i, 0, 0))],
        out_shape=[jax.ShapeDtypeStruct((n * TR, D_MODEL), F32),
                   jax.ShapeDtypeStruct((n, ROUTE_ROWS, TR), F32),
                   jax.ShapeDtypeStruct((n, SUBLANES, LANES), F32)],
        scratch_shapes=[pltpu.VMEM((D_MODEL, D_MODEL), BF16)],
        compiler_params=_cparams(("arbitrary",)),
        name="mixer_out",
    )(hf, hr, z, z, z, z, x, mod, mod, mod, pool_w, pool_b, pool_scale, w_out,
      ln_g, ln_b, rw_pad, rb_pad, tri)


def _one_hot_rows(targets, index, weights=None):
    matches = [index == t for t in targets]
    hit = jnp.zeros(index.shape, F32)
    for m in reversed(matches):
        hit = jnp.where(m, 1.0, hit)
    if weights is None:
        return hit
    picked = jnp.zeros(index.shape, F32)
    for m, w in zip(reversed(matches), reversed(weights)):
        picked = jnp.where(m, w, picked)
    return hit, picked


def _for_each_group(block, len_ref, body):
    def step(e, carry):
        n = pl.multiple_of(len_ref[block, e], GROUP)

        @pl.when(n > 0)
        def _():
            body(e, n)

        return carry

    lax.fori_loop(0, N_EXPERTS, step, 0)


def _group(start, n):
    return pl.ds(pl.multiple_of(start, GROUP), n)


def _dispatch_kernel(src_ref, len_ref, dst_ref, tot_ref, tail_s_ref, tail_n_ref, na_ref,
                     lpos_ref, x_ref, sh2_ref, sc2_ref, xs_ref, stage_ref, zero_ref, sems):
    i = pl.program_id(0)
    n_blocks = pl.num_programs(0) - 1
    slot = i % 2

    @pl.when(i < n_blocks)
    def _():
        lpos = lpos_ref[...]
        targets = [lpos[k:k + 1, :] for k in range(TOP_K)]
        gates = [lpos[TOP_K + k:TOP_K + k + 1, :] for k in range(TOP_K)]
        h = _modulate(x_ref[...], sh2_ref, sc2_ref).astype(BF16)

        def permute(jb, carry):
            base = pl.multiple_of(jb * MB, MB)
            rows = (lax.broadcasted_iota(jnp.int32, (MB, TR), 0) + base).astype(F32)
            sel, picked = _one_hot_rows(targets, rows, gates)
            stage_ref[slot, pl.ds(base, MB), 0:D_MODEL] = jnp.dot(
                sel.astype(BF16), h, preferred_element_type=F32)
            gate = jnp.sum(picked, axis=1, keepdims=True)
            stage_ref[slot, pl.ds(base, MB), D_MODEL:D_AUG] = jnp.broadcast_to(gate, (MB, LANES))
            return carry

        lax.fori_loop(0, (tot_ref[i] + MB - 1) // MB, permute, 0)
        _for_each_group(i, len_ref, lambda e, n: pltpu.make_async_copy(
            stage_ref.at[slot, _group(src_ref[i, e], n)],
            xs_ref.at[_group(dst_ref[i, e], n)], sems.at[slot]).start())

    @pl.when(i > 0)
    def _():
        rows = _group(0, pl.multiple_of(tot_ref[i - 1], GROUP))
        pltpu.make_async_copy(stage_ref.at[1 - slot, rows], xs_ref.at[rows],
                              sems.at[1 - slot]).wait()

    @pl.when(i == n_blocks)
    def _():
        zero_ref[...] = jnp.zeros_like(zero_ref)
        fill = sems.at[2]

        def expert_tail(e, n):
            return pltpu.make_async_copy(zero_ref.at[_group(0, n)],
                                         xs_ref.at[_group(tail_s_ref[0, e], n)], fill)

        def whole_tile(t):
            return pltpu.make_async_copy(zero_ref, xs_ref.at[_group(t * TM, TM)], fill)

        n_tiles = xs_ref.shape[0] // TM
        _for_each_group(0, tail_n_ref, lambda e, n: expert_tail(e, n).start())
        lax.fori_loop(na_ref[0], n_tiles, lambda t, c: (whole_tile(t).start(), c)[1], 0)
        _for_each_group(0, tail_n_ref, lambda e, n: expert_tail(e, n).wait())
        lax.fori_loop(na_ref[0], n_tiles, lambda t, c: (whole_tile(t).wait(), c)[1], 0)


def _dispatch(t_off, plan, x, mod):
    n = x.shape[0] // TR
    last = n - 1
    grid_spec = pltpu.PrefetchScalarGridSpec(
        num_scalar_prefetch=7,
        grid=(n + 1,),
        in_specs=[
            pl.BlockSpec((None, SUBLANES, TR), lambda i, *_: (jnp.minimum(i, last), 0, 0)),
            pl.BlockSpec((TR, D_MODEL), lambda i, *_: (jnp.minimum(i, last), 0)),
            _mod_spec(3, t_off), _mod_spec(4, t_off),
        ],
        out_specs=pl.BlockSpec(memory_space=pl.ANY),
        scratch_shapes=[pltpu.VMEM((2, RB_STAGE, D_AUG), F32),
                        pltpu.VMEM((TM, D_AUG), F32),
                        pltpu.SemaphoreType.DMA((3,))],
    )
    return pl.pallas_call(
        _dispatch_kernel,
        grid_spec=grid_spec,
        out_shape=jax.ShapeDtypeStruct((plan["n_sorted"], D_AUG), F32),
        compiler_params=_cparams(("arbitrary",)),
        name="moe_dispatch",
    )(plan["src_row"], plan["n_rows"], plan["dst_row"], plan["block_rows"],
      plan["tail_start"][None], plan["tail_len"][None], plan["n_active"], plan["lpos_t"],
      x, mod, mod)


def _expert_kernel(layer, te_ref, na_ref, first_ref, nxt_ref, slot_ref,
                   x_ref, b1_ref, b2_ref, w1_hbm, w2_hbm, o_ref,
                   w1buf_ref, w2buf_ref, w1c_ref, w2c_ref, sems):
    i = pl.program_id(0)
    e = te_ref[i]
    slot = slot_ref[i]

    def fetch(expert, s):
        return (pltpu.make_async_copy(w1_hbm.at[layer, expert], w1buf_ref.at[s], sems.at[s, 0]),
                pltpu.make_async_copy(w2_hbm.at[layer, expert], w2buf_ref.at[s], sems.at[s, 1]))

    @pl.when(i == 0)
    def _():
        for copy in fetch(e, slot):
            copy.start()

    @pl.when(first_ref[i] == 1)
    def _():
        for copy in fetch(e, slot):
            copy.wait()
        w1c_ref[...] = w1buf_ref[slot].astype(BF16)
        w2c_ref[...] = w2buf_ref[slot].astype(BF16)

        @pl.when(nxt_ref[i] >= 0)
        def _():
            for copy in fetch(nxt_ref[i], 1 - slot):
                copy.start()

    @pl.when(i < na_ref[0])
    def _():
        gate = x_ref[:, D_MODEL:D_MODEL + 1]
        x = x_ref[:, 0:D_MODEL].astype(BF16)
        y = jnp.broadcast_to(b2_ref[...], (TM, D_MODEL))
        for c in range(D_EXPERT // FFN_CHUNK):
            g_cols = slice(c * FFN_CHUNK, (c + 1) * FFN_CHUNK)
            l_cols = slice(D_EXPERT + c * FFN_CHUNK, D_EXPERT + (c + 1) * FFN_CHUNK)
            glu = jnp.dot(x, w1c_ref[:, g_cols], preferred_element_type=F32) + b1_ref[:, g_cols]
            lin = jnp.dot(x, w1c_ref[:, l_cols], preferred_element_type=F32) + b1_ref[:, l_cols]
            glu = jnp.minimum(glu, SWIGLU_LIMIT)
            lin = jnp.clip(lin, -SWIGLU_LIMIT, SWIGLU_LIMIT)
            act = glu * _sigmoid(SWIGLU_ALPHA * glu) * (lin + 1.0)
            y = y + jnp.dot(act.astype(BF16), w2c_ref[g_cols, :], preferred_element_type=F32)
        o_ref[...] = gate * y

    @pl.when(i >= na_ref[0])
    def _():
        o_ref[...] = jnp.zeros_like(o_ref)


def _experts(layer, plan, xs, w1, b1, w2, b2):
    n_tiles = xs.shape[0] // TM
    act_tile = lambda i, te, na, *_: (jnp.maximum(jnp.minimum(i, na[0] - 1), 0), 0)
    expert = lambda i, te, *_: (layer, te[i], 0, 0)
    grid_spec = pltpu.PrefetchScalarGridSpec(
        num_scalar_prefetch=5,
        grid=(n_tiles,),
        in_specs=[
            pl.BlockSpec((TM, D_AUG), act_tile),
            pl.BlockSpec((None, None, 1, 2 * D_EXPERT), expert),
            pl.BlockSpec((None, None, 1, D_MODEL), expert),
            pl.BlockSpec(memory_space=pl.ANY),
            pl.BlockSpec(memory_space=pl.ANY),
        ],
        out_specs=pl.BlockSpec((TM, D_MODEL), lambda i, *_: (i, 0)),
        scratch_shapes=[pltpu.VMEM((2, D_MODEL, 2 * D_EXPERT), F32),
                        pltpu.VMEM((2, D_EXPERT, D_MODEL), F32),
                        pltpu.VMEM((D_MODEL, 2 * D_EXPERT), BF16),
                        pltpu.VMEM((D_EXPERT, D_MODEL), BF16),
                        pltpu.SemaphoreType.DMA((2, 2))],
    )
    return pl.pallas_call(
        functools.partial(_expert_kernel, layer),
        grid_spec=grid_spec,
        out_shape=jax.ShapeDtypeStruct((xs.shape[0], D_MODEL), F32),
        compiler_params=_cparams(("arbitrary",)),
        name="moe_experts",
    )(plan["tile_expert"], plan["n_active"], plan["tile_first"], plan["tile_next"],
      plan["tile_slot"], xs, b1.reshape(DEPTH, N_EXPERTS, 1, -1),
      b2.reshape(DEPTH, N_EXPERTS, 1, -1), w1, w2)


def _combine_kernel(last, src_ref, len_ref, dst_ref, tot_ref, lpos_ref, x_ref,
                    g2_ref, lg_ref, lb_ref, ys_ref, *rest):
    if last:
        o_ref, ybuf_ref, sems = rest
    else:
        sh_ref, sc_ref, w_ref, o_ref, z_ref, ybuf_ref, sems, wbf_ref = rest
        _cast_once(w_ref, wbf_ref)
    i = pl.program_id(0)
    n_blocks = pl.num_programs(0)
    slot = i % 2

    def start_block(block, s):
        _for_each_group(block, len_ref, lambda e, n: pltpu.make_async_copy(
            ys_ref.at[_group(dst_ref[block, e], n)],
            ybuf_ref.at[s, _group(src_ref[block, e], n)], sems.at[s]).start())

    @pl.when(i == 0)
    def _():
        ybuf_ref[...] = jnp.zeros_like(ybuf_ref)
        start_block(0, 0)

    @pl.when(i + 1 < n_blocks)
    def _():
        start_block(i + 1, 1 - slot)

    rows = _group(0, pl.multiple_of(tot_ref[i], GROUP))
    pltpu.make_async_copy(ys_ref.at[rows], ybuf_ref.at[slot, rows], sems.at[slot]).wait()
    lpos = lpos_ref[...]
    targets = [jnp.broadcast_to(lpos[:, k:k + 1], (TR, KB)) for k in range(TOP_K)]
    cols = lax.broadcasted_iota(jnp.int32, (TR, KB), 1)
    f = jnp.zeros((TR, D_MODEL), F32)
    for jb in range(RB // KB):
        sel = _one_hot_rows(targets, cols + jb * KB).astype(BF16)
        f = f + jnp.dot(sel, ybuf_ref[slot, jb * KB:(jb + 1) * KB, :].astype(BF16),
                        preferred_element_type=F32)
    x = ALPHA * x_ref[...] + _per_batch(f, g2_ref[0], jnp.multiply)
    out = _layer_norm(x) * lg_ref[...] + lb_ref[...]
    if last:
        o_ref[...] = jnp.transpose(out.reshape(TT, BATCH, D_MODEL), (1, 0, 2))
    else:
        o_ref[...] = out
        z_ref[...] = _in_projection(out, sh_ref, sc_ref, wbf_ref)


def _combine(t_off, plan, x, mod, ln_g, ln_b, ys, next_layer=None):
    n = x.shape[0] // TR
    last = next_layer is None
    row = lambda w: pl.BlockSpec((TR, w), lambda i, *_: (i, 0))
    const = lambda shape: pl.BlockSpec(shape, lambda i, *_: (0,) * len(shape))
    if last:
        extra_specs, extra_args, extra_scratch = [], (), []
        out_spec = pl.BlockSpec((BATCH, TT, D_MODEL), lambda i, *_: (0, i, 0))
        out_shape = jax.ShapeDtypeStruct((BATCH, n * TT, D_MODEL), F32)
    else:
        next_mod, w_in, next_l = next_layer
        extra_specs = [_mod_spec(0, t_off), _mod_spec(1, t_off),
                       pl.BlockSpec((None, D_MODEL, D_IN), lambda i, *_: (next_l, 0, 0))]
        extra_args = (next_mod, next_mod, w_in)
        extra_scratch = [pltpu.VMEM((D_MODEL, D_IN), BF16)]
        out_spec = [row(D_MODEL), row(D_IN)]
        out_shape = [jax.ShapeDtypeStruct((n * TR, D_MODEL), F32),
                     jax.ShapeDtypeStruct((n * TR, D_IN), F32)]
    grid_spec = pltpu.PrefetchScalarGridSpec(
        num_scalar_prefetch=4,
        grid=(n,),
        in_specs=[
            row(TOP_K), row(D_MODEL), _mod_spec(5, t_off),
            const((1, D_MODEL)), const((1, D_MODEL)),
            pl.BlockSpec(memory_space=pl.ANY),
        ] + extra_specs,
        out_specs=out_spec,
        scratch_shapes=[pltpu.VMEM((2, RB, D_MODEL), F32),
                        pltpu.SemaphoreType.DMA((2,))] + extra_scratch,
    )
    return pl.pallas_call(
        functools.partial(_combine_kernel, last),
        grid_spec=grid_spec,
        out_shape=out_shape,
        compiler_params=_cparams(("arbitrary",)),
        name="moe_combine",
    )(plan["src_row"], plan["n_rows"], plan["dst_row"], plan["block_rows"], plan["lpos"],
      x, mod, ln_g, ln_b, ys, *extra_args)

def _routing_plan(route, counts):
    i32 = jnp.int32
    nb = counts.shape[0]
    experts = jnp.arange(N_EXPERTS, dtype=i32)
    eid = route[:, LANE_EID:LANE_EID + TOP_K].astype(i32)
    rank = route[:, LANE_RANK:LANE_RANK + TOP_K].astype(i32)
    n = counts[:, 0, :N_EXPERTS].astype(i32)
    n = (n + GROUP - 1) // GROUP * GROUP
    src_row = jnp.cumsum(n, axis=1) - n
    seg = jnp.sum(n, axis=0)
    tiles_e = (seg + TM - 1) // TM
    tile_end = jnp.cumsum(tiles_e)
    e_start = (tile_end - tiles_e) * TM
    n_active = tile_end[-1]
    dst_row = e_start[None, :] + jnp.cumsum(n, axis=0) - n

    onehot = eid[..., None] == experts
    lpos = jnp.sum(jnp.where(onehot, src_row[:, None, None, :], 0), axis=-1) + rank
    lpos_t = jnp.concatenate([lpos.astype(F32), route[:, LANE_P:LANE_P + TOP_K]], axis=1)

    n_tiles = (nb * RB + TM - 1) // TM + N_EXPERTS
    t = jnp.minimum(jnp.arange(n_tiles, dtype=i32), n_active - 1)
    tile_expert = jnp.sum((t[:, None] >= tile_end[None, :]).astype(i32), axis=1)
    tile_expert = jnp.minimum(tile_expert, N_EXPERTS - 1)
    tile_first = jnp.concatenate(
        [jnp.ones((1,), i32), (tile_expert[1:] != tile_expert[:-1]).astype(i32)])
    has_rows = tiles_e > 0
    later = jnp.where(has_rows[None, :] & (experts[None, :] > experts[:, None]),
                      experts[None, :], N_EXPERTS)
    next_e = jnp.min(later, axis=1)
    next_e = jnp.where(next_e == N_EXPERTS, -1, next_e)
    group = jnp.cumsum(has_rows.astype(i32)) - 1
    pick = tile_expert[:, None] == experts[None, :]
    tile_next = jnp.sum(jnp.where(pick, next_e[None, :], 0), axis=1)
    tile_slot = jnp.sum(jnp.where(pick, group[None, :], 0), axis=1) % 2
    return dict(lpos=lpos.transpose(0, 2, 1).reshape(nb * TR, TOP_K), lpos_t=lpos_t, src_row=src_row, n_rows=n,
                dst_row=dst_row, block_rows=jnp.sum(n, axis=1), tail_start=e_start + seg,
                tail_len=tile_end * TM - e_start - seg,
                tile_expert=tile_expert, n_active=n_active.reshape(1), tile_first=tile_first,
                tile_next=tile_next, tile_slot=tile_slot,
                n_sorted=n_tiles * TM)


def _grid_sincos_parts():
    quarter = D_MODEL // 4
    omega = 1.0 / (10000.0 ** (jnp.arange(quarter, dtype=F32) / quarter))

    def emb1d(n):
        ang = jnp.arange(n, dtype=F32)[:, None] * omega[None, :]
        return jnp.concatenate([jnp.sin(ang), jnp.cos(ang)], axis=-1)

    er = emb1d(SEQ // GRID_W)
    ec = jnp.repeat(emb1d(GRID_W), BATCH, axis=0)
    return er.reshape(SEQ // GRID_W, 1, D_MODEL // 2), ec


def _gate_blocks(w):
    per = GATE_BLK // LRU_HEAD_DIM
    w = w.reshape(2, D_LRU // GATE_BLK, per, LRU_HEAD_DIM, LRU_HEAD_DIM)
    eye = jnp.eye(per, dtype=w.dtype)
    blk = jnp.einsum('dkpij,pq->dkpiqj', w, eye)
    return blk.reshape(2, D_LRU // GATE_BLK, GATE_BLK, GATE_BLK)


def kernel(x, c, ctx, c_ctx, w_mod, b_mod, w_in, conv_w, conv_b, gate_a_w, gate_a_b,
           gate_x_w, gate_x_b, lru_lambda, pool_w, pool_b, pool_scale, w_out, ln1_g, ln1_b,
           router_w, router_b, exp_w1, exp_b1, exp_w2, exp_b2, ln2_g, ln2_b):
    cvec = jnp.concatenate([c, c_ctx[None], jnp.zeros((2 * SUBLANES - BATCH - 1, D_MODEL), F32)])
    mod_all = _modulation(cvec, w_mod, b_mod)
    mods = jnp.stack([jnp.broadcast_to(mod_all[:, BATCH:BATCH + 1], (DEPTH, BATCH, 6 * D_MODEL)),
                      mod_all[:, :BATCH]], axis=1)
    tri = jnp.tril(jnp.ones((TR, TR), F32), -1).astype(BF16)

    er, ec = _grid_sincos_parts()
    xs, z = _entry(ctx, x, er, ec, mods[0], w_in)
    for l in range(DEPTH):
        last = l == DEPTH - 1
        t_off = CTX_TILES if last else 0
        mod = mods[l]
        gate_w = jnp.stack([_gate_blocks(gate_a_w[l]), _gate_blocks(gate_x_w[l])],
                           axis=1).astype(BF16)
        gate_b = jnp.stack([gate_a_b[l], gate_x_b[l]], axis=1)
        hf, hr = _scan(z, conv_w[l], conv_b[l][None], gate_w, gate_b, lru_lambda[l])
        rw_pad = jnp.zeros((D_MODEL, LANES), F32).at[:, :N_EXPERTS].set(router_w[l])
        rw_hi = rw_pad.astype(BF16)
        rw_pad = jnp.stack([rw_hi, (rw_pad - rw_hi.astype(F32)).astype(BF16)])
        rb_pad = jnp.full((1, LANES), -1e30, F32).at[0, :N_EXPERTS].set(router_b[l])
        xs, route, counts = _mixer(
            l, t_off, hf, hr, z, xs, mod, pool_w, pool_b[l][None], pool_scale[l][None], w_out,
            ln1_g[l][None], ln1_b[l][None], rw_pad, rb_pad, tri)
        plan = _routing_plan(route, counts)
        x_sorted = _dispatch(t_off, plan, xs, mod)
        y_sorted = _experts(l, plan, x_sorted, exp_w1, exp_b1, exp_w2, exp_b2)
        if last:
            return _combine(t_off, plan, xs, mod, ln2_g[l][None], ln2_b[l][None], y_sorted)
        xs, z = _combine(t_off, plan, xs, mod, ln2_g[l][None], ln2_b[l][None], y_sorted,
                         (mods[l + 1], w_in, l + 1))
```

```python
import functools

import jax
import jax.numpy as jnp
from jax import lax
from jax.experimental import pallas as pl
from jax.experimental.pallas import tpu as pltpu

D_MODEL = 1024
BATCH = 8
SEQ = 2048
DEPTH = 2
GRID_W = 64
CTX_LEN = 256
D_LRU = 512
N_LRU_HEADS = 8
LRU_HEAD_DIM = D_LRU // N_LRU_HEADS
LRU_C = 8.0
D_POOL = 512
POOL_WINDOWS = (2, 4, 8, 16)
POOL_GROUP_DIM = D_POOL // len(POOL_WINDOWS)
D_IN = 2 * D_LRU + D_POOL
N_EXPERTS = 32
TOP_K = 4
D_EXPERT = D_MODEL
SWIGLU_LIMIT = 7.0
SWIGLU_ALPHA = 1.702
LN_EPS = 1e-5
ALPHA = (2.0 * DEPTH) ** 0.25

F32 = jnp.float32
BF16 = jnp.bfloat16

SUBLANES = 8
LANES = 128
TT = GRID_W
TR = TT * BATCH
T_ALL = CTX_LEN + SEQ
R_ALL = T_ALL * BATCH
N_TILES = T_ALL // TT
CTX_TILES = CTX_LEN // TT
GATE_BLK = 256
POOL_HALO = max(POOL_WINDOWS) // 2
TM = 512
FFN_CHUNK = 512
GROUP = SUBLANES
RB = TOP_K * TR + N_EXPERTS * GROUP
KB = 256
MB = 512
RB_STAGE = (RB + MB - 1) // MB * MB
D_AUG = D_MODEL + LANES
LANE_P, LANE_EID, LANE_RANK = 0, 4, 8
ROUTE_ROWS = 16
VMEM_LIMIT = 56 * 1024 * 1024


def _cparams(sem):
    return pltpu.CompilerParams(dimension_semantics=sem, vmem_limit_bytes=VMEM_LIMIT)


def _sigmoid(x):
    return 0.5 * (1.0 + jnp.tanh(0.5 * x))


def _layer_norm(x):
    mu = jnp.mean(x, axis=-1, keepdims=True)
    xc = x - mu
    var = jnp.mean(xc * xc, axis=-1, keepdims=True)
    return xc * lax.rsqrt(var + LN_EPS)


def _per_batch(x, v, op):
    r, d = x.shape
    return op(x.reshape(r // BATCH, BATCH, d), v[None]).reshape(r, d)


def _mod_kernel(c_ref, w_ref, b_ref, o_ref):
    c = c_ref[...]
    s = c * _sigmoid(c)
    w = w_ref[...]
    s_hi, w_hi = s.astype(BF16), w.astype(BF16)
    s_lo = (s - s_hi.astype(F32)).astype(BF16)
    w_lo = (w - w_hi.astype(F32)).astype(BF16)
    o_ref[...] = (jnp.dot(s_hi, w_hi, preferred_element_type=F32)
                  + jnp.dot(s_lo, w_hi, preferred_element_type=F32)
                  + jnp.dot(s_hi, w_lo, preferred_element_type=F32)) + b_ref[...]


def _modulation(cvec, w_mod, b_mod):
    tn = 512
    return pl.pallas_call(
        _mod_kernel,
        grid=(DEPTH, 6 * D_MODEL // tn),
        in_specs=[
            pl.BlockSpec((2 * SUBLANES, D_MODEL), lambda l, j: (0, 0)),
            pl.BlockSpec((None, D_MODEL, tn), lambda l, j: (l, 0, j)),
            pl.BlockSpec((None, 1, tn), lambda l, j: (l, 0, j)),
        ],
        out_specs=pl.BlockSpec((None, 2 * SUBLANES, tn), lambda l, j: (l, 0, j)),
        out_shape=jax.ShapeDtypeStruct((DEPTH, 2 * SUBLANES, 6 * D_MODEL), F32),
        compiler_params=_cparams(("arbitrary", "arbitrary")),
        name="modulation",
    )(cvec, w_mod, b_mod.reshape(DEPTH, 1, 6 * D_MODEL))


def _modulate(x, sh_ref, sc_ref):
    h = _per_batch(x, 1.0 + sc_ref[0], jnp.multiply)
    return _per_batch(h, sh_ref[0], jnp.add)


def _in_projection(x, sh_ref, sc_ref, w_ref):
    return jnp.dot(_modulate(x, sh_ref, sc_ref).astype(BF16), w_ref[...],
                   preferred_element_type=F32)


def _cast_once(w_ref, wbf_ref):
    @pl.when(pl.program_id(0) == 0)
    def _():
        wbf_ref[...] = w_ref[...].astype(BF16)


def _entry_kernel(ctx_ref, x_ref, er_ref, ec_ref, sh_ref, sc_ref, w_ref, o_ref, z_ref, wbf_ref):
    i = pl.program_id(0)
    _cast_once(w_ref, wbf_ref)

    def time_major(src_ref):
        return jnp.transpose(src_ref[...], (1, 0, 2)).reshape(TR, D_MODEL)

    @pl.when(i < CTX_TILES)
    def _():
        o_ref[...] = _layer_norm(time_major(ctx_ref))

    @pl.when(i >= CTX_TILES)
    def _():
        pos = jnp.concatenate(
            [jnp.broadcast_to(er_ref[0], (TR, D_MODEL // 2)), ec_ref[...]], axis=1)
        o_ref[...] = _layer_norm(time_major(x_ref) + pos)

    z_ref[...] = _in_projection(o_ref[...], sh_ref, sc_ref, wbf_ref)


def _mod_spec(chunk, t_off=0):
    return pl.BlockSpec((1, BATCH, D_MODEL),
                        lambda i, *_: ((i + t_off >= CTX_TILES).astype(jnp.int32), 0, chunk))


def _entry(ctx, x, er, ec, mod, w_in):
    return pl.pallas_call(
        _entry_kernel,
        grid=(N_TILES,),
        in_specs=[
            pl.BlockSpec((BATCH, TT, D_MODEL), lambda i: (0, jnp.minimum(i, CTX_TILES - 1), 0)),
            pl.BlockSpec((BATCH, TT, D_MODEL), lambda i: (0, jnp.maximum(i - CTX_TILES, 0), 0)),
            pl.BlockSpec((1, 1, D_MODEL // 2), lambda i: (jnp.maximum(i - CTX_TILES, 0), 0, 0)),
            pl.BlockSpec((TR, D_MODEL // 2), lambda i: (0, 0)),
            _mod_spec(0), _mod_spec(1),
            pl.BlockSpec((None, D_MODEL, D_IN), lambda i: (0, 0, 0)),
        ],
        out_specs=[pl.BlockSpec((TR, D_MODEL), lambda i: (i, 0)),
                   pl.BlockSpec((TR, D_IN), lambda i: (i, 0))],
        out_shape=[jax.ShapeDtypeStruct((R_ALL, D_MODEL), F32),
                   jax.ShapeDtypeStruct((R_ALL, D_IN), F32)],
        scratch_shapes=[pltpu.VMEM((D_MODEL, D_IN), BF16)],
        compiler_params=_cparams(("arbitrary",)),
        name="entry_ln",
    )(ctx, x, er, ec, mod, mod, w_in)


def _rev_tile(i):
    return jnp.where(i < CTX_TILES, CTX_TILES - 1 - i, N_TILES - 1 + CTX_TILES - i)


def _block_diag_dot(u_bf, w_ref, d, g):
    halves = [jnp.dot(u_bf[:, k * GATE_BLK:(k + 1) * GATE_BLK], w_ref[d, g, k],
                      preferred_element_type=F32) for k in range(D_LRU // GATE_BLK)]
    return jnp.concatenate(halves, axis=1)


def _lru_coeffs(tile, z_ref, zp_ref, zn_ref, cw_ref, cb_ref, gw_ref, gb_ref, lam_ref, d):
    x = z_ref[...]
    seg_first = (tile == 0) | (tile == CTX_TILES)
    seg_last = (tile == CTX_TILES - 1) | (tile == N_TILES - 1)
    prev = jnp.where(seg_first, 0.0, zp_ref[...])
    nxt = jnp.where(seg_last, 0.0, zn_ref[...])
    xm2 = jnp.concatenate([prev, x[:-2 * BATCH]], axis=0)
    xm1 = jnp.concatenate([prev[BATCH:], x[:-BATCH]], axis=0)
    xp1 = jnp.concatenate([x[BATCH:], nxt], axis=0)
    u = (cb_ref[...] + xm2 * cw_ref[0:1] + xm1 * cw_ref[1:2]
         + x * cw_ref[2:3] + xp1 * cw_ref[3:4])
    u_bf = u.astype(BF16)
    r = _sigmoid(_block_diag_dot(u_bf, gw_ref, d, 0) + gb_ref[d, 0:1])
    ig = _sigmoid(_block_diag_dot(u_bf, gw_ref, d, 1) + gb_ref[d, 1:2])
    nl = -lam_ref[d:d + 1]
    softplus = jnp.maximum(nl, 0.0) + jnp.log(1.0 + jnp.exp(-jnp.abs(nl)))
    log_a = (-LRU_C) * r * softplus
    a = jnp.exp(log_a)
    gap = 1.0 - a * a
    mult = jnp.where(gap > 0.0, gap * lax.rsqrt(gap), 0.0)
    return a, mult * (ig * u)


def _scan_kernel(zf_ref, zfp_ref, zfn_ref, zr_ref, zrp_ref, zrn_ref,
                 cw_ref, cb_ref, gw_ref, gb_ref, lam_ref,
                 hf_ref, hr_ref, state_ref):
    i = pl.program_id(0)

    @pl.when(i == 0)
    def _():
        state_ref[...] = jnp.zeros_like(state_ref)

    af, bf = _lru_coeffs(i, zf_ref, zfp_ref, zfn_ref, cw_ref, cb_ref, gw_ref, gb_ref,
                         lam_ref, 0)
    ar, br = _lru_coeffs(_rev_tile(i), zr_ref, zrp_ref, zrn_ref, cw_ref, cb_ref, gw_ref,
                         gb_ref, lam_ref, 1)
    hf = state_ref[0]
    hr = state_ref[1]
    for s in range(TT):
        f = slice(s * BATCH, (s + 1) * BATCH)
        hf = af[f] * hf + bf[f]
        hf_ref[f, :] = hf
        b = slice((TT - 1 - s) * BATCH, (TT - s) * BATCH)
        hr = ar[b] * hr + br[b]
        hr_ref[b, :] = hr
    state_ref[0] = hf
    state_ref[1] = hr


def _scan(z, conv_w, conv_b, gate_w, gate_b, lam):
    prev_rows = 2 * BATCH
    tiles_per_prev = TR // prev_rows
    tiles_per_next = TR // BATCH
    last_next = R_ALL // BATCH - 1

    def cur(f):
        return pl.BlockSpec((TR, D_LRU), lambda i: (f(i), 0))

    def prev(f):
        return pl.BlockSpec((prev_rows, D_LRU),
                            lambda i: (jnp.maximum(f(i) * tiles_per_prev - 1, 0), 0))

    def nxt(f):
        return pl.BlockSpec((BATCH, D_LRU),
                            lambda i: (jnp.minimum((f(i) + 1) * tiles_per_next, last_next), 0))

    fwd = lambda i: i
    const = lambda shape: pl.BlockSpec(shape, lambda i: (0,) * len(shape))
    return pl.pallas_call(
        _scan_kernel,
        grid=(N_TILES,),
        in_specs=[cur(fwd), prev(fwd), nxt(fwd), cur(_rev_tile), prev(_rev_tile), nxt(_rev_tile),
                  const((4, D_LRU)), const((1, D_LRU)),
                  const((2, 2, D_LRU // GATE_BLK, GATE_BLK, GATE_BLK)),
                  const((2, 2, D_LRU)), const((2, D_LRU))],
        out_specs=[pl.BlockSpec((TR, D_LRU), lambda i: (i, 0)),
                   pl.BlockSpec((TR, D_LRU), lambda i: (_rev_tile(i), 0))],
        out_shape=[jax.ShapeDtypeStruct((R_ALL, D_LRU), F32)] * 2,
        scratch_shapes=[pltpu.VMEM((2, BATCH, D_LRU), F32)],
        compiler_params=_cparams(("arbitrary",)),
        name="lru_scan",
    )(z, z, z, z, z, z, conv_w, conv_b, gate_w, gate_b, lam)


def _gelu_tanh(y):
    c = 0.7978845608028654
    return 0.5 * y * (1.0 + jnp.tanh(c * (y + 0.044715 * (y * y * y))))


def _pool_inverse_counts():
    t = jnp.arange(TT)[None, :, None]
    kind = jnp.arange(CTX_TILES + 1)[:, None, None]
    lo = jnp.where(kind < CTX_TILES, -TT * kind, 0)
    hi = jnp.where(kind < CTX_TILES, CTX_LEN - TT * kind, TT)
    half = (jnp.asarray(POOL_WINDOWS) // 2)[None, None, :]
    cnt = jnp.minimum(t + half, hi) - jnp.maximum(t - half, lo)
    inv = 1.0 / cnt.astype(F32)
    inv = jnp.repeat(jnp.repeat(inv, POOL_GROUP_DIM, axis=2), BATCH, axis=1)
    return inv


def _pool_groups(tile, xp_ref, xpp_ref, xpn_ref, inv_ref):
    in_ctx = tile < CTX_TILES
    prev_ok = in_ctx & (tile > 0)
    next_ok = in_ctx & (tile < CTX_TILES - 1)
    x = xp_ref[...]
    p = jnp.concatenate([jnp.where(prev_ok, xpp_ref[...], 0.0), x,
                         jnp.where(next_ok, xpn_ref[...], 0.0)], axis=0)
    p = p.reshape(TT + 2 * POOL_HALO, BATCH, D_POOL)
    inv = inv_ref[...].reshape(TT, BATCH, D_POOL)
    outs = []
    for g, win in enumerate(POOL_WINDOWS):
        half = win // 2
        acc = p[:, :, g * POOL_GROUP_DIM:(g + 1) * POOL_GROUP_DIM]
        width = 1
        while width < win:
            acc = acc[:acc.shape[0] - width] + acc[width:]
            width *= 2
        start = POOL_HALO - half
        wsum = acc[start:start + TT]
        lanes = slice(g * POOL_GROUP_DIM, (g + 1) * POOL_GROUP_DIM)
        centre = p[POOL_HALO:POOL_HALO + TT, :, lanes]
        outs.append((wsum * inv[:, :, lanes] - centre).reshape(TR, POOL_GROUP_DIM))
    return outs


def _mixer_kernel(t_off, hf_ref, hr_ref, y_ref, xp_ref, xpp_ref, xpn_ref, inv_ref, x_ref,
                  g1_ref, sh2_ref, sc2_ref, pw_ref, pb_ref, ps_ref, wo_ref,
                  lg_ref, lb_ref, rw_ref, rb_ref, tri_ref,
                  xo_ref, route_ref, cnt_ref, wobf_ref):
    i = pl.program_id(0)
    tile = i + t_off
    _cast_once(wo_ref, wobf_ref)

    lru = (hf_ref[...] + hr_ref[...]) * _gelu_tanh(y_ref[...])
    diffs = _pool_groups(tile, xp_ref, xpp_ref, xpn_ref, inv_ref)
    pooled = jnp.concatenate(
        [jnp.dot(d.astype(BF16), pw_ref[g].astype(BF16), preferred_element_type=F32)
         for g, d in enumerate(diffs)], axis=1)
    pooled = (pooled + pb_ref[...]) * ps_ref[...]
    mix = (jnp.dot(lru.astype(BF16), wobf_ref[0:D_LRU, :], preferred_element_type=F32)
           + jnp.dot(pooled.astype(BF16), wobf_ref[D_LRU:, :], preferred_element_type=F32))
    x = ALPHA * x_ref[...] + _per_batch(mix, g1_ref[0], jnp.multiply)
    x = _layer_norm(x) * lg_ref[...] + lb_ref[...]
    xo_ref[...] = x
    h2 = _modulate(x, sh2_ref, sc2_ref)

    h_hi = h2.astype(BF16)
    h_lo = (h2 - h_hi.astype(F32)).astype(BF16)
    by_hi = jnp.dot(h_hi, rw_ref[...], preferred_element_type=F32)
    logits = (by_hi[:, :LANES] + by_hi[:, LANES:]
              + jnp.dot(h_lo, rw_ref[:, 0:LANES], preferred_element_type=F32)) + rb_ref[...]
    lane = lax.broadcasted_iota(jnp.int32, (TR, LANES), 1).astype(F32)
    work = logits
    vals, idxs, sels = [], [], []
    for _ in range(TOP_K):
        m = jnp.max(work, axis=1, keepdims=True)
        idx = jnp.min(jnp.where(work == m, lane, float(LANES)), axis=1, keepdims=True)
        sel = lane == idx
        vals.append(m)
        idxs.append(idx)
        sels.append(sel)
        work = jnp.where(sel, -jnp.inf, work)
    exps = [jnp.exp(v - vals[0]) for v in vals]
    denom = exps[0] + exps[1] + exps[2] + exps[3]
    chosen = jnp.zeros((TR, LANES), F32)
    for sel in sels:
        chosen = chosen + sel.astype(F32)
    before = jnp.dot(tri_ref[...], chosen.astype(BF16), preferred_element_type=F32)
    route = jnp.zeros((TR, LANES), F32)
    for k in range(TOP_K):
        rank = jnp.sum(jnp.where(sels[k], before, 0.0), axis=1, keepdims=True)
        for base, val in ((LANE_P, exps[k] / denom), (LANE_EID, idxs[k]), (LANE_RANK, rank)):
            route = jnp.where(lane == float(base + k), val, route)
    route_ref[0] = route.T[0:ROUTE_ROWS, :]
    cnt_ref[0] = jnp.broadcast_to(jnp.sum(chosen, axis=0, keepdims=True), (SUBLANES, LANES))


def _mixer(layer, t_off, hf, hr, z, inv_cnt, x, mod, pool_w, pool_b, pool_scale, w_out,
           ln_g, ln_b, rw_pad, rb_pad, tri):
    n = N_TILES - t_off
    halo_rows = POOL_HALO * BATCH
    per = TR // halo_rows
    last_halo = R_ALL // halo_rows - 1
    xp_col = 2 * D_LRU // D_POOL
    row = lambda w: pl.BlockSpec((TR, w), lambda i: (i + t_off, 0))
    out_row = lambda w: pl.BlockSpec((TR, w), lambda i: (i, 0))
    const = lambda shape: pl.BlockSpec(shape, lambda i: (0,) * len(shape))

    def mod_spec(chunk):
        return pl.BlockSpec((1, BATCH, D_MODEL),
                            lambda i: ((i + t_off >= CTX_TILES).astype(jnp.int32), 0, chunk))

    return pl.pallas_call(
        functools.partial(_mixer_kernel, t_off),
        grid=(n,),
        in_specs=[
            row(D_LRU), row(D_LRU),
            pl.BlockSpec((TR, D_LRU), lambda i: (i + t_off, 1)),
            pl.BlockSpec((TR, D_POOL), lambda i: (i + t_off, xp_col)),
            pl.BlockSpec((halo_rows, D_POOL),
                         lambda i: (jnp.maximum((i + t_off) * per - 1, 0), xp_col)),
            pl.BlockSpec((halo_rows, D_POOL),
                         lambda i: (jnp.minimum((i + t_off + 1) * per, last_halo), xp_col)),
            pl.BlockSpec((None, TR, D_POOL), lambda i: (jnp.minimum(i + t_off, CTX_TILES), 0, 0)),
            row(D_MODEL),
            mod_spec(2), mod_spec(3), mod_spec(4),
            pl.BlockSpec((None, len(POOL_WINDOWS), POOL_GROUP_DIM, POOL_GROUP_DIM),
                         lambda i: (layer, 0, 0, 0)),
            const((1, D_POOL)), const((1, D_POOL)),
            pl.BlockSpec((None, D_MODEL, D_MODEL), lambda i: (layer, 0, 0)),
            const((1, D_MODEL)), const((1, D_MODEL)),
            const((D_MODEL, 2 * LANES)), const((1, LANES)),
            const((TR, TR)),
        ],
        out_specs=[out_row(D_MODEL),
                   pl.BlockSpec((1, ROUTE_ROWS, TR), lambda i: (i, 0, 0)),
                   pl.BlockSpec((1, SUBLANES, LANES), lambda i: (i, 0, 0))],
        out_shape=[jax.ShapeDtypeStruct((n * TR, D_MODEL), F32),
                   jax.ShapeDtypeStruct((n, ROUTE_ROWS, TR), F32),
                   jax.ShapeDtypeStruct((n, SUBLANES, LANES), F32)],
        scratch_shapes=[pltpu.VMEM((D_MODEL, D_MODEL), BF16)],
        compiler_params=_cparams(("arbitrary",)),
        name="mixer_out",
    )(hf, hr, z, z, z, z, inv_cnt, x, mod, mod, mod, pool_w, pool_b, pool_scale, w_out,
      ln_g, ln_b, rw_pad, rb_pad, tri)


def _one_hot_rows(targets, index, weights=None):
    matches = [index == t for t in targets]
    hit = jnp.zeros(index.shape, F32)
    for m in reversed(matches):
        hit = jnp.where(m, 1.0, hit)
    if weights is None:
        return hit
    picked = jnp.zeros(index.shape, F32)
    for m, w in zip(reversed(matches), reversed(weights)):
        picked = jnp.where(m, w, picked)
    return hit, picked


def _for_each_group(block, len_ref, body):
    def step(e, carry):
        n = pl.multiple_of(len_ref[block, e], GROUP)

        @pl.when(n > 0)
        def _():
            body(e, n)

        return carry

    lax.fori_loop(0, N_EXPERTS, step, 0)


def _group(start, n):
    return pl.ds(pl.multiple_of(start, GROUP), n)


def _dispatch_kernel(src_ref, len_ref, dst_ref, tot_ref, tail_s_ref, tail_n_ref, na_ref,
                     lpos_ref, x_ref, sh2_ref, sc2_ref, xs_ref, stage_ref, zero_ref, sems):
    i = pl.program_id(0)
    n_blocks = pl.num_programs(0) - 1
    slot = i % 2

    @pl.when(i < n_blocks)
    def _():
        lpos = lpos_ref[...]
        targets = [lpos[k:k + 1, :] for k in range(TOP_K)]
        gates = [lpos[TOP_K + k:TOP_K + k + 1, :] for k in range(TOP_K)]
        h = _modulate(x_ref[...], sh2_ref, sc2_ref).astype(BF16)

        def permute(jb, carry):
            base = pl.multiple_of(jb * MB, MB)
            rows = (lax.broadcasted_iota(jnp.int32, (MB, TR), 0) + base).astype(F32)
            sel, picked = _one_hot_rows(targets, rows, gates)
            stage_ref[slot, pl.ds(base, MB), 0:D_MODEL] = jnp.dot(
                sel.astype(BF16), h, preferred_element_type=F32)
            gate = jnp.sum(picked, axis=1, keepdims=True)
            stage_ref[slot, pl.ds(base, MB), D_MODEL:D_AUG] = jnp.broadcast_to(gate, (MB, LANES))
            return carry

        lax.fori_loop(0, (tot_ref[i] + MB - 1) // MB, permute, 0)
        _for_each_group(i, len_ref, lambda e, n: pltpu.make_async_copy(
            stage_ref.at[slot, _group(src_ref[i, e], n)],
            xs_ref.at[_group(dst_ref[i, e], n)], sems.at[slot]).start())

    @pl.when(i > 0)
    def _():
        rows = _group(0, pl.multiple_of(tot_ref[i - 1], GROUP))
        pltpu.make_async_copy(stage_ref.at[1 - slot, rows], xs_ref.at[rows],
                              sems.at[1 - slot]).wait()

    @pl.when(i == n_blocks)
    def _():
        zero_ref[...] = jnp.zeros_like(zero_ref)
        fill = sems.at[2]

        def expert_tail(e, n):
            return pltpu.make_async_copy(zero_ref.at[_group(0, n)],
                                         xs_ref.at[_group(tail_s_ref[0, e], n)], fill)

        def whole_tile(t):
            return pltpu.make_async_copy(zero_ref, xs_ref.at[_group(t * TM, TM)], fill)

        n_tiles = xs_ref.shape[0] // TM
        _for_each_group(0, tail_n_ref, lambda e, n: expert_tail(e, n).start())
        lax.fori_loop(na_ref[0], n_tiles, lambda t, c: (whole_tile(t).start(), c)[1], 0)
        _for_each_group(0, tail_n_ref, lambda e, n: expert_tail(e, n).wait())
        lax.fori_loop(na_ref[0], n_tiles, lambda t, c: (whole_tile(t).wait(), c)[1], 0)


def _dispatch(t_off, plan, x, mod):
    n = x.shape[0] // TR
    last = n - 1
    grid_spec = pltpu.PrefetchScalarGridSpec(
        num_scalar_prefetch=7,
        grid=(n + 1,),
        in_specs=[
            pl.BlockSpec((None, SUBLANES, TR), lambda i, *_: (jnp.minimum(i, last), 0, 0)),
            pl.BlockSpec((TR, D_MODEL), lambda i, *_: (jnp.minimum(i, last), 0)),
            _mod_spec(3, t_off), _mod_spec(4, t_off),
        ],
        out_specs=pl.BlockSpec(memory_space=pl.ANY),
        scratch_shapes=[pltpu.VMEM((2, RB_STAGE, D_AUG), F32),
                        pltpu.VMEM((TM, D_AUG), F32),
                        pltpu.SemaphoreType.DMA((3,))],
    )
    return pl.pallas_call(
        _dispatch_kernel,
        grid_spec=grid_spec,
        out_shape=jax.ShapeDtypeStruct((plan["n_sorted"], D_AUG), F32),
        compiler_params=_cparams(("arbitrary",)),
        name="moe_dispatch",
    )(plan["src_row"], plan["n_rows"], plan["dst_row"], plan["block_rows"],
      plan["tail_start"][None], plan["tail_len"][None], plan["n_active"], plan["lpos_t"],
      x, mod, mod)


def _expert_kernel(layer, te_ref, na_ref, first_ref, nxt_ref, slot_ref,
                   x_ref, b1_ref, b2_ref, w1_hbm, w2_hbm, o_ref,
                   w1buf_ref, w2buf_ref, w1c_ref, w2c_ref, sems):
    i = pl.program_id(0)
    e = te_ref[i]
    slot = slot_ref[i]

    def fetch(expert, s):
        return (pltpu.make_async_copy(w1_hbm.at[layer, expert], w1buf_ref.at[s], sems.at[s, 0]),
                pltpu.make_async_copy(w2_hbm.at[layer, expert], w2buf_ref.at[s], sems.at[s, 1]))

    @pl.when(i == 0)
    def _():
        for copy in fetch(e, slot):
            copy.start()

    @pl.when(first_ref[i] == 1)
    def _():
        for copy in fetch(e, slot):
            copy.wait()
        w1c_ref[...] = w1buf_ref[slot].astype(BF16)
        w2c_ref[...] = w2buf_ref[slot].astype(BF16)

        @pl.when(nxt_ref[i] >= 0)
        def _():
            for copy in fetch(nxt_ref[i], 1 - slot):
                copy.start()

    @pl.when(i < na_ref[0])
    def _():
        gate = x_ref[:, D_MODEL:D_MODEL + 1]
        x = x_ref[:, 0:D_MODEL].astype(BF16)
        y = jnp.broadcast_to(b2_ref[...], (TM, D_MODEL))
        for c in range(D_EXPERT // FFN_CHUNK):
            g_cols = slice(c * FFN_CHUNK, (c + 1) * FFN_CHUNK)
            l_cols = slice(D_EXPERT + c * FFN_CHUNK, D_EXPERT + (c + 1) * FFN_CHUNK)
            glu = jnp.dot(x, w1c_ref[:, g_cols], preferred_element_type=F32) + b1_ref[:, g_cols]
            lin = jnp.dot(x, w1c_ref[:, l_cols], preferred_element_type=F32) + b1_ref[:, l_cols]
            glu = jnp.minimum(glu, SWIGLU_LIMIT)
            lin = jnp.clip(lin, -SWIGLU_LIMIT, SWIGLU_LIMIT)
            act = glu * _sigmoid(SWIGLU_ALPHA * glu) * (lin + 1.0)
            y = y + jnp.dot(act.astype(BF16), w2c_ref[g_cols, :], preferred_element_type=F32)
        o_ref[...] = gate * y

    @pl.when(i >= na_ref[0])
    def _():
        o_ref[...] = jnp.zeros_like(o_ref)


def _experts(layer, plan, xs, w1, b1, w2, b2):
    n_tiles = xs.shape[0] // TM
    act_tile = lambda i, te, na, *_: (jnp.maximum(jnp.minimum(i, na[0] - 1), 0), 0)
    expert = lambda i, te, *_: (layer, te[i], 0, 0)
    grid_spec = pltpu.PrefetchScalarGridSpec(
        num_scalar_prefetch=5,
        grid=(n_tiles,),
        in_specs=[
            pl.BlockSpec((TM, D_AUG), act_tile),
            pl.BlockSpec((None, None, 1, 2 * D_EXPERT), expert),
            pl.BlockSpec((None, None, 1, D_MODEL), expert),
            pl.BlockSpec(memory_space=pl.ANY),
            pl.BlockSpec(memory_space=pl.ANY),
        ],
        out_specs=pl.BlockSpec((TM, D_MODEL), lambda i, *_: (i, 0)),
        scratch_shapes=[pltpu.VMEM((2, D_MODEL, 2 * D_EXPERT), F32),
                        pltpu.VMEM((2, D_EXPERT, D_MODEL), F32),
                        pltpu.VMEM((D_MODEL, 2 * D_EXPERT), BF16),
                        pltpu.VMEM((D_EXPERT, D_MODEL), BF16),
                        pltpu.SemaphoreType.DMA((2, 2))],
    )
    return pl.pallas_call(
        functools.partial(_expert_kernel, layer),
        grid_spec=grid_spec,
        out_shape=jax.ShapeDtypeStruct((xs.shape[0], D_MODEL), F32),
        compiler_params=_cparams(("arbitrary",)),
        name="moe_experts",
    )(plan["tile_expert"], plan["n_active"], plan["tile_first"], plan["tile_next"],
      plan["tile_slot"], xs, b1.reshape(DEPTH, N_EXPERTS, 1, -1),
      b2.reshape(DEPTH, N_EXPERTS, 1, -1), w1, w2)


def _combine_kernel(last, src_ref, len_ref, dst_ref, tot_ref, lpos_ref, x_ref,
                    g2_ref, lg_ref, lb_ref, ys_ref, *rest):
    if last:
        o_ref, ybuf_ref, sems = rest
    else:
        sh_ref, sc_ref, w_ref, o_ref, z_ref, ybuf_ref, sems, wbf_ref = rest
        _cast_once(w_ref, wbf_ref)
    i = pl.program_id(0)
    n_blocks = pl.num_programs(0)
    slot = i % 2

    def start_block(block, s):
        _for_each_group(block, len_ref, lambda e, n: pltpu.make_async_copy(
            ys_ref.at[_group(dst_ref[block, e], n)],
            ybuf_ref.at[s, _group(src_ref[block, e], n)], sems.at[s]).start())

    @pl.when(i == 0)
    def _():
        ybuf_ref[...] = jnp.zeros_like(ybuf_ref)
        start_block(0, 0)

    @pl.when(i + 1 < n_blocks)
    def _():
        start_block(i + 1, 1 - slot)

    rows = _group(0, pl.multiple_of(tot_ref[i], GROUP))
    pltpu.make_async_copy(ys_ref.at[rows], ybuf_ref.at[slot, rows], sems.at[slot]).wait()
    lpos = lpos_ref[...]
    targets = [jnp.broadcast_to(lpos[:, k:k + 1], (TR, KB)) for k in range(TOP_K)]
    cols = lax.broadcasted_iota(jnp.int32, (TR, KB), 1)
    f = jnp.zeros((TR, D_MODEL), F32)
    for jb in range(RB // KB):
        sel = _one_hot_rows(targets, cols + jb * KB).astype(BF16)
        f = f + jnp.dot(sel, ybuf_ref[slot, jb * KB:(jb + 1) * KB, :].astype(BF16),
                        preferred_element_type=F32)
    x = ALPHA * x_ref[...] + _per_batch(f, g2_ref[0], jnp.multiply)
    out = _layer_norm(x) * lg_ref[...] + lb_ref[...]
    if last:
        o_ref[...] = jnp.transpose(out.reshape(TT, BATCH, D_MODEL), (1, 0, 2))
    else:
        o_ref[...] = out
        z_ref[...] = _in_projection(out, sh_ref, sc_ref, wbf_ref)


def _combine(t_off, plan, x, mod, ln_g, ln_b, ys, next_layer=None):
    n = x.shape[0] // TR
    last = next_layer is None
    row = lambda w: pl.BlockSpec((TR, w), lambda i, *_: (i, 0))
    const = lambda shape: pl.BlockSpec(shape, lambda i, *_: (0,) * len(shape))
    if last:
        extra_specs, extra_args, extra_scratch = [], (), []
        out_spec = pl.BlockSpec((BATCH, TT, D_MODEL), lambda i, *_: (0, i, 0))
        out_shape = jax.ShapeDtypeStruct((BATCH, n * TT, D_MODEL), F32)
    else:
        next_mod, w_in, next_l = next_layer
        extra_specs = [_mod_spec(0, t_off), _mod_spec(1, t_off),
                       pl.BlockSpec((None, D_MODEL, D_IN), lambda i, *_: (next_l, 0, 0))]
        extra_args = (next_mod, next_mod, w_in)
        extra_scratch = [pltpu.VMEM((D_MODEL, D_IN), BF16)]
        out_spec = [row(D_MODEL), row(D_IN)]
        out_shape = [jax.ShapeDtypeStruct((n * TR, D_MODEL), F32),
                     jax.ShapeDtypeStruct((n * TR, D_IN), F32)]
    grid_spec = pltpu.PrefetchScalarGridSpec(
        num_scalar_prefetch=4,
        grid=(n,),
        in_specs=[
            row(TOP_K), row(D_MODEL), _mod_spec(5, t_off),
            const((1, D_MODEL)), const((1, D_MODEL)),
            pl.BlockSpec(memory_space=pl.ANY),
        ] + extra_specs,
        out_specs=out_spec,
        scratch_shapes=[pltpu.VMEM((2, RB, D_MODEL), F32),
                        pltpu.SemaphoreType.DMA((2,))] + extra_scratch,
    )
    return pl.pallas_call(
        functools.partial(_combine_kernel, last),
        grid_spec=grid_spec,
        out_shape=out_shape,
        compiler_params=_cparams(("arbitrary",)),
        name="moe_combine",
    )(plan["src_row"], plan["n_rows"], plan["dst_row"], plan["block_rows"], plan["lpos"],
      x, mod, ln_g, ln_b, ys, *extra_args)

def _routing_plan(route, counts):
    i32 = jnp.int32
    nb = counts.shape[0]
    experts = jnp.arange(N_EXPERTS, dtype=i32)
    eid = route[:, LANE_EID:LANE_EID + TOP_K].astype(i32)
    rank = route[:, LANE_RANK:LANE_RANK + TOP_K].astype(i32)
    n = counts[:, 0, :N_EXPERTS].astype(i32)
    n = (n + GROUP - 1) // GROUP * GROUP
    src_row = jnp.cumsum(n, axis=1) - n
    seg = jnp.sum(n, axis=0)
    tiles_e = (seg + TM - 1) // TM
    tile_end = jnp.cumsum(tiles_e)
    e_start = (tile_end - tiles_e) * TM
    n_active = tile_end[-1]
    dst_row = e_start[None, :] + jnp.cumsum(n, axis=0) - n

    onehot = eid[..., None] == experts
    lpos = jnp.sum(jnp.where(onehot, src_row[:, None, None, :], 0), axis=-1) + rank
    lpos_t = jnp.concatenate([lpos.astype(F32), route[:, LANE_P:LANE_P + TOP_K]], axis=1)

    n_tiles = (nb * RB + TM - 1) // TM + N_EXPERTS
    t = jnp.minimum(jnp.arange(n_tiles, dtype=i32), n_active - 1)
    tile_expert = jnp.sum((t[:, None] >= tile_end[None, :]).astype(i32), axis=1)
    tile_expert = jnp.minimum(tile_expert, N_EXPERTS - 1)
    tile_first = jnp.concatenate(
        [jnp.ones((1,), i32), (tile_expert[1:] != tile_expert[:-1]).astype(i32)])
    has_rows = tiles_e > 0
    later = jnp.where(has_rows[None, :] & (experts[None, :] > experts[:, None]),
                      experts[None, :], N_EXPERTS)
    next_e = jnp.min(later, axis=1)
    next_e = jnp.where(next_e == N_EXPERTS, -1, next_e)
    group = jnp.cumsum(has_rows.astype(i32)) - 1
    pick = tile_expert[:, None] == experts[None, :]
    tile_next = jnp.sum(jnp.where(pick, next_e[None, :], 0), axis=1)
    tile_slot = jnp.sum(jnp.where(pick, group[None, :], 0), axis=1) % 2
    return dict(lpos=lpos.transpose(0, 2, 1).reshape(nb * TR, TOP_K), lpos_t=lpos_t, src_row=src_row, n_rows=n,
                dst_row=dst_row, block_rows=jnp.sum(n, axis=1), tail_start=e_start + seg,
                tail_len=tile_end * TM - e_start - seg,
                tile_expert=tile_expert, n_active=n_active.reshape(1), tile_first=tile_first,
                tile_next=tile_next, tile_slot=tile_slot,
                n_sorted=n_tiles * TM)


def _grid_sincos_parts():
    quarter = D_MODEL // 4
    omega = 1.0 / (10000.0 ** (jnp.arange(quarter, dtype=F32) / quarter))

    def emb1d(n):
        ang = jnp.arange(n, dtype=F32)[:, None] * omega[None, :]
        return jnp.concatenate([jnp.sin(ang), jnp.cos(ang)], axis=-1)

    er = emb1d(SEQ // GRID_W)
    ec = jnp.repeat(emb1d(GRID_W), BATCH, axis=0)
    return er.reshape(SEQ // GRID_W, 1, D_MODEL // 2), ec


def _gate_blocks(w):
    per = GATE_BLK // LRU_HEAD_DIM
    w = w.reshape(2, D_LRU // GATE_BLK, per, LRU_HEAD_DIM, LRU_HEAD_DIM)
    eye = jnp.eye(per, dtype=w.dtype)
    blk = jnp.einsum('dkpij,pq->dkpiqj', w, eye)
    return blk.reshape(2, D_LRU // GATE_BLK, GATE_BLK, GATE_BLK)


def kernel(x, c, ctx, c_ctx, w_mod, b_mod, w_in, conv_w, conv_b, gate_a_w, gate_a_b,
           gate_x_w, gate_x_b, lru_lambda, pool_w, pool_b, pool_scale, w_out, ln1_g, ln1_b,
           router_w, router_b, exp_w1, exp_b1, exp_w2, exp_b2, ln2_g, ln2_b):
    cvec = jnp.concatenate([c, c_ctx[None], jnp.zeros((2 * SUBLANES - BATCH - 1, D_MODEL), F32)])
    mod_all = _modulation(cvec, w_mod, b_mod)
    mods = jnp.stack([jnp.broadcast_to(mod_all[:, BATCH:BATCH + 1], (DEPTH, BATCH, 6 * D_MODEL)),
                      mod_all[:, :BATCH]], axis=1)
    tri = jnp.tril(jnp.ones((TR, TR), F32), -1).astype(BF16)
    inv_cnt = _pool_inverse_counts()

    er, ec = _grid_sincos_parts()
    xs, z = _entry(ctx, x, er, ec, mods[0], w_in)
    for l in range(DEPTH):
        last = l == DEPTH - 1
        t_off = CTX_TILES if last else 0
        mod = mods[l]
        gate_w = jnp.stack([_gate_blocks(gate_a_w[l]), _gate_blocks(gate_x_w[l])],
                           axis=1).astype(BF16)
        gate_b = jnp.stack([gate_a_b[l], gate_x_b[l]], axis=1)
        hf, hr = _scan(z, conv_w[l], conv_b[l][None], gate_w, gate_b, lru_lambda[l])
        rw_pad = jnp.zeros((D_MODEL, LANES), F32).at[:, :N_EXPERTS].set(router_w[l])
        rw_hi = rw_pad.astype(BF16)
        rw_pad = jnp.concatenate([rw_hi, (rw_pad - rw_hi.astype(F32)).astype(BF16)], axis=1)
        rb_pad = jnp.full((1, LANES), -1e30, F32).at[0, :N_EXPERTS].set(router_b[l])
        xs, route, counts = _mixer(
            l, t_off, hf, hr, z, inv_cnt, xs, mod, pool_w, pool_b[l][None], pool_scale[l][None], w_out,
            ln1_g[l][None], ln1_b[l][None], rw_pad, rb_pad, tri)
        plan = _routing_plan(route, counts)
        x_sorted = _dispatch(t_off, plan, xs, mod)
        y_sorted = _experts(l, plan, x_sorted, exp_w1, exp_b1, exp_w2, exp_b2)
        if last:
            return _combine(t_off, plan, xs, mod, ln2_g[l][None], ln2_b[l][None], y_sorted)
        xs, z = _combine(t_off, plan, xs, mod, ln2_g[l][None], ln2_b[l][None], y_sorted,
                         (mods[l + 1], w_in, l + 1))
```

```python
import functools

import jax
import jax.numpy as jnp
from jax import lax
from jax.experimental import pallas as pl
from jax.experimental.pallas import tpu as pltpu

D_MODEL = 1024
BATCH = 8
SEQ = 2048
DEPTH = 2
GRID_W = 64
CTX_LEN = 256
D_LRU = 512
N_LRU_HEADS = 8
LRU_HEAD_DIM = D_LRU // N_LRU_HEADS
LRU_C = 8.0
D_POOL = 512
POOL_WINDOWS = (2, 4, 8, 16)
POOL_GROUP_DIM = D_POOL // len(POOL_WINDOWS)
D_IN = 2 * D_LRU + D_POOL
N_EXPERTS = 32
TOP_K = 4
D_EXPERT = D_MODEL
SWIGLU_LIMIT = 7.0
SWIGLU_ALPHA = 1.702
LN_EPS = 1e-5
ALPHA = (2.0 * DEPTH) ** 0.25

F32 = jnp.float32
BF16 = jnp.bfloat16

SUBLANES = 8
LANES = 128
TT = GRID_W
TR = TT * BATCH
T_ALL = CTX_LEN + SEQ
R_ALL = T_ALL * BATCH
N_TILES = T_ALL // TT
CTX_TILES = CTX_LEN // TT
GATE_BLK = 256
POOL_HALO = max(POOL_WINDOWS) // 2
TM = 512
FFN_CHUNK = 512
GROUP = SUBLANES
RB = TOP_K * TR + N_EXPERTS * GROUP
KB = 256
MB = 512
RB_STAGE = (RB + MB - 1) // MB * MB
D_AUG = D_MODEL + LANES
LANE_P, LANE_EID, LANE_RANK = 0, 4, 8
ROUTE_ROWS = 16
VMEM_LIMIT = 56 * 1024 * 1024


def _cparams(sem):
    return pltpu.CompilerParams(dimension_semantics=sem, vmem_limit_bytes=VMEM_LIMIT)


def _sigmoid(x):
    return 0.5 * (1.0 + jnp.tanh(0.5 * x))


def _layer_norm(x):
    mu = jnp.mean(x, axis=-1, keepdims=True)
    xc = x - mu
    var = jnp.mean(xc * xc, axis=-1, keepdims=True)
    return xc * lax.rsqrt(var + LN_EPS)


def _per_batch(x, v, op):
    r, d = x.shape
    return op(x.reshape(r // BATCH, BATCH, d), v[None]).reshape(r, d)


def _mod_kernel(c_ref, w_ref, b_ref, o_ref):
    c = c_ref[...]
    s = c * _sigmoid(c)
    w = w_ref[...]
    s_hi, w_hi = s.astype(BF16), w.astype(BF16)
    s_lo = (s - s_hi.astype(F32)).astype(BF16)
    w_lo = (w - w_hi.astype(F32)).astype(BF16)
    o_ref[...] = (jnp.dot(s_hi, w_hi, preferred_element_type=F32)
                  + jnp.dot(s_lo, w_hi, preferred_element_type=F32)
                  + jnp.dot(s_hi, w_lo, preferred_element_type=F32)) + b_ref[...]


def _modulation(cvec, w_mod, b_mod):
    tn = 512
    return pl.pallas_call(
        _mod_kernel,
        grid=(DEPTH, 6 * D_MODEL // tn),
        in_specs=[
            pl.BlockSpec((2 * SUBLANES, D_MODEL), lambda l, j: (0, 0)),
            pl.BlockSpec((None, D_MODEL, tn), lambda l, j: (l, 0, j)),
            pl.BlockSpec((None, 1, tn), lambda l, j: (l, 0, j)),
        ],
        out_specs=pl.BlockSpec((None, 2 * SUBLANES, tn), lambda l, j: (l, 0, j)),
        out_shape=jax.ShapeDtypeStruct((DEPTH, 2 * SUBLANES, 6 * D_MODEL), F32),
        compiler_params=_cparams(("arbitrary", "arbitrary")),
        name="modulation",
    )(cvec, w_mod, b_mod.reshape(DEPTH, 1, 6 * D_MODEL))


def _modulate(x, sh_ref, sc_ref):
    h = _per_batch(x, 1.0 + sc_ref[0], jnp.multiply)
    return _per_batch(h, sh_ref[0], jnp.add)


def _in_projection(x, sh_ref, sc_ref, w_ref):
    return jnp.dot(_modulate(x, sh_ref, sc_ref).astype(BF16), w_ref[...],
                   preferred_element_type=F32)


def _cast_once(w_ref, wbf_ref):
    @pl.when(pl.program_id(0) == 0)
    def _():
        wbf_ref[...] = w_ref[...].astype(BF16)


def _entry_kernel(ctx_ref, x_ref, er_ref, ec_ref, sh_ref, sc_ref, w_ref, o_ref, z_ref, wbf_ref):
    i = pl.program_id(0)
    _cast_once(w_ref, wbf_ref)

    def time_major(src_ref):
        return jnp.transpose(src_ref[...], (1, 0, 2)).reshape(TR, D_MODEL)

    @pl.when(i < CTX_TILES)
    def _():
        o_ref[...] = _layer_norm(time_major(ctx_ref))

    @pl.when(i >= CTX_TILES)
    def _():
        pos = jnp.concatenate(
            [jnp.broadcast_to(er_ref[0], (TR, D_MODEL // 2)), ec_ref[...]], axis=1)
        o_ref[...] = _layer_norm(time_major(x_ref) + pos)

    z_ref[...] = _in_projection(o_ref[...], sh_ref, sc_ref, wbf_ref)


def _mod_spec(chunk, t_off=0):
    return pl.BlockSpec((1, BATCH, D_MODEL),
                        lambda i, *_: ((i + t_off >= CTX_TILES).astype(jnp.int32), 0, chunk))


def _entry(ctx, x, er, ec, mod, w_in):
    return pl.pallas_call(
        _entry_kernel,
        grid=(N_TILES,),
        in_specs=[
            pl.BlockSpec((BATCH, TT, D_MODEL), lambda i: (0, jnp.minimum(i, CTX_TILES - 1), 0)),
            pl.BlockSpec((BATCH, TT, D_MODEL), lambda i: (0, jnp.maximum(i - CTX_TILES, 0), 0)),
            pl.BlockSpec((1, 1, D_MODEL // 2), lambda i: (jnp.maximum(i - CTX_TILES, 0), 0, 0)),
            pl.BlockSpec((TR, D_MODEL // 2), lambda i: (0, 0)),
            _mod_spec(0), _mod_spec(1),
            pl.BlockSpec((None, D_MODEL, D_IN), lambda i: (0, 0, 0)),
        ],
        out_specs=[pl.BlockSpec((TR, D_MODEL), lambda i: (i, 0)),
                   pl.BlockSpec((TR, D_IN), lambda i: (i, 0))],
        out_shape=[jax.ShapeDtypeStruct((R_ALL, D_MODEL), F32),
                   jax.ShapeDtypeStruct((R_ALL, D_IN), F32)],
        scratch_shapes=[pltpu.VMEM((D_MODEL, D_IN), BF16)],
        compiler_params=_cparams(("arbitrary",)),
        name="entry_ln",
    )(ctx, x, er, ec, mod, mod, w_in)


def _rev_tile(i):
    return jnp.where(i < CTX_TILES, CTX_TILES - 1 - i, N_TILES - 1 + CTX_TILES - i)


def _block_diag_dot(u_bf, w_ref, d, g):
    halves = [jnp.dot(u_bf[:, k * GATE_BLK:(k + 1) * GATE_BLK], w_ref[d, g, k],
                      preferred_element_type=F32) for k in range(D_LRU // GATE_BLK)]
    return jnp.concatenate(halves, axis=1)


def _lru_coeffs(tile, z_ref, zp_ref, zn_ref, cw_ref, cb_ref, gw_ref, gb_ref, lam_ref, d):
    x = z_ref[...]
    seg_first = (tile == 0) | (tile == CTX_TILES)
    seg_last = (tile == CTX_TILES - 1) | (tile == N_TILES - 1)
    prev = jnp.where(seg_first, 0.0, zp_ref[...])
    nxt = jnp.where(seg_last, 0.0, zn_ref[...])
    xm2 = jnp.concatenate([prev, x[:-2 * BATCH]], axis=0)
    xm1 = jnp.concatenate([prev[BATCH:], x[:-BATCH]], axis=0)
    xp1 = jnp.concatenate([x[BATCH:], nxt], axis=0)
    u = (cb_ref[...] + xm2 * cw_ref[0:1] + xm1 * cw_ref[1:2]
         + x * cw_ref[2:3] + xp1 * cw_ref[3:4])
    u_bf = u.astype(BF16)
    r = _sigmoid(_block_diag_dot(u_bf, gw_ref, d, 0) + gb_ref[d, 0:1])
    ig = _sigmoid(_block_diag_dot(u_bf, gw_ref, d, 1) + gb_ref[d, 1:2])
    nl = -lam_ref[d:d + 1]
    softplus = jnp.maximum(nl, 0.0) + jnp.log(1.0 + jnp.exp(-jnp.abs(nl)))
    log_a = (-LRU_C) * r * softplus
    a = jnp.exp(log_a)
    gap = 1.0 - a * a
    mult = jnp.where(gap > 0.0, gap * lax.rsqrt(gap), 0.0)
    return a, mult * (ig * u)


def _scan_kernel(zf_ref, zfp_ref, zfn_ref, zr_ref, zrp_ref, zrn_ref,
                 cw_ref, cb_ref, gw_ref, gb_ref, lam_ref,
                 hf_ref, hr_ref, state_ref):
    i = pl.program_id(0)

    @pl.when(i == 0)
    def _():
        state_ref[...] = jnp.zeros_like(state_ref)

    af, bf = _lru_coeffs(i, zf_ref, zfp_ref, zfn_ref, cw_ref, cb_ref, gw_ref, gb_ref,
                         lam_ref, 0)
    ar, br = _lru_coeffs(_rev_tile(i), zr_ref, zrp_ref, zrn_ref, cw_ref, cb_ref, gw_ref,
                         gb_ref, lam_ref, 1)
    hf = state_ref[0]
    hr = state_ref[1]
    for s in range(TT):
        f = slice(s * BATCH, (s + 1) * BATCH)
        hf = af[f] * hf + bf[f]
        hf_ref[f, :] = hf
        b = slice((TT - 1 - s) * BATCH, (TT - s) * BATCH)
        hr = ar[b] * hr + br[b]
        hr_ref[b, :] = hr
    state_ref[0] = hf
    state_ref[1] = hr


def _scan(z, conv_w, conv_b, gate_w, gate_b, lam):
    prev_rows = 2 * BATCH
    tiles_per_prev = TR // prev_rows
    tiles_per_next = TR // BATCH
    last_next = R_ALL // BATCH - 1

    def cur(f):
        return pl.BlockSpec((TR, D_LRU), lambda i: (f(i), 0))

    def prev(f):
        return pl.BlockSpec((prev_rows, D_LRU),
                            lambda i: (jnp.maximum(f(i) * tiles_per_prev - 1, 0), 0))

    def nxt(f):
        return pl.BlockSpec((BATCH, D_LRU),
                            lambda i: (jnp.minimum((f(i) + 1) * tiles_per_next, last_next), 0))

    fwd = lambda i: i
    const = lambda shape: pl.BlockSpec(shape, lambda i: (0,) * len(shape))
    return pl.pallas_call(
        _scan_kernel,
        grid=(N_TILES,),
        in_specs=[cur(fwd), prev(fwd), nxt(fwd), cur(_rev_tile), prev(_rev_tile), nxt(_rev_tile),
                  const((4, D_LRU)), const((1, D_LRU)),
                  const((2, 2, D_LRU // GATE_BLK, GATE_BLK, GATE_BLK)),
                  const((2, 2, D_LRU)), const((2, D_LRU))],
        out_specs=[pl.BlockSpec((TR, D_LRU), lambda i: (i, 0)),
                   pl.BlockSpec((TR, D_LRU), lambda i: (_rev_tile(i), 0))],
        out_shape=[jax.ShapeDtypeStruct((R_ALL, D_LRU), F32)] * 2,
        scratch_shapes=[pltpu.VMEM((2, BATCH, D_LRU), F32)],
        compiler_params=_cparams(("arbitrary",)),
        name="lru_scan",
    )(z, z, z, z, z, z, conv_w, conv_b, gate_w, gate_b, lam)


def _gelu_tanh(y):
    c = 0.7978845608028654
    return 0.5 * y * (1.0 + jnp.tanh(c * (y + 0.044715 * (y * y * y))))


def _pool_inverse_counts():
    t = jnp.arange(TT)[None, :, None]
    kind = jnp.arange(CTX_TILES + 1)[:, None, None]
    lo = jnp.where(kind < CTX_TILES, -TT * kind, 0)
    hi = jnp.where(kind < CTX_TILES, CTX_LEN - TT * kind, TT)
    half = (jnp.asarray(POOL_WINDOWS) // 2)[None, None, :]
    cnt = jnp.minimum(t + half, hi) - jnp.maximum(t - half, lo)
    inv = 1.0 / cnt.astype(F32)
    inv = jnp.repeat(jnp.repeat(inv, POOL_GROUP_DIM, axis=2), BATCH, axis=1)
    return inv


def _pool_groups(tile, xp_ref, xpp_ref, xpn_ref, inv_ref):
    in_ctx = tile < CTX_TILES
    prev_ok = in_ctx & (tile > 0)
    next_ok = in_ctx & (tile < CTX_TILES - 1)
    x = xp_ref[...]
    p = jnp.concatenate([jnp.where(prev_ok, xpp_ref[...], 0.0), x,
                         jnp.where(next_ok, xpn_ref[...], 0.0)], axis=0)
    p = p.reshape(TT + 2 * POOL_HALO, BATCH, D_POOL)
    inv = inv_ref[...].reshape(TT, BATCH, D_POOL)
    outs = []
    for g, win in enumerate(POOL_WINDOWS):
        half = win // 2
        acc = p[:, :, g * POOL_GROUP_DIM:(g + 1) * POOL_GROUP_DIM]
        width = 1
        while width < win:
            acc = acc[:acc.shape[0] - width] + acc[width:]
            width *= 2
        start = POOL_HALO - half
        wsum = acc[start:start + TT]
        lanes = slice(g * POOL_GROUP_DIM, (g + 1) * POOL_GROUP_DIM)
        centre = p[POOL_HALO:POOL_HALO + TT, :, lanes]
        outs.append((wsum * inv[:, :, lanes] - centre).reshape(TR, POOL_GROUP_DIM))
    return outs


def _mixer_kernel(t_off, hf_ref, hr_ref, y_ref, xp_ref, xpp_ref, xpn_ref, inv_ref, x_ref,
                  g1_ref, sh2_ref, sc2_ref, pw_ref, pb_ref, ps_ref, wo_ref,
                  lg_ref, lb_ref, rw_ref, rb_ref, tri_ref,
                  xo_ref, route_ref, cnt_ref, wobf_ref):
    i = pl.program_id(0)
    tile = i + t_off
    _cast_once(wo_ref, wobf_ref)

    lru = (hf_ref[...] + hr_ref[...]) * _gelu_tanh(y_ref[...])
    diffs = _pool_groups(tile, xp_ref, xpp_ref, xpn_ref, inv_ref)
    pooled = jnp.concatenate(
        [jnp.dot(d.astype(BF16), pw_ref[g].astype(BF16), preferred_element_type=F32)
         for g, d in enumerate(diffs)], axis=1)
    pooled = (pooled + pb_ref[...]) * ps_ref[...]
    mix = (jnp.dot(lru.astype(BF16), wobf_ref[0:D_LRU, :], preferred_element_type=F32)
           + jnp.dot(pooled.astype(BF16), wobf_ref[D_LRU:, :], preferred_element_type=F32))
    x = ALPHA * x_ref[...] + _per_batch(mix, g1_ref[0], jnp.multiply)
    x = _layer_norm(x) * lg_ref[...] + lb_ref[...]
    xo_ref[...] = x
    h2 = _modulate(x, sh2_ref, sc2_ref)

    h_hi = h2.astype(BF16)
    h_lo = (h2 - h_hi.astype(F32)).astype(BF16)
    by_hi = jnp.dot(h_hi, rw_ref[...], preferred_element_type=F32)
    logits = (by_hi[:, :LANES] + by_hi[:, LANES:]
              + jnp.dot(h_lo, rw_ref[:, 0:LANES], preferred_element_type=F32)) + rb_ref[...]
    work = logits.T[0:N_EXPERTS, :]
    expert = lax.broadcasted_iota(jnp.int32, (N_EXPERTS, TR), 0).astype(F32)
    vals, idxs, sels = [], [], []
    for _ in range(TOP_K):
        m = jnp.max(work, axis=0, keepdims=True)
        idx = jnp.min(jnp.where(work == m, expert, float(N_EXPERTS)), axis=0, keepdims=True)
        sel = expert == idx
        vals.append(m)
        idxs.append(idx)
        sels.append(sel)
        work = jnp.where(sel, -jnp.inf, work)
    exps = [jnp.exp(v - vals[0]) for v in vals]
    denom = exps[0] + exps[1] + exps[2] + exps[3]
    chosen = jnp.zeros((N_EXPERTS, TR), F32)
    for sel in sels:
        chosen = chosen + sel.astype(F32)
    before = jnp.dot(chosen.astype(BF16), tri_ref[...], preferred_element_type=F32)
    ranks = [jnp.sum(jnp.where(sel, before, 0.0), axis=0, keepdims=True) for sel in sels]
    route_ref[0] = jnp.concatenate(
        [e / denom for e in exps] + idxs + ranks
        + [jnp.zeros((ROUTE_ROWS - 3 * TOP_K, TR), F32)], axis=0)
    cnt_ref[0] = jnp.broadcast_to(jnp.sum(chosen, axis=1, keepdims=True), (N_EXPERTS, LANES))


def _mixer(layer, t_off, hf, hr, z, inv_cnt, x, mod, pool_w, pool_b, pool_scale, w_out,
           ln_g, ln_b, rw_pad, rb_pad, tri):
    n = N_TILES - t_off
    halo_rows = POOL_HALO * BATCH
    per = TR // halo_rows
    last_halo = R_ALL // halo_rows - 1
    xp_col = 2 * D_LRU // D_POOL
    row = lambda w: pl.BlockSpec((TR, w), lambda i: (i + t_off, 0))
    out_row = lambda w: pl.BlockSpec((TR, w), lambda i: (i, 0))
    const = lambda shape: pl.BlockSpec(shape, lambda i: (0,) * len(shape))

    def mod_spec(chunk):
        return pl.BlockSpec((1, BATCH, D_MODEL),
                            lambda i: ((i + t_off >= CTX_TILES).astype(jnp.int32), 0, chunk))

    return pl.pallas_call(
        functools.partial(_mixer_kernel, t_off),
        grid=(n,),
        in_specs=[
            row(D_LRU), row(D_LRU),
            pl.BlockSpec((TR, D_LRU), lambda i: (i + t_off, 1)),
            pl.BlockSpec((TR, D_POOL), lambda i: (i + t_off, xp_col)),
            pl.BlockSpec((halo_rows, D_POOL),
                         lambda i: (jnp.maximum((i + t_off) * per - 1, 0), xp_col)),
            pl.BlockSpec((halo_rows, D_POOL),
                         lambda i: (jnp.minimum((i + t_off + 1) * per, last_halo), xp_col)),
            pl.BlockSpec((None, TR, D_POOL), lambda i: (jnp.minimum(i + t_off, CTX_TILES), 0, 0)),
            row(D_MODEL),
            mod_spec(2), mod_spec(3), mod_spec(4),
            pl.BlockSpec((None, len(POOL_WINDOWS), POOL_GROUP_DIM, POOL_GROUP_DIM),
                         lambda i: (layer, 0, 0, 0)),
            const((1, D_POOL)), const((1, D_POOL)),
            pl.BlockSpec((None, D_MODEL, D_MODEL), lambda i: (layer, 0, 0)),
            const((1, D_MODEL)), const((1, D_MODEL)),
            const((D_MODEL, 2 * LANES)), const((1, LANES)),
            const((TR, TR)),
        ],
        out_specs=[out_row(D_MODEL),
                   pl.BlockSpec((1, ROUTE_ROWS, TR), lambda i: (i, 0, 0)),
                   pl.BlockSpec((1, N_EXPERTS, LANES), lambda i: (i, 0, 0))],
        out_shape=[jax.ShapeDtypeStruct((n * TR, D_MODEL), F32),
                   jax.ShapeDtypeStruct((n, ROUTE_ROWS, TR), F32),
                   jax.ShapeDtypeStruct((n, N_EXPERTS, LANES), F32)],
        scratch_shapes=[pltpu.VMEM((D_MODEL, D_MODEL), BF16)],
        compiler_params=_cparams(("arbitrary",)),
        name="mixer_out",
    )(hf, hr, z, z, z, z, inv_cnt, x, mod, mod, mod, pool_w, pool_b, pool_scale, w_out,
      ln_g, ln_b, rw_pad, rb_pad, tri)


def _one_hot_rows(targets, index, weights=None):
    matches = [index == t for t in targets]
    hit = jnp.zeros(index.shape, F32)
    for m in reversed(matches):
        hit = jnp.where(m, 1.0, hit)
    if weights is None:
        return hit
    picked = jnp.zeros(index.shape, F32)
    for m, w in zip(reversed(matches), reversed(weights)):
        picked = jnp.where(m, w, picked)
    return hit, picked


def _for_each_group(block, len_ref, body):
    def step(e, carry):
        n = pl.multiple_of(len_ref[block, e], GROUP)

        @pl.when(n > 0)
        def _():
            body(e, n)

        return carry

    lax.fori_loop(0, N_EXPERTS, step, 0)


def _group(start, n):
    return pl.ds(pl.multiple_of(start, GROUP), n)


def _dispatch_kernel(src_ref, len_ref, dst_ref, tot_ref, tail_s_ref, tail_n_ref, na_ref,
                     lpos_ref, x_ref, sh2_ref, sc2_ref, xs_ref, stage_ref, zero_ref, sems):
    i = pl.program_id(0)
    n_blocks = pl.num_programs(0) - 1
    slot = i % 2

    @pl.when(i < n_blocks)
    def _():
        lpos = lpos_ref[...]
        targets = [lpos[k:k + 1, :] for k in range(TOP_K)]
        gates = [lpos[TOP_K + k:TOP_K + k + 1, :] for k in range(TOP_K)]
        h = _modulate(x_ref[...], sh2_ref, sc2_ref).astype(BF16)

        def permute(jb, carry):
            base = pl.multiple_of(jb * MB, MB)
            rows = (lax.broadcasted_iota(jnp.int32, (MB, TR), 0) + base).astype(F32)
            sel, picked = _one_hot_rows(targets, rows, gates)
            stage_ref[slot, pl.ds(base, MB), 0:D_MODEL] = jnp.dot(
                sel.astype(BF16), h, preferred_element_type=F32)
            gate = jnp.sum(picked, axis=1, keepdims=True)
            stage_ref[slot, pl.ds(base, MB), D_MODEL:D_AUG] = jnp.broadcast_to(gate, (MB, LANES))
            return carry

        lax.fori_loop(0, (tot_ref[i] + MB - 1) // MB, permute, 0)
        _for_each_group(i, len_ref, lambda e, n: pltpu.make_async_copy(
            stage_ref.at[slot, _group(src_ref[i, e], n)],
            xs_ref.at[_group(dst_ref[i, e], n)], sems.at[slot]).start())

    @pl.when(i > 0)
    def _():
        rows = _group(0, pl.multiple_of(tot_ref[i - 1], GROUP))
        pltpu.make_async_copy(stage_ref.at[1 - slot, rows], xs_ref.at[rows],
                              sems.at[1 - slot]).wait()

    @pl.when(i == n_blocks)
    def _():
        zero_ref[...] = jnp.zeros_like(zero_ref)
        fill = sems.at[2]

        def expert_tail(e, n):
            return pltpu.make_async_copy(zero_ref.at[_group(0, n)],
                                         xs_ref.at[_group(tail_s_ref[0, e], n)], fill)

        def whole_tile(t):
            return pltpu.make_async_copy(zero_ref, xs_ref.at[_group(t * TM, TM)], fill)

        n_tiles = xs_ref.shape[0] // TM
        _for_each_group(0, tail_n_ref, lambda e, n: expert_tail(e, n).start())
        lax.fori_loop(na_ref[0], n_tiles, lambda t, c: (whole_tile(t).start(), c)[1], 0)
        _for_each_group(0, tail_n_ref, lambda e, n: expert_tail(e, n).wait())
        lax.fori_loop(na_ref[0], n_tiles, lambda t, c: (whole_tile(t).wait(), c)[1], 0)


def _dispatch(t_off, plan, x, mod):
    n = x.shape[0] // TR
    last = n - 1
    grid_spec = pltpu.PrefetchScalarGridSpec(
        num_scalar_prefetch=7,
        grid=(n + 1,),
        in_specs=[
            pl.BlockSpec((None, SUBLANES, TR), lambda i, *_: (jnp.minimum(i, last), 0, 0)),
            pl.BlockSpec((TR, D_MODEL), lambda i, *_: (jnp.minimum(i, last), 0)),
            _mod_spec(3, t_off), _mod_spec(4, t_off),
        ],
        out_specs=pl.BlockSpec(memory_space=pl.ANY),
        scratch_shapes=[pltpu.VMEM((2, RB_STAGE, D_AUG), F32),
                        pltpu.VMEM((TM, D_AUG), F32),
                        pltpu.SemaphoreType.DMA((3,))],
    )
    return pl.pallas_call(
        _dispatch_kernel,
        grid_spec=grid_spec,
        out_shape=jax.ShapeDtypeStruct((plan["n_sorted"], D_AUG), F32),
        compiler_params=_cparams(("arbitrary",)),
        name="moe_dispatch",
    )(plan["src_row"], plan["n_rows"], plan["dst_row"], plan["block_rows"],
      plan["tail_start"][None], plan["tail_len"][None], plan["n_active"], plan["lpos_t"],
      x, mod, mod)


def _expert_kernel(layer, te_ref, na_ref, first_ref, nxt_ref, slot_ref,
                   x_ref, b1_ref, b2_ref, w1_hbm, w2_hbm, o_ref,
                   w1buf_ref, w2buf_ref, w1c_ref, w2c_ref, sems):
    i = pl.program_id(0)
    e = te_ref[i]
    slot = slot_ref[i]

    def fetch(expert, s):
        return (pltpu.make_async_copy(w1_hbm.at[layer, expert], w1buf_ref.at[s], sems.at[s, 0]),
                pltpu.make_async_copy(w2_hbm.at[layer, expert], w2buf_ref.at[s], sems.at[s, 1]))

    @pl.when(i == 0)
    def _():
        for copy in fetch(e, slot):
            copy.start()

    @pl.when(first_ref[i] == 1)
    def _():
        for copy in fetch(e, slot):
            copy.wait()
        w1c_ref[...] = w1buf_ref[slot].astype(BF16)
        w2c_ref[...] = w2buf_ref[slot].astype(BF16)

        @pl.when(nxt_ref[i] >= 0)
        def _():
            for copy in fetch(nxt_ref[i], 1 - slot):
                copy.start()

    @pl.when(i < na_ref[0])
    def _():
        gate = x_ref[:, D_MODEL:D_MODEL + 1]
        x = x_ref[:, 0:D_MODEL].astype(BF16)
        y = jnp.broadcast_to(b2_ref[...], (TM, D_MODEL))
        for c in range(D_EXPERT // FFN_CHUNK):
            g_cols = slice(c * FFN_CHUNK, (c + 1) * FFN_CHUNK)
            l_cols = slice(D_EXPERT + c * FFN_CHUNK, D_EXPERT + (c + 1) * FFN_CHUNK)
            glu = jnp.dot(x, w1c_ref[:, g_cols], preferred_element_type=F32) + b1_ref[:, g_cols]
            lin = jnp.dot(x, w1c_ref[:, l_cols], preferred_element_type=F32) + b1_ref[:, l_cols]
            glu = jnp.minimum(glu, SWIGLU_LIMIT)
            lin = jnp.clip(lin, -SWIGLU_LIMIT, SWIGLU_LIMIT)
            act = glu * _sigmoid(SWIGLU_ALPHA * glu) * (lin + 1.0)
            y = y + jnp.dot(act.astype(BF16), w2c_ref[g_cols, :], preferred_element_type=F32)
        o_ref[...] = gate * y

    @pl.when(i >= na_ref[0])
    def _():
        o_ref[...] = jnp.zeros_like(o_ref)


def _experts(layer, plan, xs, w1, b1, w2, b2):
    n_tiles = xs.shape[0] // TM
    act_tile = lambda i, te, na, *_: (jnp.maximum(jnp.minimum(i, na[0] - 1), 0), 0)
    expert = lambda i, te, *_: (layer, te[i], 0, 0)
    grid_spec = pltpu.PrefetchScalarGridSpec(
        num_scalar_prefetch=5,
        grid=(n_tiles,),
        in_specs=[
            pl.BlockSpec((TM, D_AUG), act_tile),
            pl.BlockSpec((None, None, 1, 2 * D_EXPERT), expert),
            pl.BlockSpec((None, None, 1, D_MODEL), expert),
            pl.BlockSpec(memory_space=pl.ANY),
            pl.BlockSpec(memory_space=pl.ANY),
        ],
        out_specs=pl.BlockSpec((TM, D_MODEL), lambda i, *_: (i, 0)),
        scratch_shapes=[pltpu.VMEM((2, D_MODEL, 2 * D_EXPERT), F32),
                        pltpu.VMEM((2, D_EXPERT, D_MODEL), F32),
                        pltpu.VMEM((D_MODEL, 2 * D_EXPERT), BF16),
                        pltpu.VMEM((D_EXPERT, D_MODEL), BF16),
                        pltpu.SemaphoreType.DMA((2, 2))],
    )
    return pl.pallas_call(
        functools.partial(_expert_kernel, layer),
        grid_spec=grid_spec,
        out_shape=jax.ShapeDtypeStruct((xs.shape[0], D_MODEL), F32),
        compiler_params=_cparams(("arbitrary",)),
        name="moe_experts",
    )(plan["tile_expert"], plan["n_active"], plan["tile_first"], plan["tile_next"],
      plan["tile_slot"], xs, b1.reshape(DEPTH, N_EXPERTS, 1, -1),
      b2.reshape(DEPTH, N_EXPERTS, 1, -1), w1, w2)


def _combine_kernel(last, src_ref, len_ref, dst_ref, tot_ref, lpos_ref, x_ref,
                    g2_ref, lg_ref, lb_ref, ys_ref, *rest):
    if last:
        o_ref, ybuf_ref, sems = rest
    else:
        sh_ref, sc_ref, w_ref, o_ref, z_ref, ybuf_ref, sems, wbf_ref = rest
        _cast_once(w_ref, wbf_ref)
    i = pl.program_id(0)
    n_blocks = pl.num_programs(0)
    slot = i % 2

    def start_block(block, s):
        _for_each_group(block, len_ref, lambda e, n: pltpu.make_async_copy(
            ys_ref.at[_group(dst_ref[block, e], n)],
            ybuf_ref.at[s, _group(src_ref[block, e], n)], sems.at[s]).start())

    @pl.when(i == 0)
    def _():
        ybuf_ref[...] = jnp.zeros_like(ybuf_ref)
        start_block(0, 0)

    @pl.when(i + 1 < n_blocks)
    def _():
        start_block(i + 1, 1 - slot)

    rows = _group(0, pl.multiple_of(tot_ref[i], GROUP))
    pltpu.make_async_copy(ys_ref.at[rows], ybuf_ref.at[slot, rows], sems.at[slot]).wait()
    lpos = lpos_ref[...]
    targets = [jnp.broadcast_to(lpos[:, k:k + 1], (TR, KB)) for k in range(TOP_K)]
    cols = lax.broadcasted_iota(jnp.int32, (TR, KB), 1)
    f = jnp.zeros((TR, D_MODEL), F32)
    for jb in range(RB // KB):
        sel = _one_hot_rows(targets, cols + jb * KB).astype(BF16)
        f = f + jnp.dot(sel, ybuf_ref[slot, jb * KB:(jb + 1) * KB, :].astype(BF16),
                        preferred_element_type=F32)
    x = ALPHA * x_ref[...] + _per_batch(f, g2_ref[0], jnp.multiply)
    out = _layer_norm(x) * lg_ref[...] + lb_ref[...]
    if last:
        o_ref[...] = jnp.transpose(out.reshape(TT, BATCH, D_MODEL), (1, 0, 2))
    else:
        o_ref[...] = out
        z_ref[...] = _in_projection(out, sh_ref, sc_ref, wbf_ref)


def _combine(t_off, plan, x, mod, ln_g, ln_b, ys, next_layer=None):
    n = x.shape[0] // TR
    last = next_layer is None
    row = lambda w: pl.BlockSpec((TR, w), lambda i, *_: (i, 0))
    const = lambda shape: pl.BlockSpec(shape, lambda i, *_: (0,) * len(shape))
    if last:
        extra_specs, extra_args, extra_scratch = [], (), []
        out_spec = pl.BlockSpec((BATCH, TT, D_MODEL), lambda i, *_: (0, i, 0))
        out_shape = jax.ShapeDtypeStruct((BATCH, n * TT, D_MODEL), F32)
    else:
        next_mod, w_in, next_l = next_layer
        extra_specs = [_mod_spec(0, t_off), _mod_spec(1, t_off),
                       pl.BlockSpec((None, D_MODEL, D_IN), lambda i, *_: (next_l, 0, 0))]
        extra_args = (next_mod, next_mod, w_in)
        extra_scratch = [pltpu.VMEM((D_MODEL, D_IN), BF16)]
        out_spec = [row(D_MODEL), row(D_IN)]
        out_shape = [jax.ShapeDtypeStruct((n * TR, D_MODEL), F32),
                     jax.ShapeDtypeStruct((n * TR, D_IN), F32)]
    grid_spec = pltpu.PrefetchScalarGridSpec(
        num_scalar_prefetch=4,
        grid=(n,),
        in_specs=[
            row(TOP_K), row(D_MODEL), _mod_spec(5, t_off),
            const((1, D_MODEL)), const((1, D_MODEL)),
            pl.BlockSpec(memory_space=pl.ANY),
        ] + extra_specs,
        out_specs=out_spec,
        scratch_shapes=[pltpu.VMEM((2, RB, D_MODEL), F32),
                        pltpu.SemaphoreType.DMA((2,))] + extra_scratch,
    )
    return pl.pallas_call(
        functools.partial(_combine_kernel, last),
        grid_spec=grid_spec,
        out_shape=out_shape,
        compiler_params=_cparams(("arbitrary",)),
        name="moe_combine",
    )(plan["src_row"], plan["n_rows"], plan["dst_row"], plan["block_rows"], plan["lpos"],
      x, mod, ln_g, ln_b, ys, *extra_args)

def _routing_plan(route, counts):
    i32 = jnp.int32
    nb = counts.shape[0]
    experts = jnp.arange(N_EXPERTS, dtype=i32)
    eid = route[:, LANE_EID:LANE_EID + TOP_K].astype(i32)
    rank = route[:, LANE_RANK:LANE_RANK + TOP_K].astype(i32)
    n = counts[:, :, 0].astype(i32)
    n = (n + GROUP - 1) // GROUP * GROUP
    src_row = jnp.cumsum(n, axis=1) - n
    seg = jnp.sum(n, axis=0)
    tiles_e = (seg + TM - 1) // TM
    tile_end = jnp.cumsum(tiles_e)
    e_start = (tile_end - tiles_e) * TM
    n_active = tile_end[-1]
    dst_row = e_start[None, :] + jnp.cumsum(n, axis=0) - n

    onehot = eid[..., None] == experts
    lpos = jnp.sum(jnp.where(onehot, src_row[:, None, None, :], 0), axis=-1) + rank
    lpos_t = jnp.concatenate([lpos.astype(F32), route[:, LANE_P:LANE_P + TOP_K]], axis=1)

    n_tiles = (nb * RB + TM - 1) // TM + N_EXPERTS
    t = jnp.minimum(jnp.arange(n_tiles, dtype=i32), n_active - 1)
    tile_expert = jnp.sum((t[:, None] >= tile_end[None, :]).astype(i32), axis=1)
    tile_expert = jnp.minimum(tile_expert, N_EXPERTS - 1)
    tile_first = jnp.concatenate(
        [jnp.ones((1,), i32), (tile_expert[1:] != tile_expert[:-1]).astype(i32)])
    has_rows = tiles_e > 0
    later = jnp.where(has_rows[None, :] & (experts[None, :] > experts[:, None]),
                      experts[None, :], N_EXPERTS)
    next_e = jnp.min(later, axis=1)
    next_e = jnp.where(next_e == N_EXPERTS, -1, next_e)
    group = jnp.cumsum(has_rows.astype(i32)) - 1
    pick = tile_expert[:, None] == experts[None, :]
    tile_next = jnp.sum(jnp.where(pick, next_e[None, :], 0), axis=1)
    tile_slot = jnp.sum(jnp.where(pick, group[None, :], 0), axis=1) % 2
    return dict(lpos=lpos.transpose(0, 2, 1).reshape(nb * TR, TOP_K), lpos_t=lpos_t, src_row=src_row, n_rows=n,
                dst_row=dst_row, block_rows=jnp.sum(n, axis=1), tail_start=e_start + seg,
                tail_len=tile_end * TM - e_start - seg,
                tile_expert=tile_expert, n_active=n_active.reshape(1), tile_first=tile_first,
                tile_next=tile_next, tile_slot=tile_slot,
                n_sorted=n_tiles * TM)


def _grid_sincos_parts():
    quarter = D_MODEL // 4
    omega = 1.0 / (10000.0 ** (jnp.arange(quarter, dtype=F32) / quarter))

    def emb1d(n):
        ang = jnp.arange(n, dtype=F32)[:, None] * omega[None, :]
        return jnp.concatenate([jnp.sin(ang), jnp.cos(ang)], axis=-1)

    er = emb1d(SEQ // GRID_W)
    ec = jnp.repeat(emb1d(GRID_W), BATCH, axis=0)
    return er.reshape(SEQ // GRID_W, 1, D_MODEL // 2), ec


def _gate_blocks(w):
    per = GATE_BLK // LRU_HEAD_DIM
    w = w.reshape(2, D_LRU // GATE_BLK, per, LRU_HEAD_DIM, LRU_HEAD_DIM)
    eye = jnp.eye(per, dtype=w.dtype)
    blk = jnp.einsum('dkpij,pq->dkpiqj', w, eye)
    return blk.reshape(2, D_LRU // GATE_BLK, GATE_BLK, GATE_BLK)


def kernel(x, c, ctx, c_ctx, w_mod, b_mod, w_in, conv_w, conv_b, gate_a_w, gate_a_b,
           gate_x_w, gate_x_b, lru_lambda, pool_w, pool_b, pool_scale, w_out, ln1_g, ln1_b,
           router_w, router_b, exp_w1, exp_b1, exp_w2, exp_b2, ln2_g, ln2_b):
    cvec = jnp.concatenate([c, c_ctx[None], jnp.zeros((2 * SUBLANES - BATCH - 1, D_MODEL), F32)])
    mod_all = _modulation(cvec, w_mod, b_mod)
    mods = jnp.stack([jnp.broadcast_to(mod_all[:, BATCH:BATCH + 1], (DEPTH, BATCH, 6 * D_MODEL)),
                      mod_all[:, :BATCH]], axis=1)
    tri = jnp.triu(jnp.ones((TR, TR), F32), 1).astype(BF16)
    inv_cnt = _pool_inverse_counts()

    er, ec = _grid_sincos_parts()
    xs, z = _entry(ctx, x, er, ec, mods[0], w_in)
    for l in range(DEPTH):
        last = l == DEPTH - 1
        t_off = CTX_TILES if last else 0
        mod = mods[l]
        gate_w = jnp.stack([_gate_blocks(gate_a_w[l]), _gate_blocks(gate_x_w[l])],
                           axis=1).astype(BF16)
        gate_b = jnp.stack([gate_a_b[l], gate_x_b[l]], axis=1)
        hf, hr = _scan(z, conv_w[l], conv_b[l][None], gate_w, gate_b, lru_lambda[l])
        rw_pad = jnp.zeros((D_MODEL, LANES), F32).at[:, :N_EXPERTS].set(router_w[l])
        rw_hi = rw_pad.astype(BF16)
        rw_pad = jnp.concatenate([rw_hi, (rw_pad - rw_hi.astype(F32)).astype(BF16)], axis=1)
        rb_pad = jnp.full((1, LANES), -1e30, F32).at[0, :N_EXPERTS].set(router_b[l])
        xs, route, counts = _mixer(
            l, t_off, hf, hr, z, inv_cnt, xs, mod, pool_w, pool_b[l][None], pool_scale[l][None], w_out,
            ln1_g[l][None], ln1_b[l][None], rw_pad, rb_pad, tri)
        plan = _routing_plan(route, counts)
        x_sorted = _dispatch(t_off, plan, xs, mod)
        y_sorted = _experts(l, plan, x_sorted, exp_w1, exp_b1, exp_w2, exp_b2)
        if last:
            return _combine(t_off, plan, xs, mod, ln2_g[l][None], ln2_b[l][None], y_sorted)
        xs, z = _combine(t_off, plan, xs, mod, ln2_g[l][None], ln2_b[l][None], y_sorted,
                         (mods[l + 1], w_in, l + 1))
```

```python
import functools

import jax
import jax.numpy as jnp
from jax import lax
from jax.experimental import pallas as pl
from jax.experimental.pallas import tpu as pltpu

D_MODEL = 1024
BATCH = 8
SEQ = 2048
DEPTH = 2
GRID_W = 64
CTX_LEN = 256
D_LRU = 512
N_LRU_HEADS = 8
LRU_HEAD_DIM = D_LRU // N_LRU_HEADS
LRU_C = 8.0
D_POOL = 512
POOL_WINDOWS = (2, 4, 8, 16)
POOL_GROUP_DIM = D_POOL // len(POOL_WINDOWS)
D_IN = 2 * D_LRU + D_POOL
N_EXPERTS = 32
TOP_K = 4
D_EXPERT = D_MODEL
SWIGLU_LIMIT = 7.0
SWIGLU_ALPHA = 1.702
LN_EPS = 1e-5
ALPHA = (2.0 * DEPTH) ** 0.25

F32 = jnp.float32
BF16 = jnp.bfloat16

SUBLANES = 8
LANES = 128
TT = GRID_W
TR = TT * BATCH
T_ALL = CTX_LEN + SEQ
R_ALL = T_ALL * BATCH
N_TILES = T_ALL // TT
CTX_TILES = CTX_LEN // TT
GATE_BLK = 256
POOL_HALO = max(POOL_WINDOWS) // 2
TM = 512
FFN_CHUNK = 1024
GROUP = SUBLANES
RB = TOP_K * TR + N_EXPERTS * GROUP
KB = 256
MB = 512
RB_STAGE = (RB + MB - 1) // MB * MB
D_AUG = D_MODEL + LANES
LANE_P, LANE_EID, LANE_RANK = 0, 4, 8
ROUTE_ROWS = 16
VMEM_LIMIT = 56 * 1024 * 1024


def _cparams(sem):
    return pltpu.CompilerParams(dimension_semantics=sem, vmem_limit_bytes=VMEM_LIMIT)


def _sigmoid(x):
    return 0.5 * (1.0 + jnp.tanh(0.5 * x))


def _layer_norm(x):
    mu = jnp.mean(x, axis=-1, keepdims=True)
    xc = x - mu
    var = jnp.mean(xc * xc, axis=-1, keepdims=True)
    return xc * lax.rsqrt(var + LN_EPS)


def _per_batch(x, v, op):
    r, d = x.shape
    return op(x.reshape(r // BATCH, BATCH, d), v[None]).reshape(r, d)


def _mod_kernel(c_ref, w_ref, b_ref, o_ref):
    c = c_ref[...]
    s = c * _sigmoid(c)
    w = w_ref[...]
    s_hi, w_hi = s.astype(BF16), w.astype(BF16)
    s_lo = (s - s_hi.astype(F32)).astype(BF16)
    w_lo = (w - w_hi.astype(F32)).astype(BF16)
    o_ref[...] = (jnp.dot(s_hi, w_hi, preferred_element_type=F32)
                  + jnp.dot(s_lo, w_hi, preferred_element_type=F32)
                  + jnp.dot(s_hi, w_lo, preferred_element_type=F32)) + b_ref[...]


def _modulation(cvec, w_mod, b_mod):
    tn = 512
    return pl.pallas_call(
        _mod_kernel,
        grid=(DEPTH, 6 * D_MODEL // tn),
        in_specs=[
            pl.BlockSpec((2 * SUBLANES, D_MODEL), lambda l, j: (0, 0)),
            pl.BlockSpec((None, D_MODEL, tn), lambda l, j: (l, 0, j)),
            pl.BlockSpec((None, 1, tn), lambda l, j: (l, 0, j)),
        ],
        out_specs=pl.BlockSpec((None, 2 * SUBLANES, tn), lambda l, j: (l, 0, j)),
        out_shape=jax.ShapeDtypeStruct((DEPTH, 2 * SUBLANES, 6 * D_MODEL), F32),
        compiler_params=_cparams(("arbitrary", "arbitrary")),
        name="modulation",
    )(cvec, w_mod, b_mod.reshape(DEPTH, 1, 6 * D_MODEL))


def _modulate(x, sh_ref, sc_ref):
    h = _per_batch(x, 1.0 + sc_ref[0], jnp.multiply)
    return _per_batch(h, sh_ref[0], jnp.add)


def _in_projection(x, sh_ref, sc_ref, w_ref):
    return jnp.dot(_modulate(x, sh_ref, sc_ref).astype(BF16), w_ref[...],
                   preferred_element_type=F32)


def _cast_once(w_ref, wbf_ref):
    @pl.when(pl.program_id(0) == 0)
    def _():
        wbf_ref[...] = w_ref[...].astype(BF16)


def _entry_kernel(ctx_ref, x_ref, er_ref, ec_ref, sh_ref, sc_ref, w_ref, o_ref, z_ref, wbf_ref):
    i = pl.program_id(0)
    _cast_once(w_ref, wbf_ref)

    def time_major(src_ref):
        return jnp.transpose(src_ref[...], (1, 0, 2)).reshape(TR, D_MODEL)

    @pl.when(i < CTX_TILES)
    def _():
        o_ref[...] = _layer_norm(time_major(ctx_ref))

    @pl.when(i >= CTX_TILES)
    def _():
        pos = jnp.concatenate(
            [jnp.broadcast_to(er_ref[0], (TR, D_MODEL // 2)), ec_ref[...]], axis=1)
        o_ref[...] = _layer_norm(time_major(x_ref) + pos)

    z_ref[...] = _in_projection(o_ref[...], sh_ref, sc_ref, wbf_ref)


def _mod_spec(chunk, t_off=0):
    return pl.BlockSpec((1, BATCH, D_MODEL),
                        lambda i, *_: ((i + t_off >= CTX_TILES).astype(jnp.int32), 0, chunk))


def _entry(ctx, x, er, ec, mod, w_in):
    return pl.pallas_call(
        _entry_kernel,
        grid=(N_TILES,),
        in_specs=[
            pl.BlockSpec((BATCH, TT, D_MODEL), lambda i: (0, jnp.minimum(i, CTX_TILES - 1), 0)),
            pl.BlockSpec((BATCH, TT, D_MODEL), lambda i: (0, jnp.maximum(i - CTX_TILES, 0), 0)),
            pl.BlockSpec((1, 1, D_MODEL // 2), lambda i: (jnp.maximum(i - CTX_TILES, 0), 0, 0)),
            pl.BlockSpec((TR, D_MODEL // 2), lambda i: (0, 0)),
            _mod_spec(0), _mod_spec(1),
            pl.BlockSpec((None, D_MODEL, D_IN), lambda i: (0, 0, 0)),
        ],
        out_specs=[pl.BlockSpec((TR, D_MODEL), lambda i: (i, 0)),
                   pl.BlockSpec((TR, D_IN), lambda i: (i, 0))],
        out_shape=[jax.ShapeDtypeStruct((R_ALL, D_MODEL), F32),
                   jax.ShapeDtypeStruct((R_ALL, D_IN), F32)],
        scratch_shapes=[pltpu.VMEM((D_MODEL, D_IN), BF16)],
        compiler_params=_cparams(("arbitrary",)),
        name="entry_ln",
    )(ctx, x, er, ec, mod, mod, w_in)


def _rev_tile(i):
    return jnp.where(i < CTX_TILES, CTX_TILES - 1 - i, N_TILES - 1 + CTX_TILES - i)


def _block_diag_dot(u_bf, w_ref, d, g):
    halves = [jnp.dot(u_bf[:, k * GATE_BLK:(k + 1) * GATE_BLK], w_ref[d, g, k],
                      preferred_element_type=F32) for k in range(D_LRU // GATE_BLK)]
    return jnp.concatenate(halves, axis=1)


def _lru_coeffs(tile, z_ref, zp_ref, zn_ref, cw_ref, cb_ref, gw_ref, gb_ref, lam_ref, d):
    x = z_ref[...]
    seg_first = (tile == 0) | (tile == CTX_TILES)
    seg_last = (tile == CTX_TILES - 1) | (tile == N_TILES - 1)
    prev = jnp.where(seg_first, 0.0, zp_ref[...])
    nxt = jnp.where(seg_last, 0.0, zn_ref[...])
    xm2 = jnp.concatenate([prev, x[:-2 * BATCH]], axis=0)
    xm1 = jnp.concatenate([prev[BATCH:], x[:-BATCH]], axis=0)
    xp1 = jnp.concatenate([x[BATCH:], nxt], axis=0)
    u = (cb_ref[...] + xm2 * cw_ref[0:1] + xm1 * cw_ref[1:2]
         + x * cw_ref[2:3] + xp1 * cw_ref[3:4])
    u_bf = u.astype(BF16)
    r = _sigmoid(_block_diag_dot(u_bf, gw_ref, d, 0) + gb_ref[d, 0:1])
    ig = _sigmoid(_block_diag_dot(u_bf, gw_ref, d, 1) + gb_ref[d, 1:2])
    nl = -lam_ref[d:d + 1]
    softplus = jnp.maximum(nl, 0.0) + jnp.log(1.0 + jnp.exp(-jnp.abs(nl)))
    log_a = (-LRU_C) * r * softplus
    a = jnp.exp(log_a)
    gap = 1.0 - a * a
    mult = jnp.where(gap > 0.0, gap * lax.rsqrt(gap), 0.0)
    return a, mult * (ig * u)


def _scan_kernel(zf_ref, zfp_ref, zfn_ref, zr_ref, zrp_ref, zrn_ref,
                 cw_ref, cb_ref, gw_ref, gb_ref, lam_ref,
                 hf_ref, hr_ref, state_ref):
    i = pl.program_id(0)

    @pl.when(i == 0)
    def _():
        state_ref[...] = jnp.zeros_like(state_ref)

    af, bf = _lru_coeffs(i, zf_ref, zfp_ref, zfn_ref, cw_ref, cb_ref, gw_ref, gb_ref,
                         lam_ref, 0)
    ar, br = _lru_coeffs(_rev_tile(i), zr_ref, zrp_ref, zrn_ref, cw_ref, cb_ref, gw_ref,
                         gb_ref, lam_ref, 1)
    hf = state_ref[0]
    hr = state_ref[1]
    for s in range(TT):
        f = slice(s * BATCH, (s + 1) * BATCH)
        hf = af[f] * hf + bf[f]
        hf_ref[f, :] = hf
        b = slice((TT - 1 - s) * BATCH, (TT - s) * BATCH)
        hr = ar[b] * hr + br[b]
        hr_ref[b, :] = hr
    state_ref[0] = hf
    state_ref[1] = hr


def _scan(z, conv_w, conv_b, gate_w, gate_b, lam):
    prev_rows = 2 * BATCH
    tiles_per_prev = TR // prev_rows
    tiles_per_next = TR // BATCH
    last_next = R_ALL // BATCH - 1

    def cur(f):
        return pl.BlockSpec((TR, D_LRU), lambda i: (f(i), 0))

    def prev(f):
        return pl.BlockSpec((prev_rows, D_LRU),
                            lambda i: (jnp.maximum(f(i) * tiles_per_prev - 1, 0), 0))

    def nxt(f):
        return pl.BlockSpec((BATCH, D_LRU),
                            lambda i: (jnp.minimum((f(i) + 1) * tiles_per_next, last_next), 0))

    fwd = lambda i: i
    const = lambda shape: pl.BlockSpec(shape, lambda i: (0,) * len(shape))
    return pl.pallas_call(
        _scan_kernel,
        grid=(N_TILES,),
        in_specs=[cur(fwd), prev(fwd), nxt(fwd), cur(_rev_tile), prev(_rev_tile), nxt(_rev_tile),
                  const((4, D_LRU)), const((1, D_LRU)),
                  const((2, 2, D_LRU // GATE_BLK, GATE_BLK, GATE_BLK)),
                  const((2, 2, D_LRU)), const((2, D_LRU))],
        out_specs=[pl.BlockSpec((TR, D_LRU), lambda i: (i, 0)),
                   pl.BlockSpec((TR, D_LRU), lambda i: (_rev_tile(i), 0))],
        out_shape=[jax.ShapeDtypeStruct((R_ALL, D_LRU), F32)] * 2,
        scratch_shapes=[pltpu.VMEM((2, BATCH, D_LRU), F32)],
        compiler_params=_cparams(("arbitrary",)),
        name="lru_scan",
    )(z, z, z, z, z, z, conv_w, conv_b, gate_w, gate_b, lam)


def _gelu_tanh(y):
    c = 0.7978845608028654
    return 0.5 * y * (1.0 + jnp.tanh(c * (y + 0.044715 * (y * y * y))))


def _pool_inverse_counts():
    t = jnp.arange(TT)[None, :, None]
    kind = jnp.arange(CTX_TILES + 1)[:, None, None]
    lo = jnp.where(kind < CTX_TILES, -TT * kind, 0)
    hi = jnp.where(kind < CTX_TILES, CTX_LEN - TT * kind, TT)
    half = (jnp.asarray(POOL_WINDOWS) // 2)[None, None, :]
    cnt = jnp.minimum(t + half, hi) - jnp.maximum(t - half, lo)
    inv = 1.0 / cnt.astype(F32)
    inv = jnp.repeat(jnp.repeat(inv, POOL_GROUP_DIM, axis=2), BATCH, axis=1)
    return inv


def _pool_groups(tile, xp_ref, xpp_ref, xpn_ref, inv_ref):
    in_ctx = tile < CTX_TILES
    prev_ok = in_ctx & (tile > 0)
    next_ok = in_ctx & (tile < CTX_TILES - 1)
    x = xp_ref[...]
    p = jnp.concatenate([jnp.where(prev_ok, xpp_ref[...], 0.0), x,
                         jnp.where(next_ok, xpn_ref[...], 0.0)], axis=0)
    p = p.reshape(TT + 2 * POOL_HALO, BATCH, D_POOL)
    inv = inv_ref[...].reshape(TT, BATCH, D_POOL)
    outs = []
    for g, win in enumerate(POOL_WINDOWS):
        half = win // 2
        acc = p[:, :, g * POOL_GROUP_DIM:(g + 1) * POOL_GROUP_DIM]
        width = 1
        while width < win:
            acc = acc[:acc.shape[0] - width] + acc[width:]
            width *= 2
        start = POOL_HALO - half
        wsum = acc[start:start + TT]
        lanes = slice(g * POOL_GROUP_DIM, (g + 1) * POOL_GROUP_DIM)
        centre = p[POOL_HALO:POOL_HALO + TT, :, lanes]
        outs.append((wsum * inv[:, :, lanes] - centre).reshape(TR, POOL_GROUP_DIM))
    return outs


def _mixer_kernel(t_off, hf_ref, hr_ref, y_ref, xp_ref, xpp_ref, xpn_ref, inv_ref, x_ref,
                  g1_ref, sh2_ref, sc2_ref, pw_ref, pb_ref, ps_ref, wo_ref,
                  lg_ref, lb_ref, rw_ref, rb_ref, tri_ref,
                  xo_ref, route_ref, cnt_ref, wobf_ref):
    i = pl.program_id(0)
    tile = i + t_off
    _cast_once(wo_ref, wobf_ref)

    lru = (hf_ref[...] + hr_ref[...]) * _gelu_tanh(y_ref[...])
    diffs = _pool_groups(tile, xp_ref, xpp_ref, xpn_ref, inv_ref)
    pooled = jnp.concatenate(
        [jnp.dot(d.astype(BF16), pw_ref[g].astype(BF16), preferred_element_type=F32)
         for g, d in enumerate(diffs)], axis=1)
    pooled = (pooled + pb_ref[...]) * ps_ref[...]
    mix = (jnp.dot(lru.astype(BF16), wobf_ref[0:D_LRU, :], preferred_element_type=F32)
           + jnp.dot(pooled.astype(BF16), wobf_ref[D_LRU:, :], preferred_element_type=F32))
    x = ALPHA * x_ref[...] + _per_batch(mix, g1_ref[0], jnp.multiply)
    x = _layer_norm(x) * lg_ref[...] + lb_ref[...]
    xo_ref[...] = x
    h2 = _modulate(x, sh2_ref, sc2_ref)

    h_hi = h2.astype(BF16)
    h_lo = (h2 - h_hi.astype(F32)).astype(BF16)
    by_hi = jnp.dot(h_hi, rw_ref[...], preferred_element_type=F32)
    logits = (by_hi[:, :LANES] + by_hi[:, LANES:]
              + jnp.dot(h_lo, rw_ref[:, 0:LANES], preferred_element_type=F32)) + rb_ref[...]
    work = logits.T[0:N_EXPERTS, :]
    expert = lax.broadcasted_iota(jnp.int32, (N_EXPERTS, TR), 0).astype(F32)
    vals, idxs, sels = [], [], []
    for _ in range(TOP_K):
        m = jnp.max(work, axis=0, keepdims=True)
        idx = jnp.min(jnp.where(work == m, expert, float(N_EXPERTS)), axis=0, keepdims=True)
        sel = expert == idx
        vals.append(m)
        idxs.append(idx)
        sels.append(sel)
        work = jnp.where(sel, -jnp.inf, work)
    exps = [jnp.exp(v - vals[0]) for v in vals]
    denom = exps[0] + exps[1] + exps[2] + exps[3]
    chosen = jnp.zeros((N_EXPERTS, TR), F32)
    for sel in sels:
        chosen = chosen + sel.astype(F32)
    before = jnp.dot(chosen.astype(BF16), tri_ref[...], preferred_element_type=F32)
    ranks = [jnp.sum(jnp.where(sel, before, 0.0), axis=0, keepdims=True) for sel in sels]
    route_ref[0] = jnp.concatenate(
        [e / denom for e in exps] + idxs + ranks
        + [jnp.zeros((ROUTE_ROWS - 3 * TOP_K, TR), F32)], axis=0)
    cnt_ref[0] = jnp.broadcast_to(jnp.sum(chosen, axis=1, keepdims=True), (N_EXPERTS, LANES))


def _mixer(layer, t_off, hf, hr, z, inv_cnt, x, mod, pool_w, pool_b, pool_scale, w_out,
           ln_g, ln_b, rw_pad, rb_pad, tri):
    n = N_TILES - t_off
    halo_rows = POOL_HALO * BATCH
    per = TR // halo_rows
    last_halo = R_ALL // halo_rows - 1
    xp_col = 2 * D_LRU // D_POOL
    row = lambda w: pl.BlockSpec((TR, w), lambda i: (i + t_off, 0))
    out_row = lambda w: pl.BlockSpec((TR, w), lambda i: (i, 0))
    const = lambda shape: pl.BlockSpec(shape, lambda i: (0,) * len(shape))

    def mod_spec(chunk):
        return pl.BlockSpec((1, BATCH, D_MODEL),
                            lambda i: ((i + t_off >= CTX_TILES).astype(jnp.int32), 0, chunk))

    return pl.pallas_call(
        functools.partial(_mixer_kernel, t_off),
        grid=(n,),
        in_specs=[
            row(D_LRU), row(D_LRU),
            pl.BlockSpec((TR, D_LRU), lambda i: (i + t_off, 1)),
            pl.BlockSpec((TR, D_POOL), lambda i: (i + t_off, xp_col)),
            pl.BlockSpec((halo_rows, D_POOL),
                         lambda i: (jnp.maximum((i + t_off) * per - 1, 0), xp_col)),
            pl.BlockSpec((halo_rows, D_POOL),
                         lambda i: (jnp.minimum((i + t_off + 1) * per, last_halo), xp_col)),
            pl.BlockSpec((None, TR, D_POOL), lambda i: (jnp.minimum(i + t_off, CTX_TILES), 0, 0)),
            row(D_MODEL),
            mod_spec(2), mod_spec(3), mod_spec(4),
            pl.BlockSpec((None, len(POOL_WINDOWS), POOL_GROUP_DIM, POOL_GROUP_DIM),
                         lambda i: (layer, 0, 0, 0)),
            const((1, D_POOL)), const((1, D_POOL)),
            pl.BlockSpec((None, D_MODEL, D_MODEL), lambda i: (layer, 0, 0)),
            const((1, D_MODEL)), const((1, D_MODEL)),
            const((D_MODEL, 2 * LANES)), const((1, LANES)),
            const((TR, TR)),
        ],
        out_specs=[out_row(D_MODEL),
                   pl.BlockSpec((1, ROUTE_ROWS, TR), lambda i: (i, 0, 0)),
                   pl.BlockSpec((1, N_EXPERTS, LANES), lambda i: (i, 0, 0))],
        out_shape=[jax.ShapeDtypeStruct((n * TR, D_MODEL), F32),
                   jax.ShapeDtypeStruct((n, ROUTE_ROWS, TR), F32),
                   jax.ShapeDtypeStruct((n, N_EXPERTS, LANES), F32)],
        scratch_shapes=[pltpu.VMEM((D_MODEL, D_MODEL), BF16)],
        compiler_params=_cparams(("arbitrary",)),
        name="mixer_out",
    )(hf, hr, z, z, z, z, inv_cnt, x, mod, mod, mod, pool_w, pool_b, pool_scale, w_out,
      ln_g, ln_b, rw_pad, rb_pad, tri)


def _one_hot_rows(targets, index, weights=None):
    matches = [index == t for t in targets]
    hit = jnp.zeros(index.shape, F32)
    for m in reversed(matches):
        hit = jnp.where(m, 1.0, hit)
    if weights is None:
        return hit
    picked = jnp.zeros(index.shape, F32)
    for m, w in zip(reversed(matches), reversed(weights)):
        picked = jnp.where(m, w, picked)
    return hit, picked


def _for_each_group(block, len_ref, body):
    def step(e, carry):
        n = pl.multiple_of(len_ref[block, e], GROUP)

        @pl.when(n > 0)
        def _():
            body(e, n)

        return carry

    lax.fori_loop(0, N_EXPERTS, step, 0)


def _group(start, n):
    return pl.ds(pl.multiple_of(start, GROUP), n)


def _dispatch_kernel(src_ref, len_ref, dst_ref, tot_ref, tail_s_ref, tail_n_ref, na_ref,
                     lpos_ref, x_ref, sh2_ref, sc2_ref, xs_ref, stage_ref, zero_ref, sems):
    i = pl.program_id(0)
    n_blocks = pl.num_programs(0) - 1
    slot = i % 2

    @pl.when(i < n_blocks)
    def _():
        lpos = lpos_ref[...]
        targets = [lpos[k:k + 1, :] for k in range(TOP_K)]
        gates = [lpos[TOP_K + k:TOP_K + k + 1, :] for k in range(TOP_K)]
        h = _modulate(x_ref[...], sh2_ref, sc2_ref).astype(BF16)

        def permute(jb, carry):
            base = pl.multiple_of(jb * MB, MB)
            rows = (lax.broadcasted_iota(jnp.int32, (MB, TR), 0) + base).astype(F32)
            sel, picked = _one_hot_rows(targets, rows, gates)
            stage_ref[slot, pl.ds(base, MB), 0:D_MODEL] = jnp.dot(
                sel.astype(BF16), h, preferred_element_type=F32)
            gate = jnp.sum(picked, axis=1, keepdims=True)
            stage_ref[slot, pl.ds(base, MB), D_MODEL:D_AUG] = jnp.broadcast_to(gate, (MB, LANES))
            return carry

        lax.fori_loop(0, (tot_ref[i] + MB - 1) // MB, permute, 0)
        _for_each_group(i, len_ref, lambda e, n: pltpu.make_async_copy(
            stage_ref.at[slot, _group(src_ref[i, e], n)],
            xs_ref.at[_group(dst_ref[i, e], n)], sems.at[slot]).start())

    @pl.when(i > 0)
    def _():
        rows = _group(0, pl.multiple_of(tot_ref[i - 1], GROUP))
        pltpu.make_async_copy(stage_ref.at[1 - slot, rows], xs_ref.at[rows],
                              sems.at[1 - slot]).wait()

    @pl.when(i == n_blocks)
    def _():
        zero_ref[...] = jnp.zeros_like(zero_ref)
        fill = sems.at[2]

        def expert_tail(e, n):
            return pltpu.make_async_copy(zero_ref.at[_group(0, n)],
                                         xs_ref.at[_group(tail_s_ref[0, e], n)], fill)

        def whole_tile(t):
            return pltpu.make_async_copy(zero_ref, xs_ref.at[_group(t * TM, TM)], fill)

        n_tiles = xs_ref.shape[0] // TM
        _for_each_group(0, tail_n_ref, lambda e, n: expert_tail(e, n).start())
        lax.fori_loop(na_ref[0], n_tiles, lambda t, c: (whole_tile(t).start(), c)[1], 0)
        _for_each_group(0, tail_n_ref, lambda e, n: expert_tail(e, n).wait())
        lax.fori_loop(na_ref[0], n_tiles, lambda t, c: (whole_tile(t).wait(), c)[1], 0)


def _dispatch(t_off, plan, x, mod):
    n = x.shape[0] // TR
    last = n - 1
    grid_spec = pltpu.PrefetchScalarGridSpec(
        num_scalar_prefetch=7,
        grid=(n + 1,),
        in_specs=[
            pl.BlockSpec((None, SUBLANES, TR), lambda i, *_: (jnp.minimum(i, last), 0, 0)),
            pl.BlockSpec((TR, D_MODEL), lambda i, *_: (jnp.minimum(i, last), 0)),
            _mod_spec(3, t_off), _mod_spec(4, t_off),
        ],
        out_specs=pl.BlockSpec(memory_space=pl.ANY),
        scratch_shapes=[pltpu.VMEM((2, RB_STAGE, D_AUG), F32),
                        pltpu.VMEM((TM, D_AUG), F32),
                        pltpu.SemaphoreType.DMA((3,))],
    )
    return pl.pallas_call(
        _dispatch_kernel,
        grid_spec=grid_spec,
        out_shape=jax.ShapeDtypeStruct((plan["n_sorted"], D_AUG), F32),
        compiler_params=_cparams(("arbitrary",)),
        name="moe_dispatch",
    )(plan["src_row"], plan["n_rows"], plan["dst_row"], plan["block_rows"],
      plan["tail_start"][None], plan["tail_len"][None], plan["n_active"], plan["lpos_t"],
      x, mod, mod)


def _expert_kernel(layer, te_ref, na_ref, first_ref, nxt_ref, slot_ref,
                   x_ref, b1_ref, b2_ref, w1_hbm, w2_hbm, o_ref,
                   w1buf_ref, w2buf_ref, w1c_ref, w2c_ref, sems):
    i = pl.program_id(0)
    e = te_ref[i]
    slot = slot_ref[i]

    def fetch(expert, s):
        return (pltpu.make_async_copy(w1_hbm.at[layer, expert], w1buf_ref.at[s], sems.at[s, 0]),
                pltpu.make_async_copy(w2_hbm.at[layer, expert], w2buf_ref.at[s], sems.at[s, 1]))

    @pl.when(i == 0)
    def _():
        for copy in fetch(e, slot):
            copy.start()

    @pl.when(first_ref[i] == 1)
    def _():
        for copy in fetch(e, slot):
            copy.wait()
        w1c_ref[...] = w1buf_ref[slot].astype(BF16)
        w2c_ref[...] = w2buf_ref[slot].astype(BF16)

        @pl.when(nxt_ref[i] >= 0)
        def _():
            for copy in fetch(nxt_ref[i], 1 - slot):
                copy.start()

    @pl.when(i < na_ref[0])
    def _():
        gate = x_ref[:, D_MODEL:D_MODEL + 1]
        x = x_ref[:, 0:D_MODEL].astype(BF16)
        y = jnp.broadcast_to(b2_ref[...], (TM, D_MODEL))
        for c in range(D_EXPERT // FFN_CHUNK):
            g_cols = slice(c * FFN_CHUNK, (c + 1) * FFN_CHUNK)
            l_cols = slice(D_EXPERT + c * FFN_CHUNK, D_EXPERT + (c + 1) * FFN_CHUNK)
            glu = jnp.dot(x, w1c_ref[:, g_cols], preferred_element_type=F32) + b1_ref[:, g_cols]
            lin = jnp.dot(x, w1c_ref[:, l_cols], preferred_element_type=F32) + b1_ref[:, l_cols]
            glu = jnp.minimum(glu, SWIGLU_LIMIT)
            lin = jnp.clip(lin, -SWIGLU_LIMIT, SWIGLU_LIMIT)
            act = glu * _sigmoid(SWIGLU_ALPHA * glu) * (lin + 1.0)
            y = y + jnp.dot(act.astype(BF16), w2c_ref[g_cols, :], preferred_element_type=F32)
        o_ref[...] = gate * y

    @pl.when(i >= na_ref[0])
    def _():
        o_ref[...] = jnp.zeros_like(o_ref)


def _experts(layer, plan, xs, w1, b1, w2, b2):
    n_tiles = xs.shape[0] // TM
    act_tile = lambda i, te, na, *_: (jnp.maximum(jnp.minimum(i, na[0] - 1), 0), 0)
    expert = lambda i, te, *_: (layer, te[i], 0, 0)
    grid_spec = pltpu.PrefetchScalarGridSpec(
        num_scalar_prefetch=5,
        grid=(n_tiles,),
        in_specs=[
            pl.BlockSpec((TM, D_AUG), act_tile),
            pl.BlockSpec((None, None, 1, 2 * D_EXPERT), expert),
            pl.BlockSpec((None, None, 1, D_MODEL), expert),
            pl.BlockSpec(memory_space=pl.ANY),
            pl.BlockSpec(memory_space=pl.ANY),
        ],
        out_specs=pl.BlockSpec((TM, D_MODEL), lambda i, *_: (i, 0)),
        scratch_shapes=[pltpu.VMEM((2, D_MODEL, 2 * D_EXPERT), F32),
                        pltpu.VMEM((2, D_EXPERT, D_MODEL), F32),
                        pltpu.VMEM((D_MODEL, 2 * D_EXPERT), BF16),
                        pltpu.VMEM((D_EXPERT, D_MODEL), BF16),
                        pltpu.SemaphoreType.DMA((2, 2))],
    )
    return pl.pallas_call(
        functools.partial(_expert_kernel, layer),
        grid_spec=grid_spec,
        out_shape=jax.ShapeDtypeStruct((xs.shape[0], D_MODEL), F32),
        compiler_params=_cparams(("arbitrary",)),
        name="moe_experts",
    )(plan["tile_expert"], plan["n_active"], plan["tile_first"], plan["tile_next"],
      plan["tile_slot"], xs, b1.reshape(DEPTH, N_EXPERTS, 1, -1),
      b2.reshape(DEPTH, N_EXPERTS, 1, -1), w1, w2)


def _combine_kernel(last, src_ref, len_ref, dst_ref, tot_ref, lpos_ref, x_ref,
                    g2_ref, lg_ref, lb_ref, ys_ref, *rest):
    if last:
        o_ref, ybuf_ref, sems = rest
    else:
        sh_ref, sc_ref, w_ref, o_ref, z_ref, ybuf_ref, sems, wbf_ref = rest
        _cast_once(w_ref, wbf_ref)
    i = pl.program_id(0)
    n_blocks = pl.num_programs(0)
    slot = i % 2

    def start_block(block, s):
        _for_each_group(block, len_ref, lambda e, n: pltpu.make_async_copy(
            ys_ref.at[_group(dst_ref[block, e], n)],
            ybuf_ref.at[s, _group(src_ref[block, e], n)], sems.at[s]).start())

    @pl.when(i == 0)
    def _():
        ybuf_ref[...] = jnp.zeros_like(ybuf_ref)
        start_block(0, 0)

    @pl.when(i + 1 < n_blocks)
    def _():
        start_block(i + 1, 1 - slot)

    rows = _group(0, pl.multiple_of(tot_ref[i], GROUP))
    pltpu.make_async_copy(ys_ref.at[rows], ybuf_ref.at[slot, rows], sems.at[slot]).wait()
    lpos = lpos_ref[...]
    targets = [jnp.broadcast_to(lpos[:, k:k + 1], (TR, KB)) for k in range(TOP_K)]
    cols = lax.broadcasted_iota(jnp.int32, (TR, KB), 1)
    f = jnp.zeros((TR, D_MODEL), F32)
    for jb in range(RB // KB):
        sel = _one_hot_rows(targets, cols + jb * KB).astype(BF16)
        f = f + jnp.dot(sel, ybuf_ref[slot, jb * KB:(jb + 1) * KB, :].astype(BF16),
                        preferred_element_type=F32)
    x = ALPHA * x_ref[...] + _per_batch(f, g2_ref[0], jnp.multiply)
    out = _layer_norm(x) * lg_ref[...] + lb_ref[...]
    if last:
        o_ref[...] = jnp.transpose(out.reshape(TT, BATCH, D_MODEL), (1, 0, 2))
    else:
        o_ref[...] = out
        z_ref[...] = _in_projection(out, sh_ref, sc_ref, wbf_ref)


def _combine(t_off, plan, x, mod, ln_g, ln_b, ys, next_layer=None):
    n = x.shape[0] // TR
    last = next_layer is None
    row = lambda w: pl.BlockSpec((TR, w), lambda i, *_: (i, 0))
    const = lambda shape: pl.BlockSpec(shape, lambda i, *_: (0,) * len(shape))
    if last:
        extra_specs, extra_args, extra_scratch = [], (), []
        out_spec = pl.BlockSpec((BATCH, TT, D_MODEL), lambda i, *_: (0, i, 0))
        out_shape = jax.ShapeDtypeStruct((BATCH, n * TT, D_MODEL), F32)
    else:
        next_mod, w_in, next_l = next_layer
        extra_specs = [_mod_spec(0, t_off), _mod_spec(1, t_off),
                       pl.BlockSpec((None, D_MODEL, D_IN), lambda i, *_: (next_l, 0, 0))]
        extra_args = (next_mod, next_mod, w_in)
        extra_scratch = [pltpu.VMEM((D_MODEL, D_IN), BF16)]
        out_spec = [row(D_MODEL), row(D_IN)]
        out_shape = [jax.ShapeDtypeStruct((n * TR, D_MODEL), F32),
                     jax.ShapeDtypeStruct((n * TR, D_IN), F32)]
    grid_spec = pltpu.PrefetchScalarGridSpec(
        num_scalar_prefetch=4,
        grid=(n,),
        in_specs=[
            row(TOP_K), row(D_MODEL), _mod_spec(5, t_off),
            const((1, D_MODEL)), const((1, D_MODEL)),
            pl.BlockSpec(memory_space=pl.ANY),
        ] + extra_specs,
        out_specs=out_spec,
        scratch_shapes=[pltpu.VMEM((2, RB, D_MODEL), F32),
                        pltpu.SemaphoreType.DMA((2,))] + extra_scratch,
    )
    return pl.pallas_call(
        functools.partial(_combine_kernel, last),
        grid_spec=grid_spec,
        out_shape=out_shape,
        compiler_params=_cparams(("arbitrary",)),
        name="moe_combine",
    )(plan["src_row"], plan["n_rows"], plan["dst_row"], plan["block_rows"], plan["lpos"],
      x, mod, ln_g, ln_b, ys, *extra_args)

def _routing_plan(route, counts):
    i32 = jnp.int32
    nb = counts.shape[0]
    experts = jnp.arange(N_EXPERTS, dtype=i32)
    eid = route[:, LANE_EID:LANE_EID + TOP_K].astype(i32)
    rank = route[:, LANE_RANK:LANE_RANK + TOP_K].astype(i32)
    n = counts[:, :, 0].astype(i32)
    n = (n + GROUP - 1) // GROUP * GROUP
    src_row = jnp.cumsum(n, axis=1) - n
    seg = jnp.sum(n, axis=0)
    tiles_e = (seg + TM - 1) // TM
    tile_end = jnp.cumsum(tiles_e)
    e_start = (tile_end - tiles_e) * TM
    n_active = tile_end[-1]
    dst_row = e_start[None, :] + jnp.cumsum(n, axis=0) - n

    onehot = eid[..., None] == experts
    lpos = jnp.sum(jnp.where(onehot, src_row[:, None, None, :], 0), axis=-1) + rank
    lpos_t = jnp.concatenate([lpos.astype(F32), route[:, LANE_P:LANE_P + TOP_K]], axis=1)

    n_tiles = (nb * RB + TM - 1) // TM + N_EXPERTS
    t = jnp.minimum(jnp.arange(n_tiles, dtype=i32), n_active - 1)
    tile_expert = jnp.sum((t[:, None] >= tile_end[None, :]).astype(i32), axis=1)
    tile_expert = jnp.minimum(tile_expert, N_EXPERTS - 1)
    tile_first = jnp.concatenate(
        [jnp.ones((1,), i32), (tile_expert[1:] != tile_expert[:-1]).astype(i32)])
    has_rows = tiles_e > 0
    later = jnp.where(has_rows[None, :] & (experts[None, :] > experts[:, None]),
                      experts[None, :], N_EXPERTS)
    next_e = jnp.min(later, axis=1)
    next_e = jnp.where(next_e == N_EXPERTS, -1, next_e)
    group = jnp.cumsum(has_rows.astype(i32)) - 1
    pick = tile_expert[:, None] == experts[None, :]
    tile_next = jnp.sum(jnp.where(pick, next_e[None, :], 0), axis=1)
    tile_slot = jnp.sum(jnp.where(pick, group[None, :], 0), axis=1) % 2
    return dict(lpos=lpos.transpose(0, 2, 1).reshape(nb * TR, TOP_K), lpos_t=lpos_t, src_row=src_row, n_rows=n,
                dst_row=dst_row, block_rows=jnp.sum(n, axis=1), tail_start=e_start + seg,
                tail_len=tile_end * TM - e_start - seg,
                tile_expert=tile_expert, n_active=n_active.reshape(1), tile_first=tile_first,
                tile_next=tile_next, tile_slot=tile_slot,
                n_sorted=n_tiles * TM)


def _grid_sincos_parts():
    quarter = D_MODEL // 4
    omega = 1.0 / (10000.0 ** (jnp.arange(quarter, dtype=F32) / quarter))

    def emb1d(n):
        ang = jnp.arange(n, dtype=F32)[:, None] * omega[None, :]
        return jnp.concatenate([jnp.sin(ang), jnp.cos(ang)], axis=-1)

    er = emb1d(SEQ // GRID_W)
    ec = jnp.repeat(emb1d(GRID_W), BATCH, axis=0)
    return er.reshape(SEQ // GRID_W, 1, D_MODEL // 2), ec


def _gate_blocks(w):
    per = GATE_BLK // LRU_HEAD_DIM
    w = w.reshape(2, D_LRU // GATE_BLK, per, LRU_HEAD_DIM, LRU_HEAD_DIM)
    eye = jnp.eye(per, dtype=w.dtype)
    blk = jnp.einsum('dkpij,pq->dkpiqj', w, eye)
    return blk.reshape(2, D_LRU // GATE_BLK, GATE_BLK, GATE_BLK)


def kernel(x, c, ctx, c_ctx, w_mod, b_mod, w_in, conv_w, conv_b, gate_a_w, gate_a_b,
           gate_x_w, gate_x_b, lru_lambda, pool_w, pool_b, pool_scale, w_out, ln1_g, ln1_b,
           router_w, router_b, exp_w1, exp_b1, exp_w2, exp_b2, ln2_g, ln2_b):
    cvec = jnp.concatenate([c, c_ctx[None], jnp.zeros((2 * SUBLANES - BATCH - 1, D_MODEL), F32)])
    mod_all = _modulation(cvec, w_mod, b_mod)
    mods = jnp.stack([jnp.broadcast_to(mod_all[:, BATCH:BATCH + 1], (DEPTH, BATCH, 6 * D_MODEL)),
                      mod_all[:, :BATCH]], axis=1)
    tri = jnp.triu(jnp.ones((TR, TR), F32), 1).astype(BF16)
    inv_cnt = _pool_inverse_counts()

    er, ec = _grid_sincos_parts()
    xs, z = _entry(ctx, x, er, ec, mods[0], w_in)
    for l in range(DEPTH):
        last = l == DEPTH - 1
        t_off = CTX_TILES if last else 0
        mod = mods[l]
        gate_w = jnp.stack([_gate_blocks(gate_a_w[l]), _gate_blocks(gate_x_w[l])],
                           axis=1).astype(BF16)
        gate_b = jnp.stack([gate_a_b[l], gate_x_b[l]], axis=1)
        hf, hr = _scan(z, conv_w[l], conv_b[l][None], gate_w, gate_b, lru_lambda[l])
        rw_pad = jnp.zeros((D_MODEL, LANES), F32).at[:, :N_EXPERTS].set(router_w[l])
        rw_hi = rw_pad.astype(BF16)
        rw_pad = jnp.concatenate([rw_hi, (rw_pad - rw_hi.astype(F32)).astype(BF16)], axis=1)
        rb_pad = jnp.full((1, LANES), -1e30, F32).at[0, :N_EXPERTS].set(router_b[l])
        xs, route, counts = _mixer(
            l, t_off, hf, hr, z, inv_cnt, xs, mod, pool_w, pool_b[l][None], pool_scale[l][None], w_out,
            ln1_g[l][None], ln1_b[l][None], rw_pad, rb_pad, tri)
        plan = _routing_plan(route, counts)
        x_sorted = _dispatch(t_off, plan, xs, mod)
        y_sorted = _experts(l, plan, x_sorted, exp_w1, exp_b1, exp_w2, exp_b2)
        if last:
            return _combine(t_off, plan, xs, mod, ln2_g[l][None], ln2_b[l][None], y_sorted)
        xs, z = _combine(t_off, plan, xs, mod, ln2_g[l][None], ln2_b[l][None], y_sorted,
                         (mods[l + 1], w_in, l + 1))
```

```python
import functools

import jax
import jax.numpy as jnp
from jax import lax
from jax.experimental import pallas as pl
from jax.experimental.pallas import tpu as pltpu

D_MODEL = 1024
BATCH = 8
SEQ = 2048
DEPTH = 2
GRID_W = 64
CTX_LEN = 256
D_LRU = 512
N_LRU_HEADS = 8
LRU_HEAD_DIM = D_LRU // N_LRU_HEADS
LRU_C = 8.0
D_POOL = 512
POOL_WINDOWS = (2, 4, 8, 16)
POOL_GROUP_DIM = D_POOL // len(POOL_WINDOWS)
D_IN = 2 * D_LRU + D_POOL
N_EXPERTS = 32
TOP_K = 4
D_EXPERT = D_MODEL
SWIGLU_LIMIT = 7.0
SWIGLU_ALPHA = 1.702
LN_EPS = 1e-5
ALPHA = (2.0 * DEPTH) ** 0.25

F32 = jnp.float32
BF16 = jnp.bfloat16

SUBLANES = 8
LANES = 128
TT = GRID_W
TR = TT * BATCH
T_ALL = CTX_LEN + SEQ
R_ALL = T_ALL * BATCH
N_TILES = T_ALL // TT
CTX_TILES = CTX_LEN // TT
GATE_BLK = 256
POOL_HALO = max(POOL_WINDOWS) // 2
TM = 512
FFN_CHUNK = 1024
GROUP = SUBLANES
RB = TOP_K * TR + N_EXPERTS * GROUP
KB = 256
MB = 512
RB_STAGE = (RB + MB - 1) // MB * MB
D_AUG = D_MODEL + LANES
LANE_P, LANE_EID, LANE_RANK = 0, 4, 8
ROUTE_ROWS = 16
V7X_VMEM_BYTES = 64 * 1024 * 1024
VMEM_LIMIT = V7X_VMEM_BYTES * 7 // 8


def _cparams(sem):
    return pltpu.CompilerParams(dimension_semantics=sem, vmem_limit_bytes=VMEM_LIMIT)


def _sigmoid(x):
    return 0.5 * (1.0 + jnp.tanh(0.5 * x))


def _layer_norm(x):
    mu = jnp.mean(x, axis=-1, keepdims=True)
    xc = x - mu
    var = jnp.mean(xc * xc, axis=-1, keepdims=True)
    return xc * lax.rsqrt(var + LN_EPS)


def _per_batch(x, v, op):
    r, d = x.shape
    return op(x.reshape(r // BATCH, BATCH, d), v[None]).reshape(r, d)


def _mod_kernel(c_ref, w_ref, b_ref, o_ref):
    c = c_ref[...]
    s = c * _sigmoid(c)
    w = w_ref[...]
    s_hi, w_hi = s.astype(BF16), w.astype(BF16)
    s_lo = (s - s_hi.astype(F32)).astype(BF16)
    w_lo = (w - w_hi.astype(F32)).astype(BF16)
    o_ref[...] = (jnp.dot(s_hi, w_hi, preferred_element_type=F32)
                  + jnp.dot(s_lo, w_hi, preferred_element_type=F32)
                  + jnp.dot(s_hi, w_lo, preferred_element_type=F32)) + b_ref[...]


def _modulation(cvec, w_mod, b_mod):
    tn = 512
    return pl.pallas_call(
        _mod_kernel,
        grid=(DEPTH, 6 * D_MODEL // tn),
        in_specs=[
            pl.BlockSpec((2 * SUBLANES, D_MODEL), lambda l, j: (0, 0)),
            pl.BlockSpec((None, D_MODEL, tn), lambda l, j: (l, 0, j)),
            pl.BlockSpec((None, 1, tn), lambda l, j: (l, 0, j)),
        ],
        out_specs=pl.BlockSpec((None, 2 * SUBLANES, tn), lambda l, j: (l, 0, j)),
        out_shape=jax.ShapeDtypeStruct((DEPTH, 2 * SUBLANES, 6 * D_MODEL), F32),
        compiler_params=_cparams(("arbitrary", "arbitrary")),
        name="modulation",
    )(cvec, w_mod, b_mod.reshape(DEPTH, 1, 6 * D_MODEL))


def _modulate(x, sh_ref, sc_ref):
    h = _per_batch(x, 1.0 + sc_ref[0], jnp.multiply)
    return _per_batch(h, sh_ref[0], jnp.add)


def _in_projection(x, sh_ref, sc_ref, w_ref):
    return jnp.dot(_modulate(x, sh_ref, sc_ref).astype(BF16), w_ref[...],
                   preferred_element_type=F32)


def _cast_once(w_ref, wbf_ref):
    @pl.when(pl.program_id(0) == 0)
    def _():
        wbf_ref[...] = w_ref[...].astype(BF16)


def _entry_kernel(ctx_ref, x_ref, er_ref, ec_ref, sh_ref, sc_ref, w_ref, o_ref, z_ref, wbf_ref):
    i = pl.program_id(0)
    _cast_once(w_ref, wbf_ref)

    def time_major(src_ref):
        return jnp.transpose(src_ref[...], (1, 0, 2)).reshape(TR, D_MODEL)

    @pl.when(i < CTX_TILES)
    def _():
        o_ref[...] = _layer_norm(time_major(ctx_ref))

    @pl.when(i >= CTX_TILES)
    def _():
        pos = jnp.concatenate(
            [jnp.broadcast_to(er_ref[0], (TR, D_MODEL // 2)), ec_ref[...]], axis=1)
        o_ref[...] = _layer_norm(time_major(x_ref) + pos)

    z_ref[...] = _in_projection(o_ref[...], sh_ref, sc_ref, wbf_ref)


def _mod_spec(chunk, t_off=0):
    return pl.BlockSpec((1, BATCH, D_MODEL),
                        lambda i, *_: ((i + t_off >= CTX_TILES).astype(jnp.int32), 0, chunk))


def _entry(ctx, x, er, ec, mod, w_in):
    return pl.pallas_call(
        _entry_kernel,
        grid=(N_TILES,),
        in_specs=[
            pl.BlockSpec((BATCH, TT, D_MODEL), lambda i: (0, jnp.minimum(i, CTX_TILES - 1), 0)),
            pl.BlockSpec((BATCH, TT, D_MODEL), lambda i: (0, jnp.maximum(i - CTX_TILES, 0), 0)),
            pl.BlockSpec((1, 1, D_MODEL // 2), lambda i: (jnp.maximum(i - CTX_TILES, 0), 0, 0)),
            pl.BlockSpec((TR, D_MODEL // 2), lambda i: (0, 0)),
            _mod_spec(0), _mod_spec(1),
            pl.BlockSpec((None, D_MODEL, D_IN), lambda i: (0, 0, 0)),
        ],
        out_specs=[pl.BlockSpec((TR, D_MODEL), lambda i: (i, 0)),
                   pl.BlockSpec((TR, D_IN), lambda i: (i, 0))],
        out_shape=[jax.ShapeDtypeStruct((R_ALL, D_MODEL), F32),
                   jax.ShapeDtypeStruct((R_ALL, D_IN), F32)],
        scratch_shapes=[pltpu.VMEM((D_MODEL, D_IN), BF16)],
        compiler_params=_cparams(("arbitrary",)),
        name="entry_ln",
    )(ctx, x, er, ec, mod, mod, w_in)


def _rev_tile(i):
    return jnp.where(i < CTX_TILES, CTX_TILES - 1 - i, N_TILES - 1 + CTX_TILES - i)


def _block_diag_dot(u_bf, w_ref, d, g):
    halves = [jnp.dot(u_bf[:, k * GATE_BLK:(k + 1) * GATE_BLK], w_ref[d, g, k],
                      preferred_element_type=F32) for k in range(D_LRU // GATE_BLK)]
    return jnp.concatenate(halves, axis=1)


def _lru_coeffs(tile, z_ref, zp_ref, zn_ref, cw_ref, cb_ref, gw_ref, gb_ref, lam_ref, d):
    x = z_ref[...]
    seg_first = (tile == 0) | (tile == CTX_TILES)
    seg_last = (tile == CTX_TILES - 1) | (tile == N_TILES - 1)
    prev = jnp.where(seg_first, 0.0, zp_ref[...])
    nxt = jnp.where(seg_last, 0.0, zn_ref[...])
    xm2 = jnp.concatenate([prev, x[:-2 * BATCH]], axis=0)
    xm1 = jnp.concatenate([prev[BATCH:], x[:-BATCH]], axis=0)
    xp1 = jnp.concatenate([x[BATCH:], nxt], axis=0)
    u = (cb_ref[...] + xm2 * cw_ref[0:1] + xm1 * cw_ref[1:2]
         + x * cw_ref[2:3] + xp1 * cw_ref[3:4])
    u_bf = u.astype(BF16)
    r = _sigmoid(_block_diag_dot(u_bf, gw_ref, d, 0) + gb_ref[d, 0:1])
    ig = _sigmoid(_block_diag_dot(u_bf, gw_ref, d, 1) + gb_ref[d, 1:2])
    nl = -lam_ref[d:d + 1]
    softplus = jnp.maximum(nl, 0.0) + jnp.log(1.0 + jnp.exp(-jnp.abs(nl)))
    log_a = (-LRU_C) * r * softplus
    a = jnp.exp(log_a)
    gap = 1.0 - a * a
    mult = jnp.where(gap > 0.0, gap * lax.rsqrt(gap), 0.0)
    return a, mult * (ig * u)


def _scan_kernel(zf_ref, zfp_ref, zfn_ref, zr_ref, zrp_ref, zrn_ref,
                 cw_ref, cb_ref, gw_ref, gb_ref, lam_ref,
                 hf_ref, hr_ref, state_ref):
    i = pl.program_id(0)

    @pl.when(i == 0)
    def _():
        state_ref[...] = jnp.zeros_like(state_ref)

    af, bf = _lru_coeffs(i, zf_ref, zfp_ref, zfn_ref, cw_ref, cb_ref, gw_ref, gb_ref,
                         lam_ref, 0)
    ar, br = _lru_coeffs(_rev_tile(i), zr_ref, zrp_ref, zrn_ref, cw_ref, cb_ref, gw_ref,
                         gb_ref, lam_ref, 1)
    hf = state_ref[0]
    hr = state_ref[1]
    for s in range(TT):
        f = slice(s * BATCH, (s + 1) * BATCH)
        hf = af[f] * hf + bf[f]
        hf_ref[f, :] = hf
        b = slice((TT - 1 - s) * BATCH, (TT - s) * BATCH)
        hr = ar[b] * hr + br[b]
        hr_ref[b, :] = hr
    state_ref[0] = hf
    state_ref[1] = hr


def _scan(z, conv_w, conv_b, gate_w, gate_b, lam):
    prev_rows = 2 * BATCH
    tiles_per_prev = TR // prev_rows
    tiles_per_next = TR // BATCH
    last_next = R_ALL // BATCH - 1

    def cur(f):
        return pl.BlockSpec((TR, D_LRU), lambda i: (f(i), 0))

    def prev(f):
        return pl.BlockSpec((prev_rows, D_LRU),
                            lambda i: (jnp.maximum(f(i) * tiles_per_prev - 1, 0), 0))

    def nxt(f):
        return pl.BlockSpec((BATCH, D_LRU),
                            lambda i: (jnp.minimum((f(i) + 1) * tiles_per_next, last_next), 0))

    fwd = lambda i: i
    const = lambda shape: pl.BlockSpec(shape, lambda i: (0,) * len(shape))
    return pl.pallas_call(
        _scan_kernel,
        grid=(N_TILES,),
        in_specs=[cur(fwd), prev(fwd), nxt(fwd), cur(_rev_tile), prev(_rev_tile), nxt(_rev_tile),
                  const((4, D_LRU)), const((1, D_LRU)),
                  const((2, 2, D_LRU // GATE_BLK, GATE_BLK, GATE_BLK)),
                  const((2, 2, D_LRU)), const((2, D_LRU))],
        out_specs=[pl.BlockSpec((TR, D_LRU), lambda i: (i, 0)),
                   pl.BlockSpec((TR, D_LRU), lambda i: (_rev_tile(i), 0))],
        out_shape=[jax.ShapeDtypeStruct((R_ALL, D_LRU), F32)] * 2,
        scratch_shapes=[pltpu.VMEM((2, BATCH, D_LRU), F32)],
        compiler_params=_cparams(("arbitrary",)),
        name="lru_scan",
    )(z, z, z, z, z, z, conv_w, conv_b, gate_w, gate_b, lam)


def _gelu_tanh(y):
    c = 0.7978845608028654
    return 0.5 * y * (1.0 + jnp.tanh(c * (y + 0.044715 * (y * y * y))))


def _pool_inverse_counts():
    t = jnp.arange(TT)[None, :, None]
    kind = jnp.arange(CTX_TILES + 1)[:, None, None]
    lo = jnp.where(kind < CTX_TILES, -TT * kind, 0)
    hi = jnp.where(kind < CTX_TILES, CTX_LEN - TT * kind, TT)
    half = (jnp.asarray(POOL_WINDOWS) // 2)[None, None, :]
    cnt = jnp.minimum(t + half, hi) - jnp.maximum(t - half, lo)
    inv = 1.0 / cnt.astype(F32)
    inv = jnp.repeat(jnp.repeat(inv, POOL_GROUP_DIM, axis=2), BATCH, axis=1)
    return inv


def _pool_groups(tile, xp_ref, xpp_ref, xpn_ref, inv_ref):
    in_ctx = tile < CTX_TILES
    prev_ok = in_ctx & (tile > 0)
    next_ok = in_ctx & (tile < CTX_TILES - 1)
    x = xp_ref[...]
    p = jnp.concatenate([jnp.where(prev_ok, xpp_ref[...], 0.0), x,
                         jnp.where(next_ok, xpn_ref[...], 0.0)], axis=0)
    p = p.reshape(TT + 2 * POOL_HALO, BATCH, D_POOL)
    inv = inv_ref[...].reshape(TT, BATCH, D_POOL)
    outs = []
    for g, win in enumerate(POOL_WINDOWS):
        half = win // 2
        acc = p[:, :, g * POOL_GROUP_DIM:(g + 1) * POOL_GROUP_DIM]
        width = 1
        while width < win:
            acc = acc[:acc.shape[0] - width] + acc[width:]
            width *= 2
        start = POOL_HALO - half
        wsum = acc[start:start + TT]
        lanes = slice(g * POOL_GROUP_DIM, (g + 1) * POOL_GROUP_DIM)
        centre = p[POOL_HALO:POOL_HALO + TT, :, lanes]
        outs.append((wsum * inv[:, :, lanes] - centre).reshape(TR, POOL_GROUP_DIM))
    return outs


def _mixer_kernel(t_off, hf_ref, hr_ref, y_ref, xp_ref, xpp_ref, xpn_ref, inv_ref, x_ref,
                  g1_ref, sh2_ref, sc2_ref, pw_ref, pb_ref, ps_ref, wo_ref,
                  lg_ref, lb_ref, rw_ref, rb_ref, tri_ref,
                  xo_ref, route_ref, cnt_ref, wobf_ref):
    i = pl.program_id(0)
    tile = i + t_off
    _cast_once(wo_ref, wobf_ref)

    lru = (hf_ref[...] + hr_ref[...]) * _gelu_tanh(y_ref[...])
    diffs = _pool_groups(tile, xp_ref, xpp_ref, xpn_ref, inv_ref)
    pooled = jnp.concatenate(
        [jnp.dot(d.astype(BF16), pw_ref[g].astype(BF16), preferred_element_type=F32)
         for g, d in enumerate(diffs)], axis=1)
    pooled = (pooled + pb_ref[...]) * ps_ref[...]
    mix = (jnp.dot(lru.astype(BF16), wobf_ref[0:D_LRU, :], preferred_element_type=F32)
           + jnp.dot(pooled.astype(BF16), wobf_ref[D_LRU:, :], preferred_element_type=F32))
    x = ALPHA * x_ref[...] + _per_batch(mix, g1_ref[0], jnp.multiply)
    x = _layer_norm(x) * lg_ref[...] + lb_ref[...]
    xo_ref[...] = x
    h2 = _modulate(x, sh2_ref, sc2_ref)

    h_hi = h2.astype(BF16)
    h_lo = (h2 - h_hi.astype(F32)).astype(BF16)
    by_hi = jnp.dot(h_hi, rw_ref[...], preferred_element_type=F32)
    logits = (by_hi[:, :LANES] + by_hi[:, LANES:]
              + jnp.dot(h_lo, rw_ref[:, 0:LANES], preferred_element_type=F32)) + rb_ref[...]
    work = logits.T[0:N_EXPERTS, :]
    expert = lax.broadcasted_iota(jnp.int32, (N_EXPERTS, TR), 0).astype(F32)
    vals, idxs, sels = [], [], []
    for _ in range(TOP_K):
        m = jnp.max(work, axis=0, keepdims=True)
        idx = jnp.min(jnp.where(work == m, expert, float(N_EXPERTS)), axis=0, keepdims=True)
        sel = expert == idx
        vals.append(m)
        idxs.append(idx)
        sels.append(sel)
        work = jnp.where(sel, -jnp.inf, work)
    exps = [jnp.exp(v - vals[0]) for v in vals]
    denom = exps[0] + exps[1] + exps[2] + exps[3]
    chosen = jnp.zeros((N_EXPERTS, TR), F32)
    for sel in sels:
        chosen = chosen + sel.astype(F32)
    before = jnp.dot(chosen.astype(BF16), tri_ref[...], preferred_element_type=F32)
    ranks = [jnp.sum(jnp.where(sel, before, 0.0), axis=0, keepdims=True) for sel in sels]
    route_ref[0] = jnp.concatenate(
        [e / denom for e in exps] + idxs + ranks
        + [jnp.zeros((ROUTE_ROWS - 3 * TOP_K, TR), F32)], axis=0)
    cnt_ref[0] = jnp.broadcast_to(jnp.sum(chosen, axis=1, keepdims=True), (N_EXPERTS, LANES))


def _mixer(layer, t_off, hf, hr, z, inv_cnt, x, mod, pool_w, pool_b, pool_scale, w_out,
           ln_g, ln_b, rw_pad, rb_pad, tri):
    n = N_TILES - t_off
    halo_rows = POOL_HALO * BATCH
    per = TR // halo_rows
    last_halo = R_ALL // halo_rows - 1
    xp_col = 2 * D_LRU // D_POOL
    row = lambda w: pl.BlockSpec((TR, w), lambda i: (i + t_off, 0))
    out_row = lambda w: pl.BlockSpec((TR, w), lambda i: (i, 0))
    const = lambda shape: pl.BlockSpec(shape, lambda i: (0,) * len(shape))

    def mod_spec(chunk):
        return pl.BlockSpec((1, BATCH, D_MODEL),
                            lambda i: ((i + t_off >= CTX_TILES).astype(jnp.int32), 0, chunk))

    return pl.pallas_call(
        functools.partial(_mixer_kernel, t_off),
        grid=(n,),
        in_specs=[
            row(D_LRU), row(D_LRU),
            pl.BlockSpec((TR, D_LRU), lambda i: (i + t_off, 1)),
            pl.BlockSpec((TR, D_POOL), lambda i: (i + t_off, xp_col)),
            pl.BlockSpec((halo_rows, D_POOL),
                         lambda i: (jnp.maximum((i + t_off) * per - 1, 0), xp_col)),
            pl.BlockSpec((halo_rows, D_POOL),
                         lambda i: (jnp.minimum((i + t_off + 1) * per, last_halo), xp_col)),
            pl.BlockSpec((None, TR, D_POOL), lambda i: (jnp.minimum(i + t_off, CTX_TILES), 0, 0)),
            row(D_MODEL),
            mod_spec(2), mod_spec(3), mod_spec(4),
            pl.BlockSpec((None, len(POOL_WINDOWS), POOL_GROUP_DIM, POOL_GROUP_DIM),
                         lambda i: (layer, 0, 0, 0)),
            const((1, D_POOL)), const((1, D_POOL)),
            pl.BlockSpec((None, D_MODEL, D_MODEL), lambda i: (layer, 0, 0)),
            const((1, D_MODEL)), const((1, D_MODEL)),
            const((D_MODEL, 2 * LANES)), const((1, LANES)),
            const((TR, TR)),
        ],
        out_specs=[out_row(D_MODEL),
                   pl.BlockSpec((1, ROUTE_ROWS, TR), lambda i: (i, 0, 0)),
                   pl.BlockSpec((1, N_EXPERTS, LANES), lambda i: (i, 0, 0))],
        out_shape=[jax.ShapeDtypeStruct((n * TR, D_MODEL), F32),
                   jax.ShapeDtypeStruct((n, ROUTE_ROWS, TR), F32),
                   jax.ShapeDtypeStruct((n, N_EXPERTS, LANES), F32)],
        scratch_shapes=[pltpu.VMEM((D_MODEL, D_MODEL), BF16)],
        compiler_params=_cparams(("arbitrary",)),
        name="mixer_out",
    )(hf, hr, z, z, z, z, inv_cnt, x, mod, mod, mod, pool_w, pool_b, pool_scale, w_out,
      ln_g, ln_b, rw_pad, rb_pad, tri)


def _one_hot_rows(targets, index, weights=None):
    matches = [index == t for t in targets]
    hit = jnp.zeros(index.shape, F32)
    for m in reversed(matches):
        hit = jnp.where(m, 1.0, hit)
    if weights is None:
        return hit
    picked = jnp.zeros(index.shape, F32)
    for m, w in zip(reversed(matches), reversed(weights)):
        picked = jnp.where(m, w, picked)
    return hit, picked


def _for_each_group(block, len_ref, body):
    def step(e, carry):
        n = pl.multiple_of(len_ref[block, e], GROUP)

        @pl.when(n > 0)
        def _():
            body(e, n)

        return carry

    lax.fori_loop(0, N_EXPERTS, step, 0)


def _group(start, n):
    return pl.ds(pl.multiple_of(start, GROUP), n)


def _dispatch_kernel(src_ref, len_ref, dst_ref, tot_ref, tail_s_ref, tail_n_ref, na_ref,
                     lpos_ref, x_ref, sh2_ref, sc2_ref, xs_ref, stage_ref, zero_ref, sems):
    i = pl.program_id(0)
    n_blocks = pl.num_programs(0) - 1
    slot = i % 2

    @pl.when(i < n_blocks)
    def _():
        lpos = lpos_ref[...]
        targets = [lpos[k:k + 1, :] for k in range(TOP_K)]
        gates = [lpos[TOP_K + k:TOP_K + k + 1, :] for k in range(TOP_K)]
        h = _modulate(x_ref[...], sh2_ref, sc2_ref).astype(BF16)

        def permute(jb, carry):
            base = pl.multiple_of(jb * MB, MB)
            rows = (lax.broadcasted_iota(jnp.int32, (MB, TR), 0) + base).astype(F32)
            sel, picked = _one_hot_rows(targets, rows, gates)
            stage_ref[slot, pl.ds(base, MB), 0:D_MODEL] = jnp.dot(
                sel.astype(BF16), h, preferred_element_type=F32)
            gate = jnp.sum(picked, axis=1, keepdims=True)
            stage_ref[slot, pl.ds(base, MB), D_MODEL:D_AUG] = jnp.broadcast_to(gate, (MB, LANES))
            return carry

        lax.fori_loop(0, (tot_ref[i] + MB - 1) // MB, permute, 0)
        _for_each_group(i, len_ref, lambda e, n: pltpu.make_async_copy(
            stage_ref.at[slot, _group(src_ref[i, e], n)],
            xs_ref.at[_group(dst_ref[i, e], n)], sems.at[slot]).start())

    @pl.when(i > 0)
    def _():
        rows = _group(0, pl.multiple_of(tot_ref[i - 1], GROUP))
        pltpu.make_async_copy(stage_ref.at[1 - slot, rows], xs_ref.at[rows],
                              sems.at[1 - slot]).wait()

    @pl.when(i == n_blocks)
    def _():
        zero_ref[...] = jnp.zeros_like(zero_ref)
        fill = sems.at[2]

        def expert_tail(e, n):
            return pltpu.make_async_copy(zero_ref.at[_group(0, n)],
                                         xs_ref.at[_group(tail_s_ref[0, e], n)], fill)

        def whole_tile(t):
            return pltpu.make_async_copy(zero_ref, xs_ref.at[_group(t * TM, TM)], fill)

        n_tiles = xs_ref.shape[0] // TM
        _for_each_group(0, tail_n_ref, lambda e, n: expert_tail(e, n).start())
        lax.fori_loop(na_ref[0], n_tiles, lambda t, c: (whole_tile(t).start(), c)[1], 0)
        _for_each_group(0, tail_n_ref, lambda e, n: expert_tail(e, n).wait())
        lax.fori_loop(na_ref[0], n_tiles, lambda t, c: (whole_tile(t).wait(), c)[1], 0)


def _dispatch(t_off, plan, x, mod):
    n = x.shape[0] // TR
    last = n - 1
    grid_spec = pltpu.PrefetchScalarGridSpec(
        num_scalar_prefetch=7,
        grid=(n + 1,),
        in_specs=[
            pl.BlockSpec((None, SUBLANES, TR), lambda i, *_: (jnp.minimum(i, last), 0, 0)),
            pl.BlockSpec((TR, D_MODEL), lambda i, *_: (jnp.minimum(i, last), 0)),
            _mod_spec(3, t_off), _mod_spec(4, t_off),
        ],
        out_specs=pl.BlockSpec(memory_space=pl.ANY),
        scratch_shapes=[pltpu.VMEM((2, RB_STAGE, D_AUG), F32),
                        pltpu.VMEM((TM, D_AUG), F32),
                        pltpu.SemaphoreType.DMA((3,))],
    )
    return pl.pallas_call(
        _dispatch_kernel,
        grid_spec=grid_spec,
        out_shape=jax.ShapeDtypeStruct((plan["n_sorted"], D_AUG), F32),
        compiler_params=_cparams(("arbitrary",)),
        name="moe_dispatch",
    )(plan["src_row"], plan["n_rows"], plan["dst_row"], plan["block_rows"],
      plan["tail_start"][None], plan["tail_len"][None], plan["n_active"], plan["lpos_t"],
      x, mod, mod)


def _expert_kernel(layer, te_ref, na_ref, first_ref, nxt_ref, slot_ref,
                   x_ref, b1_ref, b2_ref, w1_hbm, w2_hbm, o_ref,
                   w1buf_ref, w2buf_ref, w1c_ref, w2c_ref, sems):
    i = pl.program_id(0)
    e = te_ref[i]
    slot = slot_ref[i]

    def fetch(expert, s):
        return (pltpu.make_async_copy(w1_hbm.at[layer, expert], w1buf_ref.at[s], sems.at[s, 0]),
                pltpu.make_async_copy(w2_hbm.at[layer, expert], w2buf_ref.at[s], sems.at[s, 1]))

    @pl.when(i == 0)
    def _():
        for copy in fetch(e, slot):
            copy.start()

    @pl.when(first_ref[i] == 1)
    def _():
        for copy in fetch(e, slot):
            copy.wait()
        w1c_ref[...] = w1buf_ref[slot].astype(BF16)
        w2c_ref[...] = w2buf_ref[slot].astype(BF16)

        @pl.when(nxt_ref[i] >= 0)
        def _():
            for copy in fetch(nxt_ref[i], 1 - slot):
                copy.start()

    @pl.when(i < na_ref[0])
    def _():
        gate = x_ref[:, D_MODEL:D_MODEL + 1]
        x = x_ref[:, 0:D_MODEL].astype(BF16)
        y = jnp.broadcast_to(b2_ref[...], (TM, D_MODEL))
        for c in range(D_EXPERT // FFN_CHUNK):
            g_cols = slice(c * FFN_CHUNK, (c + 1) * FFN_CHUNK)
            l_cols = slice(D_EXPERT + c * FFN_CHUNK, D_EXPERT + (c + 1) * FFN_CHUNK)
            glu = jnp.dot(x, w1c_ref[:, g_cols], preferred_element_type=F32) + b1_ref[:, g_cols]
            lin = jnp.dot(x, w1c_ref[:, l_cols], preferred_element_type=F32) + b1_ref[:, l_cols]
            glu = jnp.minimum(glu, SWIGLU_LIMIT)
            lin = jnp.clip(lin, -SWIGLU_LIMIT, SWIGLU_LIMIT)
            act = glu * _sigmoid(SWIGLU_ALPHA * glu) * (lin + 1.0)
            y = y + jnp.dot(act.astype(BF16), w2c_ref[g_cols, :], preferred_element_type=F32)
        o_ref[...] = gate * y

    @pl.when(i >= na_ref[0])
    def _():
        o_ref[...] = jnp.zeros_like(o_ref)


def _experts(layer, plan, xs, w1, b1, w2, b2):
    n_tiles = xs.shape[0] // TM
    act_tile = lambda i, te, na, *_: (jnp.maximum(jnp.minimum(i, na[0] - 1), 0), 0)
    expert = lambda i, te, *_: (layer, te[i], 0, 0)
    grid_spec = pltpu.PrefetchScalarGridSpec(
        num_scalar_prefetch=5,
        grid=(n_tiles,),
        in_specs=[
            pl.BlockSpec((TM, D_AUG), act_tile),
            pl.BlockSpec((None, None, 1, 2 * D_EXPERT), expert),
            pl.BlockSpec((None, None, 1, D_MODEL), expert),
            pl.BlockSpec(memory_space=pl.ANY),
            pl.BlockSpec(memory_space=pl.ANY),
        ],
        out_specs=pl.BlockSpec((TM, D_MODEL), lambda i, *_: (i, 0)),
        scratch_shapes=[pltpu.VMEM((2, D_MODEL, 2 * D_EXPERT), F32),
                        pltpu.VMEM((2, D_EXPERT, D_MODEL), F32),
                        pltpu.VMEM((D_MODEL, 2 * D_EXPERT), BF16),
                        pltpu.VMEM((D_EXPERT, D_MODEL), BF16),
                        pltpu.SemaphoreType.DMA((2, 2))],
    )
    return pl.pallas_call(
        functools.partial(_expert_kernel, layer),
        grid_spec=grid_spec,
        out_shape=jax.ShapeDtypeStruct((xs.shape[0], D_MODEL), F32),
        compiler_params=_cparams(("arbitrary",)),
        name="moe_experts",
    )(plan["tile_expert"], plan["n_active"], plan["tile_first"], plan["tile_next"],
      plan["tile_slot"], xs, b1.reshape(DEPTH, N_EXPERTS, 1, -1),
      b2.reshape(DEPTH, N_EXPERTS, 1, -1), w1, w2)


def _combine_kernel(last, src_ref, len_ref, dst_ref, tot_ref, lpos_ref, x_ref,
                    g2_ref, lg_ref, lb_ref, ys_ref, *rest):
    if last:
        o_ref, ybuf_ref, sems = rest
    else:
        sh_ref, sc_ref, w_ref, o_ref, z_ref, ybuf_ref, sems, wbf_ref = rest
        _cast_once(w_ref, wbf_ref)
    i = pl.program_id(0)
    n_blocks = pl.num_programs(0)
    slot = i % 2

    def start_block(block, s):
        _for_each_group(block, len_ref, lambda e, n: pltpu.make_async_copy(
            ys_ref.at[_group(dst_ref[block, e], n)],
            ybuf_ref.at[s, _group(src_ref[block, e], n)], sems.at[s]).start())

    @pl.when(i == 0)
    def _():
        ybuf_ref[...] = jnp.zeros_like(ybuf_ref)
        start_block(0, 0)

    @pl.when(i + 1 < n_blocks)
    def _():
        start_block(i + 1, 1 - slot)

    rows = _group(0, pl.multiple_of(tot_ref[i], GROUP))
    pltpu.make_async_copy(ys_ref.at[rows], ybuf_ref.at[slot, rows], sems.at[slot]).wait()
    lpos = lpos_ref[...]
    targets = [jnp.broadcast_to(lpos[:, k:k + 1], (TR, KB)) for k in range(TOP_K)]
    cols = lax.broadcasted_iota(jnp.int32, (TR, KB), 1)
    f = jnp.zeros((TR, D_MODEL), F32)
    for jb in range(RB // KB):
        sel = _one_hot_rows(targets, cols + jb * KB).astype(BF16)
        f = f + jnp.dot(sel, ybuf_ref[slot, jb * KB:(jb + 1) * KB, :].astype(BF16),
                        preferred_element_type=F32)
    x = ALPHA * x_ref[...] + _per_batch(f, g2_ref[0], jnp.multiply)
    out = _layer_norm(x) * lg_ref[...] + lb_ref[...]
    if last:
        o_ref[...] = jnp.transpose(out.reshape(TT, BATCH, D_MODEL), (1, 0, 2))
    else:
        o_ref[...] = out
        z_ref[...] = _in_projection(out, sh_ref, sc_ref, wbf_ref)


def _combine(t_off, plan, x, mod, ln_g, ln_b, ys, next_layer=None):
    n = x.shape[0] // TR
    last = next_layer is None
    row = lambda w: pl.BlockSpec((TR, w), lambda i, *_: (i, 0))
    const = lambda shape: pl.BlockSpec(shape, lambda i, *_: (0,) * len(shape))
    if last:
        extra_specs, extra_args, extra_scratch = [], (), []
        out_spec = pl.BlockSpec((BATCH, TT, D_MODEL), lambda i, *_: (0, i, 0))
        out_shape = jax.ShapeDtypeStruct((BATCH, n * TT, D_MODEL), F32)
    else:
        next_mod, w_in, next_l = next_layer
        extra_specs = [_mod_spec(0, t_off), _mod_spec(1, t_off),
                       pl.BlockSpec((None, D_MODEL, D_IN), lambda i, *_: (next_l, 0, 0))]
        extra_args = (next_mod, next_mod, w_in)
        extra_scratch = [pltpu.VMEM((D_MODEL, D_IN), BF16)]
        out_spec = [row(D_MODEL), row(D_IN)]
        out_shape = [jax.ShapeDtypeStruct((n * TR, D_MODEL), F32),
                     jax.ShapeDtypeStruct((n * TR, D_IN), F32)]
    grid_spec = pltpu.PrefetchScalarGridSpec(
        num_scalar_prefetch=4,
        grid=(n,),
        in_specs=[
            row(TOP_K), row(D_MODEL), _mod_spec(5, t_off),
            const((1, D_MODEL)), const((1, D_MODEL)),
            pl.BlockSpec(memory_space=pl.ANY),
        ] + extra_specs,
        out_specs=out_spec,
        scratch_shapes=[pltpu.VMEM((2, RB, D_MODEL), F32),
                        pltpu.SemaphoreType.DMA((2,))] + extra_scratch,
    )
    return pl.pallas_call(
        functools.partial(_combine_kernel, last),
        grid_spec=grid_spec,
        out_shape=out_shape,
        compiler_params=_cparams(("arbitrary",)),
        name="moe_combine",
    )(plan["src_row"], plan["n_rows"], plan["dst_row"], plan["block_rows"], plan["lpos"],
      x, mod, ln_g, ln_b, ys, *extra_args)

def _routing_plan(route, counts):
    i32 = jnp.int32
    nb = counts.shape[0]
    experts = jnp.arange(N_EXPERTS, dtype=i32)
    eid = route[:, LANE_EID:LANE_EID + TOP_K].astype(i32)
    rank = route[:, LANE_RANK:LANE_RANK + TOP_K].astype(i32)
    n = counts[:, :, 0].astype(i32)
    n = (n + GROUP - 1) // GROUP * GROUP
    src_row = jnp.cumsum(n, axis=1) - n
    seg = jnp.sum(n, axis=0)
    tiles_e = (seg + TM - 1) // TM
    tile_end = jnp.cumsum(tiles_e)
    e_start = (tile_end - tiles_e) * TM
    n_active = tile_end[-1]
    dst_row = e_start[None, :] + jnp.cumsum(n, axis=0) - n

    onehot = eid[..., None] == experts
    lpos = jnp.sum(jnp.where(onehot, src_row[:, None, None, :], 0), axis=-1) + rank
    lpos_t = jnp.concatenate([lpos.astype(F32), route[:, LANE_P:LANE_P + TOP_K]], axis=1)

    n_tiles = (nb * RB + TM - 1) // TM + N_EXPERTS
    t = jnp.minimum(jnp.arange(n_tiles, dtype=i32), n_active - 1)
    tile_expert = jnp.sum((t[:, None] >= tile_end[None, :]).astype(i32), axis=1)
    tile_expert = jnp.minimum(tile_expert, N_EXPERTS - 1)
    tile_first = jnp.concatenate(
        [jnp.ones((1,), i32), (tile_expert[1:] != tile_expert[:-1]).astype(i32)])
    has_rows = tiles_e > 0
    later = jnp.where(has_rows[None, :] & (experts[None, :] > experts[:, None]),
                      experts[None, :], N_EXPERTS)
    next_e = jnp.min(later, axis=1)
    next_e = jnp.where(next_e == N_EXPERTS, -1, next_e)
    group = jnp.cumsum(has_rows.astype(i32)) - 1
    pick = tile_expert[:, None] == experts[None, :]
    tile_next = jnp.sum(jnp.where(pick, next_e[None, :], 0), axis=1)
    tile_slot = jnp.sum(jnp.where(pick, group[None, :], 0), axis=1) % 2
    return dict(lpos=lpos.transpose(0, 2, 1).reshape(nb * TR, TOP_K), lpos_t=lpos_t, src_row=src_row, n_rows=n,
                dst_row=dst_row, block_rows=jnp.sum(n, axis=1), tail_start=e_start + seg,
                tail_len=tile_end * TM - e_start - seg,
                tile_expert=tile_expert, n_active=n_active.reshape(1), tile_first=tile_first,
                tile_next=tile_next, tile_slot=tile_slot,
                n_sorted=n_tiles * TM)


def _grid_sincos_parts():
    quarter = D_MODEL // 4
    omega = 1.0 / (10000.0 ** (jnp.arange(quarter, dtype=F32) / quarter))

    def emb1d(n):
        ang = jnp.arange(n, dtype=F32)[:, None] * omega[None, :]
        return jnp.concatenate([jnp.sin(ang), jnp.cos(ang)], axis=-1)

    er = emb1d(SEQ // GRID_W)
    ec = jnp.repeat(emb1d(GRID_W), BATCH, axis=0)
    return er.reshape(SEQ // GRID_W, 1, D_MODEL // 2), ec


def _gate_blocks(w):
    per = GATE_BLK // LRU_HEAD_DIM
    w = w.reshape(2, D_LRU // GATE_BLK, per, LRU_HEAD_DIM, LRU_HEAD_DIM)
    eye = jnp.eye(per, dtype=w.dtype)
    blk = jnp.einsum('dkpij,pq->dkpiqj', w, eye)
    return blk.reshape(2, D_LRU // GATE_BLK, GATE_BLK, GATE_BLK)


def kernel(x, c, ctx, c_ctx, w_mod, b_mod, w_in, conv_w, conv_b, gate_a_w, gate_a_b,
           gate_x_w, gate_x_b, lru_lambda, pool_w, pool_b, pool_scale, w_out, ln1_g, ln1_b,
           router_w, router_b, exp_w1, exp_b1, exp_w2, exp_b2, ln2_g, ln2_b):
    cvec = jnp.concatenate([c, c_ctx[None], jnp.zeros((2 * SUBLANES - BATCH - 1, D_MODEL), F32)])
    mod_all = _modulation(cvec, w_mod, b_mod)
    mods = jnp.stack([jnp.broadcast_to(mod_all[:, BATCH:BATCH + 1], (DEPTH, BATCH, 6 * D_MODEL)),
                      mod_all[:, :BATCH]], axis=1)
    tri = jnp.triu(jnp.ones((TR, TR), F32), 1).astype(BF16)
    inv_cnt = _pool_inverse_counts()

    er, ec = _grid_sincos_parts()
    xs, z = _entry(ctx, x, er, ec, mods[0], w_in)
    for l in range(DEPTH):
        last = l == DEPTH - 1
        t_off = CTX_TILES if last else 0
        mod = mods[l]
        gate_w = jnp.stack([_gate_blocks(gate_a_w[l]), _gate_blocks(gate_x_w[l])],
                           axis=1).astype(BF16)
        gate_b = jnp.stack([gate_a_b[l], gate_x_b[l]], axis=1)
        hf, hr = _scan(z, conv_w[l], conv_b[l][None], gate_w, gate_b, lru_lambda[l])
        rw_pad = jnp.zeros((D_MODEL, LANES), F32).at[:, :N_EXPERTS].set(router_w[l])
        rw_hi = rw_pad.astype(BF16)
        rw_pad = jnp.concatenate([rw_hi, (rw_pad - rw_hi.astype(F32)).astype(BF16)], axis=1)
        rb_pad = jnp.full((1, LANES), -1e30, F32).at[0, :N_EXPERTS].set(router_b[l])
        xs, route, counts = _mixer(
            l, t_off, hf, hr, z, inv_cnt, xs, mod, pool_w, pool_b[l][None], pool_scale[l][None], w_out,
            ln1_g[l][None], ln1_b[l][None], rw_pad, rb_pad, tri)
        plan = _routing_plan(route, counts)
        x_sorted = _dispatch(t_off, plan, xs, mod)
        y_sorted = _experts(l, plan, x_sorted, exp_w1, exp_b1, exp_w2, exp_b2)
        if last:
            return _combine(t_off, plan, xs, mod, ln2_g[l][None], ln2_b[l][None], y_sorted)
        xs, z = _combine(t_off, plan, xs, mod, ln2_g[l][None], ln2_b[l][None], y_sorted,
                         (mods[l + 1], w_in, l + 1))
```

```python
import functools

import jax
import jax.numpy as jnp
from jax import lax
from jax.experimental import pallas as pl
from jax.experimental.pallas import tpu as pltpu

D_MODEL = 1024
BATCH = 8
SEQ = 2048
DEPTH = 2
GRID_W = 64
CTX_LEN = 256
D_LRU = 512
N_LRU_HEADS = 8
LRU_HEAD_DIM = D_LRU // N_LRU_HEADS
LRU_C = 8.0
D_POOL = 512
POOL_WINDOWS = (2, 4, 8, 16)
POOL_GROUP_DIM = D_POOL // len(POOL_WINDOWS)
D_IN = 2 * D_LRU + D_POOL
N_EXPERTS = 32
TOP_K = 4
D_EXPERT = D_MODEL
SWIGLU_LIMIT = 7.0
SWIGLU_ALPHA = 1.702
LN_EPS = 1e-5
ALPHA = (2.0 * DEPTH) ** 0.25

F32 = jnp.float32
BF16 = jnp.bfloat16

SUBLANES = 8
LANES = 128
TT = GRID_W
TR = TT * BATCH
T_ALL = CTX_LEN + SEQ
R_ALL = T_ALL * BATCH
N_TILES = T_ALL // TT
CTX_TILES = CTX_LEN // TT
GATE_BLK = 256
POOL_HALO = max(POOL_WINDOWS) // 2
TM = 512
FFN_CHUNK = 1024
GROUP = SUBLANES
RB = TOP_K * TR + N_EXPERTS * GROUP
KB = 256
MB = 512
RB_STAGE = (RB + MB - 1) // MB * MB
LANE_P, LANE_EID, LANE_RANK = 0, 4, 8
ROUTE_ROWS = 16
V7X_VMEM_BYTES = 64 * 1024 * 1024
VMEM_LIMIT = V7X_VMEM_BYTES * 7 // 8


def _cparams(sem):
    return pltpu.CompilerParams(dimension_semantics=sem, vmem_limit_bytes=VMEM_LIMIT)


def _sigmoid(x):
    return 0.5 * (1.0 + jnp.tanh(0.5 * x))


def _layer_norm(x):
    mu = jnp.mean(x, axis=-1, keepdims=True)
    xc = x - mu
    var = jnp.mean(xc * xc, axis=-1, keepdims=True)
    return xc * lax.rsqrt(var + LN_EPS)


def _per_batch(x, v, op):
    r, d = x.shape
    return op(x.reshape(r // BATCH, BATCH, d), v[None]).reshape(r, d)


def _mod_kernel(c_ref, w_ref, b_ref, o_ref):
    c = c_ref[...]
    s = c * _sigmoid(c)
    w = w_ref[...]
    s_hi, w_hi = s.astype(BF16), w.astype(BF16)
    s_lo = (s - s_hi.astype(F32)).astype(BF16)
    w_lo = (w - w_hi.astype(F32)).astype(BF16)
    o_ref[...] = (jnp.dot(s_hi, w_hi, preferred_element_type=F32)
                  + jnp.dot(s_lo, w_hi, preferred_element_type=F32)
                  + jnp.dot(s_hi, w_lo, preferred_element_type=F32)) + b_ref[...]


def _modulation(cvec, w_mod, b_mod):
    tn = 512
    return pl.pallas_call(
        _mod_kernel,
        grid=(DEPTH, 6 * D_MODEL // tn),
        in_specs=[
            pl.BlockSpec((2 * SUBLANES, D_MODEL), lambda l, j: (0, 0)),
            pl.BlockSpec((None, D_MODEL, tn), lambda l, j: (l, 0, j)),
            pl.BlockSpec((None, 1, tn), lambda l, j: (l, 0, j)),
        ],
        out_specs=pl.BlockSpec((None, 2 * SUBLANES, tn), lambda l, j: (l, 0, j)),
        out_shape=jax.ShapeDtypeStruct((DEPTH, 2 * SUBLANES, 6 * D_MODEL), F32),
        compiler_params=_cparams(("arbitrary", "arbitrary")),
        name="modulation",
    )(cvec, w_mod, b_mod.reshape(DEPTH, 1, 6 * D_MODEL))


def _modulate(x, sh_ref, sc_ref):
    h = _per_batch(x, 1.0 + sc_ref[0], jnp.multiply)
    return _per_batch(h, sh_ref[0], jnp.add)


def _in_projection(x, sh_ref, sc_ref, w_ref):
    return jnp.dot(_modulate(x, sh_ref, sc_ref).astype(BF16), w_ref[...],
                   preferred_element_type=F32)


def _cast_once(w_ref, wbf_ref):
    @pl.when(pl.program_id(0) == 0)
    def _():
        wbf_ref[...] = w_ref[...].astype(BF16)


def _entry_kernel(ctx_ref, x_ref, er_ref, ec_ref, sh_ref, sc_ref, w_ref, o_ref, z_ref, wbf_ref):
    i = pl.program_id(0)
    _cast_once(w_ref, wbf_ref)

    def time_major(src_ref):
        return jnp.transpose(src_ref[...], (1, 0, 2)).reshape(TR, D_MODEL)

    @pl.when(i < CTX_TILES)
    def _():
        o_ref[...] = _layer_norm(time_major(ctx_ref))

    @pl.when(i >= CTX_TILES)
    def _():
        pos = jnp.concatenate(
            [jnp.broadcast_to(er_ref[0], (TR, D_MODEL // 2)), ec_ref[...]], axis=1)
        o_ref[...] = _layer_norm(time_major(x_ref) + pos)

    z_ref[...] = _in_projection(o_ref[...], sh_ref, sc_ref, wbf_ref)


def _mod_spec(chunk, t_off=0):
    return pl.BlockSpec((1, BATCH, D_MODEL),
                        lambda i, *_: ((i + t_off >= CTX_TILES).astype(jnp.int32), 0, chunk))


def _entry(ctx, x, er, ec, mod, w_in):
    return pl.pallas_call(
        _entry_kernel,
        grid=(N_TILES,),
        in_specs=[
            pl.BlockSpec((BATCH, TT, D_MODEL), lambda i: (0, jnp.minimum(i, CTX_TILES - 1), 0)),
            pl.BlockSpec((BATCH, TT, D_MODEL), lambda i: (0, jnp.maximum(i - CTX_TILES, 0), 0)),
            pl.BlockSpec((1, 1, D_MODEL // 2), lambda i: (jnp.maximum(i - CTX_TILES, 0), 0, 0)),
            pl.BlockSpec((TR, D_MODEL // 2), lambda i: (0, 0)),
            _mod_spec(0), _mod_spec(1),
            pl.BlockSpec((None, D_MODEL, D_IN), lambda i: (0, 0, 0)),
        ],
        out_specs=[pl.BlockSpec((TR, D_MODEL), lambda i: (i, 0)),
                   pl.BlockSpec((TR, D_IN), lambda i: (i, 0))],
        out_shape=[jax.ShapeDtypeStruct((R_ALL, D_MODEL), F32),
                   jax.ShapeDtypeStruct((R_ALL, D_IN), F32)],
        scratch_shapes=[pltpu.VMEM((D_MODEL, D_IN), BF16)],
        compiler_params=_cparams(("arbitrary",)),
        name="entry_ln",
    )(ctx, x, er, ec, mod, mod, w_in)


def _rev_tile(i):
    return jnp.where(i < CTX_TILES, CTX_TILES - 1 - i, N_TILES - 1 + CTX_TILES - i)


def _block_diag_dot(u_bf, w_ref, d, g):
    halves = [jnp.dot(u_bf[:, k * GATE_BLK:(k + 1) * GATE_BLK], w_ref[d, g, k],
                      preferred_element_type=F32) for k in range(D_LRU // GATE_BLK)]
    return jnp.concatenate(halves, axis=1)


def _lru_coeffs(tile, z_ref, zp_ref, zn_ref, cw_ref, cb_ref, gw_ref, gb_ref, lam_ref, d):
    x = z_ref[...]
    seg_first = (tile == 0) | (tile == CTX_TILES)
    seg_last = (tile == CTX_TILES - 1) | (tile == N_TILES - 1)
    prev = jnp.where(seg_first, 0.0, zp_ref[...])
    nxt = jnp.where(seg_last, 0.0, zn_ref[...])
    xm2 = jnp.concatenate([prev, x[:-2 * BATCH]], axis=0)
    xm1 = jnp.concatenate([prev[BATCH:], x[:-BATCH]], axis=0)
    xp1 = jnp.concatenate([x[BATCH:], nxt], axis=0)
    u = (cb_ref[...] + xm2 * cw_ref[0:1] + xm1 * cw_ref[1:2]
         + x * cw_ref[2:3] + xp1 * cw_ref[3:4])
    u_bf = u.astype(BF16)
    r = _sigmoid(_block_diag_dot(u_bf, gw_ref, d, 0) + gb_ref[d, 0:1])
    ig = _sigmoid(_block_diag_dot(u_bf, gw_ref, d, 1) + gb_ref[d, 1:2])
    nl = -lam_ref[d:d + 1]
    softplus = jnp.maximum(nl, 0.0) + jnp.log(1.0 + jnp.exp(-jnp.abs(nl)))
    log_a = (-LRU_C) * r * softplus
    a = jnp.exp(log_a)
    gap = 1.0 - a * a
    mult = jnp.where(gap > 0.0, gap * lax.rsqrt(gap), 0.0)
    return a, mult * (ig * u)


def _scan_kernel(zf_ref, zfp_ref, zfn_ref, zr_ref, zrp_ref, zrn_ref,
                 cw_ref, cb_ref, gw_ref, gb_ref, lam_ref,
                 hf_ref, hr_ref, state_ref):
    i = pl.program_id(0)

    @pl.when(i == 0)
    def _():
        state_ref[...] = jnp.zeros_like(state_ref)

    af, bf = _lru_coeffs(i, zf_ref, zfp_ref, zfn_ref, cw_ref, cb_ref, gw_ref, gb_ref,
                         lam_ref, 0)
    ar, br = _lru_coeffs(_rev_tile(i), zr_ref, zrp_ref, zrn_ref, cw_ref, cb_ref, gw_ref,
                         gb_ref, lam_ref, 1)
    hf = state_ref[0]
    hr = state_ref[1]
    for s in range(TT):
        f = slice(s * BATCH, (s + 1) * BATCH)
        hf = af[f] * hf + bf[f]
        hf_ref[f, :] = hf
        b = slice((TT - 1 - s) * BATCH, (TT - s) * BATCH)
        hr = ar[b] * hr + br[b]
        hr_ref[b, :] = hr
    state_ref[0] = hf
    state_ref[1] = hr


def _scan(z, conv_w, conv_b, gate_w, gate_b, lam):
    prev_rows = 2 * BATCH
    tiles_per_prev = TR // prev_rows
    tiles_per_next = TR // BATCH
    last_next = R_ALL // BATCH - 1

    def cur(f):
        return pl.BlockSpec((TR, D_LRU), lambda i: (f(i), 0))

    def prev(f):
        return pl.BlockSpec((prev_rows, D_LRU),
                            lambda i: (jnp.maximum(f(i) * tiles_per_prev - 1, 0), 0))

    def nxt(f):
        return pl.BlockSpec((BATCH, D_LRU),
                            lambda i: (jnp.minimum((f(i) + 1) * tiles_per_next, last_next), 0))

    fwd = lambda i: i
    const = lambda shape: pl.BlockSpec(shape, lambda i: (0,) * len(shape))
    return pl.pallas_call(
        _scan_kernel,
        grid=(N_TILES,),
        in_specs=[cur(fwd), prev(fwd), nxt(fwd), cur(_rev_tile), prev(_rev_tile), nxt(_rev_tile),
                  const((4, D_LRU)), const((1, D_LRU)),
                  const((2, 2, D_LRU // GATE_BLK, GATE_BLK, GATE_BLK)),
                  const((2, 2, D_LRU)), const((2, D_LRU))],
        out_specs=[pl.BlockSpec((TR, D_LRU), lambda i: (i, 0)),
                   pl.BlockSpec((TR, D_LRU), lambda i: (_rev_tile(i), 0))],
        out_shape=[jax.ShapeDtypeStruct((R_ALL, D_LRU), F32)] * 2,
        scratch_shapes=[pltpu.VMEM((2, BATCH, D_LRU), F32)],
        compiler_params=_cparams(("arbitrary",)),
        name="lru_scan",
    )(z, z, z, z, z, z, conv_w, conv_b, gate_w, gate_b, lam)


def _gelu_tanh(y):
    c = 0.7978845608028654
    return 0.5 * y * (1.0 + jnp.tanh(c * (y + 0.044715 * (y * y * y))))


def _pool_inverse_counts():
    t = jnp.arange(TT)[None, :, None]
    kind = jnp.arange(CTX_TILES + 1)[:, None, None]
    lo = jnp.where(kind < CTX_TILES, -TT * kind, 0)
    hi = jnp.where(kind < CTX_TILES, CTX_LEN - TT * kind, TT)
    half = (jnp.asarray(POOL_WINDOWS) // 2)[None, None, :]
    cnt = jnp.minimum(t + half, hi) - jnp.maximum(t - half, lo)
    inv = 1.0 / cnt.astype(F32)
    inv = jnp.repeat(jnp.repeat(inv, POOL_GROUP_DIM, axis=2), BATCH, axis=1)
    return inv


def _pool_groups(tile, xp_ref, xpp_ref, xpn_ref, inv_ref):
    in_ctx = tile < CTX_TILES
    prev_ok = in_ctx & (tile > 0)
    next_ok = in_ctx & (tile < CTX_TILES - 1)
    x = xp_ref[...]
    p = jnp.concatenate([jnp.where(prev_ok, xpp_ref[...], 0.0), x,
                         jnp.where(next_ok, xpn_ref[...], 0.0)], axis=0)
    p = p.reshape(TT + 2 * POOL_HALO, BATCH, D_POOL)
    inv = inv_ref[...].reshape(TT, BATCH, D_POOL)
    outs = []
    for g, win in enumerate(POOL_WINDOWS):
        half = win // 2
        acc = p[:, :, g * POOL_GROUP_DIM:(g + 1) * POOL_GROUP_DIM]
        width = 1
        while width < win:
            acc = acc[:acc.shape[0] - width] + acc[width:]
            width *= 2
        start = POOL_HALO - half
        wsum = acc[start:start + TT]
        lanes = slice(g * POOL_GROUP_DIM, (g + 1) * POOL_GROUP_DIM)
        centre = p[POOL_HALO:POOL_HALO + TT, :, lanes]
        outs.append((wsum * inv[:, :, lanes] - centre).reshape(TR, POOL_GROUP_DIM))
    return outs


def _mixer_kernel(t_off, hf_ref, hr_ref, y_ref, xp_ref, xpp_ref, xpn_ref, inv_ref, x_ref,
                  g1_ref, sh2_ref, sc2_ref, pw_ref, pb_ref, ps_ref, wo_ref,
                  lg_ref, lb_ref, rw_ref, rb_ref, tri_ref,
                  xo_ref, route_ref, cnt_ref, wobf_ref):
    i = pl.program_id(0)
    tile = i + t_off
    _cast_once(wo_ref, wobf_ref)

    lru = (hf_ref[...] + hr_ref[...]) * _gelu_tanh(y_ref[...])
    diffs = _pool_groups(tile, xp_ref, xpp_ref, xpn_ref, inv_ref)
    pooled = jnp.concatenate(
        [jnp.dot(d.astype(BF16), pw_ref[g].astype(BF16), preferred_element_type=F32)
         for g, d in enumerate(diffs)], axis=1)
    pooled = (pooled + pb_ref[...]) * ps_ref[...]
    mix = (jnp.dot(lru.astype(BF16), wobf_ref[0:D_LRU, :], preferred_element_type=F32)
           + jnp.dot(pooled.astype(BF16), wobf_ref[D_LRU:, :], preferred_element_type=F32))
    x = ALPHA * x_ref[...] + _per_batch(mix, g1_ref[0], jnp.multiply)
    x = _layer_norm(x) * lg_ref[...] + lb_ref[...]
    xo_ref[...] = x
    h2 = _modulate(x, sh2_ref, sc2_ref)

    h_hi = h2.astype(BF16)
    h_lo = (h2 - h_hi.astype(F32)).astype(BF16)
    by_hi = jnp.dot(h_hi, rw_ref[...], preferred_element_type=F32)
    logits = (by_hi[:, :LANES] + by_hi[:, LANES:]
              + jnp.dot(h_lo, rw_ref[:, 0:LANES], preferred_element_type=F32)) + rb_ref[...]
    work = logits.T[0:N_EXPERTS, :]
    expert = lax.broadcasted_iota(jnp.int32, (N_EXPERTS, TR), 0).astype(F32)
    vals, idxs, sels = [], [], []
    for _ in range(TOP_K):
        m = jnp.max(work, axis=0, keepdims=True)
        idx = jnp.min(jnp.where(work == m, expert, float(N_EXPERTS)), axis=0, keepdims=True)
        sel = expert == idx
        vals.append(m)
        idxs.append(idx)
        sels.append(sel)
        work = jnp.where(sel, -jnp.inf, work)
    exps = [jnp.exp(v - vals[0]) for v in vals]
    denom = exps[0] + exps[1] + exps[2] + exps[3]
    chosen = jnp.zeros((N_EXPERTS, TR), F32)
    for sel in sels:
        chosen = chosen + sel.astype(F32)
    before = jnp.dot(chosen.astype(BF16), tri_ref[...], preferred_element_type=F32)
    ranks = [jnp.sum(jnp.where(sel, before, 0.0), axis=0, keepdims=True) for sel in sels]
    route_ref[0] = jnp.concatenate(
        [e / denom for e in exps] + idxs + ranks
        + [jnp.zeros((ROUTE_ROWS - 3 * TOP_K, TR), F32)], axis=0)
    cnt_ref[0] = jnp.broadcast_to(jnp.sum(chosen, axis=1, keepdims=True), (N_EXPERTS, LANES))


def _mixer(layer, t_off, hf, hr, z, inv_cnt, x, mod, pool_w, pool_b, pool_scale, w_out,
           ln_g, ln_b, rw_pad, rb_pad, tri):
    n = N_TILES - t_off
    halo_rows = POOL_HALO * BATCH
    per = TR // halo_rows
    last_halo = R_ALL // halo_rows - 1
    xp_col = 2 * D_LRU // D_POOL
    row = lambda w: pl.BlockSpec((TR, w), lambda i: (i + t_off, 0))
    out_row = lambda w: pl.BlockSpec((TR, w), lambda i: (i, 0))
    const = lambda shape: pl.BlockSpec(shape, lambda i: (0,) * len(shape))

    def mod_spec(chunk):
        return pl.BlockSpec((1, BATCH, D_MODEL),
                            lambda i: ((i + t_off >= CTX_TILES).astype(jnp.int32), 0, chunk))

    return pl.pallas_call(
        functools.partial(_mixer_kernel, t_off),
        grid=(n,),
        in_specs=[
            row(D_LRU), row(D_LRU),
            pl.BlockSpec((TR, D_LRU), lambda i: (i + t_off, 1)),
            pl.BlockSpec((TR, D_POOL), lambda i: (i + t_off, xp_col)),
            pl.BlockSpec((halo_rows, D_POOL),
                         lambda i: (jnp.maximum((i + t_off) * per - 1, 0), xp_col)),
            pl.BlockSpec((halo_rows, D_POOL),
                         lambda i: (jnp.minimum((i + t_off + 1) * per, last_halo), xp_col)),
            pl.BlockSpec((None, TR, D_POOL), lambda i: (jnp.minimum(i + t_off, CTX_TILES), 0, 0)),
            row(D_MODEL),
            mod_spec(2), mod_spec(3), mod_spec(4),
            pl.BlockSpec((None, len(POOL_WINDOWS), POOL_GROUP_DIM, POOL_GROUP_DIM),
                         lambda i: (layer, 0, 0, 0)),
            const((1, D_POOL)), const((1, D_POOL)),
            pl.BlockSpec((None, D_MODEL, D_MODEL), lambda i: (layer, 0, 0)),
            const((1, D_MODEL)), const((1, D_MODEL)),
            const((D_MODEL, 2 * LANES)), const((1, LANES)),
            const((TR, TR)),
        ],
        out_specs=[out_row(D_MODEL),
                   pl.BlockSpec((1, ROUTE_ROWS, TR), lambda i: (i, 0, 0)),
                   pl.BlockSpec((1, N_EXPERTS, LANES), lambda i: (i, 0, 0))],
        out_shape=[jax.ShapeDtypeStruct((n * TR, D_MODEL), F32),
                   jax.ShapeDtypeStruct((n, ROUTE_ROWS, TR), F32),
                   jax.ShapeDtypeStruct((n, N_EXPERTS, LANES), F32)],
        scratch_shapes=[pltpu.VMEM((D_MODEL, D_MODEL), BF16)],
        compiler_params=_cparams(("arbitrary",)),
        name="mixer_out",
    )(hf, hr, z, z, z, z, inv_cnt, x, mod, mod, mod, pool_w, pool_b, pool_scale, w_out,
      ln_g, ln_b, rw_pad, rb_pad, tri)


def _one_hot_rows(targets, index, weights=None):
    matches = [index == t for t in targets]
    hit = jnp.zeros(index.shape, F32)
    for m in reversed(matches):
        hit = jnp.where(m, 1.0, hit)
    if weights is None:
        return hit
    picked = jnp.zeros(index.shape, F32)
    for m, w in zip(reversed(matches), reversed(weights)):
        picked = jnp.where(m, w, picked)
    return hit, picked


def _for_each_group(block, len_ref, body):
    def step(e, carry):
        n = pl.multiple_of(len_ref[block, e], GROUP)

        @pl.when(n > 0)
        def _():
            body(e, n)

        return carry

    lax.fori_loop(0, N_EXPERTS, step, 0)


def _group(start, n):
    return pl.ds(pl.multiple_of(start, GROUP), n)


def _dispatch_kernel(src_ref, len_ref, dst_ref, tot_ref, tail_s_ref, tail_n_ref, na_ref,
                     lpos_ref, x_ref, sh2_ref, sc2_ref, xs_ref, stage_ref, zero_ref, sems):
    i = pl.program_id(0)
    n_blocks = pl.num_programs(0) - 1
    slot = i % 2

    @pl.when(i < n_blocks)
    def _():
        lpos = lpos_ref[...]
        targets = [lpos[k:k + 1, :] for k in range(TOP_K)]
        h = _modulate(x_ref[...], sh2_ref, sc2_ref).astype(BF16)

        def permute(jb, carry):
            base = pl.multiple_of(jb * MB, MB)
            rows = (lax.broadcasted_iota(jnp.int32, (MB, TR), 0) + base).astype(F32)
            sel = _one_hot_rows(targets, rows).astype(BF16)
            stage_ref[slot, pl.ds(base, MB), :] = jnp.dot(sel, h, preferred_element_type=F32)
            return carry

        lax.fori_loop(0, (tot_ref[i] + MB - 1) // MB, permute, 0)
        _for_each_group(i, len_ref, lambda e, n: pltpu.make_async_copy(
            stage_ref.at[slot, _group(src_ref[i, e], n)],
            xs_ref.at[_group(dst_ref[i, e], n)], sems.at[slot]).start())

    @pl.when(i > 0)
    def _():
        rows = _group(0, pl.multiple_of(tot_ref[i - 1], GROUP))
        pltpu.make_async_copy(stage_ref.at[1 - slot, rows], xs_ref.at[rows],
                              sems.at[1 - slot]).wait()

    @pl.when(i == n_blocks)
    def _():
        zero_ref[...] = jnp.zeros_like(zero_ref)
        fill = sems.at[2]

        def expert_tail(e, n):
            return pltpu.make_async_copy(zero_ref.at[_group(0, n)],
                                         xs_ref.at[_group(tail_s_ref[0, e], n)], fill)

        def whole_tile(t):
            return pltpu.make_async_copy(zero_ref, xs_ref.at[_group(t * TM, TM)], fill)

        n_tiles = xs_ref.shape[0] // TM
        _for_each_group(0, tail_n_ref, lambda e, n: expert_tail(e, n).start())
        lax.fori_loop(na_ref[0], n_tiles, lambda t, c: (whole_tile(t).start(), c)[1], 0)
        _for_each_group(0, tail_n_ref, lambda e, n: expert_tail(e, n).wait())
        lax.fori_loop(na_ref[0], n_tiles, lambda t, c: (whole_tile(t).wait(), c)[1], 0)


def _dispatch(t_off, plan, x, mod):
    n = x.shape[0] // TR
    last = n - 1
    grid_spec = pltpu.PrefetchScalarGridSpec(
        num_scalar_prefetch=7,
        grid=(n + 1,),
        in_specs=[
            pl.BlockSpec((None, SUBLANES, TR), lambda i, *_: (jnp.minimum(i, last), 0, 0)),
            pl.BlockSpec((TR, D_MODEL), lambda i, *_: (jnp.minimum(i, last), 0)),
            _mod_spec(3, t_off), _mod_spec(4, t_off),
        ],
        out_specs=pl.BlockSpec(memory_space=pl.ANY),
        scratch_shapes=[pltpu.VMEM((2, RB_STAGE, D_MODEL), F32),
                        pltpu.VMEM((TM, D_MODEL), F32),
                        pltpu.SemaphoreType.DMA((3,))],
    )
    return pl.pallas_call(
        _dispatch_kernel,
        grid_spec=grid_spec,
        out_shape=jax.ShapeDtypeStruct((plan["n_sorted"], D_MODEL), F32),
        compiler_params=_cparams(("arbitrary",)),
        name="moe_dispatch",
    )(plan["src_row"], plan["n_rows"], plan["dst_row"], plan["block_rows"],
      plan["tail_start"][None], plan["tail_len"][None], plan["n_active"], plan["lpos_t"],
      x, mod, mod)


def _expert_kernel(layer, te_ref, na_ref, first_ref, nxt_ref, slot_ref,
                   x_ref, b1_ref, b2_ref, w1_hbm, w2_hbm, o_ref,
                   w1buf_ref, w2buf_ref, w1c_ref, w2c_ref, sems):
    i = pl.program_id(0)
    e = te_ref[i]
    slot = slot_ref[i]

    def fetch(expert, s):
        return (pltpu.make_async_copy(w1_hbm.at[layer, expert], w1buf_ref.at[s], sems.at[s, 0]),
                pltpu.make_async_copy(w2_hbm.at[layer, expert], w2buf_ref.at[s], sems.at[s, 1]))

    @pl.when(i == 0)
    def _():
        for copy in fetch(e, slot):
            copy.start()

    @pl.when(first_ref[i] == 1)
    def _():
        for copy in fetch(e, slot):
            copy.wait()
        w1c_ref[...] = w1buf_ref[slot].astype(BF16)
        w2c_ref[...] = w2buf_ref[slot].astype(BF16)

        @pl.when(nxt_ref[i] >= 0)
        def _():
            for copy in fetch(nxt_ref[i], 1 - slot):
                copy.start()

    @pl.when(i < na_ref[0])
    def _():
        x = x_ref[...].astype(BF16)
        y = jnp.broadcast_to(b2_ref[...], (TM, D_MODEL))
        for c in range(D_EXPERT // FFN_CHUNK):
            g_cols = slice(c * FFN_CHUNK, (c + 1) * FFN_CHUNK)
            l_cols = slice(D_EXPERT + c * FFN_CHUNK, D_EXPERT + (c + 1) * FFN_CHUNK)
            glu = jnp.dot(x, w1c_ref[:, g_cols], preferred_element_type=F32) + b1_ref[:, g_cols]
            lin = jnp.dot(x, w1c_ref[:, l_cols], preferred_element_type=F32) + b1_ref[:, l_cols]
            glu = jnp.minimum(glu, SWIGLU_LIMIT)
            lin = jnp.clip(lin, -SWIGLU_LIMIT, SWIGLU_LIMIT)
            act = glu * _sigmoid(SWIGLU_ALPHA * glu) * (lin + 1.0)
            y = y + jnp.dot(act.astype(BF16), w2c_ref[g_cols, :], preferred_element_type=F32)
        o_ref[...] = y

    @pl.when(i >= na_ref[0])
    def _():
        o_ref[...] = jnp.zeros_like(o_ref)


def _experts(layer, plan, xs, w1, b1, w2, b2):
    n_tiles = xs.shape[0] // TM
    act_tile = lambda i, te, na, *_: (jnp.maximum(jnp.minimum(i, na[0] - 1), 0), 0)
    expert = lambda i, te, *_: (layer, te[i], 0, 0)
    grid_spec = pltpu.PrefetchScalarGridSpec(
        num_scalar_prefetch=5,
        grid=(n_tiles,),
        in_specs=[
            pl.BlockSpec((TM, D_MODEL), act_tile),
            pl.BlockSpec((None, None, 1, 2 * D_EXPERT), expert),
            pl.BlockSpec((None, None, 1, D_MODEL), expert),
            pl.BlockSpec(memory_space=pl.ANY),
            pl.BlockSpec(memory_space=pl.ANY),
        ],
        out_specs=pl.BlockSpec((TM, D_MODEL), lambda i, *_: (i, 0)),
        scratch_shapes=[pltpu.VMEM((2, D_MODEL, 2 * D_EXPERT), F32),
                        pltpu.VMEM((2, D_EXPERT, D_MODEL), F32),
                        pltpu.VMEM((D_MODEL, 2 * D_EXPERT), BF16),
                        pltpu.VMEM((D_EXPERT, D_MODEL), BF16),
                        pltpu.SemaphoreType.DMA((2, 2))],
    )
    return pl.pallas_call(
        functools.partial(_expert_kernel, layer),
        grid_spec=grid_spec,
        out_shape=jax.ShapeDtypeStruct((xs.shape[0], D_MODEL), F32),
        compiler_params=_cparams(("arbitrary",)),
        name="moe_experts",
    )(plan["tile_expert"], plan["n_active"], plan["tile_first"], plan["tile_next"],
      plan["tile_slot"], xs, b1.reshape(DEPTH, N_EXPERTS, 1, -1),
      b2.reshape(DEPTH, N_EXPERTS, 1, -1), w1, w2)


def _combine_kernel(last, src_ref, len_ref, dst_ref, tot_ref, lpos_ref, gate_ref, x_ref,
                    g2_ref, lg_ref, lb_ref, ys_ref, *rest):
    if last:
        o_ref, ybuf_ref, sems = rest
    else:
        sh_ref, sc_ref, w_ref, o_ref, z_ref, ybuf_ref, sems, wbf_ref = rest
        _cast_once(w_ref, wbf_ref)
    i = pl.program_id(0)
    n_blocks = pl.num_programs(0)
    slot = i % 2

    def start_block(block, s):
        _for_each_group(block, len_ref, lambda e, n: pltpu.make_async_copy(
            ys_ref.at[_group(dst_ref[block, e], n)],
            ybuf_ref.at[s, _group(src_ref[block, e], n)], sems.at[s]).start())

    @pl.when(i == 0)
    def _():
        ybuf_ref[...] = jnp.zeros_like(ybuf_ref)
        start_block(0, 0)

    @pl.when(i + 1 < n_blocks)
    def _():
        start_block(i + 1, 1 - slot)

    rows = _group(0, pl.multiple_of(tot_ref[i], GROUP))
    pltpu.make_async_copy(ys_ref.at[rows], ybuf_ref.at[slot, rows], sems.at[slot]).wait()
    lpos = lpos_ref[...]
    gate = gate_ref[...]
    targets = [jnp.broadcast_to(lpos[:, k:k + 1], (TR, KB)) for k in range(TOP_K)]
    weights = [jnp.broadcast_to(gate[:, k:k + 1], (TR, KB)) for k in range(TOP_K)]
    cols = lax.broadcasted_iota(jnp.int32, (TR, KB), 1)
    f = jnp.zeros((TR, D_MODEL), F32)
    for jb in range(RB // KB):
        _, picked = _one_hot_rows(targets, cols + jb * KB, weights)
        f = f + jnp.dot(picked.astype(BF16),
                        ybuf_ref[slot, jb * KB:(jb + 1) * KB, :].astype(BF16),
                        preferred_element_type=F32)
    x = ALPHA * x_ref[...] + _per_batch(f, g2_ref[0], jnp.multiply)
    out = _layer_norm(x) * lg_ref[...] + lb_ref[...]
    if last:
        o_ref[...] = jnp.transpose(out.reshape(TT, BATCH, D_MODEL), (1, 0, 2))
    else:
        o_ref[...] = out
        z_ref[...] = _in_projection(out, sh_ref, sc_ref, wbf_ref)


def _combine(t_off, plan, x, mod, ln_g, ln_b, ys, next_layer=None):
    n = x.shape[0] // TR
    last = next_layer is None
    row = lambda w: pl.BlockSpec((TR, w), lambda i, *_: (i, 0))
    const = lambda shape: pl.BlockSpec(shape, lambda i, *_: (0,) * len(shape))
    if last:
        extra_specs, extra_args, extra_scratch = [], (), []
        out_spec = pl.BlockSpec((BATCH, TT, D_MODEL), lambda i, *_: (0, i, 0))
        out_shape = jax.ShapeDtypeStruct((BATCH, n * TT, D_MODEL), F32)
    else:
        next_mod, w_in, next_l = next_layer
        extra_specs = [_mod_spec(0, t_off), _mod_spec(1, t_off),
                       pl.BlockSpec((None, D_MODEL, D_IN), lambda i, *_: (next_l, 0, 0))]
        extra_args = (next_mod, next_mod, w_in)
        extra_scratch = [pltpu.VMEM((D_MODEL, D_IN), BF16)]
        out_spec = [row(D_MODEL), row(D_IN)]
        out_shape = [jax.ShapeDtypeStruct((n * TR, D_MODEL), F32),
                     jax.ShapeDtypeStruct((n * TR, D_IN), F32)]
    grid_spec = pltpu.PrefetchScalarGridSpec(
        num_scalar_prefetch=4,
        grid=(n,),
        in_specs=[
            row(TOP_K), row(TOP_K), row(D_MODEL), _mod_spec(5, t_off),
            const((1, D_MODEL)), const((1, D_MODEL)),
            pl.BlockSpec(memory_space=pl.ANY),
        ] + extra_specs,
        out_specs=out_spec,
        scratch_shapes=[pltpu.VMEM((2, RB, D_MODEL), F32),
                        pltpu.SemaphoreType.DMA((2,))] + extra_scratch,
    )
    return pl.pallas_call(
        functools.partial(_combine_kernel, last),
        grid_spec=grid_spec,
        out_shape=out_shape,
        compiler_params=_cparams(("arbitrary",)),
        name="moe_combine",
    )(plan["src_row"], plan["n_rows"], plan["dst_row"], plan["block_rows"], plan["lpos"],
      plan["gate"], x, mod, ln_g, ln_b, ys, *extra_args)

def _routing_plan(route, counts):
    i32 = jnp.int32
    nb = counts.shape[0]
    experts = jnp.arange(N_EXPERTS, dtype=i32)
    eid = route[:, LANE_EID:LANE_EID + TOP_K].astype(i32)
    rank = route[:, LANE_RANK:LANE_RANK + TOP_K].astype(i32)
    n = counts[:, :, 0].astype(i32)
    n = (n + GROUP - 1) // GROUP * GROUP
    src_row = jnp.cumsum(n, axis=1) - n
    seg = jnp.sum(n, axis=0)
    tiles_e = (seg + TM - 1) // TM
    tile_end = jnp.cumsum(tiles_e)
    e_start = (tile_end - tiles_e) * TM
    n_active = tile_end[-1]
    dst_row = e_start[None, :] + jnp.cumsum(n, axis=0) - n

    onehot = eid[..., None] == experts
    lpos = jnp.sum(jnp.where(onehot, src_row[:, None, None, :], 0), axis=-1) + rank
    lpos_t = jnp.concatenate([lpos.astype(F32), route[:, LANE_P:LANE_P + TOP_K]], axis=1)

    n_tiles = (nb * RB + TM - 1) // TM + N_EXPERTS
    t = jnp.minimum(jnp.arange(n_tiles, dtype=i32), n_active - 1)
    tile_expert = jnp.sum((t[:, None] >= tile_end[None, :]).astype(i32), axis=1)
    tile_expert = jnp.minimum(tile_expert, N_EXPERTS - 1)
    tile_first = jnp.concatenate(
        [jnp.ones((1,), i32), (tile_expert[1:] != tile_expert[:-1]).astype(i32)])
    has_rows = tiles_e > 0
    later = jnp.where(has_rows[None, :] & (experts[None, :] > experts[:, None]),
                      experts[None, :], N_EXPERTS)
    next_e = jnp.min(later, axis=1)
    next_e = jnp.where(next_e == N_EXPERTS, -1, next_e)
    group = jnp.cumsum(has_rows.astype(i32)) - 1
    pick = tile_expert[:, None] == experts[None, :]
    tile_next = jnp.sum(jnp.where(pick, next_e[None, :], 0), axis=1)
    tile_slot = jnp.sum(jnp.where(pick, group[None, :], 0), axis=1) % 2
    gate = route[:, LANE_P:LANE_P + TOP_K].transpose(0, 2, 1).reshape(nb * TR, TOP_K)
    return dict(lpos=lpos.transpose(0, 2, 1).reshape(nb * TR, TOP_K), gate=gate, lpos_t=lpos_t, src_row=src_row, n_rows=n,
                dst_row=dst_row, block_rows=jnp.sum(n, axis=1), tail_start=e_start + seg,
                tail_len=tile_end * TM - e_start - seg,
                tile_expert=tile_expert, n_active=n_active.reshape(1), tile_first=tile_first,
                tile_next=tile_next, tile_slot=tile_slot,
                n_sorted=n_tiles * TM)


def _grid_sincos_parts():
    quarter = D_MODEL // 4
    omega = 1.0 / (10000.0 ** (jnp.arange(quarter, dtype=F32) / quarter))

    def emb1d(n):
        ang = jnp.arange(n, dtype=F32)[:, None] * omega[None, :]
        return jnp.concatenate([jnp.sin(ang), jnp.cos(ang)], axis=-1)

    er = emb1d(SEQ // GRID_W)
    ec = jnp.repeat(emb1d(GRID_W), BATCH, axis=0)
    return er.reshape(SEQ // GRID_W, 1, D_MODEL // 2), ec


def _gate_blocks(w):
    per = GATE_BLK // LRU_HEAD_DIM
    w = w.reshape(2, D_LRU // GATE_BLK, per, LRU_HEAD_DIM, LRU_HEAD_DIM)
    eye = jnp.eye(per, dtype=w.dtype)
    blk = jnp.einsum('dkpij,pq->dkpiqj', w, eye)
    return blk.reshape(2, D_LRU // GATE_BLK, GATE_BLK, GATE_BLK)


def kernel(x, c, ctx, c_ctx, w_mod, b_mod, w_in, conv_w, conv_b, gate_a_w, gate_a_b,
           gate_x_w, gate_x_b, lru_lambda, pool_w, pool_b, pool_scale, w_out, ln1_g, ln1_b,
           router_w, router_b, exp_w1, exp_b1, exp_w2, exp_b2, ln2_g, ln2_b):
    cvec = jnp.concatenate([c, c_ctx[None], jnp.zeros((2 * SUBLANES - BATCH - 1, D_MODEL), F32)])
    mod_all = _modulation(cvec, w_mod, b_mod)
    mods = jnp.stack([jnp.broadcast_to(mod_all[:, BATCH:BATCH + 1], (DEPTH, BATCH, 6 * D_MODEL)),
                      mod_all[:, :BATCH]], axis=1)
    tri = jnp.triu(jnp.ones((TR, TR), F32), 1).astype(BF16)
    inv_cnt = _pool_inverse_counts()

    er, ec = _grid_sincos_parts()
    xs, z = _entry(ctx, x, er, ec, mods[0], w_in)
    for l in range(DEPTH):
        last = l == DEPTH - 1
        t_off = CTX_TILES if last else 0
        mod = mods[l]
        gate_w = jnp.stack([_gate_blocks(gate_a_w[l]), _gate_blocks(gate_x_w[l])],
                           axis=1).astype(BF16)
        gate_b = jnp.stack([gate_a_b[l], gate_x_b[l]], axis=1)
        hf, hr = _scan(z, conv_w[l], conv_b[l][None], gate_w, gate_b, lru_lambda[l])
        rw_pad = jnp.zeros((D_MODEL, LANES), F32).at[:, :N_EXPERTS].set(router_w[l])
        rw_hi = rw_pad.astype(BF16)
        rw_pad = jnp.concatenate([rw_hi, (rw_pad - rw_hi.astype(F32)).astype(BF16)], axis=1)
        rb_pad = jnp.full((1, LANES), -1e30, F32).at[0, :N_EXPERTS].set(router_b[l])
        xs, route, counts = _mixer(
            l, t_off, hf, hr, z, inv_cnt, xs, mod, pool_w, pool_b[l][None], pool_scale[l][None], w_out,
            ln1_g[l][None], ln1_b[l][None], rw_pad, rb_pad, tri)
        plan = _routing_plan(route, counts)
        x_sorted = _dispatch(t_off, plan, xs, mod)
        y_sorted = _experts(l, plan, x_sorted, exp_w1, exp_b1, exp_w2, exp_b2)
        if last:
            return _combine(t_off, plan, xs, mod, ln2_g[l][None], ln2_b[l][None], y_sorted)
        xs, z = _combine(t_off, plan, xs, mod, ln2_g[l][None], ln2_b[l][None], y_sorted,
                         (mods[l + 1], w_in, l + 1))
```

```python
import functools

import jax
import jax.numpy as jnp
from jax import lax
from jax.experimental import pallas as pl
from jax.experimental.pallas import tpu as pltpu

D_MODEL = 1024
BATCH = 8
SEQ = 2048
DEPTH = 2
GRID_W = 64
CTX_LEN = 256
D_LRU = 512
N_LRU_HEADS = 8
LRU_HEAD_DIM = D_LRU // N_LRU_HEADS
LRU_C = 8.0
D_POOL = 512
POOL_WINDOWS = (2, 4, 8, 16)
POOL_GROUP_DIM = D_POOL // len(POOL_WINDOWS)
D_IN = 2 * D_LRU + D_POOL
N_EXPERTS = 32
TOP_K = 4
D_EXPERT = D_MODEL
SWIGLU_LIMIT = 7.0
SWIGLU_ALPHA = 1.702
LN_EPS = 1e-5
ALPHA = (2.0 * DEPTH) ** 0.25

F32 = jnp.float32
BF16 = jnp.bfloat16

SUBLANES = 8
LANES = 128
TT = GRID_W
TR = TT * BATCH
T_ALL = CTX_LEN + SEQ
R_ALL = T_ALL * BATCH
N_TILES = T_ALL // TT
CTX_TILES = CTX_LEN // TT
GATE_BLK = 256
POOL_HALO = max(POOL_WINDOWS) // 2
TM = 512
FFN_CHUNK = 1024
GROUP = SUBLANES
RB = TOP_K * TR + N_EXPERTS * GROUP
KB = 256
MB = 512
RB_STAGE = (RB + MB - 1) // MB * MB
LANE_P, LANE_EID, LANE_RANK = 0, 4, 8
ROUTE_ROWS = 16
V7X_VMEM_BYTES = 64 * 1024 * 1024
VMEM_LIMIT = V7X_VMEM_BYTES * 7 // 8


def _cparams(sem):
    return pltpu.CompilerParams(dimension_semantics=sem, vmem_limit_bytes=VMEM_LIMIT)


def _sigmoid(x):
    return 0.5 * (1.0 + jnp.tanh(0.5 * x))


def _layer_norm(x):
    mu = jnp.mean(x, axis=-1, keepdims=True)
    xc = x - mu
    var = jnp.mean(xc * xc, axis=-1, keepdims=True)
    return xc * lax.rsqrt(var + LN_EPS)


def _per_batch(x, v, op):
    r, d = x.shape
    return op(x.reshape(r // BATCH, BATCH, d), v[None]).reshape(r, d)


def _mod_kernel(c_ref, w_ref, b_ref, o_ref):
    c = c_ref[...]
    s = c * _sigmoid(c)
    w = w_ref[...]
    s_hi, w_hi = s.astype(BF16), w.astype(BF16)
    s_lo = (s - s_hi.astype(F32)).astype(BF16)
    w_lo = (w - w_hi.astype(F32)).astype(BF16)
    o_ref[...] = (jnp.dot(s_hi, w_hi, preferred_element_type=F32)
                  + jnp.dot(s_lo, w_hi, preferred_element_type=F32)
                  + jnp.dot(s_hi, w_lo, preferred_element_type=F32)) + b_ref[...]


def _modulation(cvec, w_mod, b_mod):
    tn = 512
    return pl.pallas_call(
        _mod_kernel,
        grid=(DEPTH, 6 * D_MODEL // tn),
        in_specs=[
            pl.BlockSpec((2 * SUBLANES, D_MODEL), lambda l, j: (0, 0)),
            pl.BlockSpec((None, D_MODEL, tn), lambda l, j: (l, 0, j)),
            pl.BlockSpec((None, 1, tn), lambda l, j: (l, 0, j)),
        ],
        out_specs=pl.BlockSpec((None, 2 * SUBLANES, tn), lambda l, j: (l, 0, j)),
        out_shape=jax.ShapeDtypeStruct((DEPTH, 2 * SUBLANES, 6 * D_MODEL), F32),
        compiler_params=_cparams(("arbitrary", "arbitrary")),
        name="modulation",
    )(cvec, w_mod, b_mod.reshape(DEPTH, 1, 6 * D_MODEL))


def _modulate(x, sh_ref, sc_ref):
    h = _per_batch(x, 1.0 + sc_ref[0], jnp.multiply)
    return _per_batch(h, sh_ref[0], jnp.add)


def _in_projection(x, sh_ref, sc_ref, w_ref):
    return jnp.dot(_modulate(x, sh_ref, sc_ref).astype(BF16), w_ref[...],
                   preferred_element_type=F32)


def _cast_once(w_ref, wbf_ref):
    @pl.when(pl.program_id(0) == 0)
    def _():
        wbf_ref[...] = w_ref[...].astype(BF16)


def _entry_kernel(ctx_ref, x_ref, er_ref, ec_ref, sh_ref, sc_ref, w_ref, o_ref, z_ref, wbf_ref):
    i = pl.program_id(0)
    _cast_once(w_ref, wbf_ref)

    def time_major(src_ref):
        return jnp.transpose(src_ref[...], (1, 0, 2)).reshape(TR, D_MODEL)

    @pl.when(i < CTX_TILES)
    def _():
        o_ref[...] = _layer_norm(time_major(ctx_ref))

    @pl.when(i >= CTX_TILES)
    def _():
        pos = jnp.concatenate(
            [jnp.broadcast_to(er_ref[0], (TR, D_MODEL // 2)), ec_ref[...]], axis=1)
        o_ref[...] = _layer_norm(time_major(x_ref) + pos)

    z_ref[...] = _in_projection(o_ref[...], sh_ref, sc_ref, wbf_ref)


def _mod_spec(chunk, t_off=0):
    return pl.BlockSpec((1, BATCH, D_MODEL),
                        lambda i, *_: ((i + t_off >= CTX_TILES).astype(jnp.int32), 0, chunk))


def _entry(ctx, x, er, ec, mod, w_in):
    return pl.pallas_call(
        _entry_kernel,
        grid=(N_TILES,),
        in_specs=[
            pl.BlockSpec((BATCH, TT, D_MODEL), lambda i: (0, jnp.minimum(i, CTX_TILES - 1), 0)),
            pl.BlockSpec((BATCH, TT, D_MODEL), lambda i: (0, jnp.maximum(i - CTX_TILES, 0), 0)),
            pl.BlockSpec((1, 1, D_MODEL // 2), lambda i: (jnp.maximum(i - CTX_TILES, 0), 0, 0)),
            pl.BlockSpec((TR, D_MODEL // 2), lambda i: (0, 0)),
            _mod_spec(0), _mod_spec(1),
            pl.BlockSpec((None, D_MODEL, D_IN), lambda i: (0, 0, 0)),
        ],
        out_specs=[pl.BlockSpec((TR, D_MODEL), lambda i: (i, 0)),
                   pl.BlockSpec((TR, D_IN), lambda i: (i, 0))],
        out_shape=[jax.ShapeDtypeStruct((R_ALL, D_MODEL), F32),
                   jax.ShapeDtypeStruct((R_ALL, D_IN), F32)],
        scratch_shapes=[pltpu.VMEM((D_MODEL, D_IN), BF16)],
        compiler_params=_cparams(("arbitrary",)),
        name="entry_ln",
    )(ctx, x, er, ec, mod, mod, w_in)


def _rev_tile(i):
    return jnp.where(i < CTX_TILES, CTX_TILES - 1 - i, N_TILES - 1 + CTX_TILES - i)


def _block_diag_dot(u_bf, w_ref, d, g):
    halves = [jnp.dot(u_bf[:, k * GATE_BLK:(k + 1) * GATE_BLK], w_ref[d, g, k],
                      preferred_element_type=F32) for k in range(D_LRU // GATE_BLK)]
    return jnp.concatenate(halves, axis=1)


def _lru_coeffs(tile, z_ref, zp_ref, zn_ref, cw_ref, cb_ref, gw_ref, gb_ref, lam_ref, d):
    x = z_ref[...]
    seg_first = (tile == 0) | (tile == CTX_TILES)
    seg_last = (tile == CTX_TILES - 1) | (tile == N_TILES - 1)
    prev = jnp.where(seg_first, 0.0, zp_ref[...])
    nxt = jnp.where(seg_last, 0.0, zn_ref[...])
    xm2 = jnp.concatenate([prev, x[:-2 * BATCH]], axis=0)
    xm1 = jnp.concatenate([prev[BATCH:], x[:-BATCH]], axis=0)
    xp1 = jnp.concatenate([x[BATCH:], nxt], axis=0)
    u = (cb_ref[...] + xm2 * cw_ref[0:1] + xm1 * cw_ref[1:2]
         + x * cw_ref[2:3] + xp1 * cw_ref[3:4])
    u_bf = u.astype(BF16)
    r = _sigmoid(_block_diag_dot(u_bf, gw_ref, d, 0) + gb_ref[d, 0:1])
    ig = _sigmoid(_block_diag_dot(u_bf, gw_ref, d, 1) + gb_ref[d, 1:2])
    nl = -lam_ref[d:d + 1]
    softplus = jnp.maximum(nl, 0.0) + jnp.log(1.0 + jnp.exp(-jnp.abs(nl)))
    log_a = (-LRU_C) * r * softplus
    a = jnp.exp(log_a)
    gap = 1.0 - a * a
    mult = jnp.where(gap > 0.0, gap * lax.rsqrt(gap), 0.0)
    return a, mult * (ig * u)


def _scan_kernel(zf_ref, zfp_ref, zfn_ref, zr_ref, zrp_ref, zrn_ref,
                 cw_ref, cb_ref, gw_ref, gb_ref, lam_ref,
                 hf_ref, hr_ref, state_ref):
    i = pl.program_id(0)

    @pl.when(i == 0)
    def _():
        state_ref[...] = jnp.zeros_like(state_ref)

    af, bf = _lru_coeffs(i, zf_ref, zfp_ref, zfn_ref, cw_ref, cb_ref, gw_ref, gb_ref,
                         lam_ref, 0)
    ar, br = _lru_coeffs(_rev_tile(i), zr_ref, zrp_ref, zrn_ref, cw_ref, cb_ref, gw_ref,
                         gb_ref, lam_ref, 1)
    hf = state_ref[0]
    hr = state_ref[1]
    for s in range(TT):
        f = slice(s * BATCH, (s + 1) * BATCH)
        hf = af[f] * hf + bf[f]
        hf_ref[f, :] = hf
        b = slice((TT - 1 - s) * BATCH, (TT - s) * BATCH)
        hr = ar[b] * hr + br[b]
        hr_ref[b, :] = hr
    state_ref[0] = hf
    state_ref[1] = hr


def _scan(z, conv_w, conv_b, gate_w, gate_b, lam):
    prev_rows = 2 * BATCH
    tiles_per_prev = TR // prev_rows
    tiles_per_next = TR // BATCH
    last_next = R_ALL // BATCH - 1

    def cur(f):
        return pl.BlockSpec((TR, D_LRU), lambda i: (f(i), 0))

    def prev(f):
        return pl.BlockSpec((prev_rows, D_LRU),
                            lambda i: (jnp.maximum(f(i) * tiles_per_prev - 1, 0), 0))

    def nxt(f):
        return pl.BlockSpec((BATCH, D_LRU),
                            lambda i: (jnp.minimum((f(i) + 1) * tiles_per_next, last_next), 0))

    fwd = lambda i: i
    const = lambda shape: pl.BlockSpec(shape, lambda i: (0,) * len(shape))
    return pl.pallas_call(
        _scan_kernel,
        grid=(N_TILES,),
        in_specs=[cur(fwd), prev(fwd), nxt(fwd), cur(_rev_tile), prev(_rev_tile), nxt(_rev_tile),
                  const((4, D_LRU)), const((1, D_LRU)),
                  const((2, 2, D_LRU // GATE_BLK, GATE_BLK, GATE_BLK)),
                  const((2, 2, D_LRU)), const((2, D_LRU))],
        out_specs=[pl.BlockSpec((TR, D_LRU), lambda i: (i, 0)),
                   pl.BlockSpec((TR, D_LRU), lambda i: (_rev_tile(i), 0))],
        out_shape=[jax.ShapeDtypeStruct((R_ALL, D_LRU), F32)] * 2,
        scratch_shapes=[pltpu.VMEM((2, BATCH, D_LRU), F32)],
        compiler_params=_cparams(("arbitrary",)),
        name="lru_scan",
    )(z, z, z, z, z, z, conv_w, conv_b, gate_w, gate_b, lam)


def _gelu_tanh(y):
    c = 0.7978845608028654
    return 0.5 * y * (1.0 + jnp.tanh(c * (y + 0.044715 * (y * y * y))))


def _pool_inverse_counts():
    t = jnp.arange(TT)[None, :, None]
    kind = jnp.arange(CTX_TILES + 1)[:, None, None]
    lo = jnp.where(kind < CTX_TILES, -TT * kind, 0)
    hi = jnp.where(kind < CTX_TILES, CTX_LEN - TT * kind, TT)
    half = (jnp.asarray(POOL_WINDOWS) // 2)[None, None, :]
    cnt = jnp.minimum(t + half, hi) - jnp.maximum(t - half, lo)
    inv = 1.0 / cnt.astype(F32)
    inv = jnp.repeat(jnp.repeat(inv, POOL_GROUP_DIM, axis=2), BATCH, axis=1)
    return inv


def _pool_groups(tile, xp_ref, xpp_ref, xpn_ref, inv_ref):
    in_ctx = tile < CTX_TILES
    prev_ok = in_ctx & (tile > 0)
    next_ok = in_ctx & (tile < CTX_TILES - 1)
    x = xp_ref[...]
    p = jnp.concatenate([jnp.where(prev_ok, xpp_ref[...], 0.0), x,
                         jnp.where(next_ok, xpn_ref[...], 0.0)], axis=0)
    p = p.reshape(TT + 2 * POOL_HALO, BATCH, D_POOL)
    inv = inv_ref[...].reshape(TT, BATCH, D_POOL)
    outs = []
    for g, win in enumerate(POOL_WINDOWS):
        half = win // 2
        acc = p[:, :, g * POOL_GROUP_DIM:(g + 1) * POOL_GROUP_DIM]
        width = 1
        while width < win:
            acc = acc[:acc.shape[0] - width] + acc[width:]
            width *= 2
        start = POOL_HALO - half
        wsum = acc[start:start + TT]
        lanes = slice(g * POOL_GROUP_DIM, (g + 1) * POOL_GROUP_DIM)
        centre = p[POOL_HALO:POOL_HALO + TT, :, lanes]
        outs.append((wsum * inv[:, :, lanes] - centre).reshape(TR, POOL_GROUP_DIM))
    return outs


def _mixer_kernel(t_off, hf_ref, hr_ref, y_ref, xp_ref, xpp_ref, xpn_ref, inv_ref, x_ref,
                  g1_ref, sh2_ref, sc2_ref, pw_ref, pb_ref, ps_ref, wo_ref,
                  lg_ref, lb_ref, rw_ref, rb_ref, tri_ref,
                  xo_ref, route_ref, cnt_ref, wobf_ref):
    i = pl.program_id(0)
    tile = i + t_off
    _cast_once(wo_ref, wobf_ref)

    lru = (hf_ref[...] + hr_ref[...]) * _gelu_tanh(y_ref[...])
    diffs = _pool_groups(tile, xp_ref, xpp_ref, xpn_ref, inv_ref)
    pooled = jnp.concatenate(
        [jnp.dot(d.astype(BF16), pw_ref[g].astype(BF16), preferred_element_type=F32)
         for g, d in enumerate(diffs)], axis=1)
    pooled = (pooled + pb_ref[...]) * ps_ref[...]
    mix = (jnp.dot(lru.astype(BF16), wobf_ref[0:D_LRU, :], preferred_element_type=F32)
           + jnp.dot(pooled.astype(BF16), wobf_ref[D_LRU:, :], preferred_element_type=F32))
    x = ALPHA * x_ref[...] + _per_batch(mix, g1_ref[0], jnp.multiply)
    x = _layer_norm(x) * lg_ref[...] + lb_ref[...]
    xo_ref[...] = x
    h2 = _modulate(x, sh2_ref, sc2_ref)

    h_hi = h2.astype(BF16)
    h_lo = (h2 - h_hi.astype(F32)).astype(BF16)
    by_hi = jnp.dot(h_hi, rw_ref[...], preferred_element_type=F32)
    logits = (by_hi[:, :LANES] + by_hi[:, LANES:]
              + jnp.dot(h_lo, rw_ref[:, 0:LANES], preferred_element_type=F32)) + rb_ref[...]
    work = logits.T[0:N_EXPERTS, :]
    expert = lax.broadcasted_iota(jnp.int32, (N_EXPERTS, TR), 0).astype(F32)
    vals, idxs, sels = [], [], []
    for _ in range(TOP_K):
        m = jnp.max(work, axis=0, keepdims=True)
        idx = jnp.min(jnp.where(work == m, expert, float(N_EXPERTS)), axis=0, keepdims=True)
        sel = expert == idx
        vals.append(m)
        idxs.append(idx)
        sels.append(sel)
        work = jnp.where(sel, -jnp.inf, work)
    exps = [jnp.exp(v - vals[0]) for v in vals]
    denom = exps[0] + exps[1] + exps[2] + exps[3]
    chosen = jnp.zeros((N_EXPERTS, TR), F32)
    for sel in sels:
        chosen = chosen + sel.astype(F32)
    before = jnp.dot(chosen.astype(BF16), tri_ref[...], preferred_element_type=F32)
    ranks = [jnp.sum(jnp.where(sel, before, 0.0), axis=0, keepdims=True) for sel in sels]
    route_ref[0] = jnp.concatenate(
        [e / denom for e in exps] + idxs + ranks
        + [jnp.zeros((ROUTE_ROWS - 3 * TOP_K, TR), F32)], axis=0)
    cnt_ref[0] = jnp.broadcast_to(jnp.sum(chosen, axis=1, keepdims=True), (N_EXPERTS, LANES))


def _mixer(layer, t_off, hf, hr, z, inv_cnt, x, mod, pool_w, pool_b, pool_scale, w_out,
           ln_g, ln_b, rw_pad, rb_pad, tri):
    n = N_TILES - t_off
    halo_rows = POOL_HALO * BATCH
    per = TR // halo_rows
    last_halo = R_ALL // halo_rows - 1
    xp_col = 2 * D_LRU // D_POOL
    row = lambda w: pl.BlockSpec((TR, w), lambda i: (i + t_off, 0))
    out_row = lambda w: pl.BlockSpec((TR, w), lambda i: (i, 0))
    const = lambda shape: pl.BlockSpec(shape, lambda i: (0,) * len(shape))

    def mod_spec(chunk):
        return pl.BlockSpec((1, BATCH, D_MODEL),
                            lambda i: ((i + t_off >= CTX_TILES).astype(jnp.int32), 0, chunk))

    return pl.pallas_call(
        functools.partial(_mixer_kernel, t_off),
        grid=(n,),
        in_specs=[
            row(D_LRU), row(D_LRU),
            pl.BlockSpec((TR, D_LRU), lambda i: (i + t_off, 1)),
            pl.BlockSpec((TR, D_POOL), lambda i: (i + t_off, xp_col)),
            pl.BlockSpec((halo_rows, D_POOL),
                         lambda i: (jnp.maximum((i + t_off) * per - 1, 0), xp_col)),
            pl.BlockSpec((halo_rows, D_POOL),
                         lambda i: (jnp.minimum((i + t_off + 1) * per, last_halo), xp_col)),
            pl.BlockSpec((None, TR, D_POOL), lambda i: (jnp.minimum(i + t_off, CTX_TILES), 0, 0)),
            row(D_MODEL),
            mod_spec(2), mod_spec(3), mod_spec(4),
            pl.BlockSpec((None, len(POOL_WINDOWS), POOL_GROUP_DIM, POOL_GROUP_DIM),
                         lambda i: (layer, 0, 0, 0)),
            const((1, D_POOL)), const((1, D_POOL)),
            pl.BlockSpec((None, D_MODEL, D_MODEL), lambda i: (layer, 0, 0)),
            const((1, D_MODEL)), const((1, D_MODEL)),
            const((D_MODEL, 2 * LANES)), const((1, LANES)),
            const((TR, TR)),
        ],
        out_specs=[out_row(D_MODEL),
                   pl.BlockSpec((1, ROUTE_ROWS, TR), lambda i: (i, 0, 0)),
                   pl.BlockSpec((1, N_EXPERTS, LANES), lambda i: (i, 0, 0))],
        out_shape=[jax.ShapeDtypeStruct((n * TR, D_MODEL), F32),
                   jax.ShapeDtypeStruct((n, ROUTE_ROWS, TR), F32),
                   jax.ShapeDtypeStruct((n, N_EXPERTS, LANES), F32)],
        scratch_shapes=[pltpu.VMEM((D_MODEL, D_MODEL), BF16)],
        compiler_params=_cparams(("arbitrary",)),
        name="mixer_out",
    )(hf, hr, z, z, z, z, inv_cnt, x, mod, mod, mod, pool_w, pool_b, pool_scale, w_out,
      ln_g, ln_b, rw_pad, rb_pad, tri)


def _one_hot_rows(targets, index, weights=None):
    matches = [index == t for t in targets]
    hit = jnp.zeros(index.shape, F32)
    for m in reversed(matches):
        hit = jnp.where(m, 1.0, hit)
    if weights is None:
        return hit
    picked = jnp.zeros(index.shape, F32)
    for m, w in zip(reversed(matches), reversed(weights)):
        picked = jnp.where(m, w, picked)
    return hit, picked


def _for_each_group(block, len_ref, body):
    def step(e, carry):
        n = pl.multiple_of(len_ref[block, e], GROUP)

        @pl.when(n > 0)
        def _():
            body(e, n)

        return carry

    lax.fori_loop(0, N_EXPERTS, step, 0)


def _group(start, n):
    return pl.ds(pl.multiple_of(start, GROUP), n)


def _dispatch_kernel(src_ref, len_ref, dst_ref, tot_ref, tail_s_ref, tail_n_ref, na_ref,
                     lpos_ref, x_ref, sh2_ref, sc2_ref, xs_ref, stage_ref, zero_ref, sems):
    i = pl.program_id(0)
    n_blocks = pl.num_programs(0) - 1
    slot = i % 2

    @pl.when(i < n_blocks)
    def _():
        lpos = lpos_ref[...]
        targets = [lpos[k:k + 1, :] for k in range(TOP_K)]
        h = _modulate(x_ref[...], sh2_ref, sc2_ref).astype(BF16)

        def permute(jb, carry):
            base = pl.multiple_of(jb * MB, MB)
            rows = (lax.broadcasted_iota(jnp.int32, (MB, TR), 0) + base).astype(F32)
            sel = _one_hot_rows(targets, rows).astype(BF16)
            stage_ref[slot, pl.ds(base, MB), :] = jnp.dot(sel, h, preferred_element_type=F32)
            return carry

        lax.fori_loop(0, (tot_ref[i] + MB - 1) // MB, permute, 0)
        _for_each_group(i, len_ref, lambda e, n: pltpu.make_async_copy(
            stage_ref.at[slot, _group(src_ref[i, e], n)],
            xs_ref.at[_group(dst_ref[i, e], n)], sems.at[slot]).start())

    @pl.when(i > 0)
    def _():
        rows = _group(0, pl.multiple_of(tot_ref[i - 1], GROUP))
        pltpu.make_async_copy(stage_ref.at[1 - slot, rows], xs_ref.at[rows],
                              sems.at[1 - slot]).wait()

    @pl.when(i == n_blocks)
    def _():
        zero_ref[...] = jnp.zeros_like(zero_ref)
        fill = sems.at[2]

        def expert_tail(e, n):
            return pltpu.make_async_copy(zero_ref.at[_group(0, n)],
                                         xs_ref.at[_group(tail_s_ref[0, e], n)], fill)

        def whole_tile(t):
            return pltpu.make_async_copy(zero_ref, xs_ref.at[_group(t * TM, TM)], fill)

        n_tiles = xs_ref.shape[0] // TM
        _for_each_group(0, tail_n_ref, lambda e, n: expert_tail(e, n).start())
        lax.fori_loop(na_ref[0], n_tiles, lambda t, c: (whole_tile(t).start(), c)[1], 0)
        _for_each_group(0, tail_n_ref, lambda e, n: expert_tail(e, n).wait())
        lax.fori_loop(na_ref[0], n_tiles, lambda t, c: (whole_tile(t).wait(), c)[1], 0)


def _dispatch(t_off, plan, x, mod):
    n = x.shape[0] // TR
    last = n - 1
    grid_spec = pltpu.PrefetchScalarGridSpec(
        num_scalar_prefetch=7,
        grid=(n + 1,),
        in_specs=[
            pl.BlockSpec((None, SUBLANES, TR), lambda i, *_: (jnp.minimum(i, last), 0, 0)),
            pl.BlockSpec((TR, D_MODEL), lambda i, *_: (jnp.minimum(i, last), 0)),
            _mod_spec(3, t_off), _mod_spec(4, t_off),
        ],
        out_specs=pl.BlockSpec(memory_space=pl.ANY),
        scratch_shapes=[pltpu.VMEM((2, RB_STAGE, D_MODEL), F32),
                        pltpu.VMEM((TM, D_MODEL), F32),
                        pltpu.SemaphoreType.DMA((3,))],
    )
    return pl.pallas_call(
        _dispatch_kernel,
        grid_spec=grid_spec,
        out_shape=jax.ShapeDtypeStruct((plan["n_sorted"], D_MODEL), F32),
        compiler_params=_cparams(("arbitrary",)),
        name="moe_dispatch",
    )(plan["src_row"], plan["n_rows"], plan["dst_row"], plan["block_rows"],
      plan["tail_start"][None], plan["tail_len"][None], plan["n_active"], plan["lpos_t"],
      x, mod, mod)


def _expert_kernel(layer, te_ref, na_ref, first_ref, nxt_ref, slot_ref,
                   x_ref, b1_ref, b2_ref, w1_hbm, w2_hbm, o_ref,
                   w1buf_ref, w2buf_ref, w1c_ref, w2c_ref, sems):
    i = pl.program_id(0)
    e = te_ref[i]
    slot = slot_ref[i]

    def fetch(expert, s):
        return (pltpu.make_async_copy(w1_hbm.at[layer, expert], w1buf_ref.at[s], sems.at[s, 0]),
                pltpu.make_async_copy(w2_hbm.at[layer, expert], w2buf_ref.at[s], sems.at[s, 1]))

    @pl.when(i == 0)
    def _():
        for copy in fetch(e, slot):
            copy.start()

    @pl.when(first_ref[i] == 1)
    def _():
        for copy in fetch(e, slot):
            copy.wait()
        w1c_ref[...] = w1buf_ref[slot].astype(BF16)
        w2c_ref[...] = w2buf_ref[slot].astype(BF16)

        @pl.when(nxt_ref[i] >= 0)
        def _():
            for copy in fetch(nxt_ref[i], 1 - slot):
                copy.start()

    @pl.when(i < na_ref[0])
    def _():
        x = x_ref[...].astype(BF16)
        y = jnp.broadcast_to(b2_ref[...], (TM, D_MODEL))
        for c in range(D_EXPERT // FFN_CHUNK):
            g_cols = slice(c * FFN_CHUNK, (c + 1) * FFN_CHUNK)
            l_cols = slice(D_EXPERT + c * FFN_CHUNK, D_EXPERT + (c + 1) * FFN_CHUNK)
            glu = jnp.dot(x, w1c_ref[:, g_cols], preferred_element_type=F32) + b1_ref[:, g_cols]
            lin = jnp.dot(x, w1c_ref[:, l_cols], preferred_element_type=F32) + b1_ref[:, l_cols]
            glu = jnp.minimum(glu, SWIGLU_LIMIT)
            lin = jnp.clip(lin, -SWIGLU_LIMIT, SWIGLU_LIMIT)
            act = glu * _sigmoid(SWIGLU_ALPHA * glu) * (lin + 1.0)
            y = y + jnp.dot(act.astype(BF16), w2c_ref[g_cols, :], preferred_element_type=F32)
        o_ref[...] = y

    @pl.when(i >= na_ref[0])
    def _():
        o_ref[...] = jnp.zeros_like(o_ref)


def _experts(layer, plan, xs, w1, b1, w2, b2):
    n_tiles = xs.shape[0] // TM
    act_tile = lambda i, te, na, *_: (jnp.maximum(jnp.minimum(i, na[0] - 1), 0), 0)
    expert = lambda i, te, *_: (layer, te[i], 0, 0)
    grid_spec = pltpu.PrefetchScalarGridSpec(
        num_scalar_prefetch=5,
        grid=(n_tiles,),
        in_specs=[
            pl.BlockSpec((TM, D_MODEL), act_tile),
            pl.BlockSpec((None, None, 1, 2 * D_EXPERT), expert),
            pl.BlockSpec((None, None, 1, D_MODEL), expert),
            pl.BlockSpec(memory_space=pl.ANY),
            pl.BlockSpec(memory_space=pl.ANY),
        ],
        out_specs=pl.BlockSpec((TM, D_MODEL), lambda i, *_: (i, 0)),
        scratch_shapes=[pltpu.VMEM((2, D_MODEL, 2 * D_EXPERT), F32),
                        pltpu.VMEM((2, D_EXPERT, D_MODEL), F32),
                        pltpu.VMEM((D_MODEL, 2 * D_EXPERT), BF16),
                        pltpu.VMEM((D_EXPERT, D_MODEL), BF16),
                        pltpu.SemaphoreType.DMA((2, 2))],
    )
    return pl.pallas_call(
        functools.partial(_expert_kernel, layer),
        grid_spec=grid_spec,
        out_shape=jax.ShapeDtypeStruct((xs.shape[0], D_MODEL), F32),
        compiler_params=_cparams(("arbitrary",)),
        name="moe_experts",
    )(plan["tile_expert"], plan["n_active"], plan["tile_first"], plan["tile_next"],
      plan["tile_slot"], xs, b1.reshape(DEPTH, N_EXPERTS, 1, -1),
      b2.reshape(DEPTH, N_EXPERTS, 1, -1), w1, w2)


def _combine_kernel(last, src_ref, len_ref, dst_ref, tot_ref, lpos_ref, gate_ref, x_ref,
                    g2_ref, lg_ref, lb_ref, ys_ref, *rest):
    if last:
        o_ref, ybuf_ref, sems = rest
    else:
        sh_ref, sc_ref, w_ref, o_ref, z_ref, ybuf_ref, sems, wbf_ref = rest
        _cast_once(w_ref, wbf_ref)
    i = pl.program_id(0)
    n_blocks = pl.num_programs(0)
    slot = i % 2

    def start_block(block, s):
        _for_each_group(block, len_ref, lambda e, n: pltpu.make_async_copy(
            ys_ref.at[_group(dst_ref[block, e], n)],
            ybuf_ref.at[s, _group(src_ref[block, e], n)], sems.at[s]).start())

    @pl.when(i == 0)
    def _():
        ybuf_ref[...] = jnp.zeros_like(ybuf_ref)
        start_block(0, 0)

    @pl.when(i + 1 < n_blocks)
    def _():
        start_block(i + 1, 1 - slot)

    rows = _group(0, pl.multiple_of(tot_ref[i], GROUP))
    pltpu.make_async_copy(ys_ref.at[rows], ybuf_ref.at[slot, rows], sems.at[slot]).wait()
    lpos = lpos_ref[...]
    gate = gate_ref[...]
    targets = [jnp.broadcast_to(lpos[:, k:k + 1], (TR, KB)) for k in range(TOP_K)]
    weights = [jnp.broadcast_to(gate[:, k:k + 1], (TR, KB)) for k in range(TOP_K)]
    cols = lax.broadcasted_iota(jnp.int32, (TR, KB), 1)
    f = jnp.zeros((TR, D_MODEL), F32)
    for jb in range(RB // KB):
        _, picked = _one_hot_rows(targets, cols + jb * KB, weights)
        f = f + jnp.dot(picked.astype(BF16),
                        ybuf_ref[slot, jb * KB:(jb + 1) * KB, :].astype(BF16),
                        preferred_element_type=F32)
    x = ALPHA * x_ref[...] + _per_batch(f, g2_ref[0], jnp.multiply)
    out = _layer_norm(x) * lg_ref[...] + lb_ref[...]
    if last:
        o_ref[...] = jnp.transpose(out.reshape(TT, BATCH, D_MODEL), (1, 0, 2))
    else:
        o_ref[...] = out
        z_ref[...] = _in_projection(out, sh_ref, sc_ref, wbf_ref)


def _combine(t_off, plan, x, mod, ln_g, ln_b, ys, next_layer=None):
    n = x.shape[0] // TR
    last = next_layer is None
    row = lambda w: pl.BlockSpec((TR, w), lambda i, *_: (i, 0))
    const = lambda shape: pl.BlockSpec(shape, lambda i, *_: (0,) * len(shape))
    if last:
        extra_specs, extra_args, extra_scratch = [], (), []
        out_spec = pl.BlockSpec((BATCH, TT, D_MODEL), lambda i, *_: (0, i, 0))
        out_shape = jax.ShapeDtypeStruct((BATCH, n * TT, D_MODEL), F32)
    else:
        next_mod, w_in, next_l = next_layer
        extra_specs = [_mod_spec(0, t_off), _mod_spec(1, t_off),
                       pl.BlockSpec((None, D_MODEL, D_IN), lambda i, *_: (next_l, 0, 0))]
        extra_args = (next_mod, next_mod, w_in)
        extra_scratch = [pltpu.VMEM((D_MODEL, D_IN), BF16)]
        out_spec = [row(D_MODEL), row(D_IN)]
        out_shape = [jax.ShapeDtypeStruct((n * TR, D_MODEL), F32),
                     jax.ShapeDtypeStruct((n * TR, D_IN), F32)]
    grid_spec = pltpu.PrefetchScalarGridSpec(
        num_scalar_prefetch=4,
        grid=(n,),
        in_specs=[
            row(TOP_K), row(TOP_K), row(D_MODEL), _mod_spec(5, t_off),
            const((1, D_MODEL)), const((1, D_MODEL)),
            pl.BlockSpec(memory_space=pl.ANY),
        ] + extra_specs,
        out_specs=out_spec,
        scratch_shapes=[pltpu.VMEM((2, RB, D_MODEL), F32),
                        pltpu.SemaphoreType.DMA((2,))] + extra_scratch,
    )
    return pl.pallas_call(
        functools.partial(_combine_kernel, last),
        grid_spec=grid_spec,
        out_shape=out_shape,
        compiler_params=_cparams(("arbitrary",)),
        name="moe_combine",
    )(plan["src_row"], plan["n_rows"], plan["dst_row"], plan["block_rows"], plan["lpos"],
      plan["gate"], x, mod, ln_g, ln_b, ys, *extra_args)

def _routing_plan(route, counts):
    i32 = jnp.int32
    nb = counts.shape[0]
    experts = jnp.arange(N_EXPERTS, dtype=i32)
    eid = route[:, LANE_EID:LANE_EID + TOP_K].astype(i32)
    rank = route[:, LANE_RANK:LANE_RANK + TOP_K].astype(i32)
    n = counts[:, :, 0].astype(i32)
    n = (n + GROUP - 1) // GROUP * GROUP
    src_row = jnp.cumsum(n, axis=1) - n
    seg = jnp.sum(n, axis=0)
    tiles_e = (seg + TM - 1) // TM
    tile_end = jnp.cumsum(tiles_e)
    e_start = (tile_end - tiles_e) * TM
    n_active = tile_end[-1]
    dst_row = e_start[None, :] + jnp.cumsum(n, axis=0) - n

    onehot = eid[..., None] == experts
    lpos = jnp.sum(jnp.where(onehot, src_row[:, None, None, :], 0), axis=-1) + rank
    lpos_t = jnp.concatenate(
        [lpos.astype(F32), jnp.full((nb, SUBLANES - TOP_K, TR), -1.0, F32)], axis=1)

    n_tiles = (nb * RB + TM - 1) // TM + N_EXPERTS
    t = jnp.minimum(jnp.arange(n_tiles, dtype=i32), n_active - 1)
    tile_expert = jnp.sum((t[:, None] >= tile_end[None, :]).astype(i32), axis=1)
    tile_expert = jnp.minimum(tile_expert, N_EXPERTS - 1)
    tile_first = jnp.concatenate(
        [jnp.ones((1,), i32), (tile_expert[1:] != tile_expert[:-1]).astype(i32)])
    has_rows = tiles_e > 0
    later = jnp.where(has_rows[None, :] & (experts[None, :] > experts[:, None]),
                      experts[None, :], N_EXPERTS)
    next_e = jnp.min(later, axis=1)
    next_e = jnp.where(next_e == N_EXPERTS, -1, next_e)
    group = jnp.cumsum(has_rows.astype(i32)) - 1
    pick = tile_expert[:, None] == experts[None, :]
    tile_next = jnp.sum(jnp.where(pick, next_e[None, :], 0), axis=1)
    tile_slot = jnp.sum(jnp.where(pick, group[None, :], 0), axis=1) % 2
    gate = route[:, LANE_P:LANE_P + TOP_K].transpose(0, 2, 1).reshape(nb * TR, TOP_K)
    return dict(lpos=lpos.transpose(0, 2, 1).reshape(nb * TR, TOP_K), gate=gate, lpos_t=lpos_t, src_row=src_row, n_rows=n,
                dst_row=dst_row, block_rows=jnp.sum(n, axis=1), tail_start=e_start + seg,
                tail_len=tile_end * TM - e_start - seg,
                tile_expert=tile_expert, n_active=n_active.reshape(1), tile_first=tile_first,
                tile_next=tile_next, tile_slot=tile_slot,
                n_sorted=n_tiles * TM)


def _grid_sincos_parts():
    quarter = D_MODEL // 4
    omega = 1.0 / (10000.0 ** (jnp.arange(quarter, dtype=F32) / quarter))

    def emb1d(n):
        ang = jnp.arange(n, dtype=F32)[:, None] * omega[None, :]
        return jnp.concatenate([jnp.sin(ang), jnp.cos(ang)], axis=-1)

    er = emb1d(SEQ // GRID_W)
    ec = jnp.repeat(emb1d(GRID_W), BATCH, axis=0)
    return er.reshape(SEQ // GRID_W, 1, D_MODEL // 2), ec


def _gate_blocks(w):
    per = GATE_BLK // LRU_HEAD_DIM
    w = w.reshape(2, D_LRU // GATE_BLK, per, LRU_HEAD_DIM, LRU_HEAD_DIM)
    eye = jnp.eye(per, dtype=w.dtype)
    blk = jnp.einsum('dkpij,pq->dkpiqj', w, eye)
    return blk.reshape(2, D_LRU // GATE_BLK, GATE_BLK, GATE_BLK)


def kernel(x, c, ctx, c_ctx, w_mod, b_mod, w_in, conv_w, conv_b, gate_a_w, gate_a_b,
           gate_x_w, gate_x_b, lru_lambda, pool_w, pool_b, pool_scale, w_out, ln1_g, ln1_b,
           router_w, router_b, exp_w1, exp_b1, exp_w2, exp_b2, ln2_g, ln2_b):
    cvec = jnp.concatenate([c, c_ctx[None], jnp.zeros((2 * SUBLANES - BATCH - 1, D_MODEL), F32)])
    mod_all = _modulation(cvec, w_mod, b_mod)
    mods = jnp.stack([jnp.broadcast_to(mod_all[:, BATCH:BATCH + 1], (DEPTH, BATCH, 6 * D_MODEL)),
                      mod_all[:, :BATCH]], axis=1)
    tri = jnp.triu(jnp.ones((TR, TR), F32), 1).astype(BF16)
    inv_cnt = _pool_inverse_counts()

    er, ec = _grid_sincos_parts()
    xs, z = _entry(ctx, x, er, ec, mods[0], w_in)
    for l in range(DEPTH):
        last = l == DEPTH - 1
        t_off = CTX_TILES if last else 0
        mod = mods[l]
        gate_w = jnp.stack([_gate_blocks(gate_a_w[l]), _gate_blocks(gate_x_w[l])],
                           axis=1).astype(BF16)
        gate_b = jnp.stack([gate_a_b[l], gate_x_b[l]], axis=1)
        hf, hr = _scan(z, conv_w[l], conv_b[l][None], gate_w, gate_b, lru_lambda[l])
        rw_pad = jnp.zeros((D_MODEL, LANES), F32).at[:, :N_EXPERTS].set(router_w[l])
        rw_hi = rw_pad.astype(BF16)
        rw_pad = jnp.concatenate([rw_hi, (rw_pad - rw_hi.astype(F32)).astype(BF16)], axis=1)
        rb_pad = jnp.full((1, LANES), -1e30, F32).at[0, :N_EXPERTS].set(router_b[l])
        xs, route, counts = _mixer(
            l, t_off, hf, hr, z, inv_cnt, xs, mod, pool_w, pool_b[l][None], pool_scale[l][None], w_out,
            ln1_g[l][None], ln1_b[l][None], rw_pad, rb_pad, tri)
        plan = _routing_plan(route, counts)
        x_sorted = _dispatch(t_off, plan, xs, mod)
        y_sorted = _experts(l, plan, x_sorted, exp_w1, exp_b1, exp_w2, exp_b2)
        if last:
            return _combine(t_off, plan, xs, mod, ln2_g[l][None], ln2_b[l][None], y_sorted)
        xs, z = _combine(t_off, plan, xs, mod, ln2_g[l][None], ln2_b[l][None], y_sorted,
                         (mods[l + 1], w_in, l + 1))
```

```python
import functools

import jax
import jax.numpy as jnp
from jax import lax
from jax.experimental import pallas as pl
from jax.experimental.pallas import tpu as pltpu

D_MODEL = 1024
BATCH = 8
SEQ = 2048
DEPTH = 2
GRID_W = 64
CTX_LEN = 256
D_LRU = 512
N_LRU_HEADS = 8
LRU_HEAD_DIM = D_LRU // N_LRU_HEADS
LRU_C = 8.0
D_POOL = 512
POOL_WINDOWS = (2, 4, 8, 16)
POOL_GROUP_DIM = D_POOL // len(POOL_WINDOWS)
D_IN = 2 * D_LRU + D_POOL
N_EXPERTS = 32
TOP_K = 4
D_EXPERT = D_MODEL
SWIGLU_LIMIT = 7.0
SWIGLU_ALPHA = 1.702
LN_EPS = 1e-5
ALPHA = (2.0 * DEPTH) ** 0.25

F32 = jnp.float32
BF16 = jnp.bfloat16

SUBLANES = 8
LANES = 128
TT = GRID_W
TR = TT * BATCH
T_ALL = CTX_LEN + SEQ
R_ALL = T_ALL * BATCH
N_TILES = T_ALL // TT
CTX_TILES = CTX_LEN // TT
GATE_BLK = 256
POOL_HALO = max(POOL_WINDOWS) // 2
TM = 512
FFN_CHUNK = 1024
GROUP = SUBLANES
RB = TOP_K * TR + N_EXPERTS * GROUP
KB = 256
MB = 512
X_SLOTS = 3
RB_STAGE = (RB + MB - 1) // MB * MB
LANE_P, LANE_EID, LANE_RANK = 0, 4, 8
ROUTE_ROWS = 16
V7X_VMEM_BYTES = 64 * 1024 * 1024
VMEM_LIMIT = V7X_VMEM_BYTES * 7 // 8


def _cparams(sem):
    return pltpu.CompilerParams(dimension_semantics=sem, vmem_limit_bytes=VMEM_LIMIT)


def _sigmoid(x):
    return 0.5 * (1.0 + jnp.tanh(0.5 * x))


def _layer_norm(x):
    mu = jnp.mean(x, axis=-1, keepdims=True)
    xc = x - mu
    var = jnp.mean(xc * xc, axis=-1, keepdims=True)
    return xc * lax.rsqrt(var + LN_EPS)


def _per_batch(x, v, op):
    r, d = x.shape
    return op(x.reshape(r // BATCH, BATCH, d), v[None]).reshape(r, d)


def _mod_kernel(c_ref, w_ref, b_ref, o_ref):
    c = c_ref[...]
    s = c * _sigmoid(c)
    w = w_ref[...]
    s_hi, w_hi = s.astype(BF16), w.astype(BF16)
    s_lo = (s - s_hi.astype(F32)).astype(BF16)
    w_lo = (w - w_hi.astype(F32)).astype(BF16)
    o_ref[...] = (jnp.dot(s_hi, w_hi, preferred_element_type=F32)
                  + jnp.dot(s_lo, w_hi, preferred_element_type=F32)
                  + jnp.dot(s_hi, w_lo, preferred_element_type=F32)) + b_ref[...]


def _modulation(cvec, w_mod, b_mod):
    tn = 512
    return pl.pallas_call(
        _mod_kernel,
        grid=(DEPTH, 6 * D_MODEL // tn),
        in_specs=[
            pl.BlockSpec((2 * SUBLANES, D_MODEL), lambda l, j: (0, 0)),
            pl.BlockSpec((None, D_MODEL, tn), lambda l, j: (l, 0, j)),
            pl.BlockSpec((None, 1, tn), lambda l, j: (l, 0, j)),
        ],
        out_specs=pl.BlockSpec((None, 2 * SUBLANES, tn), lambda l, j: (l, 0, j)),
        out_shape=jax.ShapeDtypeStruct((DEPTH, 2 * SUBLANES, 6 * D_MODEL), F32),
        compiler_params=_cparams(("arbitrary", "arbitrary")),
        name="modulation",
    )(cvec, w_mod, b_mod.reshape(DEPTH, 1, 6 * D_MODEL))


def _modulate(x, sh_ref, sc_ref):
    h = _per_batch(x, 1.0 + sc_ref[0], jnp.multiply)
    return _per_batch(h, sh_ref[0], jnp.add)


def _in_projection(x, sh_ref, sc_ref, w_ref):
    return jnp.dot(_modulate(x, sh_ref, sc_ref).astype(BF16), w_ref[...],
                   preferred_element_type=F32)


def _cast_once(w_ref, wbf_ref):
    @pl.when(pl.program_id(0) == 0)
    def _():
        wbf_ref[...] = w_ref[...].astype(BF16)


def _entry_kernel(ctx_ref, x_ref, er_ref, ec_ref, sh_ref, sc_ref, w_ref, o_ref, z_ref, wbf_ref):
    i = pl.program_id(0)
    _cast_once(w_ref, wbf_ref)

    def time_major(src_ref):
        return jnp.transpose(src_ref[...], (1, 0, 2)).reshape(TR, D_MODEL)

    @pl.when(i < CTX_TILES)
    def _():
        o_ref[...] = _layer_norm(time_major(ctx_ref))

    @pl.when(i >= CTX_TILES)
    def _():
        pos = jnp.concatenate(
            [jnp.broadcast_to(er_ref[0], (TR, D_MODEL // 2)), ec_ref[...]], axis=1)
        o_ref[...] = _layer_norm(time_major(x_ref) + pos)

    z_ref[...] = _in_projection(o_ref[...], sh_ref, sc_ref, wbf_ref)


def _mod_spec(chunk, t_off=0):
    return pl.BlockSpec((1, BATCH, D_MODEL),
                        lambda i, *_: ((i + t_off >= CTX_TILES).astype(jnp.int32), 0, chunk))


def _entry(ctx, x, er, ec, mod, w_in):
    return pl.pallas_call(
        _entry_kernel,
        grid=(N_TILES,),
        in_specs=[
            pl.BlockSpec((BATCH, TT, D_MODEL), lambda i: (0, jnp.minimum(i, CTX_TILES - 1), 0)),
            pl.BlockSpec((BATCH, TT, D_MODEL), lambda i: (0, jnp.maximum(i - CTX_TILES, 0), 0)),
            pl.BlockSpec((1, 1, D_MODEL // 2), lambda i: (jnp.maximum(i - CTX_TILES, 0), 0, 0)),
            pl.BlockSpec((TR, D_MODEL // 2), lambda i: (0, 0)),
            _mod_spec(0), _mod_spec(1),
            pl.BlockSpec((None, D_MODEL, D_IN), lambda i: (0, 0, 0)),
        ],
        out_specs=[pl.BlockSpec((TR, D_MODEL), lambda i: (i, 0)),
                   pl.BlockSpec((TR, D_IN), lambda i: (i, 0))],
        out_shape=[jax.ShapeDtypeStruct((R_ALL, D_MODEL), F32),
                   jax.ShapeDtypeStruct((R_ALL, D_IN), F32)],
        scratch_shapes=[pltpu.VMEM((D_MODEL, D_IN), BF16)],
        compiler_params=_cparams(("arbitrary",)),
        name="entry_ln",
    )(ctx, x, er, ec, mod, mod, w_in)


def _rev_tile(i):
    return jnp.where(i < CTX_TILES, CTX_TILES - 1 - i, N_TILES - 1 + CTX_TILES - i)


def _block_diag_dot(u_bf, w_ref, d, g):
    halves = [jnp.dot(u_bf[:, k * GATE_BLK:(k + 1) * GATE_BLK], w_ref[d, g, k],
                      preferred_element_type=F32) for k in range(D_LRU // GATE_BLK)]
    return jnp.concatenate(halves, axis=1)


def _lru_coeffs(tile, z_ref, zp_ref, zn_ref, cw_ref, cb_ref, gw_ref, gb_ref, lam_ref, d):
    x = z_ref[...]
    seg_first = (tile == 0) | (tile == CTX_TILES)
    seg_last = (tile == CTX_TILES - 1) | (tile == N_TILES - 1)
    prev = jnp.where(seg_first, 0.0, zp_ref[...])
    nxt = jnp.where(seg_last, 0.0, zn_ref[...])
    xm2 = jnp.concatenate([prev, x[:-2 * BATCH]], axis=0)
    xm1 = jnp.concatenate([prev[BATCH:], x[:-BATCH]], axis=0)
    xp1 = jnp.concatenate([x[BATCH:], nxt], axis=0)
    u = (cb_ref[...] + xm2 * cw_ref[0:1] + xm1 * cw_ref[1:2]
         + x * cw_ref[2:3] + xp1 * cw_ref[3:4])
    u_bf = u.astype(BF16)
    r = _sigmoid(_block_diag_dot(u_bf, gw_ref, d, 0) + gb_ref[d, 0:1])
    ig = _sigmoid(_block_diag_dot(u_bf, gw_ref, d, 1) + gb_ref[d, 1:2])
    nl = -lam_ref[d:d + 1]
    softplus = jnp.maximum(nl, 0.0) + jnp.log(1.0 + jnp.exp(-jnp.abs(nl)))
    log_a = (-LRU_C) * r * softplus
    a = jnp.exp(log_a)
    gap = 1.0 - a * a
    mult = jnp.where(gap > 0.0, gap * lax.rsqrt(gap), 0.0)
    return a, mult * (ig * u)


def _scan_kernel(zf_ref, zfp_ref, zfn_ref, zr_ref, zrp_ref, zrn_ref,
                 cw_ref, cb_ref, gw_ref, gb_ref, lam_ref,
                 hf_ref, hr_ref, state_ref):
    i = pl.program_id(0)

    @pl.when(i == 0)
    def _():
        state_ref[...] = jnp.zeros_like(state_ref)

    af, bf = _lru_coeffs(i, zf_ref, zfp_ref, zfn_ref, cw_ref, cb_ref, gw_ref, gb_ref,
                         lam_ref, 0)
    ar, br = _lru_coeffs(_rev_tile(i), zr_ref, zrp_ref, zrn_ref, cw_ref, cb_ref, gw_ref,
                         gb_ref, lam_ref, 1)
    hf = state_ref[0]
    hr = state_ref[1]
    for s in range(TT):
        f = slice(s * BATCH, (s + 1) * BATCH)
        hf = af[f] * hf + bf[f]
        hf_ref[f, :] = hf
        b = slice((TT - 1 - s) * BATCH, (TT - s) * BATCH)
        hr = ar[b] * hr + br[b]
        hr_ref[b, :] = hr
    state_ref[0] = hf
    state_ref[1] = hr


def _scan(z, conv_w, conv_b, gate_w, gate_b, lam):
    prev_rows = 2 * BATCH
    tiles_per_prev = TR // prev_rows
    tiles_per_next = TR // BATCH
    last_next = R_ALL // BATCH - 1

    def cur(f):
        return pl.BlockSpec((TR, D_LRU), lambda i: (f(i), 0))

    def prev(f):
        return pl.BlockSpec((prev_rows, D_LRU),
                            lambda i: (jnp.maximum(f(i) * tiles_per_prev - 1, 0), 0))

    def nxt(f):
        return pl.BlockSpec((BATCH, D_LRU),
                            lambda i: (jnp.minimum((f(i) + 1) * tiles_per_next, last_next), 0))

    fwd = lambda i: i
    const = lambda shape: pl.BlockSpec(shape, lambda i: (0,) * len(shape))
    return pl.pallas_call(
        _scan_kernel,
        grid=(N_TILES,),
        in_specs=[cur(fwd), prev(fwd), nxt(fwd), cur(_rev_tile), prev(_rev_tile), nxt(_rev_tile),
                  const((4, D_LRU)), const((1, D_LRU)),
                  const((2, 2, D_LRU // GATE_BLK, GATE_BLK, GATE_BLK)),
                  const((2, 2, D_LRU)), const((2, D_LRU))],
        out_specs=[pl.BlockSpec((TR, D_LRU), lambda i: (i, 0)),
                   pl.BlockSpec((TR, D_LRU), lambda i: (_rev_tile(i), 0))],
        out_shape=[jax.ShapeDtypeStruct((R_ALL, D_LRU), F32)] * 2,
        scratch_shapes=[pltpu.VMEM((2, BATCH, D_LRU), F32)],
        compiler_params=_cparams(("arbitrary",)),
        name="lru_scan",
    )(z, z, z, z, z, z, conv_w, conv_b, gate_w, gate_b, lam)


def _gelu_tanh(y):
    c = 0.7978845608028654
    return 0.5 * y * (1.0 + jnp.tanh(c * (y + 0.044715 * (y * y * y))))


def _pool_inverse_counts():
    t = jnp.arange(TT)[None, :, None]
    kind = jnp.arange(CTX_TILES + 1)[:, None, None]
    lo = jnp.where(kind < CTX_TILES, -TT * kind, 0)
    hi = jnp.where(kind < CTX_TILES, CTX_LEN - TT * kind, TT)
    half = (jnp.asarray(POOL_WINDOWS) // 2)[None, None, :]
    cnt = jnp.minimum(t + half, hi) - jnp.maximum(t - half, lo)
    inv = 1.0 / cnt.astype(F32)
    inv = jnp.repeat(jnp.repeat(inv, POOL_GROUP_DIM, axis=2), BATCH, axis=1)
    return inv


def _pool_groups(tile, xp_ref, xpp_ref, xpn_ref, inv_ref):
    in_ctx = tile < CTX_TILES
    prev_ok = in_ctx & (tile > 0)
    next_ok = in_ctx & (tile < CTX_TILES - 1)
    x = xp_ref[...]
    p = jnp.concatenate([jnp.where(prev_ok, xpp_ref[...], 0.0), x,
                         jnp.where(next_ok, xpn_ref[...], 0.0)], axis=0)
    p = p.reshape(TT + 2 * POOL_HALO, BATCH, D_POOL)
    inv = inv_ref[...].reshape(TT, BATCH, D_POOL)
    outs = []
    for g, win in enumerate(POOL_WINDOWS):
        half = win // 2
        acc = p[:, :, g * POOL_GROUP_DIM:(g + 1) * POOL_GROUP_DIM]
        width = 1
        while width < win:
            acc = acc[:acc.shape[0] - width] + acc[width:]
            width *= 2
        start = POOL_HALO - half
        wsum = acc[start:start + TT]
        lanes = slice(g * POOL_GROUP_DIM, (g + 1) * POOL_GROUP_DIM)
        centre = p[POOL_HALO:POOL_HALO + TT, :, lanes]
        outs.append((wsum * inv[:, :, lanes] - centre).reshape(TR, POOL_GROUP_DIM))
    return outs


def _mixer_kernel(t_off, hf_ref, hr_ref, y_ref, xp_ref, xpp_ref, xpn_ref, inv_ref, x_hbm,
                  g1_ref, sh2_ref, sc2_ref, pw_ref, pb_ref, ps_ref, wo_ref,
                  lg_ref, lb_ref, rw_ref, rb_ref, tri_ref,
                  xo_ref, route_ref, cnt_ref, wobf_ref, xbuf_ref, xsems):
    i = pl.program_id(0)
    n_steps = pl.num_programs(0)
    tile = i + t_off
    _cast_once(wo_ref, wobf_ref)

    def fetch(step):
        rows = pl.ds(pl.multiple_of((step + t_off) * TR, TR), TR)
        return pltpu.make_async_copy(x_hbm.at[rows], xbuf_ref.at[step % X_SLOTS],
                                     xsems.at[step % X_SLOTS])

    @pl.when(i == 0)
    def _():
        for s in range(X_SLOTS - 1):
            fetch(s).start()

    @pl.when(i + X_SLOTS - 1 < n_steps)
    def _():
        fetch(i + X_SLOTS - 1).start()

    fetch(i).wait()
    x_ref = xbuf_ref.at[i % X_SLOTS]

    lru = (hf_ref[...] + hr_ref[...]) * _gelu_tanh(y_ref[...])
    diffs = _pool_groups(tile, xp_ref, xpp_ref, xpn_ref, inv_ref)
    pooled = jnp.concatenate(
        [jnp.dot(d.astype(BF16), pw_ref[g].astype(BF16), preferred_element_type=F32)
         for g, d in enumerate(diffs)], axis=1)
    pooled = (pooled + pb_ref[...]) * ps_ref[...]
    mix = (jnp.dot(lru.astype(BF16), wobf_ref[0:D_LRU, :], preferred_element_type=F32)
           + jnp.dot(pooled.astype(BF16), wobf_ref[D_LRU:, :], preferred_element_type=F32))
    x = ALPHA * x_ref[...] + _per_batch(mix, g1_ref[0], jnp.multiply)
    x = _layer_norm(x) * lg_ref[...] + lb_ref[...]
    xo_ref[...] = x
    h2 = _modulate(x, sh2_ref, sc2_ref)

    h_hi = h2.astype(BF16)
    h_lo = (h2 - h_hi.astype(F32)).astype(BF16)
    by_hi = jnp.dot(h_hi, rw_ref[...], preferred_element_type=F32)
    logits = (by_hi[:, :LANES] + by_hi[:, LANES:]
              + jnp.dot(h_lo, rw_ref[:, 0:LANES], preferred_element_type=F32)) + rb_ref[...]
    work = logits.T[0:N_EXPERTS, :]
    expert = lax.broadcasted_iota(jnp.int32, (N_EXPERTS, TR), 0).astype(F32)
    vals, idxs, sels = [], [], []
    for _ in range(TOP_K):
        m = jnp.max(work, axis=0, keepdims=True)
        idx = jnp.min(jnp.where(work == m, expert, float(N_EXPERTS)), axis=0, keepdims=True)
        sel = expert == idx
        vals.append(m)
        idxs.append(idx)
        sels.append(sel)
        work = jnp.where(sel, -jnp.inf, work)
    exps = [jnp.exp(v - vals[0]) for v in vals]
    denom = exps[0] + exps[1] + exps[2] + exps[3]
    chosen = jnp.zeros((N_EXPERTS, TR), F32)
    for sel in sels:
        chosen = chosen + sel.astype(F32)
    before = jnp.dot(chosen.astype(BF16), tri_ref[...], preferred_element_type=F32)
    ranks = [jnp.sum(jnp.where(sel, before, 0.0), axis=0, keepdims=True) for sel in sels]
    route_ref[0] = jnp.concatenate(
        [e / denom for e in exps] + idxs + ranks
        + [jnp.zeros((ROUTE_ROWS - 3 * TOP_K, TR), F32)], axis=0)
    cnt_ref[0] = jnp.broadcast_to(jnp.sum(chosen, axis=1, keepdims=True), (N_EXPERTS, LANES))


def _mixer(layer, t_off, hf, hr, z, inv_cnt, x, mod, pool_w, pool_b, pool_scale, w_out,
           ln_g, ln_b, rw_pad, rb_pad, tri):
    n = N_TILES - t_off
    halo_rows = POOL_HALO * BATCH
    per = TR // halo_rows
    last_halo = R_ALL // halo_rows - 1
    xp_col = 2 * D_LRU // D_POOL
    row = lambda w: pl.BlockSpec((TR, w), lambda i: (i + t_off, 0))
    out_row = lambda w: pl.BlockSpec((TR, w), lambda i: (i, 0))
    const = lambda shape: pl.BlockSpec(shape, lambda i: (0,) * len(shape))

    def mod_spec(chunk):
        return pl.BlockSpec((1, BATCH, D_MODEL),
                            lambda i: ((i + t_off >= CTX_TILES).astype(jnp.int32), 0, chunk))

    return pl.pallas_call(
        functools.partial(_mixer_kernel, t_off),
        grid=(n,),
        in_specs=[
            row(D_LRU), row(D_LRU),
            pl.BlockSpec((TR, D_LRU), lambda i: (i + t_off, 1)),
            pl.BlockSpec((TR, D_POOL), lambda i: (i + t_off, xp_col)),
            pl.BlockSpec((halo_rows, D_POOL),
                         lambda i: (jnp.maximum((i + t_off) * per - 1, 0), xp_col)),
            pl.BlockSpec((halo_rows, D_POOL),
                         lambda i: (jnp.minimum((i + t_off + 1) * per, last_halo), xp_col)),
            pl.BlockSpec((None, TR, D_POOL), lambda i: (jnp.minimum(i + t_off, CTX_TILES), 0, 0)),
            pl.BlockSpec(memory_space=pl.ANY),
            mod_spec(2), mod_spec(3), mod_spec(4),
            pl.BlockSpec((None, len(POOL_WINDOWS), POOL_GROUP_DIM, POOL_GROUP_DIM),
                         lambda i: (layer, 0, 0, 0)),
            const((1, D_POOL)), const((1, D_POOL)),
            pl.BlockSpec((None, D_MODEL, D_MODEL), lambda i: (layer, 0, 0)),
            const((1, D_MODEL)), const((1, D_MODEL)),
            const((D_MODEL, 2 * LANES)), const((1, LANES)),
            const((TR, TR)),
        ],
        out_specs=[out_row(D_MODEL),
                   pl.BlockSpec((1, ROUTE_ROWS, TR), lambda i: (i, 0, 0)),
                   pl.BlockSpec((1, N_EXPERTS, LANES), lambda i: (i, 0, 0))],
        out_shape=[jax.ShapeDtypeStruct((n * TR, D_MODEL), F32),
                   jax.ShapeDtypeStruct((n, ROUTE_ROWS, TR), F32),
                   jax.ShapeDtypeStruct((n, N_EXPERTS, LANES), F32)],
        scratch_shapes=[pltpu.VMEM((D_MODEL, D_MODEL), BF16),
                        pltpu.VMEM((X_SLOTS, TR, D_MODEL), F32),
                        pltpu.SemaphoreType.DMA((X_SLOTS,))],
        compiler_params=_cparams(("arbitrary",)),
        name="mixer_out",
    )(hf, hr, z, z, z, z, inv_cnt, x, mod, mod, mod, pool_w, pool_b, pool_scale, w_out,
      ln_g, ln_b, rw_pad, rb_pad, tri)


def _one_hot_rows(targets, index, weights=None):
    matches = [index == t for t in targets]
    hit = jnp.zeros(index.shape, F32)
    for m in reversed(matches):
        hit = jnp.where(m, 1.0, hit)
    if weights is None:
        return hit
    picked = jnp.zeros(index.shape, F32)
    for m, w in zip(reversed(matches), reversed(weights)):
        picked = jnp.where(m, w, picked)
    return hit, picked


def _for_each_group(block, len_ref, body):
    def step(e, carry):
        n = pl.multiple_of(len_ref[block, e], GROUP)

        @pl.when(n > 0)
        def _():
            body(e, n)

        return carry

    lax.fori_loop(0, N_EXPERTS, step, 0)


def _group(start, n):
    return pl.ds(pl.multiple_of(start, GROUP), n)


def _dispatch_kernel(src_ref, len_ref, dst_ref, tot_ref, tail_s_ref, tail_n_ref, na_ref,
                     lpos_ref, x_ref, sh2_ref, sc2_ref, xs_ref, stage_ref, zero_ref, sems):
    i = pl.program_id(0)
    n_blocks = pl.num_programs(0) - 1
    slot = i % 2

    @pl.when(i < n_blocks)
    def _():
        lpos = lpos_ref[...]
        targets = [lpos[k:k + 1, :] for k in range(TOP_K)]
        h = _modulate(x_ref[...], sh2_ref, sc2_ref).astype(BF16)

        def permute(jb, carry):
            base = pl.multiple_of(jb * MB, MB)
            rows = (lax.broadcasted_iota(jnp.int32, (MB, TR), 0) + base).astype(F32)
            sel = _one_hot_rows(targets, rows).astype(BF16)
            stage_ref[slot, pl.ds(base, MB), :] = jnp.dot(sel, h, preferred_element_type=F32)
            return carry

        lax.fori_loop(0, (tot_ref[i] + MB - 1) // MB, permute, 0)
        _for_each_group(i, len_ref, lambda e, n: pltpu.make_async_copy(
            stage_ref.at[slot, _group(src_ref[i, e], n)],
            xs_ref.at[_group(dst_ref[i, e], n)], sems.at[slot]).start())

    @pl.when(i > 0)
    def _():
        rows = _group(0, pl.multiple_of(tot_ref[i - 1], GROUP))
        pltpu.make_async_copy(stage_ref.at[1 - slot, rows], xs_ref.at[rows],
                              sems.at[1 - slot]).wait()

    @pl.when(i == n_blocks)
    def _():
        zero_ref[...] = jnp.zeros_like(zero_ref)
        fill = sems.at[2]

        def expert_tail(e, n):
            return pltpu.make_async_copy(zero_ref.at[_group(0, n)],
                                         xs_ref.at[_group(tail_s_ref[0, e], n)], fill)

        def whole_tile(t):
            return pltpu.make_async_copy(zero_ref, xs_ref.at[_group(t * TM, TM)], fill)

        n_tiles = xs_ref.shape[0] // TM
        _for_each_group(0, tail_n_ref, lambda e, n: expert_tail(e, n).start())
        lax.fori_loop(na_ref[0], n_tiles, lambda t, c: (whole_tile(t).start(), c)[1], 0)
        _for_each_group(0, tail_n_ref, lambda e, n: expert_tail(e, n).wait())
        lax.fori_loop(na_ref[0], n_tiles, lambda t, c: (whole_tile(t).wait(), c)[1], 0)


def _dispatch(t_off, plan, x, mod):
    n = x.shape[0] // TR
    last = n - 1
    grid_spec = pltpu.PrefetchScalarGridSpec(
        num_scalar_prefetch=7,
        grid=(n + 1,),
        in_specs=[
            pl.BlockSpec((None, SUBLANES, TR), lambda i, *_: (jnp.minimum(i, last), 0, 0)),
            pl.BlockSpec((TR, D_MODEL), lambda i, *_: (jnp.minimum(i, last), 0)),
            _mod_spec(3, t_off), _mod_spec(4, t_off),
        ],
        out_specs=pl.BlockSpec(memory_space=pl.ANY),
        scratch_shapes=[pltpu.VMEM((2, RB_STAGE, D_MODEL), F32),
                        pltpu.VMEM((TM, D_MODEL), F32),
                        pltpu.SemaphoreType.DMA((3,))],
    )
    return pl.pallas_call(
        _dispatch_kernel,
        grid_spec=grid_spec,
        out_shape=jax.ShapeDtypeStruct((plan["n_sorted"], D_MODEL), F32),
        compiler_params=_cparams(("arbitrary",)),
        name="moe_dispatch",
    )(plan["src_row"], plan["n_rows"], plan["dst_row"], plan["block_rows"],
      plan["tail_start"][None], plan["tail_len"][None], plan["n_active"], plan["lpos_t"],
      x, mod, mod)


def _expert_kernel(layer, te_ref, na_ref, first_ref, nxt_ref, slot_ref,
                   x_ref, b1_ref, b2_ref, w1_hbm, w2_hbm, o_ref,
                   w1buf_ref, w2buf_ref, w1c_ref, w2c_ref, sems):
    i = pl.program_id(0)
    e = te_ref[i]
    slot = slot_ref[i]

    def fetch(expert, s):
        return (pltpu.make_async_copy(w1_hbm.at[layer, expert], w1buf_ref.at[s], sems.at[s, 0]),
                pltpu.make_async_copy(w2_hbm.at[layer, expert], w2buf_ref.at[s], sems.at[s, 1]))

    @pl.when(i == 0)
    def _():
        for copy in fetch(e, slot):
            copy.start()

    @pl.when(first_ref[i] == 1)
    def _():
        for copy in fetch(e, slot):
            copy.wait()
        w1c_ref[...] = w1buf_ref[slot].astype(BF16)
        w2c_ref[...] = w2buf_ref[slot].astype(BF16)

        @pl.when(nxt_ref[i] >= 0)
        def _():
            for copy in fetch(nxt_ref[i], 1 - slot):
                copy.start()

    @pl.when(i < na_ref[0])
    def _():
        x = x_ref[...].astype(BF16)
        y = jnp.broadcast_to(b2_ref[...], (TM, D_MODEL))
        for c in range(D_EXPERT // FFN_CHUNK):
            g_cols = slice(c * FFN_CHUNK, (c + 1) * FFN_CHUNK)
            l_cols = slice(D_EXPERT + c * FFN_CHUNK, D_EXPERT + (c + 1) * FFN_CHUNK)
            glu = jnp.dot(x, w1c_ref[:, g_cols], preferred_element_type=F32) + b1_ref[:, g_cols]
            lin = jnp.dot(x, w1c_ref[:, l_cols], preferred_element_type=F32) + b1_ref[:, l_cols]
            glu = jnp.minimum(glu, SWIGLU_LIMIT)
            lin = jnp.clip(lin, -SWIGLU_LIMIT, SWIGLU_LIMIT)
            act = glu * _sigmoid(SWIGLU_ALPHA * glu) * (lin + 1.0)
            y = y + jnp.dot(act.astype(BF16), w2c_ref[g_cols, :], preferred_element_type=F32)
        o_ref[...] = y

    @pl.when(i >= na_ref[0])
    def _():
        o_ref[...] = jnp.zeros_like(o_ref)


def _experts(layer, plan, xs, w1, b1, w2, b2):
    n_tiles = xs.shape[0] // TM
    act_tile = lambda i, te, na, *_: (jnp.maximum(jnp.minimum(i, na[0] - 1), 0), 0)
    expert = lambda i, te, *_: (layer, te[i], 0, 0)
    grid_spec = pltpu.PrefetchScalarGridSpec(
        num_scalar_prefetch=5,
        grid=(n_tiles,),
        in_specs=[
            pl.BlockSpec((TM, D_MODEL), act_tile),
            pl.BlockSpec((None, None, 1, 2 * D_EXPERT), expert),
            pl.BlockSpec((None, None, 1, D_MODEL), expert),
            pl.BlockSpec(memory_space=pl.ANY),
            pl.BlockSpec(memory_space=pl.ANY),
        ],
        out_specs=pl.BlockSpec((TM, D_MODEL), lambda i, *_: (i, 0)),
        scratch_shapes=[pltpu.VMEM((2, D_MODEL, 2 * D_EXPERT), F32),
                        pltpu.VMEM((2, D_EXPERT, D_MODEL), F32),
                        pltpu.VMEM((D_MODEL, 2 * D_EXPERT), BF16),
                        pltpu.VMEM((D_EXPERT, D_MODEL), BF16),
                        pltpu.SemaphoreType.DMA((2, 2))],
    )
    return pl.pallas_call(
        functools.partial(_expert_kernel, layer),
        grid_spec=grid_spec,
        out_shape=jax.ShapeDtypeStruct((xs.shape[0], D_MODEL), F32),
        compiler_params=_cparams(("arbitrary",)),
        name="moe_experts",
    )(plan["tile_expert"], plan["n_active"], plan["tile_first"], plan["tile_next"],
      plan["tile_slot"], xs, b1.reshape(DEPTH, N_EXPERTS, 1, -1),
      b2.reshape(DEPTH, N_EXPERTS, 1, -1), w1, w2)


def _combine_kernel(last, src_ref, len_ref, dst_ref, tot_ref, lpos_ref, gate_ref, x_ref,
                    g2_ref, lg_ref, lb_ref, ys_ref, *rest):
    if last:
        o_ref, ybuf_ref, sems = rest
    else:
        sh_ref, sc_ref, w_ref, o_ref, z_ref, ybuf_ref, sems, wbf_ref = rest
        _cast_once(w_ref, wbf_ref)
    i = pl.program_id(0)
    n_blocks = pl.num_programs(0)
    slot = i % 2

    def start_block(block, s):
        _for_each_group(block, len_ref, lambda e, n: pltpu.make_async_copy(
            ys_ref.at[_group(dst_ref[block, e], n)],
            ybuf_ref.at[s, _group(src_ref[block, e], n)], sems.at[s]).start())

    @pl.when(i == 0)
    def _():
        ybuf_ref[...] = jnp.zeros_like(ybuf_ref)
        start_block(0, 0)

    @pl.when(i + 1 < n_blocks)
    def _():
        start_block(i + 1, 1 - slot)

    rows = _group(0, pl.multiple_of(tot_ref[i], GROUP))
    pltpu.make_async_copy(ys_ref.at[rows], ybuf_ref.at[slot, rows], sems.at[slot]).wait()
    lpos = lpos_ref[...]
    gate = gate_ref[...]
    targets = [jnp.broadcast_to(lpos[:, k:k + 1], (TR, KB)) for k in range(TOP_K)]
    weights = [jnp.broadcast_to(gate[:, k:k + 1], (TR, KB)) for k in range(TOP_K)]
    cols = lax.broadcasted_iota(jnp.int32, (TR, KB), 1)
    f = jnp.zeros((TR, D_MODEL), F32)
    for jb in range(RB // KB):
        _, picked = _one_hot_rows(targets, cols + jb * KB, weights)
        f = f + jnp.dot(picked.astype(BF16),
                        ybuf_ref[slot, jb * KB:(jb + 1) * KB, :].astype(BF16),
                        preferred_element_type=F32)
    x = ALPHA * x_ref[...] + _per_batch(f, g2_ref[0], jnp.multiply)
    out = _layer_norm(x) * lg_ref[...] + lb_ref[...]
    if last:
        o_ref[...] = jnp.transpose(out.reshape(TT, BATCH, D_MODEL), (1, 0, 2))
    else:
        o_ref[...] = out
        z_ref[...] = _in_projection(out, sh_ref, sc_ref, wbf_ref)


def _combine(t_off, plan, x, mod, ln_g, ln_b, ys, next_layer=None):
    n = x.shape[0] // TR
    last = next_layer is None
    row = lambda w: pl.BlockSpec((TR, w), lambda i, *_: (i, 0))
    const = lambda shape: pl.BlockSpec(shape, lambda i, *_: (0,) * len(shape))
    if last:
        extra_specs, extra_args, extra_scratch = [], (), []
        out_spec = pl.BlockSpec((BATCH, TT, D_MODEL), lambda i, *_: (0, i, 0))
        out_shape = jax.ShapeDtypeStruct((BATCH, n * TT, D_MODEL), F32)
    else:
        next_mod, w_in, next_l = next_layer
        extra_specs = [_mod_spec(0, t_off), _mod_spec(1, t_off),
                       pl.BlockSpec((None, D_MODEL, D_IN), lambda i, *_: (next_l, 0, 0))]
        extra_args = (next_mod, next_mod, w_in)
        extra_scratch = [pltpu.VMEM((D_MODEL, D_IN), BF16)]
        out_spec = [row(D_MODEL), row(D_IN)]
        out_shape = [jax.ShapeDtypeStruct((n * TR, D_MODEL), F32),
                     jax.ShapeDtypeStruct((n * TR, D_IN), F32)]
    grid_spec = pltpu.PrefetchScalarGridSpec(
        num_scalar_prefetch=4,
        grid=(n,),
        in_specs=[
            row(TOP_K), row(TOP_K), row(D_MODEL), _mod_spec(5, t_off),
            const((1, D_MODEL)), const((1, D_MODEL)),
            pl.BlockSpec(memory_space=pl.ANY),
        ] + extra_specs,
        out_specs=out_spec,
        scratch_shapes=[pltpu.VMEM((2, RB, D_MODEL), F32),
                        pltpu.SemaphoreType.DMA((2,))] + extra_scratch,
    )
    return pl.pallas_call(
        functools.partial(_combine_kernel, last),
        grid_spec=grid_spec,
        out_shape=out_shape,
        compiler_params=_cparams(("arbitrary",)),
        name="moe_combine",
    )(plan["src_row"], plan["n_rows"], plan["dst_row"], plan["block_rows"], plan["lpos"],
      plan["gate"], x, mod, ln_g, ln_b, ys, *extra_args)

def _routing_plan(route, counts):
    i32 = jnp.int32
    nb = counts.shape[0]
    experts = jnp.arange(N_EXPERTS, dtype=i32)
    eid = route[:, LANE_EID:LANE_EID + TOP_K].astype(i32)
    rank = route[:, LANE_RANK:LANE_RANK + TOP_K].astype(i32)
    n = counts[:, :, 0].astype(i32)
    n = (n + GROUP - 1) // GROUP * GROUP
    src_row = jnp.cumsum(n, axis=1) - n
    seg = jnp.sum(n, axis=0)
    tiles_e = (seg + TM - 1) // TM
    tile_end = jnp.cumsum(tiles_e)
    e_start = (tile_end - tiles_e) * TM
    n_active = tile_end[-1]
    dst_row = e_start[None, :] + jnp.cumsum(n, axis=0) - n

    onehot = eid[..., None] == experts
    lpos = jnp.sum(jnp.where(onehot, src_row[:, None, None, :], 0), axis=-1) + rank
    lpos_t = jnp.concatenate(
        [lpos.astype(F32), jnp.full((nb, SUBLANES - TOP_K, TR), -1.0, F32)], axis=1)

    n_tiles = (nb * RB + TM - 1) // TM + N_EXPERTS
    t = jnp.minimum(jnp.arange(n_tiles, dtype=i32), n_active - 1)
    tile_expert = jnp.sum((t[:, None] >= tile_end[None, :]).astype(i32), axis=1)
    tile_expert = jnp.minimum(tile_expert, N_EXPERTS - 1)
    tile_first = jnp.concatenate(
        [jnp.ones((1,), i32), (tile_expert[1:] != tile_expert[:-1]).astype(i32)])
    has_rows = tiles_e > 0
    later = jnp.where(has_rows[None, :] & (experts[None, :] > experts[:, None]),
                      experts[None, :], N_EXPERTS)
    next_e = jnp.min(later, axis=1)
    next_e = jnp.where(next_e == N_EXPERTS, -1, next_e)
    group = jnp.cumsum(has_rows.astype(i32)) - 1
    pick = tile_expert[:, None] == experts[None, :]
    tile_next = jnp.sum(jnp.where(pick, next_e[None, :], 0), axis=1)
    tile_slot = jnp.sum(jnp.where(pick, group[None, :], 0), axis=1) % 2
    gate = route[:, LANE_P:LANE_P + TOP_K].transpose(0, 2, 1).reshape(nb * TR, TOP_K)
    return dict(lpos=lpos.transpose(0, 2, 1).reshape(nb * TR, TOP_K), gate=gate, lpos_t=lpos_t, src_row=src_row, n_rows=n,
                dst_row=dst_row, block_rows=jnp.sum(n, axis=1), tail_start=e_start + seg,
                tail_len=tile_end * TM - e_start - seg,
                tile_expert=tile_expert, n_active=n_active.reshape(1), tile_first=tile_first,
                tile_next=tile_next, tile_slot=tile_slot,
                n_sorted=n_tiles * TM)


def _grid_sincos_parts():
    quarter = D_MODEL // 4
    omega = 1.0 / (10000.0 ** (jnp.arange(quarter, dtype=F32) / quarter))

    def emb1d(n):
        ang = jnp.arange(n, dtype=F32)[:, None] * omega[None, :]
        return jnp.concatenate([jnp.sin(ang), jnp.cos(ang)], axis=-1)

    er = emb1d(SEQ // GRID_W)
    ec = jnp.repeat(emb1d(GRID_W), BATCH, axis=0)
    return er.reshape(SEQ // GRID_W, 1, D_MODEL // 2), ec


def _gate_blocks(w):
    per = GATE_BLK // LRU_HEAD_DIM
    w = w.reshape(2, D_LRU // GATE_BLK, per, LRU_HEAD_DIM, LRU_HEAD_DIM)
    eye = jnp.eye(per, dtype=w.dtype)
    blk = jnp.einsum('dkpij,pq->dkpiqj', w, eye)
    return blk.reshape(2, D_LRU // GATE_BLK, GATE_BLK, GATE_BLK)


def kernel(x, c, ctx, c_ctx, w_mod, b_mod, w_in, conv_w, conv_b, gate_a_w, gate_a_b,
           gate_x_w, gate_x_b, lru_lambda, pool_w, pool_b, pool_scale, w_out, ln1_g, ln1_b,
           router_w, router_b, exp_w1, exp_b1, exp_w2, exp_b2, ln2_g, ln2_b):
    cvec = jnp.concatenate([c, c_ctx[None], jnp.zeros((2 * SUBLANES - BATCH - 1, D_MODEL), F32)])
    mod_all = _modulation(cvec, w_mod, b_mod)
    mods = jnp.stack([jnp.broadcast_to(mod_all[:, BATCH:BATCH + 1], (DEPTH, BATCH, 6 * D_MODEL)),
                      mod_all[:, :BATCH]], axis=1)
    tri = jnp.triu(jnp.ones((TR, TR), F32), 1).astype(BF16)
    inv_cnt = _pool_inverse_counts()

    er, ec = _grid_sincos_parts()
    xs, z = _entry(ctx, x, er, ec, mods[0], w_in)
    for l in range(DEPTH):
        last = l == DEPTH - 1
        t_off = CTX_TILES if last else 0
        mod = mods[l]
        gate_w = jnp.stack([_gate_blocks(gate_a_w[l]), _gate_blocks(gate_x_w[l])],
                           axis=1).astype(BF16)
        gate_b = jnp.stack([gate_a_b[l], gate_x_b[l]], axis=1)
        hf, hr = _scan(z, conv_w[l], conv_b[l][None], gate_w, gate_b, lru_lambda[l])
        rw_pad = jnp.zeros((D_MODEL, LANES), F32).at[:, :N_EXPERTS].set(router_w[l])
        rw_hi = rw_pad.astype(BF16)
        rw_pad = jnp.concatenate([rw_hi, (rw_pad - rw_hi.astype(F32)).astype(BF16)], axis=1)
        rb_pad = jnp.full((1, LANES), -1e30, F32).at[0, :N_EXPERTS].set(router_b[l])
        xs, route, counts = _mixer(
            l, t_off, hf, hr, z, inv_cnt, xs, mod, pool_w, pool_b[l][None], pool_scale[l][None], w_out,
            ln1_g[l][None], ln1_b[l][None], rw_pad, rb_pad, tri)
        plan = _routing_plan(route, counts)
        x_sorted = _dispatch(t_off, plan, xs, mod)
        y_sorted = _experts(l, plan, x_sorted, exp_w1, exp_b1, exp_w2, exp_b2)
        if last:
            return _combine(t_off, plan, xs, mod, ln2_g[l][None], ln2_b[l][None], y_sorted)
        xs, z = _combine(t_off, plan, xs, mod, ln2_g[l][None], ln2_b[l][None], y_sorted,
                         (mods[l + 1], w_in, l + 1))
```
